```python
import jax, jax.numpy as jnp
from jax import lax
import numpy as np

D_MODEL = 1024
BATCH = 16
SEQ = 256
DEPTH = 2
DEC_BATCH = 4
DEC_SEQ = 1024
PAST_LEN = 256

GRID_W = 64
N_HEADS = 8
N_KV_HEADS = 2
HEAD_DIM = 128
Q_GROUP = N_HEADS // N_KV_HEADS
D_ATTN = N_HEADS * HEAD_DIM
D_KV = N_KV_HEADS * HEAD_DIM
WINDOW = 128
Q_BLOCK = 128
D_RNN = D_MODEL
N_RNN_BLOCKS = 8
RNN_BLOCK = D_RNN // N_RNN_BLOCKS
CONV_W = 4
CONV_LEFT = 2
LRU_C = 8.0
ROPE_BASE = 10000.0
D_IN = 2 * D_ATTN + 2 * D_KV + 2 * D_RNN + 2 * D_MODEL
DEEPNORM_ALPHA = (2 * DEPTH) ** 0.25
DEEPNORM_BETA = (8 * DEPTH) ** -0.25
LN_EPS = 1e-6
NEG_INF = -1e30

kernel_name = "hybrid_gqa_rglru_diffusion_step"


def _layer_norm(x, g=None, b=None):
    xf = x.astype(jnp.float32)
    mu = jnp.mean(xf, axis=-1, keepdims=True)
    var = jnp.mean(jnp.square(xf - mu), axis=-1, keepdims=True)
    y = (xf - mu) * lax.rsqrt(var + LN_EPS)
    if g is not None:
        y = y * g.astype(jnp.float32) + b.astype(jnp.float32)
    return y.astype(x.dtype)


def _modulation(cond, w_mod, b_mod):
    m = jax.nn.silu(cond) @ w_mod + b_mod
    return jnp.split(m, 3, axis=-1)


def _project(x, shift, scale, w_in):
    h = _layer_norm(x) * (1 + scale) + shift
    offs = np.cumsum([D_ATTN, D_KV, D_KV, D_ATTN, D_RNN, D_RNN, D_MODEL]).tolist()
    return jnp.split(h @ w_in, offs, axis=-1)


def _rope_2d(x, rows, cols):
    half = HEAD_DIM // 2
    quarter = half // 2
    inv_freq = ROPE_BASE ** (-jnp.arange(quarter, dtype=jnp.float32) / quarter)

    def rot(xa, pos):
        ang = pos.astype(jnp.float32)[:, None] * inv_freq[None, :]
        cos = jnp.cos(ang)[None, :, None, :]
        sin = jnp.sin(ang)[None, :, None, :]
        xa = xa.astype(jnp.float32)
        x1, x2 = xa[..., :quarter], xa[..., quarter:]
        return jnp.concatenate([x1 * cos - x2 * sin, x2 * cos + x1 * sin], axis=-1)

    out = jnp.concatenate([rot(x[..., :half], rows), rot(x[..., half:], cols)], axis=-1)
    return out.astype(x.dtype)


def _sink_logits(sink, B):
    s = sink.astype(jnp.float32).reshape(N_KV_HEADS, Q_GROUP)[None, :, :, None, None]
    return jnp.broadcast_to(s, (B, N_KV_HEADS, Q_GROUP, Q_BLOCK, 1))


def _context_attention(q, k, v, sink):
    B, S = q.shape[0], q.shape[1]
    nb = S // Q_BLOCK
    scale = HEAD_DIM ** -0.5
    qb = jnp.moveaxis(q.reshape(B, nb, Q_BLOCK, N_KV_HEADS, Q_GROUP, HEAD_DIM), 1, 0)
    sink_l = _sink_logits(sink, B)

    def block(qj):
        s = jnp.einsum('bqkgd,bskd->bkgqs', qj, k).astype(jnp.float32) * scale
        p = jax.nn.softmax(jnp.concatenate([s, sink_l], axis=-1), axis=-1)[..., :S]
        return jnp.einsum('bkgqs,bskd->bqkgd', p.astype(v.dtype), v)

    out = lax.map(block, qb)
    return jnp.moveaxis(out, 0, 1).reshape(B, S, D_ATTN)


def _latent_attention(q, k, v, ck, cv, sink):
    B, T = q.shape[0], q.shape[1]
    nb = T // Q_BLOCK
    span = Q_BLOCK + 2 * WINDOW
    scale = HEAD_DIM ** -0.5
    P = ck.shape[1]
    pad = ((0, 0), (WINDOW, WINDOW), (0, 0), (0, 0))
    kp = jnp.pad(k, pad)
    vp = jnp.pad(v, pad)
    qb = jnp.moveaxis(q.reshape(B, nb, Q_BLOCK, N_KV_HEADS, Q_GROUP, HEAD_DIM), 1, 0)
    sink_l = _sink_logits(sink, B)

    def block(args):
        j, qj = args
        start = j * Q_BLOCK
        kj = lax.dynamic_slice_in_dim(kp, start, span, axis=1)
        vj = lax.dynamic_slice_in_dim(vp, start, span, axis=1)
        qpos = start + jnp.arange(Q_BLOCK)
        kpos = start - WINDOW + jnp.arange(span)
        valid = ((jnp.abs(kpos[None, :] - qpos[:, None]) <= WINDOW)
                 & (kpos >= 0)[None, :] & (kpos < T)[None, :])
        s_loc = jnp.einsum('bqkgd,bskd->bkgqs', qj, kj).astype(jnp.float32) * scale
        s_loc = jnp.where(valid[None, None, None], s_loc, NEG_INF)
        s_ctx = jnp.einsum('bqkgd,bskd->bkgqs', qj, ck).astype(jnp.float32) * scale
        p = jax.nn.softmax(jnp.concatenate([s_loc, s_ctx, sink_l], axis=-1), axis=-1)
        p_loc = p[..., :span].astype(v.dtype)
        p_ctx = p[..., span:span + P].astype(v.dtype)
        return (jnp.einsum('bkgqs,bskd->bqkgd', p_loc, vj)
                + jnp.einsum('bkgqs,bskd->bqkgd', p_ctx, cv))

    out = lax.map(block, (jnp.arange(nb), qb))
    return jnp.moveaxis(out, 0, 1).reshape(B, T, D_ATTN)


def _centred_conv(x, w, b):
    T = x.shape[1]
    xp = jnp.pad(x, ((0, 0), (CONV_LEFT, CONV_W - 1 - CONV_LEFT), (0, 0)))
    acc = w[0] * xp[:, 0:T]
    for j in range(1, CONV_W):
        acc = acc + w[j] * xp[:, j:j + T]
    return acc + b


def _lin_combine(e1, e2):
    a1, b1 = e1
    a2, b2 = e2
    return a1 * a2, a2 * b1 + b2


def _rglru(xc, h0, wa, ba, wx, bx, lam, reverse):
    B, T, _ = xc.shape
    xf = xc.astype(jnp.float32)
    xb = xf.reshape(B, T, N_RNN_BLOCKS, RNN_BLOCK)
    r = jax.nn.sigmoid(jnp.einsum('btnc,ncd->btnd', xb, wa.astype(jnp.float32)).reshape(B, T, D_RNN)
                       + ba.astype(jnp.float32))
    i = jax.nn.sigmoid(jnp.einsum('btnc,ncd->btnd', xb, wx.astype(jnp.float32)).reshape(B, T, D_RNN)
                       + bx.astype(jnp.float32))
    log_a = -LRU_C * r * jax.nn.softplus(-lam.astype(jnp.float32))
    a = jnp.exp(log_a)
    bterm = jnp.sqrt(-jnp.expm1(2.0 * log_a)) * (i * xf)
    first = T - 1 if reverse else 0
    bterm = bterm.at[:, first].add(a[:, first] * h0.astype(jnp.float32))
    _, h = lax.associative_scan(_lin_combine, (a, bterm), axis=1, reverse=reverse)
    h_last = h[:, 0] if reverse else h[:, -1]
    return h, h_last


def _rnn_merge_residual(x, gate, o_attn, xr, ga, gr, mga, mgr, h0, conv_w, conv_b,
                        wa, ba, wx, bx, lam, w_br_attn, w_br_rnn, w_out, ln_g, ln_b):
    xc = _centred_conv(xr, conv_w, conv_b)
    y_f, h_f = _rglru(xc, h0[:, 0], wa[0], ba[0], wx[0], bx[0], lam[0], reverse=False)
    y_b, h_b = _rglru(xc, h0[:, 1], wa[1], ba[1], wx[1], bx[1], lam[1], reverse=True)
    o_rnn = (y_f + y_b).astype(x.dtype)
    y_attn = (o_attn * jax.nn.silu(ga)) @ w_br_attn
    y_rnn = (o_rnn * jax.nn.silu(gr)) @ w_br_rnn
    merged = jax.nn.sigmoid(mga) * y_attn + jax.nn.sigmoid(mgr) * y_rnn
    out = merged @ w_out
    x_new = _layer_norm(DEEPNORM_ALPHA * x + gate * out, ln_g, ln_b)
    return x_new, jnp.stack([h_f, h_b], axis=1)


def setup_inputs(seed: int = 0) -> dict:
    key = jax.random.key(seed)
    ks = jax.random.split(key, 24)
    f32 = jnp.float32
    nrm = lambda k, shape, s: jax.random.normal(k, shape, f32) * s
    u = jax.random.uniform(ks[15], (DEPTH, 2, D_RNN), f32, 0.9, 0.999)
    a0 = u ** (1.0 / LRU_C)
    rg_lambda = jnp.log(a0) - jnp.log1p(-a0)
    return {
        "x_prompt": nrm(ks[0], (BATCH, SEQ, D_MODEL), 1.0),
        "x_sample": nrm(ks[1], (DEC_BATCH, DEC_SEQ, D_MODEL), 1.0),
        "cache_k": nrm(ks[2], (DEC_BATCH, DEPTH, PAST_LEN, N_KV_HEADS, HEAD_DIM), 1.0),
        "cache_v": nrm(ks[3], (DEC_BATCH, DEPTH, PAST_LEN, N_KV_HEADS, HEAD_DIM), 1.0),
        "state_h": nrm(ks[4], (DEC_BATCH, DEPTH, 2, D_RNN), 0.5),
        "c": nrm(ks[5], (DEC_BATCH, D_MODEL), 1.0),
        "c_ctx": nrm(ks[6], (D_MODEL,), 1.0),
        "w_mod": nrm(ks[7], (DEPTH, D_MODEL, 3 * D_MODEL), 0.5 * D_MODEL ** -0.5),
        "b_mod": nrm(ks[8], (DEPTH, 3 * D_MODEL), 0.01),
        "w_in": nrm(ks[9], (DEPTH, D_MODEL, D_IN), D_MODEL ** -0.5),
        "attn_sink": nrm(ks[10], (DEPTH, N_HEADS), 0.5),
        "conv_w": nrm(ks[11], (DEPTH, CONV_W, D_RNN), CONV_W ** -0.5),
        "conv_b": nrm(ks[12], (DEPTH, D_RNN), 0.01),
        "rg_wa": nrm(ks[13], (DEPTH, 2, N_RNN_BLOCKS, RNN_BLOCK, RNN_BLOCK), RNN_BLOCK ** -0.5),
        "rg_ba": nrm(ks[14], (DEPTH, 2, D_RNN), 0.01),
        "rg_wx": nrm(ks[16], (DEPTH, 2, N_RNN_BLOCKS, RNN_BLOCK, RNN_BLOCK), RNN_BLOCK ** -0.5),
        "rg_bx": nrm(ks[17], (DEPTH, 2, D_RNN), 0.01),
        "rg_lambda": rg_lambda,
        "w_br_attn": nrm(ks[18], (DEPTH, D_ATTN, D_MODEL), DEEPNORM_BETA * D_ATTN ** -0.5),
        "w_br_rnn": nrm(ks[19], (DEPTH, D_RNN, D_MODEL), DEEPNORM_BETA * D_RNN ** -0.5),
        "w_out": nrm(ks[20], (DEPTH, D_MODEL, D_MODEL), DEEPNORM_BETA * D_MODEL ** -0.5),
        "ln_g": 1.0 + nrm(ks[21], (DEPTH, D_MODEL), 0.02),
        "ln_b": nrm(ks[22], (DEPTH, D_MODEL), 0.02),
    }


def reference(x_prompt, x_sample, cache_k, cache_v, state_h, c, c_ctx, w_mod, b_mod, w_in,
              attn_sink, conv_w, conv_b, rg_wa, rg_ba, rg_wx, rg_bx, rg_lambda,
              w_br_attn, w_br_rnn, w_out, ln_g, ln_b):
    Bp, S = x_prompt.shape[0], x_prompt.shape[1]
    h0_zero = jnp.zeros((Bp, 2, D_RNN), jnp.float32)
    x = x_prompt
    ks_, vs_, hs_ = [], [], []
    for l in range(DEPTH):
        shift, scale, gate = _modulation(c_ctx, w_mod[l], b_mod[l])
        q, k, v, ga, xr, gr, mga, mgr = _project(x, shift, scale, w_in[l])
        q = q.reshape(Bp, S, N_HEADS, HEAD_DIM)
        k = k.reshape(Bp, S, N_KV_HEADS, HEAD_DIM)
        v = v.reshape(Bp, S, N_KV_HEADS, HEAD_DIM)
        o_attn = _context_attention(q, k, v, attn_sink[l])
        x, h_fin = _rnn_merge_residual(x, gate, o_attn, xr, ga, gr, mga, mgr, h0_zero,
                                       conv_w[l], conv_b[l], rg_wa[l], rg_ba[l], rg_wx[l],
                                       rg_bx[l], rg_lambda[l], w_br_attn[l], w_br_rnn[l],
                                       w_out[l], ln_g[l], ln_b[l])
        ks_.append(k)
        vs_.append(v)
        hs_.append(h_fin.astype(x_prompt.dtype))
    y_prompt = x
    new_cache_k = jnp.stack(ks_, axis=1)
    new_cache_v = jnp.stack(vs_, axis=1)
    new_state_h = jnp.stack(hs_, axis=1)

    Bd, T = x_sample.shape[0], x_sample.shape[1]
    n_rows = T // GRID_W
    rows = jnp.repeat(jnp.arange(n_rows), GRID_W)
    cols = jnp.tile(jnp.arange(GRID_W), n_rows)
    x = x_sample
    for l in range(DEPTH):
        shift, scale, gate = _modulation(c, w_mod[l], b_mod[l])
        shift, scale, gate = shift[:, None, :], scale[:, None, :], gate[:, None, :]
        q, k, v, ga, xr, gr, mga, mgr = _project(x, shift, scale, w_in[l])
        q = _rope_2d(q.reshape(Bd, T, N_HEADS, HEAD_DIM), rows, cols)
        k = _rope_2d(k.reshape(Bd, T, N_KV_HEADS, HEAD_DIM), rows, cols)
        v = v.reshape(Bd, T, N_KV_HEADS, HEAD_DIM)
        o_attn = _latent_attention(q, k, v, cache_k[:, l], cache_v[:, l], attn_sink[l])
        x, _ = _rnn_merge_residual(x, gate, o_attn, xr, ga, gr, mga, mgr, state_h[:, l],
                                   conv_w[l], conv_b[l], rg_wa[l], rg_ba[l], rg_wx[l],
                                   rg_bx[l], rg_lambda[l], w_br_attn[l], w_br_rnn[l],
                                   w_out[l], ln_g[l], ln_b[l])
    y_sample = x
    return (y_prompt, y_sample, new_cache_k, new_cache_v, new_state_h)
```

```python
import functools

import jax
import jax.numpy as jnp
from jax import lax
from jax.experimental import pallas as pl
from jax.experimental.pallas import tpu as pltpu

D_MODEL = 1024
DEPTH = 2
GRID_W = 64
N_HEADS = 8
N_KV_HEADS = 2
HEAD_DIM = 128
Q_GROUP = N_HEADS // N_KV_HEADS
D_ATTN = N_HEADS * HEAD_DIM
D_KV = N_KV_HEADS * HEAD_DIM
WINDOW = 128
Q_BLOCK = 128
D_RNN = D_MODEL
N_RNN_BLOCKS = 8
RNN_BLOCK = D_RNN // N_RNN_BLOCKS
CONV_W = 4
CONV_LEFT = 2
LRU_C = 8.0
ROPE_BASE = 10000.0
D_IN = 2 * D_ATTN + 2 * D_KV + 2 * D_RNN + 2 * D_MODEL
DEEPNORM_ALPHA = (2 * DEPTH) ** 0.25
LN_EPS = 1e-6
NEG_INF = -1e30
ATTN_SCALE = HEAD_DIM ** -0.5

LANES = 128
COND_ROWS = 8
PROJ_TN = 512
N_PROJ_BLOCKS = D_IN // PROJ_TN
D_REST = D_IN - D_ATTN - 2 * D_KV
VMEM_LIMIT = 48 * 1024 * 1024

_BF16 = jnp.bfloat16
_F32 = jnp.float32


def _silu(x):
    return x * jax.nn.sigmoid(x)


def _layer_norm_rows(x):
    mu = jnp.mean(x, axis=-1, keepdims=True)
    xc = x - mu
    var = jnp.mean(xc * xc, axis=-1, keepdims=True)
    return xc * lax.rsqrt(var + LN_EPS)


def _mod_kernel(cond_ref, w_ref, b_ref, o_ref):
    a = _silu(cond_ref[...]).astype(_BF16)
    o_ref[...] = jnp.dot(a, w_ref[...].astype(_BF16), preferred_element_type=_F32) + b_ref[...]


def _modulation(cond8, w_mod, b_mod):
    tn = D_MODEL
    return pl.pallas_call(
        _mod_kernel,
        grid=(DEPTH, 3 * D_MODEL // tn),
        in_specs=[
            pl.BlockSpec((COND_ROWS, D_MODEL), lambda l, j: (0, 0)),
            pl.BlockSpec((None, D_MODEL, tn), lambda l, j: (l, 0, j)),
            pl.BlockSpec((None, 1, tn), lambda l, j: (l, 0, j)),
        ],
        out_specs=pl.BlockSpec((None, COND_ROWS, tn), lambda l, j: (l, 0, j)),
        out_shape=jax.ShapeDtypeStruct((DEPTH, COND_ROWS, 3 * D_MODEL), _F32),
        compiler_params=pltpu.CompilerParams(dimension_semantics=("parallel", "parallel")),
        name="modulation",
    )(cond8, w_mod, b_mod.reshape(DEPTH, 1, 3 * D_MODEL))


def _rope(x, cos, sin_signed):
    lane = lax.broadcasted_iota(jnp.int32, x.shape, 1)
    partner = jnp.where((lane // 32) % 2 == 0,
                        pltpu.roll(x, 3 * 32, axis=1),
                        pltpu.roll(x, 32, axis=1))
    return x * cos + partner * sin_signed


def _proj_kernel(x_ref, shift_ref, scale_ref, w_ref, *rest, rotary):
    if rotary:
        cos_ref, sin_ref, q_ref, k_ref, v_ref, r_ref, h_scr = rest
    else:
        q_ref, k_ref, v_ref, r_ref, h_scr = rest
    j = pl.program_id(1)

    @pl.when(j == 0)
    def _():
        y = _layer_norm_rows(x_ref[...])
        h_scr[...] = (y * (1.0 + scale_ref[...]) + shift_ref[...]).astype(_BF16)

    acc = jnp.dot(h_scr[...], w_ref[...], preferred_element_type=_F32)

    def maybe_rope(blk):
        return _rope(blk, cos_ref[...], sin_ref[...]) if rotary else blk

    @pl.when(j < D_ATTN // PROJ_TN)
    def _():
        for hh in range(PROJ_TN // HEAD_DIM):
            sl = slice(hh * HEAD_DIM, (hh + 1) * HEAD_DIM)
            q_ref[:, sl] = maybe_rope(acc[:, sl]).astype(q_ref.dtype)

    @pl.when(j == D_ATTN // PROJ_TN)
    def _():
        for hh in range(N_KV_HEADS):
            sl = slice(hh * HEAD_DIM, (hh + 1) * HEAD_DIM)
            k_ref[:, sl] = maybe_rope(acc[:, sl]).astype(k_ref.dtype)
        v_ref[...] = acc[:, D_KV:].astype(v_ref.dtype)

    @pl.when(j > D_ATTN // PROJ_TN)
    def _():
        r_ref[...] = acc


def _project(x2, mod3, w_in_bf, cond_row_of_block, tm, rope_tabs, kv_dtype):
    n = x2.shape[0]
    rotary = rope_tabs is not None
    nq = D_ATTN // PROJ_TN
    in_specs = [
        pl.BlockSpec((tm, D_MODEL), lambda i, j: (i, 0)),
        pl.BlockSpec((None, 1, D_MODEL), lambda i, j: (cond_row_of_block(i), 0, 0)),
        pl.BlockSpec((None, 1, D_MODEL), lambda i, j: (cond_row_of_block(i), 0, 1)),
        pl.BlockSpec((D_MODEL, PROJ_TN), lambda i, j: (0, j)),
    ]
    args = [x2, mod3, mod3, w_in_bf]
    if rotary:
        cos, sin = rope_tabs
        blocks_per_seq = cos.shape[0] // tm
        in_specs += [pl.BlockSpec((tm, HEAD_DIM), lambda i, j: (i % blocks_per_seq, 0))] * 2
        args += [cos, sin]
    return pl.pallas_call(
        functools.partial(_proj_kernel, rotary=rotary),
        grid=(n // tm, N_PROJ_BLOCKS),
        in_specs=in_specs,
        out_specs=[
            pl.BlockSpec((tm, PROJ_TN), lambda i, j: (i, jnp.minimum(j, nq - 1))),
            pl.BlockSpec((tm, D_KV), lambda i, j: (i, 0)),
            pl.BlockSpec((tm, D_KV), lambda i, j: (i, 0)),
            pl.BlockSpec((tm, PROJ_TN), lambda i, j: (i, jnp.maximum(j - nq - 1, 0))),
        ],
        out_shape=[
            jax.ShapeDtypeStruct((n, D_ATTN), _BF16),
            jax.ShapeDtypeStruct((n, D_KV), kv_dtype),
            jax.ShapeDtypeStruct((n, D_KV), kv_dtype),
            jax.ShapeDtypeStruct((n, D_REST), _F32),
        ],
        scratch_shapes=[pltpu.VMEM((tm, D_MODEL), _BF16)],
        compiler_params=pltpu.CompilerParams(
            dimension_semantics=("parallel", "arbitrary"), vmem_limit_bytes=VMEM_LIMIT),
        name="proj_rope" if rotary else "proj",
    )(*args)


def _softmax_pv(s, sink_col, v_bf):
    m = jnp.maximum(jnp.max(s, axis=-1, keepdims=True), sink_col)
    e = jnp.exp(s - m)
    denom = jnp.sum(e, axis=-1, keepdims=True) + jnp.exp(sink_col - m)
    o = jnp.dot(e.astype(_BF16), v_bf, preferred_element_type=_F32)
    return o / denom


def _sink_column(sink_ref, kvh, rows):
    head = lax.broadcasted_iota(jnp.int32, (Q_GROUP * rows, 1), 0) // rows
    col = jnp.full((Q_GROUP * rows, 1), sink_ref[kvh * Q_GROUP], _F32)
    for g in range(1, Q_GROUP):
        col = jnp.where(head == g, sink_ref[kvh * Q_GROUP + g], col)
    return col


def _stack_heads(q_rows):
    return jnp.concatenate(
        [q_rows[:, g * HEAD_DIM:(g + 1) * HEAD_DIM] for g in range(Q_GROUP)], axis=0)


def _ctx_attn_kernel(sink_ref, q_ref, k_ref, v_ref, o_ref):
    kvh = pl.program_id(1)
    s_len = q_ref.shape[0]
    q = _stack_heads(q_ref[...])
    k = k_ref[...].astype(_BF16)
    v = v_ref[...].astype(_BF16)
    s = lax.dot_general(q, k, (((1,), (1,)), ((), ())), preferred_element_type=_F32) * ATTN_SCALE
    o = _softmax_pv(s, _sink_column(sink_ref, kvh, s_len), v)
    for g in range(Q_GROUP):
        o_ref[:, g * HEAD_DIM:(g + 1) * HEAD_DIM] = o[g * s_len:(g + 1) * s_len]


def _context_attention(q, k, v, sink, batch, s_len):
    n = batch * s_len
    gw = Q_GROUP * HEAD_DIM
    return pl.pallas_call(
        _ctx_attn_kernel,
        grid=(batch, N_KV_HEADS),
        in_specs=[
            pl.BlockSpec(memory_space=pltpu.SMEM),
            pl.BlockSpec((s_len, gw), lambda b, h: (b, h)),
            pl.BlockSpec((s_len, HEAD_DIM), lambda b, h: (b, h)),
            pl.BlockSpec((s_len, HEAD_DIM), lambda b, h: (b, h)),
        ],
        out_specs=pl.BlockSpec((s_len, gw), lambda b, h: (b, h)),
        out_shape=jax.ShapeDtypeStruct((n, D_ATTN), _F32),
        compiler_params=pltpu.CompilerParams(dimension_semantics=("parallel", "parallel")),
        name="ctx_attention",
    )(sink, q, k, v)


def _lat_attn_kernel(sink_ref, q_ref, k_ref, v_ref, ck_ref, cv_ref, o_ref):
    kvh = pl.program_id(1)
    t_len = q_ref.shape[0]
    nb = t_len // Q_BLOCK
    ck = ck_ref[...].astype(_BF16)
    cv = cv_ref[...].astype(_BF16)
    sink_col = _sink_column(sink_ref, kvh, Q_BLOCK)
    rows = Q_GROUP * Q_BLOCK
    r = lax.broadcasted_iota(jnp.int32, (rows, Q_BLOCK), 0) % Q_BLOCK
    c = lax.broadcasted_iota(jnp.int32, (rows, Q_BLOCK), 1)
    for jb in range(nb):
        r0 = jb * Q_BLOCK
        lo = max(r0 - WINDOW, 0)
        hi = min(r0 + Q_BLOCK + WINDOW, t_len)
        q = _stack_heads(q_ref[r0:r0 + Q_BLOCK, :])
        keys = jnp.concatenate([k_ref[lo:hi, :], ck], axis=0)
        vals = jnp.concatenate([v_ref[lo:hi, :], cv], axis=0)
        s = lax.dot_general(q, keys, (((1,), (1,)), ((), ())), preferred_element_type=_F32) * ATTN_SCALE
        parts = []
        off = 0
        if jb > 0:
            parts.append(jnp.where(c >= r, s[:, :Q_BLOCK], NEG_INF))
            off = Q_BLOCK
        parts.append(s[:, off:off + Q_BLOCK])
        off += Q_BLOCK
        if jb < nb - 1:
            parts.append(jnp.where(c <= r, s[:, off:off + Q_BLOCK], NEG_INF))
            off += Q_BLOCK
        parts.append(s[:, off:])
        o = _softmax_pv(jnp.concatenate(parts, axis=1), sink_col, vals)
        for g in range(Q_GROUP):
            o_ref[r0:r0 + Q_BLOCK, g * HEAD_DIM:(g + 1) * HEAD_DIM] = o[g * Q_BLOCK:(g + 1) * Q_BLOCK]


def _latent_attention(q, k, v, cache_k4, cache_v4, layer, sink, batch, t_len):
    n = batch * t_len
    gw = Q_GROUP * HEAD_DIM
    past = cache_k4.shape[2]
    cache_spec = pl.BlockSpec((None, None, past, HEAD_DIM), lambda b, h: (b, layer, 0, h))
    return pl.pallas_call(
        _lat_attn_kernel,
        grid=(batch, N_KV_HEADS),
        in_specs=[
            pl.BlockSpec(memory_space=pltpu.SMEM),
            pl.BlockSpec((t_len, gw), lambda b, h: (b, h)),
            pl.BlockSpec((t_len, HEAD_DIM), lambda b, h: (b, h)),
            pl.BlockSpec((t_len, HEAD_DIM), lambda b, h: (b, h)),
            cache_spec,
            cache_spec,
        ],
        out_specs=pl.BlockSpec((t_len, gw), lambda b, h: (b, h)),
        out_shape=jax.ShapeDtypeStruct((n, D_ATTN), _F32),
        compiler_params=pltpu.CompilerParams(
            dimension_semantics=("parallel", "parallel"), vmem_limit_bytes=VMEM_LIMIT),
        name="latent_attention",
    )(sink, q, k, v, cache_k4, cache_v4)


def _shift_rows(x, d, fill):
    t_len = x.shape[0]
    row = lax.broadcasted_iota(jnp.int32, x.shape, 0)
    rolled = pltpu.roll(x, d % t_len, axis=0)
    valid = (row >= d) if d > 0 else (row < t_len + d)
    return jnp.where(valid, rolled, fill)


def _linear_scan(a, b, reverse):
    t_len = a.shape[0]
    d = 1
    while d < t_len:
        s = -d if reverse else d
        b = a * _shift_rows(b, s, 0.0) + b
        if 2 * d < t_len:
            a = a * _shift_rows(a, s, 1.0)
        d *= 2
    return b


def _rnn_kernel(x_ref, cw_ref, cb_ref, wg_ref, bg_ref, lam_ref, h0_ref, y_ref, hl_ref):
    x = x_ref[...]
    t_len = x.shape[0]
    cw = cw_ref[...]
    xc = cw[0:1] * _shift_rows(x, 2, 0.0)
    xc = xc + cw[1:2] * _shift_rows(x, 1, 0.0)
    xc = xc + cw[2:3] * x
    xc = xc + cw[3:4] * _shift_rows(x, -1, 0.0)
    xc = xc + cb_ref[...]
    gates = jnp.dot(xc.astype(_BF16), wg_ref[...], preferred_element_type=_F32) + bg_ref[...]
    lam = lam_ref[...]
    neg_c_softplus = -LRU_C * (jnp.maximum(-lam, 0.0) + jnp.log1p(jnp.exp(-jnp.abs(lam))))
    h0 = h0_ref[...]
    y = None
    for d in range(2):
        reverse = d == 1
        r = jax.nn.sigmoid(gates[:, (2 * d) * LANES:(2 * d + 1) * LANES])
        i = jax.nn.sigmoid(gates[:, (2 * d + 1) * LANES:(2 * d + 2) * LANES])
        log_a = r * neg_c_softplus[d:d + 1]
        a = jnp.exp(log_a)
        b = jnp.sqrt(-jnp.tanh(log_a) * (a * a + 1.0)) * (i * xc)
        row = lax.broadcasted_iota(jnp.int32, b.shape, 0)
        first = t_len - 1 if reverse else 0
        b = jnp.where(row == first, b + a * h0[d:d + 1], b)
        h = _linear_scan(a, b, reverse)
        last = 0 if reverse else t_len - 1
        hl_ref[d:d + 1, :] = h[last:last + 1, :]
        y = h if y is None else y + h
    y_ref[...] = y


def _rnn(rest, h0, conv_w, conv_b, wg, bg, lam, batch, t_len):
    n = batch * t_len
    xr_block0 = D_MODEL // LANES
    return pl.pallas_call(
        _rnn_kernel,
        grid=(batch, N_RNN_BLOCKS),
        in_specs=[
            pl.BlockSpec((t_len, LANES), lambda b, c: (b, xr_block0 + c)),
            pl.BlockSpec((CONV_W, LANES), lambda b, c: (0, c)),
            pl.BlockSpec((1, LANES), lambda b, c: (0, c)),
            pl.BlockSpec((None, RNN_BLOCK, 4 * LANES), lambda b, c: (c, 0, 0)),
            pl.BlockSpec((None, 1, 4 * LANES), lambda b, c: (c, 0, 0)),
            pl.BlockSpec((2, LANES), lambda b, c: (0, c)),
            pl.BlockSpec((None, 2, LANES), lambda b, c: (b, 0, c)),
        ],
        out_specs=[
            pl.BlockSpec((t_len, LANES), lambda b, c: (b, c)),
            pl.BlockSpec((None, 2, LANES), lambda b, c: (b, 0, c)),
        ],
        out_shape=[
            jax.ShapeDtypeStruct((n, D_RNN), _F32),
            jax.ShapeDtypeStruct((batch, 2, D_RNN), _F32),
        ],
        compiler_params=pltpu.CompilerParams(
            dimension_semantics=("parallel", "parallel"), vmem_limit_bytes=VMEM_LIMIT),
        name="rnn",
    )(rest, conv_w, conv_b.reshape(1, D_RNN), wg, bg, lam, h0)


def _out_kernel(x_ref, gate_ref, oa_ref, ga_ref, or_ref, gr_ref, mga_ref, mgr_ref,
                wa_ref, wr_ref, wo_ref, g_ref, b_ref, o_ref):
    ya = jnp.dot((oa_ref[...] * _silu(ga_ref[...])).astype(_BF16), wa_ref[...],
                 preferred_element_type=_F32)
    yr = jnp.dot((or_ref[...] * _silu(gr_ref[...])).astype(_BF16), wr_ref[...],
                 preferred_element_type=_F32)
    merged = jax.nn.sigmoid(mga_ref[...]) * ya + jax.nn.sigmoid(mgr_ref[...]) * yr
    out = jnp.dot(merged.astype(_BF16), wo_ref[...], preferred_element_type=_F32)
    z = DEEPNORM_ALPHA * x_ref[...] + gate_ref[...] * out
    o_ref[...] = _layer_norm_rows(z) * g_ref[...] + b_ref[...]


def _merge_residual(x2, mod3, o_attn, o_rnn, rest, wa_bf, wr_bf, wo_bf, ln_g, ln_b,
                    cond_row_of_block, tm):
    n = x2.shape[0]
    row_blk = lambda col: pl.BlockSpec((tm, D_MODEL), lambda i, col=col: (i, col))
    whole = pl.BlockSpec((D_MODEL, D_MODEL), lambda i: (0, 0))
    vec = pl.BlockSpec((1, D_MODEL), lambda i: (0, 0))
    return pl.pallas_call(
        _out_kernel,
        grid=(n // tm,),
        in_specs=[
            row_blk(0),
            pl.BlockSpec((None, 1, D_MODEL), lambda i: (cond_row_of_block(i), 0, 2)),
            row_blk(0), row_blk(0), row_blk(0), row_blk(2), row_blk(3), row_blk(4),
            whole, whole, whole, vec, vec,
        ],
        out_specs=row_blk(0),
        out_shape=jax.ShapeDtypeStruct((n, D_MODEL), _F32),
        compiler_params=pltpu.CompilerParams(
            dimension_semantics=("parallel",), vmem_limit_bytes=VMEM_LIMIT),
        name="merge_residual",
    )(x2, mod3, o_attn, rest, o_rnn, rest, rest, rest, wa_bf, wr_bf, wo_bf,
      ln_g.reshape(1, D_MODEL), ln_b.reshape(1, D_MODEL))


def _rope_tables(t_len):
    quarter = HEAD_DIM // 4
    inv_freq = ROPE_BASE ** (-jnp.arange(quarter, dtype=_F32) / quarter)
    pos = jnp.arange(t_len)
    ang_r = (pos // GRID_W).astype(_F32)[:, None] * inv_freq[None, :]
    ang_c = (pos % GRID_W).astype(_F32)[:, None] * inv_freq[None, :]
    cos = jnp.concatenate([jnp.cos(ang_r)] * 2 + [jnp.cos(ang_c)] * 2, axis=1)
    sin = jnp.concatenate([-jnp.sin(ang_r), jnp.sin(ang_r), -jnp.sin(ang_c), jnp.sin(ang_c)], axis=1)
    return cos, sin


def _gate_weights(rg_wa, rg_ba, rg_wx, rg_bx):
    wg = jnp.concatenate([rg_wa[0], rg_wx[0], rg_wa[1], rg_wx[1]], axis=-1).astype(_BF16)
    blk = lambda v: v.reshape(N_RNN_BLOCKS, 1, RNN_BLOCK)
    bg = jnp.concatenate([blk(rg_ba[0]), blk(rg_bx[0]), blk(rg_ba[1]), blk(rg_bx[1])], axis=-1)
    return wg, bg


def kernel(x_prompt, x_sample, cache_k, cache_v, state_h, c, c_ctx, w_mod, b_mod, w_in,
           attn_sink, conv_w, conv_b, rg_wa, rg_ba, rg_wx, rg_bx, rg_lambda,
           w_br_attn, w_br_rnn, w_out, ln_g, ln_b):
    bp, s_len, _ = x_prompt.shape
    bd, t_len, _ = x_sample.shape
    past = cache_k.shape[2]

    cond8 = jnp.zeros((COND_ROWS, D_MODEL), _F32).at[0].set(c_ctx).at[1:1 + bd].set(c)
    mod = _modulation(cond8, w_mod, b_mod)

    w_in_bf = w_in.astype(_BF16)
    wa_bf = w_br_attn.astype(_BF16)
    wr_bf = w_br_rnn.astype(_BF16)
    wo_bf = w_out.astype(_BF16)
    rope_tabs = _rope_tables(t_len)
    cache_k4 = cache_k.reshape(bd, DEPTH, past, D_KV)
    cache_v4 = cache_v.reshape(bd, DEPTH, past, D_KV)

    tm_proj, tm_out = 1024, 256
    ctx_row = lambda i: 0
    lat_row_proj = lambda i: 1 + (i * tm_proj) // t_len
    lat_row_out = lambda i: 1 + (i * tm_out) // t_len

    xp = x_prompt.reshape(bp * s_len, D_MODEL)
    xs = x_sample.reshape(bd * t_len, D_MODEL)
    h0_zero = jnp.zeros((bp, 2, D_RNN), _F32)
    ks, vs, hs = [], [], []
    for l in range(DEPTH):
        mod3 = mod[l].reshape(COND_ROWS, 1, 3 * D_MODEL)
        wg, bg = _gate_weights(rg_wa[l], rg_ba[l], rg_wx[l], rg_bx[l])

        q, k, v, rest = _project(xp, mod3, w_in_bf[l], ctx_row, tm_proj, None, _F32)
        o_attn = _context_attention(q, k, v, attn_sink[l], bp, s_len)
        o_rnn, h_fin = _rnn(rest, h0_zero, conv_w[l], conv_b[l], wg, bg, rg_lambda[l], bp, s_len)
        xp = _merge_residual(xp, mod3, o_attn, o_rnn, rest, wa_bf[l], wr_bf[l], wo_bf[l],
                             ln_g[l], ln_b[l], ctx_row, tm_out)
        ks.append(k.reshape(bp, s_len, N_KV_HEADS, HEAD_DIM))
        vs.append(v.reshape(bp, s_len, N_KV_HEADS, HEAD_DIM))
        hs.append(h_fin)

        q, k, v, rest = _project(xs, mod3, w_in_bf[l], lat_row_proj, tm_proj, rope_tabs, _BF16)
        o_attn = _latent_attention(q, k, v, cache_k4, cache_v4, l, attn_sink[l], bd, t_len)
        o_rnn, _ = _rnn(rest, state_h[:, l], conv_w[l], conv_b[l], wg, bg, rg_lambda[l], bd, t_len)
        xs = _merge_residual(xs, mod3, o_attn, o_rnn, rest, wa_bf[l], wr_bf[l], wo_bf[l],
                             ln_g[l], ln_b[l], lat_row_out, tm_out)

    y_prompt = xp.reshape(bp, s_len, D_MODEL)
    y_sample = xs.reshape(bd, t_len, D_MODEL)
    return (y_prompt, y_sample, jnp.stack(ks, axis=1), jnp.stack(vs, axis=1), jnp.stack(hs, axis=1))
```

```python
import functools

import jax
import jax.numpy as jnp
from jax import lax
from jax.experimental import pallas as pl
from jax.experimental.pallas import tpu as pltpu

D_MODEL = 1024
DEPTH = 2
GRID_W = 64
N_HEADS = 8
N_KV_HEADS = 2
HEAD_DIM = 128
Q_GROUP = N_HEADS // N_KV_HEADS
D_ATTN = N_HEADS * HEAD_DIM
D_KV = N_KV_HEADS * HEAD_DIM
WINDOW = 128
Q_BLOCK = 128
D_RNN = D_MODEL
N_RNN_BLOCKS = 8
RNN_BLOCK = D_RNN // N_RNN_BLOCKS
CONV_W = 4
CONV_LEFT = 2
LRU_C = 8.0
ROPE_BASE = 10000.0
D_IN = 2 * D_ATTN + 2 * D_KV + 2 * D_RNN + 2 * D_MODEL
DEEPNORM_ALPHA = (2 * DEPTH) ** 0.25
LN_EPS = 1e-6
NEG_INF = -1e30
ATTN_SCALE = HEAD_DIM ** -0.5

LANES = 128
SUBLANES = 8
SEQ_SLOTS = SUBLANES
SLOT_PAD = SUBLANES
COND_ROWS = 8
PROJ_TN = 512
N_PROJ_BLOCKS = D_IN // PROJ_TN
D_REST = D_IN - D_ATTN - 2 * D_KV
VMEM_LIMIT = 48 * 1024 * 1024
RNN_VMEM_LIMIT = 56 * 1024 * 1024
F32_TINY = 1.1754943508222875e-38
LOG2_E = 1.4426950408889634

_BF16 = jnp.bfloat16
_F32 = jnp.float32


def _silu(x):
    return x * jax.nn.sigmoid(x)


def _layer_norm_rows(x):
    mu = jnp.mean(x, axis=-1, keepdims=True)
    xc = x - mu
    var = jnp.mean(xc * xc, axis=-1, keepdims=True)
    return xc * lax.rsqrt(var + LN_EPS)


def _mod_kernel(cond_ref, w_ref, b_ref, o_ref):
    a = _silu(cond_ref[...]).astype(_BF16)
    o_ref[...] = jnp.dot(a, w_ref[...].astype(_BF16), preferred_element_type=_F32) + b_ref[...]


def _modulation(cond8, w_mod, b_mod):
    tn = D_MODEL
    return pl.pallas_call(
        _mod_kernel,
        grid=(DEPTH, 3 * D_MODEL // tn),
        in_specs=[
            pl.BlockSpec((COND_ROWS, D_MODEL), lambda l, j: (0, 0)),
            pl.BlockSpec((None, D_MODEL, tn), lambda l, j: (l, 0, j)),
            pl.BlockSpec((None, 1, tn), lambda l, j: (l, 0, j)),
        ],
        out_specs=pl.BlockSpec((None, COND_ROWS, tn), lambda l, j: (l, 0, j)),
        out_shape=jax.ShapeDtypeStruct((DEPTH, COND_ROWS, 3 * D_MODEL), _F32),
        compiler_params=pltpu.CompilerParams(dimension_semantics=("parallel", "parallel")),
        name="modulation",
    )(cond8, w_mod, b_mod.reshape(DEPTH, 1, 3 * D_MODEL))


def _rope(x, cos, sin_signed):
    lane = lax.broadcasted_iota(jnp.int32, x.shape, 1)
    partner = jnp.where((lane // 32) % 2 == 0,
                        pltpu.roll(x, 3 * 32, axis=1),
                        pltpu.roll(x, 32, axis=1))
    return x * cos + partner * sin_signed


def _proj_kernel(x_ref, shift_ref, scale_ref, w_ref, *rest, rotary):
    if rotary:
        cos_ref, sin_ref, q_ref, k_ref, v_ref, r_ref, h_scr = rest
    else:
        q_ref, k_ref, v_ref, r_ref, h_scr = rest
    j = pl.program_id(1)

    @pl.when(j == 0)
    def _():
        y = _layer_norm_rows(x_ref[...])
        h_scr[...] = (y * (1.0 + scale_ref[...]) + shift_ref[...]).astype(_BF16)

    acc = jnp.dot(h_scr[...], w_ref[...], preferred_element_type=_F32)

    def maybe_rope(blk):
        return _rope(blk, cos_ref[...], sin_ref[...]) if rotary else blk

    @pl.when(j < D_ATTN // PROJ_TN)
    def _():
        for hh in range(PROJ_TN // HEAD_DIM):
            sl = slice(hh * HEAD_DIM, (hh + 1) * HEAD_DIM)
            q_ref[:, sl] = maybe_rope(acc[:, sl]).astype(q_ref.dtype)

    @pl.when(j == D_ATTN // PROJ_TN)
    def _():
        for hh in range(N_KV_HEADS):
            sl = slice(hh * HEAD_DIM, (hh + 1) * HEAD_DIM)
            k_ref[:, sl] = maybe_rope(acc[:, sl]).astype(k_ref.dtype)
        v_ref[...] = acc[:, D_KV:].astype(v_ref.dtype)

    @pl.when(j > D_ATTN // PROJ_TN)
    def _():
        r_ref[...] = acc


def _project(x2, mod3, w_in_bf, cond_row_of_block, tm, rope_tabs, kv_dtype):
    n = x2.shape[0]
    rotary = rope_tabs is not None
    nq = D_ATTN // PROJ_TN
    in_specs = [
        pl.BlockSpec((tm, D_MODEL), lambda i, j: (i, 0)),
        pl.BlockSpec((None, 1, D_MODEL), lambda i, j: (cond_row_of_block(i), 0, 0)),
        pl.BlockSpec((None, 1, D_MODEL), lambda i, j: (cond_row_of_block(i), 0, 1)),
        pl.BlockSpec((D_MODEL, PROJ_TN), lambda i, j: (0, j)),
    ]
    args = [x2, mod3, mod3, w_in_bf]
    if rotary:
        cos, sin = rope_tabs
        blocks_per_seq = cos.shape[0] // tm
        in_specs += [pl.BlockSpec((tm, HEAD_DIM), lambda i, j: (i % blocks_per_seq, 0))] * 2
        args += [cos, sin]
    return pl.pallas_call(
        functools.partial(_proj_kernel, rotary=rotary),
        grid=(n // tm, N_PROJ_BLOCKS),
        in_specs=in_specs,
        out_specs=[
            pl.BlockSpec((tm, PROJ_TN), lambda i, j: (i, jnp.minimum(j, nq - 1))),
            pl.BlockSpec((tm, D_KV), lambda i, j: (i, 0)),
            pl.BlockSpec((tm, D_KV), lambda i, j: (i, 0)),
            pl.BlockSpec((tm, PROJ_TN), lambda i, j: (i, jnp.maximum(j - nq - 1, 0))),
        ],
        out_shape=[
            jax.ShapeDtypeStruct((n, D_ATTN), _BF16),
            jax.ShapeDtypeStruct((n, D_KV), kv_dtype),
            jax.ShapeDtypeStruct((n, D_KV), kv_dtype),
            jax.ShapeDtypeStruct((n, D_REST), _F32),
        ],
        scratch_shapes=[pltpu.VMEM((tm, D_MODEL), _BF16)],
        compiler_params=pltpu.CompilerParams(
            dimension_semantics=("parallel", "arbitrary"), vmem_limit_bytes=VMEM_LIMIT),
        name="proj_rope" if rotary else "proj",
    )(*args)


def _softmax_pv(s, sink_col, v_bf):
    m = jnp.maximum(jnp.max(s, axis=-1, keepdims=True), sink_col)
    e = jnp.exp(s - m)
    denom = jnp.sum(e, axis=-1, keepdims=True) + jnp.exp(sink_col - m)
    o = jnp.dot(e.astype(_BF16), v_bf, preferred_element_type=_F32)
    return o / denom


def _sink_column(sink_ref, kvh, rows):
    head = lax.broadcasted_iota(jnp.int32, (Q_GROUP * rows, 1), 0) // rows
    col = jnp.full((Q_GROUP * rows, 1), sink_ref[kvh * Q_GROUP], _F32)
    for g in range(1, Q_GROUP):
        col = jnp.where(head == g, sink_ref[kvh * Q_GROUP + g], col)
    return col


def _stack_heads(q_rows):
    return jnp.concatenate(
        [q_rows[:, g * HEAD_DIM:(g + 1) * HEAD_DIM] for g in range(Q_GROUP)], axis=0)


def _ctx_attn_kernel(sink_ref, q_ref, k_ref, v_ref, o_ref):
    kvh = pl.program_id(1)
    s_len = q_ref.shape[0]
    q = _stack_heads(q_ref[...])
    k = k_ref[...].astype(_BF16)
    v = v_ref[...].astype(_BF16)
    s = lax.dot_general(q, k, (((1,), (1,)), ((), ())), preferred_element_type=_F32) * ATTN_SCALE
    o = _softmax_pv(s, _sink_column(sink_ref, kvh, s_len), v)
    for g in range(Q_GROUP):
        o_ref[:, g * HEAD_DIM:(g + 1) * HEAD_DIM] = o[g * s_len:(g + 1) * s_len]


def _context_attention(q, k, v, sink, batch, s_len):
    n = batch * s_len
    gw = Q_GROUP * HEAD_DIM
    return pl.pallas_call(
        _ctx_attn_kernel,
        grid=(batch, N_KV_HEADS),
        in_specs=[
            pl.BlockSpec(memory_space=pltpu.SMEM),
            pl.BlockSpec((s_len, gw), lambda b, h: (b, h)),
            pl.BlockSpec((s_len, HEAD_DIM), lambda b, h: (b, h)),
            pl.BlockSpec((s_len, HEAD_DIM), lambda b, h: (b, h)),
        ],
        out_specs=pl.BlockSpec((s_len, gw), lambda b, h: (b, h)),
        out_shape=jax.ShapeDtypeStruct((n, D_ATTN), _F32),
        compiler_params=pltpu.CompilerParams(dimension_semantics=("parallel", "parallel")),
        name="ctx_attention",
    )(sink, q, k, v)


def _lat_attn_kernel(sink_ref, q_ref, k_ref, v_ref, ck_ref, cv_ref, o_ref):
    kvh = pl.program_id(1)
    t_len = q_ref.shape[0]
    nb = t_len // Q_BLOCK
    ck = ck_ref[...].astype(_BF16)
    cv = cv_ref[...].astype(_BF16)
    sink_col = _sink_column(sink_ref, kvh, Q_BLOCK)
    rows = Q_GROUP * Q_BLOCK
    r = lax.broadcasted_iota(jnp.int32, (rows, Q_BLOCK), 0) % Q_BLOCK
    c = lax.broadcasted_iota(jnp.int32, (rows, Q_BLOCK), 1)
    for jb in range(nb):
        r0 = jb * Q_BLOCK
        lo = max(r0 - WINDOW, 0)
        hi = min(r0 + Q_BLOCK + WINDOW, t_len)
        q = _stack_heads(q_ref[r0:r0 + Q_BLOCK, :])
        keys = jnp.concatenate([k_ref[lo:hi, :], ck], axis=0)
        vals = jnp.concatenate([v_ref[lo:hi, :], cv], axis=0)
        s = lax.dot_general(q, keys, (((1,), (1,)), ((), ())), preferred_element_type=_F32) * ATTN_SCALE
        parts = []
        off = 0
        if jb > 0:
            parts.append(jnp.where(c >= r, s[:, :Q_BLOCK], NEG_INF))
            off = Q_BLOCK
        parts.append(s[:, off:off + Q_BLOCK])
        off += Q_BLOCK
        if jb < nb - 1:
            parts.append(jnp.where(c <= r, s[:, off:off + Q_BLOCK], NEG_INF))
            off += Q_BLOCK
        parts.append(s[:, off:])
        o = _softmax_pv(jnp.concatenate(parts, axis=1), sink_col, vals)
        for g in range(Q_GROUP):
            o_ref[r0:r0 + Q_BLOCK, g * HEAD_DIM:(g + 1) * HEAD_DIM] = o[g * Q_BLOCK:(g + 1) * Q_BLOCK]


def _latent_attention(q, k, v, cache_k4, cache_v4, layer, sink, batch, t_len):
    n = batch * t_len
    gw = Q_GROUP * HEAD_DIM
    past = cache_k4.shape[2]
    cache_spec = pl.BlockSpec((None, None, past, HEAD_DIM), lambda b, h: (b, layer, 0, h))
    return pl.pallas_call(
        _lat_attn_kernel,
        grid=(batch, N_KV_HEADS),
        in_specs=[
            pl.BlockSpec(memory_space=pltpu.SMEM),
            pl.BlockSpec((t_len, gw), lambda b, h: (b, h)),
            pl.BlockSpec((t_len, HEAD_DIM), lambda b, h: (b, h)),
            pl.BlockSpec((t_len, HEAD_DIM), lambda b, h: (b, h)),
            cache_spec,
            cache_spec,
        ],
        out_specs=pl.BlockSpec((t_len, gw), lambda b, h: (b, h)),
        out_shape=jax.ShapeDtypeStruct((n, D_ATTN), _F32),
        compiler_params=pltpu.CompilerParams(
            dimension_semantics=("parallel", "parallel"), vmem_limit_bytes=VMEM_LIMIT),
        name="latent_attention",
    )(sink, q, k, v, cache_k4, cache_v4)


def _shift_rows(x, d, fill):
    t_len = x.shape[0]
    row = lax.broadcasted_iota(jnp.int32, x.shape, 0)
    rolled = pltpu.roll(x, d % t_len, axis=0)
    valid = (row >= d) if d > 0 else (row < t_len + d)
    return jnp.where(valid, rolled, fill)


def _sqrt_nonneg(z):
    return z * lax.rsqrt(jnp.maximum(z, F32_TINY))


def _rnn_kernel(x_ref, cw_ref, cb_ref, wg_ref, bg_ref, lam_ref, h0_ref, y_ref, hl_ref,
                af_scr, bf_scr, ab_scr, bb_scr, hf_scr, hb_scr, *, nb, nc):
    t_len = x_ref.shape[1]
    pitch = t_len + SLOT_PAD
    scr = ((af_scr, bf_scr), (ab_scr, bb_scr))

    for cl in range(nc):
        lanes = slice(cl * LANES, (cl + 1) * LANES)
        cw = cw_ref[:, lanes]
        cb = cb_ref[:, lanes]
        wg = wg_ref[cl]
        bg = bg_ref[cl]
        lam = lam_ref[:, lanes]
        c_softplus = LRU_C * (jnp.maximum(-lam, 0.0) + jnp.log1p(jnp.exp(-jnp.abs(lam))))

        def per_batch(b, carry, lanes=lanes, cw=cw, cb=cb, wg=wg, bg=bg,
                      c_softplus=c_softplus, cl=cl):
            x = x_ref[b, :, lanes]
            xc = cw[0:1] * _shift_rows(x, 2, 0.0)
            xc = xc + cw[1:2] * _shift_rows(x, 1, 0.0)
            xc = xc + cw[2:3] * x
            xc = xc + cw[3:4] * _shift_rows(x, -1, 0.0)
            xc = xc + cb
            half_gates = jnp.dot(xc.astype(_BF16), wg, preferred_element_type=_F32) + bg
            half_xc = 0.5 * xc
            row0 = pl.multiple_of((cl * nb + b) * pitch, 8)
            for d in range(2):
                t_r = jnp.tanh(half_gates[:, (2 * d) * LANES:(2 * d + 1) * LANES])
                t_i = jnp.tanh(half_gates[:, (2 * d + 1) * LANES:(2 * d + 2) * LANES])
                k = 0.5 * c_softplus[d:d + 1]
                k2 = -LOG2_E * k
                neg_log_a = t_r * k + k
                a = jnp.exp2(t_r * k2 + k2)
                bterm = _sqrt_nonneg(jnp.tanh(neg_log_a) * (a * a + 1.0)) * ((t_i + 1.0) * half_xc)
                scr[d][0][pl.ds(row0, t_len), :] = a
                scr[d][1][pl.ds(row0, t_len), :] = bterm
            return carry

        lax.fori_loop(0, nb, per_batch, 0)

    def slot_rows(ref3, d):
        rows = [ref3[b, d:d + 1, cl * LANES:(cl + 1) * LANES] for cl in range(nc) for b in range(nb)]
        return jnp.concatenate(rows, axis=0)

    def step(t, h):
        hf, hb = h
        fwd = pl.ds(t, SEQ_SLOTS, stride=pitch)
        hf = af_scr[fwd, :] * hf + bf_scr[fwd, :]
        hf_scr[fwd, :] = hf
        bwd = pl.ds(t_len - 1 - t, SEQ_SLOTS, stride=pitch)
        hb = ab_scr[bwd, :] * hb + bb_scr[bwd, :]
        hb_scr[bwd, :] = hb
        return hf, hb

    hf, hb = lax.fori_loop(0, t_len, step, (slot_rows(h0_ref, 0), slot_rows(h0_ref, 1)), unroll=8)

    for cl in range(nc):
        lanes = slice(cl * LANES, (cl + 1) * LANES)
        for b in range(nb):
            s = cl * nb + b
            hl_ref[b, 0:1, lanes] = hf[s:s + 1, :]
            hl_ref[b, 1:2, lanes] = hb[s:s + 1, :]
            rows = pl.ds(s * pitch, t_len)
            y_ref[b, :, lanes] = hf_scr[rows, :] + hb_scr[rows, :]


def _rnn(rest, h0, conv_w, conv_b, wg, bg, lam, batch, t_len):
    nb = min(batch, SEQ_SLOTS)
    nc = SEQ_SLOTS // nb
    cw = nc * LANES
    xr_block0 = D_MODEL // cw
    scratch = pltpu.VMEM((SEQ_SLOTS * (t_len + SLOT_PAD), LANES), _F32)
    y, h_last = pl.pallas_call(
        functools.partial(_rnn_kernel, nb=nb, nc=nc),
        grid=(batch // nb, D_RNN // cw),
        in_specs=[
            pl.BlockSpec((nb, t_len, cw), lambda g, c: (g, 0, xr_block0 + c)),
            pl.BlockSpec((CONV_W, cw), lambda g, c: (0, c)),
            pl.BlockSpec((1, cw), lambda g, c: (0, c)),
            pl.BlockSpec((nc, RNN_BLOCK, 4 * LANES), lambda g, c: (c, 0, 0)),
            pl.BlockSpec((nc, 1, 4 * LANES), lambda g, c: (c, 0, 0)),
            pl.BlockSpec((2, cw), lambda g, c: (0, c)),
            pl.BlockSpec((nb, 2, cw), lambda g, c: (g, 0, c)),
        ],
        out_specs=[
            pl.BlockSpec((nb, t_len, cw), lambda g, c: (g, 0, c)),
            pl.BlockSpec((nb, 2, cw), lambda g, c: (g, 0, c)),
        ],
        out_shape=[
            jax.ShapeDtypeStruct((batch, t_len, D_RNN), _F32),
            jax.ShapeDtypeStruct((batch, 2, D_RNN), _F32),
        ],
        scratch_shapes=[scratch] * 6,
        compiler_params=pltpu.CompilerParams(
            dimension_semantics=("parallel", "parallel"), vmem_limit_bytes=RNN_VMEM_LIMIT),
        name="rnn",
    )(rest.reshape(batch, t_len, D_REST), conv_w, conv_b.reshape(1, D_RNN), wg, bg, lam, h0)
    return y.reshape(batch * t_len, D_RNN), h_last


def _out_kernel(x_ref, gate_ref, oa_ref, ga_ref, or_ref, gr_ref, mga_ref, mgr_ref,
                wa_ref, wr_ref, wo_ref, g_ref, b_ref, o_ref):
    ya = jnp.dot((oa_ref[...] * _silu(ga_ref[...])).astype(_BF16), wa_ref[...],
                 preferred_element_type=_F32)
    yr = jnp.dot((or_ref[...] * _silu(gr_ref[...])).astype(_BF16), wr_ref[...],
                 preferred_element_type=_F32)
    merged = jax.nn.sigmoid(mga_ref[...]) * ya + jax.nn.sigmoid(mgr_ref[...]) * yr
    out = jnp.dot(merged.astype(_BF16), wo_ref[...], preferred_element_type=_F32)
    z = DEEPNORM_ALPHA * x_ref[...] + gate_ref[...] * out
    o_ref[...] = _layer_norm_rows(z) * g_ref[...] + b_ref[...]


def _merge_residual(x2, mod3, o_attn, o_rnn, rest, wa_bf, wr_bf, wo_bf, ln_g, ln_b,
                    cond_row_of_block, tm):
    n = x2.shape[0]
    row_blk = lambda col: pl.BlockSpec((tm, D_MODEL), lambda i, col=col: (i, col))
    whole = pl.BlockSpec((D_MODEL, D_MODEL), lambda i: (0, 0))
    vec = pl.BlockSpec((1, D_MODEL), lambda i: (0, 0))
    return pl.pallas_call(
        _out_kernel,
        grid=(n // tm,),
        in_specs=[
            row_blk(0),
            pl.BlockSpec((None, 1, D_MODEL), lambda i: (cond_row_of_block(i), 0, 2)),
            row_blk(0), row_blk(0), row_blk(0), row_blk(2), row_blk(3), row_blk(4),
            whole, whole, whole, vec, vec,
        ],
        out_specs=row_blk(0),
        out_shape=jax.ShapeDtypeStruct((n, D_MODEL), _F32),
        compiler_params=pltpu.CompilerParams(
            dimension_semantics=("parallel",), vmem_limit_bytes=VMEM_LIMIT),
        name="merge_residual",
    )(x2, mod3, o_attn, rest, o_rnn, rest, rest, rest, wa_bf, wr_bf, wo_bf,
      ln_g.reshape(1, D_MODEL), ln_b.reshape(1, D_MODEL))


def _rope_tables(t_len):
    quarter = HEAD_DIM // 4
    inv_freq = ROPE_BASE ** (-jnp.arange(quarter, dtype=_F32) / quarter)
    pos = jnp.arange(t_len)
    ang_r = (pos // GRID_W).astype(_F32)[:, None] * inv_freq[None, :]
    ang_c = (pos % GRID_W).astype(_F32)[:, None] * inv_freq[None, :]
    cos = jnp.concatenate([jnp.cos(ang_r)] * 2 + [jnp.cos(ang_c)] * 2, axis=1)
    sin = jnp.concatenate([-jnp.sin(ang_r), jnp.sin(ang_r), -jnp.sin(ang_c), jnp.sin(ang_c)], axis=1)
    return cos, sin


def _gate_weights(rg_wa, rg_ba, rg_wx, rg_bx):
    wg = (0.5 * jnp.concatenate([rg_wa[0], rg_wx[0], rg_wa[1], rg_wx[1]], axis=-1)).astype(_BF16)
    blk = lambda v: v.reshape(N_RNN_BLOCKS, 1, RNN_BLOCK)
    bg = 0.5 * jnp.concatenate([blk(rg_ba[0]), blk(rg_bx[0]), blk(rg_ba[1]), blk(rg_bx[1])], axis=-1)
    return wg, bg


def kernel(x_prompt, x_sample, cache_k, cache_v, state_h, c, c_ctx, w_mod, b_mod, w_in,
           attn_sink, conv_w, conv_b, rg_wa, rg_ba, rg_wx, rg_bx, rg_lambda,
           w_br_attn, w_br_rnn, w_out, ln_g, ln_b):
    bp, s_len, _ = x_prompt.shape
    bd, t_len, _ = x_sample.shape
    past = cache_k.shape[2]

    cond8 = jnp.zeros((COND_ROWS, D_MODEL), _F32).at[0].set(c_ctx).at[1:1 + bd].set(c)
    mod = _modulation(cond8, w_mod, b_mod)

    w_in_bf = w_in.astype(_BF16)
    wa_bf = w_br_attn.astype(_BF16)
    wr_bf = w_br_rnn.astype(_BF16)
    wo_bf = w_out.astype(_BF16)
    rope_tabs = _rope_tables(t_len)
    cache_k4 = cache_k.reshape(bd, DEPTH, past, D_KV)
    cache_v4 = cache_v.reshape(bd, DEPTH, past, D_KV)

    tm_proj, tm_out = 1024, 256
    ctx_row = lambda i: 0
    lat_row_proj = lambda i: 1 + (i * tm_proj) // t_len
    lat_row_out = lambda i: 1 + (i * tm_out) // t_len

    xp = x_prompt.reshape(bp * s_len, D_MODEL)
    xs = x_sample.reshape(bd * t_len, D_MODEL)
    h0_zero = jnp.zeros((bp, 2, D_RNN), _F32)
    ks, vs, hs = [], [], []
    for l in range(DEPTH):
        mod3 = mod[l].reshape(COND_ROWS, 1, 3 * D_MODEL)
        wg, bg = _gate_weights(rg_wa[l], rg_ba[l], rg_wx[l], rg_bx[l])

        q, k, v, rest = _project(xp, mod3, w_in_bf[l], ctx_row, tm_proj, None, _F32)
        o_attn = _context_attention(q, k, v, attn_sink[l], bp, s_len)
        o_rnn, h_fin = _rnn(rest, h0_zero, conv_w[l], conv_b[l], wg, bg, rg_lambda[l], bp, s_len)
        xp = _merge_residual(xp, mod3, o_attn, o_rnn, rest, wa_bf[l], wr_bf[l], wo_bf[l],
                             ln_g[l], ln_b[l], ctx_row, tm_out)
        ks.append(k.reshape(bp, s_len, N_KV_HEADS, HEAD_DIM))
        vs.append(v.reshape(bp, s_len, N_KV_HEADS, HEAD_DIM))
        hs.append(h_fin)

        q, k, v, rest = _project(xs, mod3, w_in_bf[l], lat_row_proj, tm_proj, rope_tabs, _BF16)
        o_attn = _latent_attention(q, k, v, cache_k4, cache_v4, l, attn_sink[l], bd, t_len)
        o_rnn, _ = _rnn(rest, state_h[:, l], conv_w[l], conv_b[l], wg, bg, rg_lambda[l], bd, t_len)
        xs = _merge_residual(xs, mod3, o_attn, o_rnn, rest, wa_bf[l], wr_bf[l], wo_bf[l],
                             ln_g[l], ln_b[l], lat_row_out, tm_out)

    y_prompt = xp.reshape(bp, s_len, D_MODEL)
    y_sample = xs.reshape(bd, t_len, D_MODEL)
    return (y_prompt, y_sample, jnp.stack(ks, axis=1), jnp.stack(vs, axis=1), jnp.stack(hs, axis=1))
```

```python
import functools

import jax
import jax.numpy as jnp
from jax import lax
from jax.experimental import pallas as pl
from jax.experimental.pallas import tpu as pltpu

D_MODEL = 1024
DEPTH = 2
GRID_W = 64
N_HEADS = 8
N_KV_HEADS = 2
HEAD_DIM = 128
Q_GROUP = N_HEADS // N_KV_HEADS
D_ATTN = N_HEADS * HEAD_DIM
D_KV = N_KV_HEADS * HEAD_DIM
WINDOW = 128
Q_BLOCK = 128
D_RNN = D_MODEL
N_RNN_BLOCKS = 8
RNN_BLOCK = D_RNN // N_RNN_BLOCKS
CONV_W = 4
CONV_LEFT = 2
LRU_C = 8.0
ROPE_BASE = 10000.0
D_IN = 2 * D_ATTN + 2 * D_KV + 2 * D_RNN + 2 * D_MODEL
DEEPNORM_ALPHA = (2 * DEPTH) ** 0.25
LN_EPS = 1e-6
NEG_INF = -1e30
ATTN_SCALE = HEAD_DIM ** -0.5

LANES = 128
SUBLANES = 8
SEQ_SLOTS = SUBLANES
SLOT_PAD = SUBLANES
COND_ROWS = 8
PROJ_TN = 512
N_PROJ_BLOCKS = D_IN // PROJ_TN
D_REST = D_IN - D_ATTN - 2 * D_KV
VMEM_LIMIT = 48 * 1024 * 1024
PROJ_VMEM_LIMIT = 56 * 1024 * 1024
RNN_VMEM_LIMIT = 56 * 1024 * 1024
F32_TINY = 1.1754943508222875e-38
LOG2_E = 1.4426950408889634

_BF16 = jnp.bfloat16
_F32 = jnp.float32


def _silu(x):
    return x * jax.nn.sigmoid(x)


def _layer_norm_rows(x):
    mu = jnp.mean(x, axis=-1, keepdims=True)
    xc = x - mu
    var = jnp.mean(xc * xc, axis=-1, keepdims=True)
    return xc * lax.rsqrt(var + LN_EPS)


def _mod_kernel(cond_ref, w_ref, b_ref, o_ref):
    a = _silu(cond_ref[...]).astype(_BF16)
    o_ref[...] = jnp.dot(a, w_ref[...].astype(_BF16), preferred_element_type=_F32) + b_ref[...]


def _modulation(cond8, w_mod, b_mod):
    tn = D_MODEL
    return pl.pallas_call(
        _mod_kernel,
        grid=(DEPTH, 3 * D_MODEL // tn),
        in_specs=[
            pl.BlockSpec((COND_ROWS, D_MODEL), lambda l, j: (0, 0)),
            pl.BlockSpec((None, D_MODEL, tn), lambda l, j: (l, 0, j)),
            pl.BlockSpec((None, 1, tn), lambda l, j: (l, 0, j)),
        ],
        out_specs=pl.BlockSpec((None, COND_ROWS, tn), lambda l, j: (l, 0, j)),
        out_shape=jax.ShapeDtypeStruct((DEPTH, COND_ROWS, 3 * D_MODEL), _F32),
        compiler_params=pltpu.CompilerParams(dimension_semantics=("parallel", "parallel")),
        name="modulation",
    )(cond8, w_mod, b_mod.reshape(DEPTH, 1, 3 * D_MODEL))


def _rope(x, cos, sin_signed):
    lane = lax.broadcasted_iota(jnp.int32, x.shape, 1)
    partner = jnp.where((lane // 32) % 2 == 0,
                        pltpu.roll(x, 3 * 32, axis=1),
                        pltpu.roll(x, 32, axis=1))
    return x * cos + partner * sin_signed


def _proj_kernel(x_ref, shift_ref, scale_ref, w_ref, *rest, rotary):
    if rotary:
        cos_ref, sin_ref, q_ref, k_ref, v_ref, r_ref = rest
    else:
        q_ref, k_ref, v_ref, r_ref = rest
    y = _layer_norm_rows(x_ref[...])
    h = (y * (1.0 + scale_ref[...]) + shift_ref[...]).astype(_BF16)

    def maybe_rope(blk):
        return _rope(blk, cos_ref[...], sin_ref[...]) if rotary else blk

    for j in range(N_PROJ_BLOCKS):
        c0 = j * PROJ_TN
        acc = jnp.dot(h, w_ref[:, c0:c0 + PROJ_TN], preferred_element_type=_F32)
        if c0 < D_ATTN:
            for hh in range(PROJ_TN // HEAD_DIM):
                sl = slice(hh * HEAD_DIM, (hh + 1) * HEAD_DIM)
                q_ref[:, c0 + hh * HEAD_DIM:c0 + (hh + 1) * HEAD_DIM] = (
                    maybe_rope(acc[:, sl]).astype(q_ref.dtype))
        elif c0 == D_ATTN:
            for hh in range(N_KV_HEADS):
                sl = slice(hh * HEAD_DIM, (hh + 1) * HEAD_DIM)
                k_ref[:, sl] = maybe_rope(acc[:, sl]).astype(k_ref.dtype)
            v_ref[...] = acc[:, D_KV:].astype(v_ref.dtype)
        else:
            r0 = c0 - D_ATTN - 2 * D_KV
            r_ref[:, r0:r0 + PROJ_TN] = acc


def _project(x2, mod3, w_in_bf, cond_row_of_block, tm, rope_tabs, kv_dtype):
    n = x2.shape[0]
    rotary = rope_tabs is not None
    in_specs = [
        pl.BlockSpec((tm, D_MODEL), lambda i: (i, 0)),
        pl.BlockSpec((None, 1, D_MODEL), lambda i: (cond_row_of_block(i), 0, 0)),
        pl.BlockSpec((None, 1, D_MODEL), lambda i: (cond_row_of_block(i), 0, 1)),
        pl.BlockSpec((D_MODEL, D_IN), lambda i: (0, 0), pipeline_mode=pl.Buffered(1)),
    ]
    args = [x2, mod3, mod3, w_in_bf]
    if rotary:
        cos, sin = rope_tabs
        blocks_per_seq = cos.shape[0] // tm
        in_specs += [pl.BlockSpec((tm, HEAD_DIM), lambda i: (i % blocks_per_seq, 0))] * 2
        args += [cos, sin]
    return pl.pallas_call(
        functools.partial(_proj_kernel, rotary=rotary),
        grid=(n // tm,),
        in_specs=in_specs,
        out_specs=[
            pl.BlockSpec((tm, D_ATTN), lambda i: (i, 0)),
            pl.BlockSpec((tm, D_KV), lambda i: (i, 0)),
            pl.BlockSpec((tm, D_KV), lambda i: (i, 0)),
            pl.BlockSpec((tm, D_REST), lambda i: (i, 0)),
        ],
        out_shape=[
            jax.ShapeDtypeStruct((n, D_ATTN), _BF16),
            jax.ShapeDtypeStruct((n, D_KV), kv_dtype),
            jax.ShapeDtypeStruct((n, D_KV), kv_dtype),
            jax.ShapeDtypeStruct((n, D_REST), _F32),
        ],
        compiler_params=pltpu.CompilerParams(
            dimension_semantics=("parallel",), vmem_limit_bytes=PROJ_VMEM_LIMIT),
        name="proj_rope" if rotary else "proj",
    )(*args)


def _softmax_pv(s, sink_col, v_bf):
    m = jnp.maximum(jnp.max(s, axis=-1, keepdims=True), sink_col)
    e = jnp.exp(s - m)
    denom = jnp.sum(e, axis=-1, keepdims=True) + jnp.exp(sink_col - m)
    o = jnp.dot(e.astype(_BF16), v_bf, preferred_element_type=_F32)
    return o / denom


def _sink_column(sink_ref, kvh, rows):
    head = lax.broadcasted_iota(jnp.int32, (Q_GROUP * rows, 1), 0) // rows
    col = jnp.full((Q_GROUP * rows, 1), sink_ref[kvh * Q_GROUP], _F32)
    for g in range(1, Q_GROUP):
        col = jnp.where(head == g, sink_ref[kvh * Q_GROUP + g], col)
    return col


def _stack_heads(q_rows):
    return jnp.concatenate(
        [q_rows[:, g * HEAD_DIM:(g + 1) * HEAD_DIM] for g in range(Q_GROUP)], axis=0)


def _ctx_attn_kernel(sink_ref, q_ref, k_ref, v_ref, o_ref):
    kvh = pl.program_id(1)
    s_len = q_ref.shape[0]
    q = _stack_heads(q_ref[...])
    k = k_ref[...].astype(_BF16)
    v = v_ref[...].astype(_BF16)
    s = lax.dot_general(q, k, (((1,), (1,)), ((), ())), preferred_element_type=_F32) * ATTN_SCALE
    o = _softmax_pv(s, _sink_column(sink_ref, kvh, s_len), v)
    for g in range(Q_GROUP):
        o_ref[:, g * HEAD_DIM:(g + 1) * HEAD_DIM] = o[g * s_len:(g + 1) * s_len]


def _context_attention(q, k, v, sink, batch, s_len):
    n = batch * s_len
    gw = Q_GROUP * HEAD_DIM
    return pl.pallas_call(
        _ctx_attn_kernel,
        grid=(batch, N_KV_HEADS),
        in_specs=[
            pl.BlockSpec(memory_space=pltpu.SMEM),
            pl.BlockSpec((s_len, gw), lambda b, h: (b, h)),
            pl.BlockSpec((s_len, HEAD_DIM), lambda b, h: (b, h)),
            pl.BlockSpec((s_len, HEAD_DIM), lambda b, h: (b, h)),
        ],
        out_specs=pl.BlockSpec((s_len, gw), lambda b, h: (b, h)),
        out_shape=jax.ShapeDtypeStruct((n, D_ATTN), _F32),
        compiler_params=pltpu.CompilerParams(dimension_semantics=("parallel", "parallel")),
        name="ctx_attention",
    )(sink, q, k, v)


def _lat_attn_kernel(sink_ref, q_ref, k_ref, v_ref, ck_ref, cv_ref, o_ref):
    kvh = pl.program_id(1)
    t_len = q_ref.shape[0]
    nb = t_len // Q_BLOCK
    ck = ck_ref[...].astype(_BF16)
    cv = cv_ref[...].astype(_BF16)
    sink_col = _sink_column(sink_ref, kvh, Q_BLOCK)
    rows = Q_GROUP * Q_BLOCK
    r = lax.broadcasted_iota(jnp.int32, (rows, Q_BLOCK), 0) % Q_BLOCK
    c = lax.broadcasted_iota(jnp.int32, (rows, Q_BLOCK), 1)
    for jb in range(nb):
        r0 = jb * Q_BLOCK
        lo = max(r0 - WINDOW, 0)
        hi = min(r0 + Q_BLOCK + WINDOW, t_len)
        q = _stack_heads(q_ref[r0:r0 + Q_BLOCK, :])
        keys = jnp.concatenate([k_ref[lo:hi, :], ck], axis=0)
        vals = jnp.concatenate([v_ref[lo:hi, :], cv], axis=0)
        s = lax.dot_general(q, keys, (((1,), (1,)), ((), ())), preferred_element_type=_F32) * ATTN_SCALE
        parts = []
        off = 0
        if jb > 0:
            parts.append(jnp.where(c >= r, s[:, :Q_BLOCK], NEG_INF))
            off = Q_BLOCK
        parts.append(s[:, off:off + Q_BLOCK])
        off += Q_BLOCK
        if jb < nb - 1:
            parts.append(jnp.where(c <= r, s[:, off:off + Q_BLOCK], NEG_INF))
            off += Q_BLOCK
        parts.append(s[:, off:])
        o = _softmax_pv(jnp.concatenate(parts, axis=1), sink_col, vals)
        for g in range(Q_GROUP):
            o_ref[r0:r0 + Q_BLOCK, g * HEAD_DIM:(g + 1) * HEAD_DIM] = o[g * Q_BLOCK:(g + 1) * Q_BLOCK]


def _latent_attention(q, k, v, cache_k4, cache_v4, layer, sink, batch, t_len):
    n = batch * t_len
    gw = Q_GROUP * HEAD_DIM
    past = cache_k4.shape[2]
    cache_spec = pl.BlockSpec((None, None, past, HEAD_DIM), lambda b, h: (b, layer, 0, h))
    return pl.pallas_call(
        _lat_attn_kernel,
        grid=(batch, N_KV_HEADS),
        in_specs=[
            pl.BlockSpec(memory_space=pltpu.SMEM),
            pl.BlockSpec((t_len, gw), lambda b, h: (b, h)),
            pl.BlockSpec((t_len, HEAD_DIM), lambda b, h: (b, h)),
            pl.BlockSpec((t_len, HEAD_DIM), lambda b, h: (b, h)),
            cache_spec,
            cache_spec,
        ],
        out_specs=pl.BlockSpec((t_len, gw), lambda b, h: (b, h)),
        out_shape=jax.ShapeDtypeStruct((n, D_ATTN), _F32),
        compiler_params=pltpu.CompilerParams(
            dimension_semantics=("parallel", "parallel"), vmem_limit_bytes=VMEM_LIMIT),
        name="latent_attention",
    )(sink, q, k, v, cache_k4, cache_v4)


def _shift_rows(x, d, fill):
    t_len = x.shape[0]
    row = lax.broadcasted_iota(jnp.int32, x.shape, 0)
    rolled = pltpu.roll(x, d % t_len, axis=0)
    valid = (row >= d) if d > 0 else (row < t_len + d)
    return jnp.where(valid, rolled, fill)


def _sqrt_nonneg(z):
    return z * lax.rsqrt(jnp.maximum(z, F32_TINY))


def _rnn_kernel(x_ref, cw_ref, cb_ref, wg_ref, bg_ref, lam_ref, h0_ref, y_ref, hl_ref,
                af_scr, bf_scr, ab_scr, bb_scr, hf_scr, hb_scr, *, nb, nc):
    t_len = x_ref.shape[1]
    pitch = t_len + SLOT_PAD
    scr = ((af_scr, bf_scr), (ab_scr, bb_scr))

    for cl in range(nc):
        lanes = slice(cl * LANES, (cl + 1) * LANES)
        cw = cw_ref[:, lanes]
        cb = cb_ref[:, lanes]
        wg = wg_ref[cl]
        bg = bg_ref[cl]
        lam = lam_ref[:, lanes]
        c_softplus = LRU_C * (jnp.maximum(-lam, 0.0) + jnp.log1p(jnp.exp(-jnp.abs(lam))))

        def per_batch(b, carry, lanes=lanes, cw=cw, cb=cb, wg=wg, bg=bg,
                      c_softplus=c_softplus, cl=cl):
            x = x_ref[b, :, lanes]
            xc = cw[0:1] * _shift_rows(x, 2, 0.0)
            xc = xc + cw[1:2] * _shift_rows(x, 1, 0.0)
            xc = xc + cw[2:3] * x
            xc = xc + cw[3:4] * _shift_rows(x, -1, 0.0)
            xc = xc + cb
            half_gates = jnp.dot(xc.astype(_BF16), wg, preferred_element_type=_F32) + bg
            half_xc = 0.5 * xc
            row0 = pl.multiple_of((cl * nb + b) * pitch, 8)
            for d in range(2):
                t_r = jnp.tanh(half_gates[:, (2 * d) * LANES:(2 * d + 1) * LANES])
                t_i = jnp.tanh(half_gates[:, (2 * d + 1) * LANES:(2 * d + 2) * LANES])
                k = 0.5 * c_softplus[d:d + 1]
                k2 = -LOG2_E * k
                neg_log_a = t_r * k + k
                a = jnp.exp2(t_r * k2 + k2)
                bterm = _sqrt_nonneg(jnp.tanh(neg_log_a) * (a * a + 1.0)) * ((t_i + 1.0) * half_xc)
                scr[d][0][pl.ds(row0, t_len), :] = a
                scr[d][1][pl.ds(row0, t_len), :] = bterm
            return carry

        lax.fori_loop(0, nb, per_batch, 0)

    def slot_rows(ref3, d):
        rows = [ref3[b, d:d + 1, cl * LANES:(cl + 1) * LANES] for cl in range(nc) for b in range(nb)]
        return jnp.concatenate(rows, axis=0)

    def step(t, h):
        hf, hb = h
        fwd = pl.ds(t, SEQ_SLOTS, stride=pitch)
        hf = af_scr[fwd, :] * hf + bf_scr[fwd, :]
        hf_scr[fwd, :] = hf
        bwd = pl.ds(t_len - 1 - t, SEQ_SLOTS, stride=pitch)
        hb = ab_scr[bwd, :] * hb + bb_scr[bwd, :]
        hb_scr[bwd, :] = hb
        return hf, hb

    hf, hb = lax.fori_loop(0, t_len, step, (slot_rows(h0_ref, 0), slot_rows(h0_ref, 1)), unroll=8)

    for cl in range(nc):
        lanes = slice(cl * LANES, (cl + 1) * LANES)
        for b in range(nb):
            s = cl * nb + b
            hl_ref[b, 0:1, lanes] = hf[s:s + 1, :]
            hl_ref[b, 1:2, lanes] = hb[s:s + 1, :]
            rows = pl.ds(s * pitch, t_len)
            y_ref[b, :, lanes] = hf_scr[rows, :] + hb_scr[rows, :]


def _rnn(rest, h0, conv_w, conv_b, wg, bg, lam, batch, t_len):
    nb = min(batch, SEQ_SLOTS)
    nc = SEQ_SLOTS // nb
    cw = nc * LANES
    xr_block0 = D_MODEL // cw
    scratch = pltpu.VMEM((SEQ_SLOTS * (t_len + SLOT_PAD), LANES), _F32)
    y, h_last = pl.pallas_call(
        functools.partial(_rnn_kernel, nb=nb, nc=nc),
        grid=(batch // nb, D_RNN // cw),
        in_specs=[
            pl.BlockSpec((nb, t_len, cw), lambda g, c: (g, 0, xr_block0 + c)),
            pl.BlockSpec((CONV_W, cw), lambda g, c: (0, c)),
            pl.BlockSpec((1, cw), lambda g, c: (0, c)),
            pl.BlockSpec((nc, RNN_BLOCK, 4 * LANES), lambda g, c: (c, 0, 0)),
            pl.BlockSpec((nc, 1, 4 * LANES), lambda g, c: (c, 0, 0)),
            pl.BlockSpec((2, cw), lambda g, c: (0, c)),
            pl.BlockSpec((nb, 2, cw), lambda g, c: (g, 0, c)),
        ],
        out_specs=[
            pl.BlockSpec((nb, t_len, cw), lambda g, c: (g, 0, c)),
            pl.BlockSpec((nb, 2, cw), lambda g, c: (g, 0, c)),
        ],
        out_shape=[
            jax.ShapeDtypeStruct((batch, t_len, D_RNN), _F32),
            jax.ShapeDtypeStruct((batch, 2, D_RNN), _F32),
        ],
        scratch_shapes=[scratch] * 6,
        compiler_params=pltpu.CompilerParams(
            dimension_semantics=("parallel", "parallel"), vmem_limit_bytes=RNN_VMEM_LIMIT),
        name="rnn",
    )(rest.reshape(batch, t_len, D_REST), conv_w, conv_b.reshape(1, D_RNN), wg, bg, lam, h0)
    return y.reshape(batch * t_len, D_RNN), h_last


def _out_kernel(x_ref, gate_ref, oa_ref, ga_ref, or_ref, gr_ref, mga_ref, mgr_ref,
                wa_ref, wr_ref, wo_ref, g_ref, b_ref, o_ref):
    ya = jnp.dot((oa_ref[...] * _silu(ga_ref[...])).astype(_BF16), wa_ref[...],
                 preferred_element_type=_F32)
    yr = jnp.dot((or_ref[...] * _silu(gr_ref[...])).astype(_BF16), wr_ref[...],
                 preferred_element_type=_F32)
    merged = jax.nn.sigmoid(mga_ref[...]) * ya + jax.nn.sigmoid(mgr_ref[...]) * yr
    out = jnp.dot(merged.astype(_BF16), wo_ref[...], preferred_element_type=_F32)
    z = DEEPNORM_ALPHA * x_ref[...] + gate_ref[...] * out
    o_ref[...] = _layer_norm_rows(z) * g_ref[...] + b_ref[...]


def _merge_residual(x2, mod3, o_attn, o_rnn, rest, wa_bf, wr_bf, wo_bf, ln_g, ln_b,
                    cond_row_of_block, tm):
    n = x2.shape[0]
    row_blk = lambda col: pl.BlockSpec((tm, D_MODEL), lambda i, col=col: (i, col))
    whole = pl.BlockSpec((D_MODEL, D_MODEL), lambda i: (0, 0))
    vec = pl.BlockSpec((1, D_MODEL), lambda i: (0, 0))
    return pl.pallas_call(
        _out_kernel,
        grid=(n // tm,),
        in_specs=[
            row_blk(0),
            pl.BlockSpec((None, 1, D_MODEL), lambda i: (cond_row_of_block(i), 0, 2)),
            row_blk(0), row_blk(0), row_blk(0), row_blk(2), row_blk(3), row_blk(4),
            whole, whole, whole, vec, vec,
        ],
        out_specs=row_blk(0),
        out_shape=jax.ShapeDtypeStruct((n, D_MODEL), _F32),
        compiler_params=pltpu.CompilerParams(
            dimension_semantics=("parallel",), vmem_limit_bytes=VMEM_LIMIT),
        name="merge_residual",
    )(x2, mod3, o_attn, rest, o_rnn, rest, rest, rest, wa_bf, wr_bf, wo_bf,
      ln_g.reshape(1, D_MODEL), ln_b.reshape(1, D_MODEL))


def _rope_tables(t_len):
    quarter = HEAD_DIM // 4
    inv_freq = ROPE_BASE ** (-jnp.arange(quarter, dtype=_F32) / quarter)
    pos = jnp.arange(t_len)
    ang_r = (pos // GRID_W).astype(_F32)[:, None] * inv_freq[None, :]
    ang_c = (pos % GRID_W).astype(_F32)[:, None] * inv_freq[None, :]
    cos = jnp.concatenate([jnp.cos(ang_r)] * 2 + [jnp.cos(ang_c)] * 2, axis=1)
    sin = jnp.concatenate([-jnp.sin(ang_r), jnp.sin(ang_r), -jnp.sin(ang_c), jnp.sin(ang_c)], axis=1)
    return cos, sin


def _gate_weights(rg_wa, rg_ba, rg_wx, rg_bx):
    wg = (0.5 * jnp.concatenate([rg_wa[0], rg_wx[0], rg_wa[1], rg_wx[1]], axis=-1)).astype(_BF16)
    blk = lambda v: v.reshape(N_RNN_BLOCKS, 1, RNN_BLOCK)
    bg = 0.5 * jnp.concatenate([blk(rg_ba[0]), blk(rg_bx[0]), blk(rg_ba[1]), blk(rg_bx[1])], axis=-1)
    return wg, bg


def kernel(x_prompt, x_sample, cache_k, cache_v, state_h, c, c_ctx, w_mod, b_mod, w_in,
           attn_sink, conv_w, conv_b, rg_wa, rg_ba, rg_wx, rg_bx, rg_lambda,
           w_br_attn, w_br_rnn, w_out, ln_g, ln_b):
    bp, s_len, _ = x_prompt.shape
    bd, t_len, _ = x_sample.shape
    past = cache_k.shape[2]

    cond8 = jnp.zeros((COND_ROWS, D_MODEL), _F32).at[0].set(c_ctx).at[1:1 + bd].set(c)
    mod = _modulation(cond8, w_mod, b_mod)

    w_in_bf = w_in.astype(_BF16)
    wa_bf = w_br_attn.astype(_BF16)
    wr_bf = w_br_rnn.astype(_BF16)
    wo_bf = w_out.astype(_BF16)
    rope_tabs = _rope_tables(t_len)
    cache_k4 = cache_k.reshape(bd, DEPTH, past, D_KV)
    cache_v4 = cache_v.reshape(bd, DEPTH, past, D_KV)

    tm_proj, tm_out = 512, 256
    ctx_row = lambda i: 0
    lat_row_proj = lambda i: 1 + (i * tm_proj) // t_len
    lat_row_out = lambda i: 1 + (i * tm_out) // t_len

    xp = x_prompt.reshape(bp * s_len, D_MODEL)
    xs = x_sample.reshape(bd * t_len, D_MODEL)
    h0_zero = jnp.zeros((bp, 2, D_RNN), _F32)
    ks, vs, hs = [], [], []
    for l in range(DEPTH):
        mod3 = mod[l].reshape(COND_ROWS, 1, 3 * D_MODEL)
        wg, bg = _gate_weights(rg_wa[l], rg_ba[l], rg_wx[l], rg_bx[l])

        q, k, v, rest = _project(xp, mod3, w_in_bf[l], ctx_row, tm_proj, None, _F32)
        o_attn = _context_attention(q, k, v, attn_sink[l], bp, s_len)
        o_rnn, h_fin = _rnn(rest, h0_zero, conv_w[l], conv_b[l], wg, bg, rg_lambda[l], bp, s_len)
        xp = _merge_residual(xp, mod3, o_attn, o_rnn, rest, wa_bf[l], wr_bf[l], wo_bf[l],
                             ln_g[l], ln_b[l], ctx_row, tm_out)
        ks.append(k.reshape(bp, s_len, N_KV_HEADS, HEAD_DIM))
        vs.append(v.reshape(bp, s_len, N_KV_HEADS, HEAD_DIM))
        hs.append(h_fin)

        q, k, v, rest = _project(xs, mod3, w_in_bf[l], lat_row_proj, tm_proj, rope_tabs, _BF16)
        o_attn = _latent_attention(q, k, v, cache_k4, cache_v4, l, attn_sink[l], bd, t_len)
        o_rnn, _ = _rnn(rest, state_h[:, l], conv_w[l], conv_b[l], wg, bg, rg_lambda[l], bd, t_len)
        xs = _merge_residual(xs, mod3, o_attn, o_rnn, rest, wa_bf[l], wr_bf[l], wo_bf[l],
                             ln_g[l], ln_b[l], lat_row_out, tm_out)

    y_prompt = xp.reshape(bp, s_len, D_MODEL)
    y_sample = xs.reshape(bd, t_len, D_MODEL)
    return (y_prompt, y_sample, jnp.stack(ks, axis=1), jnp.stack(vs, axis=1), jnp.stack(hs, axis=1))
```

```python
import functools

import jax
import jax.numpy as jnp
from jax import lax
from jax.experimental import pallas as pl
from jax.experimental.pallas import tpu as pltpu

D_MODEL = 1024
DEPTH = 2
GRID_W = 64
N_HEADS = 8
N_KV_HEADS = 2
HEAD_DIM = 128
Q_GROUP = N_HEADS // N_KV_HEADS
D_ATTN = N_HEADS * HEAD_DIM
D_KV = N_KV_HEADS * HEAD_DIM
WINDOW = 128
Q_BLOCK = 128
D_RNN = D_MODEL
N_RNN_BLOCKS = 8
RNN_BLOCK = D_RNN // N_RNN_BLOCKS
CONV_W = 4
CONV_LEFT = 2
LRU_C = 8.0
ROPE_BASE = 10000.0
D_IN = 2 * D_ATTN + 2 * D_KV + 2 * D_RNN + 2 * D_MODEL
DEEPNORM_ALPHA = (2 * DEPTH) ** 0.25
LN_EPS = 1e-6
NEG_INF = -1e30
ATTN_SCALE = HEAD_DIM ** -0.5

LANES = 128
SUBLANES = 8
SEQ_SLOTS = SUBLANES
SLOT_PAD = SUBLANES
COND_ROWS = 8
PROJ_TN = 512
N_PROJ_BLOCKS = D_IN // PROJ_TN
D_REST = D_IN - D_ATTN - 2 * D_KV
VMEM_LIMIT = 48 * 1024 * 1024
PROJ_VMEM_LIMIT = 56 * 1024 * 1024
RNN_VMEM_LIMIT = 56 * 1024 * 1024
F32_TINY = 1.1754943508222875e-38
LOG2_E = 1.4426950408889634

_BF16 = jnp.bfloat16
_F32 = jnp.float32


def _silu(x):
    return x * jax.nn.sigmoid(x)


def _layer_norm_rows(x):
    mu = jnp.mean(x, axis=-1, keepdims=True)
    xc = x - mu
    var = jnp.mean(xc * xc, axis=-1, keepdims=True)
    return xc * lax.rsqrt(var + LN_EPS)


def _mod_kernel(cond_ref, w_ref, b_ref, o_ref):
    a = _silu(cond_ref[...]).astype(_BF16)
    o_ref[...] = jnp.dot(a, w_ref[...].astype(_BF16), preferred_element_type=_F32) + b_ref[...]


def _modulation(cond8, w_mod, b_mod):
    tn = D_MODEL
    return pl.pallas_call(
        _mod_kernel,
        grid=(DEPTH, 3 * D_MODEL // tn),
        in_specs=[
            pl.BlockSpec((COND_ROWS, D_MODEL), lambda l, j: (0, 0)),
            pl.BlockSpec((None, D_MODEL, tn), lambda l, j: (l, 0, j)),
            pl.BlockSpec((None, 1, tn), lambda l, j: (l, 0, j)),
        ],
        out_specs=pl.BlockSpec((None, COND_ROWS, tn), lambda l, j: (l, 0, j)),
        out_shape=jax.ShapeDtypeStruct((DEPTH, COND_ROWS, 3 * D_MODEL), _F32),
        compiler_params=pltpu.CompilerParams(dimension_semantics=("parallel", "parallel")),
        name="modulation",
    )(cond8, w_mod, b_mod.reshape(DEPTH, 1, 3 * D_MODEL))


def _rope(x, cos, sin_signed):
    lane = lax.broadcasted_iota(jnp.int32, x.shape, 1)
    partner = jnp.where((lane // 32) % 2 == 0,
                        pltpu.roll(x, 3 * 32, axis=1),
                        pltpu.roll(x, 32, axis=1))
    return x * cos + partner * sin_signed


def _proj_kernel(x_ref, shift_ref, scale_ref, w_ref, *rest, rotary):
    if rotary:
        cos_ref, sin_ref, q_ref, k_ref, v_ref, r_ref = rest
    else:
        q_ref, k_ref, v_ref, r_ref = rest
    y = _layer_norm_rows(x_ref[...])
    h = (y * (1.0 + scale_ref[...]) + shift_ref[...]).astype(_BF16)

    def maybe_rope(blk):
        return _rope(blk, cos_ref[...], sin_ref[...]) if rotary else blk

    for j in range(N_PROJ_BLOCKS):
        c0 = j * PROJ_TN
        acc = jnp.dot(h, w_ref[:, c0:c0 + PROJ_TN], preferred_element_type=_F32)
        if c0 < D_ATTN:
            for hh in range(PROJ_TN // HEAD_DIM):
                sl = slice(hh * HEAD_DIM, (hh + 1) * HEAD_DIM)
                q_ref[:, c0 + hh * HEAD_DIM:c0 + (hh + 1) * HEAD_DIM] = (
                    maybe_rope(acc[:, sl]).astype(q_ref.dtype))
        elif c0 == D_ATTN:
            for hh in range(N_KV_HEADS):
                sl = slice(hh * HEAD_DIM, (hh + 1) * HEAD_DIM)
                k_ref[:, sl] = maybe_rope(acc[:, sl]).astype(k_ref.dtype)
            v_ref[...] = acc[:, D_KV:].astype(v_ref.dtype)
        else:
            r0 = c0 - D_ATTN - 2 * D_KV
            r_ref[:, r0:r0 + PROJ_TN] = acc.astype(r_ref.dtype)


def _project(x2, mod3, w_in_bf, layer, cond_row_of_block, tm, rope_tabs, kv_dtype):
    n = x2.shape[0]
    rotary = rope_tabs is not None
    in_specs = [
        pl.BlockSpec((tm, D_MODEL), lambda i: (i, 0)),
        pl.BlockSpec((None, 1, D_MODEL), lambda i: (cond_row_of_block(i), 0, 0)),
        pl.BlockSpec((None, 1, D_MODEL), lambda i: (cond_row_of_block(i), 0, 1)),
        pl.BlockSpec((None, D_MODEL, D_IN), lambda i: (layer, 0, 0), pipeline_mode=pl.Buffered(1)),
    ]
    args = [x2, mod3, mod3, w_in_bf]
    if rotary:
        cos, sin = rope_tabs
        blocks_per_seq = cos.shape[0] // tm
        in_specs += [pl.BlockSpec((tm, HEAD_DIM), lambda i: (i % blocks_per_seq, 0))] * 2
        args += [cos, sin]
    return pl.pallas_call(
        functools.partial(_proj_kernel, rotary=rotary),
        grid=(n // tm,),
        in_specs=in_specs,
        out_specs=[
            pl.BlockSpec((tm, D_ATTN), lambda i: (i, 0)),
            pl.BlockSpec((tm, D_KV), lambda i: (i, 0)),
            pl.BlockSpec((tm, D_KV), lambda i: (i, 0)),
            pl.BlockSpec((tm, D_REST), lambda i: (i, 0)),
        ],
        out_shape=[
            jax.ShapeDtypeStruct((n, D_ATTN), _BF16),
            jax.ShapeDtypeStruct((n, D_KV), kv_dtype),
            jax.ShapeDtypeStruct((n, D_KV), kv_dtype),
            jax.ShapeDtypeStruct((n, D_REST), _BF16),
        ],
        compiler_params=pltpu.CompilerParams(
            dimension_semantics=("parallel",), vmem_limit_bytes=PROJ_VMEM_LIMIT),
        name="proj_rope" if rotary else "proj",
    )(*args)


def _softmax_pv(s, sink_col, v_bf):
    m = jnp.maximum(jnp.max(s, axis=-1, keepdims=True), sink_col)
    e = jnp.exp(s - m)
    denom = jnp.sum(e, axis=-1, keepdims=True) + jnp.exp(sink_col - m)
    o = jnp.dot(e.astype(_BF16), v_bf, preferred_element_type=_F32)
    return o / denom


def _sink_column(sink_ref, kvh, rows):
    head = lax.broadcasted_iota(jnp.int32, (Q_GROUP * rows, 1), 0) // rows
    col = jnp.full((Q_GROUP * rows, 1), sink_ref[kvh * Q_GROUP], _F32)
    for g in range(1, Q_GROUP):
        col = jnp.where(head == g, sink_ref[kvh * Q_GROUP + g], col)
    return col


def _stack_heads(q_rows):
    return jnp.concatenate(
        [q_rows[:, g * HEAD_DIM:(g + 1) * HEAD_DIM] for g in range(Q_GROUP)], axis=0)


def _ctx_attn_kernel(sink_ref, q_ref, k_ref, v_ref, o_ref):
    kvh = pl.program_id(1)
    s_len = q_ref.shape[0]
    q = _stack_heads(q_ref[...])
    k = k_ref[...].astype(_BF16)
    v = v_ref[...].astype(_BF16)
    s = lax.dot_general(q, k, (((1,), (1,)), ((), ())), preferred_element_type=_F32) * ATTN_SCALE
    o = _softmax_pv(s, _sink_column(sink_ref, kvh, s_len), v)
    for g in range(Q_GROUP):
        o_ref[:, g * HEAD_DIM:(g + 1) * HEAD_DIM] = o[g * s_len:(g + 1) * s_len].astype(o_ref.dtype)


def _context_attention(q, k, v, sink, batch, s_len):
    n = batch * s_len
    gw = Q_GROUP * HEAD_DIM
    return pl.pallas_call(
        _ctx_attn_kernel,
        grid=(batch, N_KV_HEADS),
        in_specs=[
            pl.BlockSpec(memory_space=pltpu.SMEM),
            pl.BlockSpec((s_len, gw), lambda b, h: (b, h)),
            pl.BlockSpec((s_len, HEAD_DIM), lambda b, h: (b, h)),
            pl.BlockSpec((s_len, HEAD_DIM), lambda b, h: (b, h)),
        ],
        out_specs=pl.BlockSpec((s_len, gw), lambda b, h: (b, h)),
        out_shape=jax.ShapeDtypeStruct((n, D_ATTN), _BF16),
        compiler_params=pltpu.CompilerParams(dimension_semantics=("parallel", "parallel")),
        name="ctx_attention",
    )(sink, q, k, v)


def _lat_attn_kernel(sink_ref, q_ref, k_ref, v_ref, ck_ref, cv_ref, o_ref):
    kvh = pl.program_id(1)
    t_len = q_ref.shape[0]
    nb = t_len // Q_BLOCK
    ck = ck_ref[...].astype(_BF16)
    cv = cv_ref[...].astype(_BF16)
    sink_col = _sink_column(sink_ref, kvh, Q_BLOCK)
    rows = Q_GROUP * Q_BLOCK
    r = lax.broadcasted_iota(jnp.int32, (rows, Q_BLOCK), 0) % Q_BLOCK
    c = lax.broadcasted_iota(jnp.int32, (rows, Q_BLOCK), 1)
    for jb in range(nb):
        r0 = jb * Q_BLOCK
        lo = max(r0 - WINDOW, 0)
        hi = min(r0 + Q_BLOCK + WINDOW, t_len)
        q = _stack_heads(q_ref[r0:r0 + Q_BLOCK, :])
        keys = jnp.concatenate([k_ref[lo:hi, :], ck], axis=0)
        vals = jnp.concatenate([v_ref[lo:hi, :], cv], axis=0)
        s = lax.dot_general(q, keys, (((1,), (1,)), ((), ())), preferred_element_type=_F32) * ATTN_SCALE
        parts = []
        off = 0
        if jb > 0:
            parts.append(jnp.where(c >= r, s[:, :Q_BLOCK], NEG_INF))
            off = Q_BLOCK
        parts.append(s[:, off:off + Q_BLOCK])
        off += Q_BLOCK
        if jb < nb - 1:
            parts.append(jnp.where(c <= r, s[:, off:off + Q_BLOCK], NEG_INF))
            off += Q_BLOCK
        parts.append(s[:, off:])
        o = _softmax_pv(jnp.concatenate(parts, axis=1), sink_col, vals)
        for g in range(Q_GROUP):
            o_ref[r0:r0 + Q_BLOCK, g * HEAD_DIM:(g + 1) * HEAD_DIM] = (
                o[g * Q_BLOCK:(g + 1) * Q_BLOCK].astype(o_ref.dtype))


def _latent_attention(q, k, v, cache_k4, cache_v4, layer, sink, batch, t_len):
    n = batch * t_len
    gw = Q_GROUP * HEAD_DIM
    past = cache_k4.shape[2]
    cache_spec = pl.BlockSpec((None, None, past, HEAD_DIM), lambda b, h: (b, layer, 0, h))
    return pl.pallas_call(
        _lat_attn_kernel,
        grid=(batch, N_KV_HEADS),
        in_specs=[
            pl.BlockSpec(memory_space=pltpu.SMEM),
            pl.BlockSpec((t_len, gw), lambda b, h: (b, h)),
            pl.BlockSpec((t_len, HEAD_DIM), lambda b, h: (b, h)),
            pl.BlockSpec((t_len, HEAD_DIM), lambda b, h: (b, h)),
            cache_spec,
            cache_spec,
        ],
        out_specs=pl.BlockSpec((t_len, gw), lambda b, h: (b, h)),
        out_shape=jax.ShapeDtypeStruct((n, D_ATTN), _BF16),
        compiler_params=pltpu.CompilerParams(
            dimension_semantics=("parallel", "parallel"), vmem_limit_bytes=VMEM_LIMIT),
        name="latent_attention",
    )(sink, q, k, v, cache_k4, cache_v4)


def _shift_rows(x, d, fill):
    t_len = x.shape[0]
    row = lax.broadcasted_iota(jnp.int32, x.shape, 0)
    rolled = pltpu.roll(x, d % t_len, axis=0)
    valid = (row >= d) if d > 0 else (row < t_len + d)
    return jnp.where(valid, rolled, fill)


def _sqrt_nonneg(z):
    return z * lax.rsqrt(jnp.maximum(z, F32_TINY))


def _rnn_kernel(x_ref, cw_ref, cb_ref, wg_ref, bg_ref, lam_ref, h0_ref, y_ref, hl_ref,
                af_scr, bf_scr, ab_scr, bb_scr, hf_scr, hb_scr, *, nb, nc):
    t_len = x_ref.shape[1]
    pitch = t_len + SLOT_PAD
    scr = ((af_scr, bf_scr), (ab_scr, bb_scr))

    for cl in range(nc):
        lanes = slice(cl * LANES, (cl + 1) * LANES)
        cw = cw_ref[:, lanes]
        cb = cb_ref[:, lanes]
        wg = wg_ref[cl]
        bg = bg_ref[cl]
        lam = lam_ref[:, lanes]
        c_softplus = LRU_C * (jnp.maximum(-lam, 0.0) + jnp.log1p(jnp.exp(-jnp.abs(lam))))

        def per_batch(b, carry, lanes=lanes, cw=cw, cb=cb, wg=wg, bg=bg,
                      c_softplus=c_softplus, cl=cl):
            x = x_ref[b, :, lanes].astype(_F32)
            xc = cw[0:1] * _shift_rows(x, 2, 0.0)
            xc = xc + cw[1:2] * _shift_rows(x, 1, 0.0)
            xc = xc + cw[2:3] * x
            xc = xc + cw[3:4] * _shift_rows(x, -1, 0.0)
            xc = xc + cb
            half_gates = jnp.dot(xc.astype(_BF16), wg, preferred_element_type=_F32) + bg
            half_xc = 0.5 * xc
            row0 = pl.multiple_of((cl * nb + b) * pitch, 8)
            for d in range(2):
                t_r = jnp.tanh(half_gates[:, (2 * d) * LANES:(2 * d + 1) * LANES])
                t_i = jnp.tanh(half_gates[:, (2 * d + 1) * LANES:(2 * d + 2) * LANES])
                k = 0.5 * c_softplus[d:d + 1]
                k2 = -LOG2_E * k
                neg_log_a = t_r * k + k
                a = jnp.exp2(t_r * k2 + k2)
                bterm = _sqrt_nonneg(jnp.tanh(neg_log_a) * (a * a + 1.0)) * ((t_i + 1.0) * half_xc)
                scr[d][0][pl.ds(row0, t_len), :] = a
                scr[d][1][pl.ds(row0, t_len), :] = bterm
            return carry

        lax.fori_loop(0, nb, per_batch, 0)

    def slot_rows(ref3, d):
        rows = [ref3[b, d:d + 1, cl * LANES:(cl + 1) * LANES] for cl in range(nc) for b in range(nb)]
        return jnp.concatenate(rows, axis=0)

    def step(t, h):
        hf, hb = h
        fwd = pl.ds(t, SEQ_SLOTS, stride=pitch)
        hf = af_scr[fwd, :] * hf + bf_scr[fwd, :]
        hf_scr[fwd, :] = hf
        bwd = pl.ds(t_len - 1 - t, SEQ_SLOTS, stride=pitch)
        hb = ab_scr[bwd, :] * hb + bb_scr[bwd, :]
        hb_scr[bwd, :] = hb
        return hf, hb

    hf, hb = lax.fori_loop(0, t_len, step, (slot_rows(h0_ref, 0), slot_rows(h0_ref, 1)), unroll=8)

    for cl in range(nc):
        lanes = slice(cl * LANES, (cl + 1) * LANES)
        for b in range(nb):
            s = cl * nb + b
            hl_ref[b, 0:1, lanes] = hf[s:s + 1, :]
            hl_ref[b, 1:2, lanes] = hb[s:s + 1, :]
            rows = pl.ds(s * pitch, t_len)
            y_ref[b, :, lanes] = (hf_scr[rows, :] + hb_scr[rows, :]).astype(y_ref.dtype)


def _rnn(rest, h0, conv_w, conv_b, wg, bg, lam, batch, t_len):
    nb = min(batch, SEQ_SLOTS)
    nc = SEQ_SLOTS // nb
    cw = nc * LANES
    xr_block0 = D_MODEL // cw
    scratch = pltpu.VMEM((SEQ_SLOTS * (t_len + SLOT_PAD), LANES), _F32)
    y, h_last = pl.pallas_call(
        functools.partial(_rnn_kernel, nb=nb, nc=nc),
        grid=(batch // nb, D_RNN // cw),
        in_specs=[
            pl.BlockSpec((nb, t_len, cw), lambda g, c: (g, 0, xr_block0 + c)),
            pl.BlockSpec((CONV_W, cw), lambda g, c: (0, c)),
            pl.BlockSpec((1, cw), lambda g, c: (0, c)),
            pl.BlockSpec((nc, RNN_BLOCK, 4 * LANES), lambda g, c: (c, 0, 0)),
            pl.BlockSpec((nc, 1, 4 * LANES), lambda g, c: (c, 0, 0)),
            pl.BlockSpec((2, cw), lambda g, c: (0, c)),
            pl.BlockSpec((nb, 2, cw), lambda g, c: (g, 0, c)),
        ],
        out_specs=[
            pl.BlockSpec((nb, t_len, cw), lambda g, c: (g, 0, c)),
            pl.BlockSpec((nb, 2, cw), lambda g, c: (g, 0, c)),
        ],
        out_shape=[
            jax.ShapeDtypeStruct((batch, t_len, D_RNN), _BF16),
            jax.ShapeDtypeStruct((batch, 2, D_RNN), _F32),
        ],
        scratch_shapes=[scratch] * 6,
        compiler_params=pltpu.CompilerParams(
            dimension_semantics=("parallel", "parallel"), vmem_limit_bytes=RNN_VMEM_LIMIT),
        name="rnn",
    )(rest.reshape(batch, t_len, D_REST), conv_w, conv_b.reshape(1, D_RNN), wg, bg, lam, h0)
    return y.reshape(batch * t_len, D_RNN), h_last


def _out_kernel(x_ref, gate_ref, oa_ref, ga_ref, or_ref, gr_ref, mga_ref, mgr_ref,
                wa_ref, wr_ref, wo_ref, g_ref, b_ref, o_ref):
    f32 = lambda ref: ref[...].astype(_F32)
    ya = jnp.dot((f32(oa_ref) * _silu(f32(ga_ref))).astype(_BF16), wa_ref[...],
                 preferred_element_type=_F32)
    yr = jnp.dot((f32(or_ref) * _silu(f32(gr_ref))).astype(_BF16), wr_ref[...],
                 preferred_element_type=_F32)
    merged = jax.nn.sigmoid(f32(mga_ref)) * ya + jax.nn.sigmoid(f32(mgr_ref)) * yr
    out = jnp.dot(merged.astype(_BF16), wo_ref[...], preferred_element_type=_F32)
    z = DEEPNORM_ALPHA * x_ref[...] + gate_ref[...] * out
    o_ref[...] = _layer_norm_rows(z) * g_ref[...] + b_ref[...]


def _merge_residual(x2, mod3, o_attn, o_rnn, rest, wa_bf, wr_bf, wo_bf, layer, ln_g, ln_b,
                    cond_row_of_block, tm):
    n = x2.shape[0]
    row_blk = lambda col: pl.BlockSpec((tm, D_MODEL), lambda i, col=col: (i, col))
    whole = pl.BlockSpec((None, D_MODEL, D_MODEL), lambda i: (layer, 0, 0))
    vec = pl.BlockSpec((1, D_MODEL), lambda i: (0, 0))
    return pl.pallas_call(
        _out_kernel,
        grid=(n // tm,),
        in_specs=[
            row_blk(0),
            pl.BlockSpec((None, 1, D_MODEL), lambda i: (cond_row_of_block(i), 0, 2)),
            row_blk(0), row_blk(0), row_blk(0), row_blk(2), row_blk(3), row_blk(4),
            whole, whole, whole, vec, vec,
        ],
        out_specs=row_blk(0),
        out_shape=jax.ShapeDtypeStruct((n, D_MODEL), _F32),
        compiler_params=pltpu.CompilerParams(
            dimension_semantics=("parallel",), vmem_limit_bytes=VMEM_LIMIT),
        name="merge_residual",
    )(x2, mod3, o_attn, rest, o_rnn, rest, rest, rest, wa_bf, wr_bf, wo_bf,
      ln_g.reshape(1, D_MODEL), ln_b.reshape(1, D_MODEL))


def _rope_tables(t_len):
    quarter = HEAD_DIM // 4
    inv_freq = ROPE_BASE ** (-jnp.arange(quarter, dtype=_F32) / quarter)
    pos = jnp.arange(t_len)
    ang_r = (pos // GRID_W).astype(_F32)[:, None] * inv_freq[None, :]
    ang_c = (pos % GRID_W).astype(_F32)[:, None] * inv_freq[None, :]
    cos = jnp.concatenate([jnp.cos(ang_r)] * 2 + [jnp.cos(ang_c)] * 2, axis=1)
    sin = jnp.concatenate([-jnp.sin(ang_r), jnp.sin(ang_r), -jnp.sin(ang_c), jnp.sin(ang_c)], axis=1)
    return cos, sin


def _gate_weights(rg_wa, rg_ba, rg_wx, rg_bx):
    wg = (0.5 * jnp.concatenate([rg_wa[0], rg_wx[0], rg_wa[1], rg_wx[1]], axis=-1)).astype(_BF16)
    blk = lambda v: v.reshape(N_RNN_BLOCKS, 1, RNN_BLOCK)
    bg = 0.5 * jnp.concatenate([blk(rg_ba[0]), blk(rg_bx[0]), blk(rg_ba[1]), blk(rg_bx[1])], axis=-1)
    return wg, bg


def kernel(x_prompt, x_sample, cache_k, cache_v, state_h, c, c_ctx, w_mod, b_mod, w_in,
           attn_sink, conv_w, conv_b, rg_wa, rg_ba, rg_wx, rg_bx, rg_lambda,
           w_br_attn, w_br_rnn, w_out, ln_g, ln_b):
    bp, s_len, _ = x_prompt.shape
    bd, t_len, _ = x_sample.shape
    past = cache_k.shape[2]

    cond8 = jnp.zeros((COND_ROWS, D_MODEL), _F32).at[0].set(c_ctx).at[1:1 + bd].set(c)
    mod = _modulation(cond8, w_mod, b_mod)

    w_in_bf = w_in.astype(_BF16)
    wa_bf = w_br_attn.astype(_BF16)
    wr_bf = w_br_rnn.astype(_BF16)
    wo_bf = w_out.astype(_BF16)
    rope_tabs = _rope_tables(t_len)
    cache_k4 = cache_k.reshape(bd, DEPTH, past, D_KV)
    cache_v4 = cache_v.reshape(bd, DEPTH, past, D_KV)

    tm_proj, tm_out = 512, 512
    ctx_row = lambda i: 0
    lat_row_proj = lambda i: 1 + (i * tm_proj) // t_len
    lat_row_out = lambda i: 1 + (i * tm_out) // t_len

    xp = x_prompt.reshape(bp * s_len, D_MODEL)
    xs = x_sample.reshape(bd * t_len, D_MODEL)
    h0_zero = jnp.zeros((bp, 2, D_RNN), _F32)
    ks, vs, hs = [], [], []
    for l in range(DEPTH):
        mod3 = mod[l].reshape(COND_ROWS, 1, 3 * D_MODEL)
        wg, bg = _gate_weights(rg_wa[l], rg_ba[l], rg_wx[l], rg_bx[l])

        q, k, v, rest = _project(xp, mod3, w_in_bf, l, ctx_row, tm_proj, None, _F32)
        o_attn = _context_attention(q, k, v, attn_sink[l], bp, s_len)
        o_rnn, h_fin = _rnn(rest, h0_zero, conv_w[l], conv_b[l], wg, bg, rg_lambda[l], bp, s_len)
        xp = _merge_residual(xp, mod3, o_attn, o_rnn, rest, wa_bf, wr_bf, wo_bf, l,
                             ln_g[l], ln_b[l], ctx_row, tm_out)
        ks.append(k.reshape(bp, s_len, N_KV_HEADS, HEAD_DIM))
        vs.append(v.reshape(bp, s_len, N_KV_HEADS, HEAD_DIM))
        hs.append(h_fin)

        q, k, v, rest = _project(xs, mod3, w_in_bf, l, lat_row_proj, tm_proj, rope_tabs, _BF16)
        o_attn = _latent_attention(q, k, v, cache_k4, cache_v4, l, attn_sink[l], bd, t_len)
        o_rnn, _ = _rnn(rest, state_h[:, l], conv_w[l], conv_b[l], wg, bg, rg_lambda[l], bd, t_len)
        xs = _merge_residual(xs, mod3, o_attn, o_rnn, rest, wa_bf, wr_bf, wo_bf, l,
                             ln_g[l], ln_b[l], lat_row_out, tm_out)

    y_prompt = xp.reshape(bp, s_len, D_MODEL)
    y_sample = xs.reshape(bd, t_len, D_MODEL)
    return (y_prompt, y_sample, jnp.stack(ks, axis=1), jnp.stack(vs, axis=1), jnp.stack(hs, axis=1))
```

```python
import functools

import jax
import jax.numpy as jnp
from jax import lax
from jax.experimental import pallas as pl
from jax.experimental.pallas import tpu as pltpu

D_MODEL = 1024
DEPTH = 2
GRID_W = 64
N_HEADS = 8
N_KV_HEADS = 2
HEAD_DIM = 128
Q_GROUP = N_HEADS // N_KV_HEADS
D_ATTN = N_HEADS * HEAD_DIM
D_KV = N_KV_HEADS * HEAD_DIM
WINDOW = 128
Q_BLOCK = 128
D_RNN = D_MODEL
N_RNN_BLOCKS = 8
RNN_BLOCK = D_RNN // N_RNN_BLOCKS
CONV_W = 4
CONV_LEFT = 2
LRU_C = 8.0
ROPE_BASE = 10000.0
D_IN = 2 * D_ATTN + 2 * D_KV + 2 * D_RNN + 2 * D_MODEL
DEEPNORM_ALPHA = (2 * DEPTH) ** 0.25
LN_EPS = 1e-6
NEG_INF = -1e30
ATTN_SCALE = HEAD_DIM ** -0.5

LANES = 128
SUBLANES = 8
SEQ_SLOTS = SUBLANES
SLOT_PAD = SUBLANES
PASS1_ROWS = 1024
COND_ROWS = 8
PROJ_TN = 512
N_PROJ_BLOCKS = D_IN // PROJ_TN
D_REST = D_IN - D_ATTN - 2 * D_KV
VMEM_LIMIT = 48 * 1024 * 1024
PROJ_VMEM_LIMIT = 56 * 1024 * 1024
RNN_VMEM_LIMIT = 56 * 1024 * 1024
F32_TINY = 1.1754943508222875e-38
LOG2_E = 1.4426950408889634

_BF16 = jnp.bfloat16
_F32 = jnp.float32


def _silu(x):
    return x * jax.nn.sigmoid(x)


def _layer_norm_rows(x):
    mu = jnp.mean(x, axis=-1, keepdims=True)
    xc = x - mu
    var = jnp.mean(xc * xc, axis=-1, keepdims=True)
    return xc * lax.rsqrt(var + LN_EPS)


def _mod_kernel(cond_ref, w_ref, b_ref, o_ref):
    a = _silu(cond_ref[...]).astype(_BF16)
    o_ref[...] = jnp.dot(a, w_ref[...].astype(_BF16), preferred_element_type=_F32) + b_ref[...]


def _modulation(cond8, w_mod, b_mod):
    tn = D_MODEL
    return pl.pallas_call(
        _mod_kernel,
        grid=(DEPTH, 3 * D_MODEL // tn),
        in_specs=[
            pl.BlockSpec((COND_ROWS, D_MODEL), lambda l, j: (0, 0)),
            pl.BlockSpec((None, D_MODEL, tn), lambda l, j: (l, 0, j)),
            pl.BlockSpec((None, 1, tn), lambda l, j: (l, 0, j)),
        ],
        out_specs=pl.BlockSpec((None, COND_ROWS, tn), lambda l, j: (l, 0, j)),
        out_shape=jax.ShapeDtypeStruct((DEPTH, COND_ROWS, 3 * D_MODEL), _F32),
        compiler_params=pltpu.CompilerParams(dimension_semantics=("parallel", "parallel")),
        name="modulation",
    )(cond8, w_mod, b_mod.reshape(DEPTH, 1, 3 * D_MODEL))


def _rope(x, cos, sin_signed):
    lane = lax.broadcasted_iota(jnp.int32, x.shape, 1)
    partner = jnp.where((lane // 32) % 2 == 0,
                        pltpu.roll(x, 3 * 32, axis=1),
                        pltpu.roll(x, 32, axis=1))
    return x * cos + partner * sin_signed


def _proj_kernel(x_ref, shift_ref, scale_ref, w_ref, *rest, rotary):
    if rotary:
        cos_ref, sin_ref, q_ref, k_ref, v_ref, r_ref = rest
    else:
        q_ref, k_ref, v_ref, r_ref = rest
    y = _layer_norm_rows(x_ref[...])
    h = (y * (1.0 + scale_ref[...]) + shift_ref[...]).astype(_BF16)

    def maybe_rope(blk):
        return _rope(blk, cos_ref[...], sin_ref[...]) if rotary else blk

    for j in range(N_PROJ_BLOCKS):
        c0 = j * PROJ_TN
        acc = jnp.dot(h, w_ref[:, c0:c0 + PROJ_TN], preferred_element_type=_F32)
        if c0 < D_ATTN:
            for hh in range(PROJ_TN // HEAD_DIM):
                sl = slice(hh * HEAD_DIM, (hh + 1) * HEAD_DIM)
                q_ref[:, c0 + hh * HEAD_DIM:c0 + (hh + 1) * HEAD_DIM] = (
                    maybe_rope(acc[:, sl]).astype(q_ref.dtype))
        elif c0 == D_ATTN:
            for hh in range(N_KV_HEADS):
                sl = slice(hh * HEAD_DIM, (hh + 1) * HEAD_DIM)
                k_ref[:, sl] = maybe_rope(acc[:, sl]).astype(k_ref.dtype)
            v_ref[...] = acc[:, D_KV:].astype(v_ref.dtype)
        else:
            r0 = c0 - D_ATTN - 2 * D_KV
            r_ref[:, r0:r0 + PROJ_TN] = acc.astype(r_ref.dtype)


def _project(x2, mod3, w_in_bf, layer, cond_row_of_block, tm, rope_tabs, kv_dtype):
    n = x2.shape[0]
    rotary = rope_tabs is not None
    in_specs = [
        pl.BlockSpec((tm, D_MODEL), lambda i: (i, 0)),
        pl.BlockSpec((None, 1, D_MODEL), lambda i: (cond_row_of_block(i), 0, 0)),
        pl.BlockSpec((None, 1, D_MODEL), lambda i: (cond_row_of_block(i), 0, 1)),
        pl.BlockSpec((None, D_MODEL, D_IN), lambda i: (layer, 0, 0), pipeline_mode=pl.Buffered(1)),
    ]
    args = [x2, mod3, mod3, w_in_bf]
    if rotary:
        cos, sin = rope_tabs
        blocks_per_seq = cos.shape[0] // tm
        in_specs += [pl.BlockSpec((tm, HEAD_DIM), lambda i: (i % blocks_per_seq, 0))] * 2
        args += [cos, sin]
    return pl.pallas_call(
        functools.partial(_proj_kernel, rotary=rotary),
        grid=(n // tm,),
        in_specs=in_specs,
        out_specs=[
            pl.BlockSpec((tm, D_ATTN), lambda i: (i, 0)),
            pl.BlockSpec((tm, D_KV), lambda i: (i, 0)),
            pl.BlockSpec((tm, D_KV), lambda i: (i, 0)),
            pl.BlockSpec((tm, D_REST), lambda i: (i, 0)),
        ],
        out_shape=[
            jax.ShapeDtypeStruct((n, D_ATTN), _BF16),
            jax.ShapeDtypeStruct((n, D_KV), kv_dtype),
            jax.ShapeDtypeStruct((n, D_KV), kv_dtype),
            jax.ShapeDtypeStruct((n, D_REST), _BF16),
        ],
        compiler_params=pltpu.CompilerParams(
            dimension_semantics=("parallel",), vmem_limit_bytes=PROJ_VMEM_LIMIT),
        name="proj_rope" if rotary else "proj",
    )(*args)


def _softmax_pv(s, sink_col, v_bf):
    m = jnp.maximum(jnp.max(s, axis=-1, keepdims=True), sink_col)
    e = jnp.exp(s - m)
    denom = jnp.sum(e, axis=-1, keepdims=True) + jnp.exp(sink_col - m)
    o = jnp.dot(e.astype(_BF16), v_bf, preferred_element_type=_F32)
    return o / denom


def _sink_column(sink_ref, kvh, rows):
    head = lax.broadcasted_iota(jnp.int32, (Q_GROUP * rows, 1), 0) // rows
    col = jnp.full((Q_GROUP * rows, 1), sink_ref[kvh * Q_GROUP], _F32)
    for g in range(1, Q_GROUP):
        col = jnp.where(head == g, sink_ref[kvh * Q_GROUP + g], col)
    return col


def _stack_heads(q_rows):
    return jnp.concatenate(
        [q_rows[:, g * HEAD_DIM:(g + 1) * HEAD_DIM] for g in range(Q_GROUP)], axis=0)


def _ctx_attn_kernel(sink_ref, q_ref, k_ref, v_ref, o_ref):
    kvh = pl.program_id(1)
    s_len = q_ref.shape[0]
    q = _stack_heads(q_ref[...])
    k = k_ref[...].astype(_BF16)
    v = v_ref[...].astype(_BF16)
    s = lax.dot_general(q, k, (((1,), (1,)), ((), ())), preferred_element_type=_F32) * ATTN_SCALE
    o = _softmax_pv(s, _sink_column(sink_ref, kvh, s_len), v)
    for g in range(Q_GROUP):
        o_ref[:, g * HEAD_DIM:(g + 1) * HEAD_DIM] = o[g * s_len:(g + 1) * s_len].astype(o_ref.dtype)


def _context_attention(q, k, v, sink, batch, s_len):
    n = batch * s_len
    gw = Q_GROUP * HEAD_DIM
    return pl.pallas_call(
        _ctx_attn_kernel,
        grid=(batch, N_KV_HEADS),
        in_specs=[
            pl.BlockSpec(memory_space=pltpu.SMEM),
            pl.BlockSpec((s_len, gw), lambda b, h: (b, h)),
            pl.BlockSpec((s_len, HEAD_DIM), lambda b, h: (b, h)),
            pl.BlockSpec((s_len, HEAD_DIM), lambda b, h: (b, h)),
        ],
        out_specs=pl.BlockSpec((s_len, gw), lambda b, h: (b, h)),
        out_shape=jax.ShapeDtypeStruct((n, D_ATTN), _BF16),
        compiler_params=pltpu.CompilerParams(dimension_semantics=("parallel", "parallel")),
        name="ctx_attention",
    )(sink, q, k, v)


def _lat_attn_kernel(sink_ref, q_ref, k_ref, v_ref, ck_ref, cv_ref, o_ref):
    kvh = pl.program_id(1)
    t_len = q_ref.shape[0]
    nb = t_len // Q_BLOCK
    ck = ck_ref[...].astype(_BF16)
    cv = cv_ref[...].astype(_BF16)
    sink_col = _sink_column(sink_ref, kvh, Q_BLOCK)
    rows = Q_GROUP * Q_BLOCK
    r = lax.broadcasted_iota(jnp.int32, (rows, Q_BLOCK), 0) % Q_BLOCK
    c = lax.broadcasted_iota(jnp.int32, (rows, Q_BLOCK), 1)
    for jb in range(nb):
        r0 = jb * Q_BLOCK
        lo = max(r0 - WINDOW, 0)
        hi = min(r0 + Q_BLOCK + WINDOW, t_len)
        q = _stack_heads(q_ref[r0:r0 + Q_BLOCK, :])
        keys = jnp.concatenate([k_ref[lo:hi, :], ck], axis=0)
        vals = jnp.concatenate([v_ref[lo:hi, :], cv], axis=0)
        s = lax.dot_general(q, keys, (((1,), (1,)), ((), ())), preferred_element_type=_F32) * ATTN_SCALE
        parts = []
        off = 0
        if jb > 0:
            parts.append(jnp.where(c >= r, s[:, :Q_BLOCK], NEG_INF))
            off = Q_BLOCK
        parts.append(s[:, off:off + Q_BLOCK])
        off += Q_BLOCK
        if jb < nb - 1:
            parts.append(jnp.where(c <= r, s[:, off:off + Q_BLOCK], NEG_INF))
            off += Q_BLOCK
        parts.append(s[:, off:])
        o = _softmax_pv(jnp.concatenate(parts, axis=1), sink_col, vals)
        for g in range(Q_GROUP):
            o_ref[r0:r0 + Q_BLOCK, g * HEAD_DIM:(g + 1) * HEAD_DIM] = (
                o[g * Q_BLOCK:(g + 1) * Q_BLOCK].astype(o_ref.dtype))


def _latent_attention(q, k, v, cache_k4, cache_v4, layer, sink, batch, t_len):
    n = batch * t_len
    gw = Q_GROUP * HEAD_DIM
    past = cache_k4.shape[2]
    cache_spec = pl.BlockSpec((None, None, past, HEAD_DIM), lambda b, h: (b, layer, 0, h))
    return pl.pallas_call(
        _lat_attn_kernel,
        grid=(batch, N_KV_HEADS),
        in_specs=[
            pl.BlockSpec(memory_space=pltpu.SMEM),
            pl.BlockSpec((t_len, gw), lambda b, h: (b, h)),
            pl.BlockSpec((t_len, HEAD_DIM), lambda b, h: (b, h)),
            pl.BlockSpec((t_len, HEAD_DIM), lambda b, h: (b, h)),
            cache_spec,
            cache_spec,
        ],
        out_specs=pl.BlockSpec((t_len, gw), lambda b, h: (b, h)),
        out_shape=jax.ShapeDtypeStruct((n, D_ATTN), _BF16),
        compiler_params=pltpu.CompilerParams(
            dimension_semantics=("parallel", "parallel"), vmem_limit_bytes=VMEM_LIMIT),
        name="latent_attention",
    )(sink, q, k, v, cache_k4, cache_v4)


def _shift_rows(x, d, t_len):
    row = lax.broadcasted_iota(jnp.int32, x.shape, 0) % t_len
    rolled = pltpu.roll(x, d % x.shape[0], axis=0)
    valid = (row >= d) if d > 0 else (row < t_len + d)
    return jnp.where(valid, rolled, 0.0)


def _sqrt_nonneg(z):
    return z * lax.rsqrt(jnp.maximum(z, F32_TINY))


def _rnn_kernel(x_ref, cw_ref, cb_ref, wg_ref, bg_ref, lam_ref, h0_ref, y_ref, hl_ref,
                af_scr, bf_scr, ab_scr, bb_scr, hf_scr, hb_scr, *, nb, nc):
    t_len = x_ref.shape[1]
    pitch = t_len + SLOT_PAD
    scr = ((af_scr, bf_scr), (ab_scr, bb_scr))
    kb = min(nb, max(1, PASS1_ROWS // t_len))

    for cl in range(nc):
        lanes = slice(cl * LANES, (cl + 1) * LANES)
        cw = cw_ref[:, lanes]
        cb = cb_ref[:, lanes]
        wg = wg_ref[cl]
        bg = bg_ref[cl]
        lam = lam_ref[:, lanes]
        c_softplus = LRU_C * (jnp.maximum(-lam, 0.0) + jnp.log1p(jnp.exp(-jnp.abs(lam))))

        def per_chunk(it, carry, lanes=lanes, cw=cw, cb=cb, wg=wg, bg=bg,
                      c_softplus=c_softplus, cl=cl):
            x = x_ref[pl.ds(it * kb, kb), :, lanes].astype(_F32).reshape(kb * t_len, LANES)
            xc = cw[0:1] * _shift_rows(x, 2, t_len)
            xc = xc + cw[1:2] * _shift_rows(x, 1, t_len)
            xc = xc + cw[2:3] * x
            xc = xc + cw[3:4] * _shift_rows(x, -1, t_len)
            xc = xc + cb
            half_gates = jnp.dot(xc.astype(_BF16), wg, preferred_element_type=_F32) + bg
            half_xc = 0.5 * xc
            for d in range(2):
                t_r = jnp.tanh(half_gates[:, (2 * d) * LANES:(2 * d + 1) * LANES])
                t_i = jnp.tanh(half_gates[:, (2 * d + 1) * LANES:(2 * d + 2) * LANES])
                k = 0.5 * c_softplus[d:d + 1]
                neg_log_a = t_r * k + k
                a = jnp.exp2(neg_log_a * -LOG2_E)
                bterm = _sqrt_nonneg(jnp.tanh(neg_log_a) * (a * a + 1.0)) * ((t_i + 1.0) * half_xc)
                for jb in range(kb):
                    row0 = pl.multiple_of((cl * nb + it * kb + jb) * pitch, SUBLANES)
                    scr[d][0][pl.ds(row0, t_len), :] = a[jb * t_len:(jb + 1) * t_len]
                    scr[d][1][pl.ds(row0, t_len), :] = bterm[jb * t_len:(jb + 1) * t_len]
            return carry

        lax.fori_loop(0, nb // kb, per_chunk, 0)

    def slot_rows(ref3, d):
        rows = [ref3[b, d:d + 1, cl * LANES:(cl + 1) * LANES] for cl in range(nc) for b in range(nb)]
        return jnp.concatenate(rows, axis=0)

    def advance2(a_scr, b_scr, h_scr, t0, t1, h):
        r0 = pl.ds(t0, SEQ_SLOTS, stride=pitch)
        r1 = pl.ds(t1, SEQ_SLOTS, stride=pitch)
        a0, b0, a1, b1 = a_scr[r0, :], b_scr[r0, :], a_scr[r1, :], b_scr[r1, :]
        h_scr[r0, :] = a0 * h + b0
        h = (a1 * a0) * h + (a1 * b0 + b1)
        h_scr[r1, :] = h
        return h

    hf, hb = slot_rows(h0_ref, 0), slot_rows(h0_ref, 1)
    for t in range(0, t_len, 2):
        hf = advance2(af_scr, bf_scr, hf_scr, t, t + 1, hf)
        hb = advance2(ab_scr, bb_scr, hb_scr, t_len - 1 - t, t_len - 2 - t, hb)

    for cl in range(nc):
        lanes = slice(cl * LANES, (cl + 1) * LANES)
        for b in range(nb):
            s = cl * nb + b
            hl_ref[b, 0:1, lanes] = hf[s:s + 1, :]
            hl_ref[b, 1:2, lanes] = hb[s:s + 1, :]
            rows = pl.ds(s * pitch, t_len)
            y_ref[b, :, lanes] = (hf_scr[rows, :] + hb_scr[rows, :]).astype(y_ref.dtype)


def _rnn(rest, h0, conv_w, conv_b, wg, bg, lam, batch, t_len):
    nb = min(batch, SEQ_SLOTS)
    nc = SEQ_SLOTS // nb
    cw = nc * LANES
    xr_block0 = D_MODEL // cw
    scratch = pltpu.VMEM((SEQ_SLOTS * (t_len + SLOT_PAD), LANES), _F32)
    y, h_last = pl.pallas_call(
        functools.partial(_rnn_kernel, nb=nb, nc=nc),
        grid=(batch // nb, D_RNN // cw),
        in_specs=[
            pl.BlockSpec((nb, t_len, cw), lambda g, c: (g, 0, xr_block0 + c)),
            pl.BlockSpec((CONV_W, cw), lambda g, c: (0, c)),
            pl.BlockSpec((1, cw), lambda g, c: (0, c)),
            pl.BlockSpec((nc, RNN_BLOCK, 4 * LANES), lambda g, c: (c, 0, 0)),
            pl.BlockSpec((nc, 1, 4 * LANES), lambda g, c: (c, 0, 0)),
            pl.BlockSpec((2, cw), lambda g, c: (0, c)),
            pl.BlockSpec((nb, 2, cw), lambda g, c: (g, 0, c)),
        ],
        out_specs=[
            pl.BlockSpec((nb, t_len, cw), lambda g, c: (g, 0, c)),
            pl.BlockSpec((nb, 2, cw), lambda g, c: (g, 0, c)),
        ],
        out_shape=[
            jax.ShapeDtypeStruct((batch, t_len, D_RNN), _BF16),
            jax.ShapeDtypeStruct((batch, 2, D_RNN), _F32),
        ],
        scratch_shapes=[scratch] * 6,
        compiler_params=pltpu.CompilerParams(
            dimension_semantics=("parallel", "parallel"), vmem_limit_bytes=RNN_VMEM_LIMIT),
        name="rnn",
    )(rest.reshape(batch, t_len, D_REST), conv_w, conv_b.reshape(1, D_RNN), wg, bg, lam, h0)
    return y.reshape(batch * t_len, D_RNN), h_last


def _out_kernel(x_ref, gate_ref, oa_ref, ga_ref, or_ref, gr_ref, mga_ref, mgr_ref,
                wa_ref, wr_ref, wo_ref, g_ref, b_ref, o_ref):
    f32 = lambda ref: ref[...].astype(_F32)
    ya = jnp.dot((f32(oa_ref) * _silu(f32(ga_ref))).astype(_BF16), wa_ref[...],
                 preferred_element_type=_F32)
    yr = jnp.dot((f32(or_ref) * _silu(f32(gr_ref))).astype(_BF16), wr_ref[...],
                 preferred_element_type=_F32)
    merged = jax.nn.sigmoid(f32(mga_ref)) * ya + jax.nn.sigmoid(f32(mgr_ref)) * yr
    out = jnp.dot(merged.astype(_BF16), wo_ref[...], preferred_element_type=_F32)
    z = DEEPNORM_ALPHA * x_ref[...] + gate_ref[...] * out
    o_ref[...] = _layer_norm_rows(z) * g_ref[...] + b_ref[...]


def _merge_residual(x2, mod3, o_attn, o_rnn, rest, wa_bf, wr_bf, wo_bf, layer, ln_g, ln_b,
                    cond_row_of_block, tm):
    n = x2.shape[0]
    row_blk = lambda col: pl.BlockSpec((tm, D_MODEL), lambda i, col=col: (i, col))
    whole = pl.BlockSpec((None, D_MODEL, D_MODEL), lambda i: (layer, 0, 0))
    vec = pl.BlockSpec((1, D_MODEL), lambda i: (0, 0))
    return pl.pallas_call(
        _out_kernel,
        grid=(n // tm,),
        in_specs=[
            row_blk(0),
            pl.BlockSpec((None, 1, D_MODEL), lambda i: (cond_row_of_block(i), 0, 2)),
            row_blk(0), row_blk(0), row_blk(0), row_blk(2), row_blk(3), row_blk(4),
            whole, whole, whole, vec, vec,
        ],
        out_specs=row_blk(0),
        out_shape=jax.ShapeDtypeStruct((n, D_MODEL), _F32),
        compiler_params=pltpu.CompilerParams(
            dimension_semantics=("parallel",), vmem_limit_bytes=VMEM_LIMIT),
        name="merge_residual",
    )(x2, mod3, o_attn, rest, o_rnn, rest, rest, rest, wa_bf, wr_bf, wo_bf,
      ln_g.reshape(1, D_MODEL), ln_b.reshape(1, D_MODEL))


def _rope_tables(t_len):
    quarter = HEAD_DIM // 4
    inv_freq = ROPE_BASE ** (-jnp.arange(quarter, dtype=_F32) / quarter)
    pos = jnp.arange(t_len)
    ang_r = (pos // GRID_W).astype(_F32)[:, None] * inv_freq[None, :]
    ang_c = (pos % GRID_W).astype(_F32)[:, None] * inv_freq[None, :]
    cos = jnp.concatenate([jnp.cos(ang_r)] * 2 + [jnp.cos(ang_c)] * 2, axis=1)
    sin = jnp.concatenate([-jnp.sin(ang_r), jnp.sin(ang_r), -jnp.sin(ang_c), jnp.sin(ang_c)], axis=1)
    return cos, sin


def _gate_weights(rg_wa, rg_ba, rg_wx, rg_bx):
    wg = (0.5 * jnp.concatenate([rg_wa[0], rg_wx[0], rg_wa[1], rg_wx[1]], axis=-1)).astype(_BF16)
    blk = lambda v: v.reshape(N_RNN_BLOCKS, 1, RNN_BLOCK)
    bg = 0.5 * jnp.concatenate([blk(rg_ba[0]), blk(rg_bx[0]), blk(rg_ba[1]), blk(rg_bx[1])], axis=-1)
    return wg, bg


def kernel(x_prompt, x_sample, cache_k, cache_v, state_h, c, c_ctx, w_mod, b_mod, w_in,
           attn_sink, conv_w, conv_b, rg_wa, rg_ba, rg_wx, rg_bx, rg_lambda,
           w_br_attn, w_br_rnn, w_out, ln_g, ln_b):
    bp, s_len, _ = x_prompt.shape
    bd, t_len, _ = x_sample.shape
    past = cache_k.shape[2]

    cond8 = jnp.zeros((COND_ROWS, D_MODEL), _F32).at[0].set(c_ctx).at[1:1 + bd].set(c)
    mod = _modulation(cond8, w_mod, b_mod)

    w_in_bf = w_in.astype(_BF16)
    wa_bf = w_br_attn.astype(_BF16)
    wr_bf = w_br_rnn.astype(_BF16)
    wo_bf = w_out.astype(_BF16)
    rope_tabs = _rope_tables(t_len)
    cache_k4 = cache_k.reshape(bd, DEPTH, past, D_KV)
    cache_v4 = cache_v.reshape(bd, DEPTH, past, D_KV)

    tm_proj, tm_out = 512, 512
    ctx_row = lambda i: 0
    lat_row_proj = lambda i: 1 + (i * tm_proj) // t_len
    lat_row_out = lambda i: 1 + (i * tm_out) // t_len

    xp = x_prompt.reshape(bp * s_len, D_MODEL)
    xs = x_sample.reshape(bd * t_len, D_MODEL)
    h0_zero = jnp.zeros((bp, 2, D_RNN), _F32)
    ks, vs, hs = [], [], []
    for l in range(DEPTH):
        mod3 = mod[l].reshape(COND_ROWS, 1, 3 * D_MODEL)
        wg, bg = _gate_weights(rg_wa[l], rg_ba[l], rg_wx[l], rg_bx[l])

        q, k, v, rest = _project(xp, mod3, w_in_bf, l, ctx_row, tm_proj, None, _F32)
        o_attn = _context_attention(q, k, v, attn_sink[l], bp, s_len)
        o_rnn, h_fin = _rnn(rest, h0_zero, conv_w[l], conv_b[l], wg, bg, rg_lambda[l], bp, s_len)
        xp = _merge_residual(xp, mod3, o_attn, o_rnn, rest, wa_bf, wr_bf, wo_bf, l,
                             ln_g[l], ln_b[l], ctx_row, tm_out)
        ks.append(k.reshape(bp, s_len, N_KV_HEADS, HEAD_DIM))
        vs.append(v.reshape(bp, s_len, N_KV_HEADS, HEAD_DIM))
        hs.append(h_fin)

        q, k, v, rest = _project(xs, mod3, w_in_bf, l, lat_row_proj, tm_proj, rope_tabs, _BF16)
        o_attn = _latent_attention(q, k, v, cache_k4, cache_v4, l, attn_sink[l], bd, t_len)
        o_rnn, _ = _rnn(rest, state_h[:, l], conv_w[l], conv_b[l], wg, bg, rg_lambda[l], bd, t_len)
        xs = _merge_residual(xs, mod3, o_attn, o_rnn, rest, wa_bf, wr_bf, wo_bf, l,
                             ln_g[l], ln_b[l], lat_row_out, tm_out)

    y_prompt = xp.reshape(bp, s_len, D_MODEL)
    y_sample = xs.reshape(bd, t_len, D_MODEL)
    return (y_prompt, y_sample, jnp.stack(ks, axis=1), jnp.stack(vs, axis=1), jnp.stack(hs, axis=1))
```

```python
import functools

import jax
import jax.numpy as jnp
from jax import lax
from jax.experimental import pallas as pl
from jax.experimental.pallas import tpu as pltpu

D_MODEL = 1024
DEPTH = 2
GRID_W = 64
N_HEADS = 8
N_KV_HEADS = 2
HEAD_DIM = 128
Q_GROUP = N_HEADS // N_KV_HEADS
D_ATTN = N_HEADS * HEAD_DIM
D_KV = N_KV_HEADS * HEAD_DIM
WINDOW = 128
Q_BLOCK = 128
D_RNN = D_MODEL
N_RNN_BLOCKS = 8
RNN_BLOCK = D_RNN // N_RNN_BLOCKS
CONV_W = 4
CONV_LEFT = 2
LRU_C = 8.0
ROPE_BASE = 10000.0
D_IN = 2 * D_ATTN + 2 * D_KV + 2 * D_RNN + 2 * D_MODEL
DEEPNORM_ALPHA = (2 * DEPTH) ** 0.25
LN_EPS = 1e-6
NEG_INF = -1e30
ATTN_SCALE = HEAD_DIM ** -0.5

LANES = 128
SUBLANES = 8
SEQ_SLOTS = SUBLANES
SLOT_PAD = SUBLANES
CTX_ATTN_SEQS = 4
PASS1_ROWS = 1024
COND_ROWS = 8
PROJ_TN = 512
N_PROJ_BLOCKS = D_IN // PROJ_TN
D_REST = D_IN - D_ATTN - 2 * D_KV
VMEM_LIMIT = 48 * 1024 * 1024
PROJ_VMEM_LIMIT = 56 * 1024 * 1024
RNN_VMEM_LIMIT = 56 * 1024 * 1024
F32_TINY = 1.1754943508222875e-38
LOG2_E = 1.4426950408889634
Q_PRESCALE = ATTN_SCALE * LOG2_E

_BF16 = jnp.bfloat16
_F32 = jnp.float32


def _silu(x):
    return x * jax.nn.sigmoid(x)


def _layer_norm_rows(x):
    mu = jnp.mean(x, axis=-1, keepdims=True)
    xc = x - mu
    var = jnp.mean(xc * xc, axis=-1, keepdims=True)
    return xc * lax.rsqrt(var + LN_EPS)


def _mod_kernel(cond_ref, w_ref, b_ref, o_ref):
    a = _silu(cond_ref[...]).astype(_BF16)
    o_ref[...] = jnp.dot(a, w_ref[...].astype(_BF16), preferred_element_type=_F32) + b_ref[...]


def _modulation(cond8, w_mod, b_mod):
    tn = D_MODEL
    return pl.pallas_call(
        _mod_kernel,
        grid=(DEPTH, 3 * D_MODEL // tn),
        in_specs=[
            pl.BlockSpec((COND_ROWS, D_MODEL), lambda l, j: (0, 0)),
            pl.BlockSpec((None, D_MODEL, tn), lambda l, j: (l, 0, j)),
            pl.BlockSpec((None, 1, tn), lambda l, j: (l, 0, j)),
        ],
        out_specs=pl.BlockSpec((None, COND_ROWS, tn), lambda l, j: (l, 0, j)),
        out_shape=jax.ShapeDtypeStruct((DEPTH, COND_ROWS, 3 * D_MODEL), _F32),
        compiler_params=pltpu.CompilerParams(dimension_semantics=("parallel", "parallel")),
        name="modulation",
    )(cond8, w_mod, b_mod.reshape(DEPTH, 1, 3 * D_MODEL))


def _rope(x, cos, sin_signed):
    lane = lax.broadcasted_iota(jnp.int32, x.shape, 1)
    partner = jnp.where((lane // 32) % 2 == 0,
                        pltpu.roll(x, 3 * 32, axis=1),
                        pltpu.roll(x, 32, axis=1))
    return x * cos + partner * sin_signed


def _proj_kernel(x_ref, shift_ref, scale_ref, w_ref, *rest, rotary):
    if rotary:
        cos_ref, sin_ref, q_ref, k_ref, v_ref, r_ref = rest
    else:
        q_ref, k_ref, v_ref, r_ref = rest
    y = _layer_norm_rows(x_ref[...])
    h = (y * (1.0 + scale_ref[...]) + shift_ref[...]).astype(_BF16)

    def maybe_rope(blk):
        return _rope(blk, cos_ref[...], sin_ref[...]) if rotary else blk

    for j in range(N_PROJ_BLOCKS):
        c0 = j * PROJ_TN
        acc = jnp.dot(h, w_ref[:, c0:c0 + PROJ_TN], preferred_element_type=_F32)
        if c0 < D_ATTN:
            for hh in range(PROJ_TN // HEAD_DIM):
                sl = slice(hh * HEAD_DIM, (hh + 1) * HEAD_DIM)
                q_ref[:, c0 + hh * HEAD_DIM:c0 + (hh + 1) * HEAD_DIM] = (
                    (maybe_rope(acc[:, sl]) * Q_PRESCALE).astype(q_ref.dtype))
        elif c0 == D_ATTN:
            for hh in range(N_KV_HEADS):
                sl = slice(hh * HEAD_DIM, (hh + 1) * HEAD_DIM)
                k_ref[:, sl] = maybe_rope(acc[:, sl]).astype(k_ref.dtype)
            v_ref[...] = acc[:, D_KV:].astype(v_ref.dtype)
        else:
            r0 = c0 - D_ATTN - 2 * D_KV
            r_ref[:, r0:r0 + PROJ_TN] = acc.astype(r_ref.dtype)


def _project(x2, mod3, w_in_bf, layer, cond_row_of_block, tm, rope_tabs, kv_dtype):
    n = x2.shape[0]
    rotary = rope_tabs is not None
    in_specs = [
        pl.BlockSpec((tm, D_MODEL), lambda i: (i, 0)),
        pl.BlockSpec((None, 1, D_MODEL), lambda i: (cond_row_of_block(i), 0, 0)),
        pl.BlockSpec((None, 1, D_MODEL), lambda i: (cond_row_of_block(i), 0, 1)),
        pl.BlockSpec((None, D_MODEL, D_IN), lambda i: (layer, 0, 0), pipeline_mode=pl.Buffered(1)),
    ]
    args = [x2, mod3, mod3, w_in_bf]
    if rotary:
        cos, sin = rope_tabs
        blocks_per_seq = cos.shape[0] // tm
        in_specs += [pl.BlockSpec((tm, HEAD_DIM), lambda i: (i % blocks_per_seq, 0))] * 2
        args += [cos, sin]
    return pl.pallas_call(
        functools.partial(_proj_kernel, rotary=rotary),
        grid=(n // tm,),
        in_specs=in_specs,
        out_specs=[
            pl.BlockSpec((tm, D_ATTN), lambda i: (i, 0)),
            pl.BlockSpec((tm, D_KV), lambda i: (i, 0)),
            pl.BlockSpec((tm, D_KV), lambda i: (i, 0)),
            pl.BlockSpec((tm, D_REST), lambda i: (i, 0)),
        ],
        out_shape=[
            jax.ShapeDtypeStruct((n, D_ATTN), _BF16),
            jax.ShapeDtypeStruct((n, D_KV), kv_dtype),
            jax.ShapeDtypeStruct((n, D_KV), kv_dtype),
            jax.ShapeDtypeStruct((n, D_REST), _BF16),
        ],
        compiler_params=pltpu.CompilerParams(
            dimension_semantics=("parallel",), vmem_limit_bytes=PROJ_VMEM_LIMIT),
        name="proj_rope" if rotary else "proj",
    )(*args)


def _softmax_pv(score_parts, sink_col, value_parts):
    m = sink_col
    for s in score_parts:
        m = jnp.maximum(m, jnp.max(s, axis=-1, keepdims=True))
    acc = None
    for s, v_aug in zip(score_parts, value_parts):
        part = jnp.dot(jnp.exp2(s - m).astype(_BF16), v_aug, preferred_element_type=_F32)
        acc = part if acc is None else acc + part
    denom = acc[:, HEAD_DIM:HEAD_DIM + 1] + jnp.exp2(sink_col - m)
    return acc[:, :HEAD_DIM] / denom


def _with_ones(v_bf):
    return jnp.concatenate([v_bf, jnp.ones_like(v_bf)], axis=1)


def _sink_column(sink_ref, kvh, rows):
    head = lax.broadcasted_iota(jnp.int32, (Q_GROUP * rows, 1), 0) // rows
    col = jnp.full((Q_GROUP * rows, 1), sink_ref[kvh * Q_GROUP], _F32)
    for g in range(1, Q_GROUP):
        col = jnp.where(head == g, sink_ref[kvh * Q_GROUP + g], col)
    return col * LOG2_E


def _stack_heads(q_rows):
    return jnp.concatenate(
        [q_rows[:, g * HEAD_DIM:(g + 1) * HEAD_DIM] for g in range(Q_GROUP)], axis=0)


def _qk(q, k_bf):
    return lax.dot_general(q, k_bf, (((1,), (1,)), ((), ())), preferred_element_type=_F32)


def _ctx_attn_kernel(sink_ref, q_ref, k_ref, v_ref, o_ref, *, s_len):
    kvh = pl.program_id(1)
    sink_col = _sink_column(sink_ref, kvh, s_len)

    def one_sequence(bi, carry):
        rows = pl.ds(pl.multiple_of(bi * s_len, s_len), s_len)
        q = _stack_heads(q_ref[rows, :])
        k = k_ref[rows, :].astype(_BF16)
        v_aug = _with_ones(v_ref[rows, :].astype(_BF16))
        o = _softmax_pv([_qk(q, k)], sink_col, [v_aug])
        for g in range(Q_GROUP):
            o_ref[rows, g * HEAD_DIM:(g + 1) * HEAD_DIM] = o[g * s_len:(g + 1) * s_len].astype(o_ref.dtype)
        return carry

    lax.fori_loop(0, q_ref.shape[0] // s_len, one_sequence, 0)


def _context_attention(q, k, v, sink, batch, s_len):
    n = batch * s_len
    tm = CTX_ATTN_SEQS * s_len
    gw = Q_GROUP * HEAD_DIM
    return pl.pallas_call(
        functools.partial(_ctx_attn_kernel, s_len=s_len),
        grid=(n // tm, N_KV_HEADS),
        in_specs=[
            pl.BlockSpec(memory_space=pltpu.SMEM),
            pl.BlockSpec((tm, gw), lambda b, h: (b, h)),
            pl.BlockSpec((tm, HEAD_DIM), lambda b, h: (b, h)),
            pl.BlockSpec((tm, HEAD_DIM), lambda b, h: (b, h)),
        ],
        out_specs=pl.BlockSpec((tm, gw), lambda b, h: (b, h)),
        out_shape=jax.ShapeDtypeStruct((n, D_ATTN), _BF16),
        compiler_params=pltpu.CompilerParams(dimension_semantics=("parallel", "parallel")),
        name="ctx_attention",
    )(sink, q, k, v)


def _lat_attn_kernel(sink_ref, q_ref, k_ref, v_ref, ck_ref, cv_ref, o_ref):
    kvh = pl.program_id(1)
    t_len = q_ref.shape[0]
    nb = t_len // Q_BLOCK
    ck = ck_ref[...].astype(_BF16)
    cv_aug = _with_ones(cv_ref[...].astype(_BF16))
    v_aug = _with_ones(v_ref[...])
    sink_col = _sink_column(sink_ref, kvh, Q_BLOCK)
    rows = Q_GROUP * Q_BLOCK
    r = lax.broadcasted_iota(jnp.int32, (rows, Q_BLOCK), 0) % Q_BLOCK
    c = lax.broadcasted_iota(jnp.int32, (rows, Q_BLOCK), 1)
    for jb in range(nb):
        r0 = jb * Q_BLOCK
        q = _stack_heads(q_ref[r0:r0 + Q_BLOCK, :])
        lo = max(r0 - WINDOW, 0)
        hi = min(r0 + Q_BLOCK + WINDOW, t_len)
        s = _qk(q, k_ref[lo:hi, :])
        parts = []
        off = 0
        if jb > 0:
            parts.append(jnp.where(c >= r, s[:, :Q_BLOCK], NEG_INF))
            off = Q_BLOCK
        parts.append(s[:, off:off + Q_BLOCK])
        off += Q_BLOCK
        if jb < nb - 1:
            parts.append(jnp.where(c <= r, s[:, off:], NEG_INF))
        s_local = jnp.concatenate(parts, axis=1)
        o = _softmax_pv([s_local, _qk(q, ck)], sink_col, [v_aug[lo:hi], cv_aug])
        for g in range(Q_GROUP):
            o_ref[r0:r0 + Q_BLOCK, g * HEAD_DIM:(g + 1) * HEAD_DIM] = (
                o[g * Q_BLOCK:(g + 1) * Q_BLOCK].astype(o_ref.dtype))


def _latent_attention(q, k, v, cache_k4, cache_v4, layer, sink, batch, t_len):
    n = batch * t_len
    gw = Q_GROUP * HEAD_DIM
    past = cache_k4.shape[2]
    cache_spec = pl.BlockSpec((None, None, past, HEAD_DIM), lambda b, h: (b, layer, 0, h))
    return pl.pallas_call(
        _lat_attn_kernel,
        grid=(batch, N_KV_HEADS),
        in_specs=[
            pl.BlockSpec(memory_space=pltpu.SMEM),
            pl.BlockSpec((t_len, gw), lambda b, h: (b, h)),
            pl.BlockSpec((t_len, HEAD_DIM), lambda b, h: (b, h)),
            pl.BlockSpec((t_len, HEAD_DIM), lambda b, h: (b, h)),
            cache_spec,
            cache_spec,
        ],
        out_specs=pl.BlockSpec((t_len, gw), lambda b, h: (b, h)),
        out_shape=jax.ShapeDtypeStruct((n, D_ATTN), _BF16),
        compiler_params=pltpu.CompilerParams(
            dimension_semantics=("parallel", "parallel"), vmem_limit_bytes=VMEM_LIMIT),
        name="latent_attention",
    )(sink, q, k, v, cache_k4, cache_v4)


def _shift_rows(x, d, t_len):
    row = lax.broadcasted_iota(jnp.int32, x.shape, 0) % t_len
    rolled = pltpu.roll(x, d % x.shape[0], axis=0)
    valid = (row >= d) if d > 0 else (row < t_len + d)
    return jnp.where(valid, rolled, 0.0)


def _sqrt_nonneg(z):
    return z * lax.rsqrt(jnp.maximum(z, F32_TINY))


def _rnn_kernel(x_ref, cw_ref, cb_ref, wg_ref, bg_ref, lam_ref, h0_ref, y_ref, hl_ref,
                af_scr, bf_scr, ab_scr, bb_scr, hf_scr, hb_scr, *, nb, nc):
    t_len = x_ref.shape[1]
    pitch = t_len + SLOT_PAD
    scr = ((af_scr, bf_scr), (ab_scr, bb_scr))
    kb = min(nb, max(1, PASS1_ROWS // t_len))

    for cl in range(nc):
        lanes = slice(cl * LANES, (cl + 1) * LANES)
        cw = cw_ref[:, lanes]
        cb = cb_ref[:, lanes]
        wg = wg_ref[cl]
        bg = bg_ref[cl]
        lam = lam_ref[:, lanes]
        c_softplus = LRU_C * (jnp.maximum(-lam, 0.0) + jnp.log1p(jnp.exp(-jnp.abs(lam))))

        def per_chunk(it, carry, lanes=lanes, cw=cw, cb=cb, wg=wg, bg=bg,
                      c_softplus=c_softplus, cl=cl):
            x = x_ref[pl.ds(it * kb, kb), :, lanes].astype(_F32).reshape(kb * t_len, LANES)
            xc = cw[0:1] * _shift_rows(x, 2, t_len)
            xc = xc + cw[1:2] * _shift_rows(x, 1, t_len)
            xc = xc + cw[2:3] * x
            xc = xc + cw[3:4] * _shift_rows(x, -1, t_len)
            xc = xc + cb
            half_gates = jnp.dot(xc.astype(_BF16), wg, preferred_element_type=_F32) + bg
            half_xc = 0.5 * xc
            for d in range(2):
                t_r = jnp.tanh(half_gates[:, (2 * d) * LANES:(2 * d + 1) * LANES])
                t_i = jnp.tanh(half_gates[:, (2 * d + 1) * LANES:(2 * d + 2) * LANES])
                k = 0.5 * c_softplus[d:d + 1]
                neg_log_a = t_r * k + k
                a = jnp.exp2(neg_log_a * -LOG2_E)
                bterm = _sqrt_nonneg(jnp.tanh(neg_log_a) * (a * a + 1.0)) * ((t_i + 1.0) * half_xc)
                for jb in range(kb):
                    row0 = pl.multiple_of((cl * nb + it * kb + jb) * pitch, SUBLANES)
                    scr[d][0][pl.ds(row0, t_len), :] = a[jb * t_len:(jb + 1) * t_len]
                    scr[d][1][pl.ds(row0, t_len), :] = bterm[jb * t_len:(jb + 1) * t_len]
            return carry

        lax.fori_loop(0, nb // kb, per_chunk, 0)

    def slot_rows(ref3, d):
        rows = [ref3[b, d:d + 1, cl * LANES:(cl + 1) * LANES] for cl in range(nc) for b in range(nb)]
        return jnp.concatenate(rows, axis=0)

    def advance2(a_scr, b_scr, h_scr, t0, t1, h):
        r0 = pl.ds(t0, SEQ_SLOTS, stride=pitch)
        r1 = pl.ds(t1, SEQ_SLOTS, stride=pitch)
        a0, b0, a1, b1 = a_scr[r0, :], b_scr[r0, :], a_scr[r1, :], b_scr[r1, :]
        h_scr[r0, :] = a0 * h + b0
        h = (a1 * a0) * h + (a1 * b0 + b1)
        h_scr[r1, :] = h
        return h

    hf, hb = slot_rows(h0_ref, 0), slot_rows(h0_ref, 1)
    for t in range(0, t_len, 2):
        hf = advance2(af_scr, bf_scr, hf_scr, t, t + 1, hf)
        hb = advance2(ab_scr, bb_scr, hb_scr, t_len - 1 - t, t_len - 2 - t, hb)

    for cl in range(nc):
        lanes = slice(cl * LANES, (cl + 1) * LANES)
        for b in range(nb):
            s = cl * nb + b
            hl_ref[b, 0:1, lanes] = hf[s:s + 1, :]
            hl_ref[b, 1:2, lanes] = hb[s:s + 1, :]
            rows = pl.ds(s * pitch, t_len)
            y_ref[b, :, lanes] = (hf_scr[rows, :] + hb_scr[rows, :]).astype(y_ref.dtype)


def _rnn(rest, h0, conv_w, conv_b, wg, bg, lam, batch, t_len):
    nb = min(batch, SEQ_SLOTS)
    nc = SEQ_SLOTS // nb
    cw = nc * LANES
    xr_block0 = D_MODEL // cw
    scratch = pltpu.VMEM((SEQ_SLOTS * (t_len + SLOT_PAD), LANES), _F32)
    y, h_last = pl.pallas_call(
        functools.partial(_rnn_kernel, nb=nb, nc=nc),
        grid=(batch // nb, D_RNN // cw),
        in_specs=[
            pl.BlockSpec((nb, t_len, cw), lambda g, c: (g, 0, xr_block0 + c)),
            pl.BlockSpec((CONV_W, cw), lambda g, c: (0, c)),
            pl.BlockSpec((1, cw), lambda g, c: (0, c)),
            pl.BlockSpec((nc, RNN_BLOCK, 4 * LANES), lambda g, c: (c, 0, 0)),
            pl.BlockSpec((nc, 1, 4 * LANES), lambda g, c: (c, 0, 0)),
            pl.BlockSpec((2, cw), lambda g, c: (0, c)),
            pl.BlockSpec((nb, 2, cw), lambda g, c: (g, 0, c)),
        ],
        out_specs=[
            pl.BlockSpec((nb, t_len, cw), lambda g, c: (g, 0, c)),
            pl.BlockSpec((nb, 2, cw), lambda g, c: (g, 0, c)),
        ],
        out_shape=[
            jax.ShapeDtypeStruct((batch, t_len, D_RNN), _BF16),
            jax.ShapeDtypeStruct((batch, 2, D_RNN), _F32),
        ],
        scratch_shapes=[scratch] * 6,
        compiler_params=pltpu.CompilerParams(
            dimension_semantics=("parallel", "parallel"), vmem_limit_bytes=RNN_VMEM_LIMIT),
        name="rnn",
    )(rest.reshape(batch, t_len, D_REST), conv_w, conv_b.reshape(1, D_RNN), wg, bg, lam, h0)
    return y.reshape(batch * t_len, D_RNN), h_last


def _out_kernel(x_ref, gate_ref, oa_ref, ga_ref, or_ref, gr_ref, mga_ref, mgr_ref,
                wa_ref, wr_ref, wo_ref, g_ref, b_ref, o_ref):
    f32 = lambda ref: ref[...].astype(_F32)
    ya = jnp.dot((f32(oa_ref) * _silu(f32(ga_ref))).astype(_BF16), wa_ref[...],
                 preferred_element_type=_F32)
    yr = jnp.dot((f32(or_ref) * _silu(f32(gr_ref))).astype(_BF16), wr_ref[...],
                 preferred_element_type=_F32)
    merged = jax.nn.sigmoid(f32(mga_ref)) * ya + jax.nn.sigmoid(f32(mgr_ref)) * yr
    out = jnp.dot(merged.astype(_BF16), wo_ref[...], preferred_element_type=_F32)
    z = DEEPNORM_ALPHA * x_ref[...] + gate_ref[...] * out
    o_ref[...] = _layer_norm_rows(z) * g_ref[...] + b_ref[...]


def _merge_residual(x2, mod3, o_attn, o_rnn, rest, wa_bf, wr_bf, wo_bf, layer, ln_g, ln_b,
                    cond_row_of_block, tm):
    n = x2.shape[0]
    row_blk = lambda col: pl.BlockSpec((tm, D_MODEL), lambda i, col=col: (i, col))
    whole = pl.BlockSpec((None, D_MODEL, D_MODEL), lambda i: (layer, 0, 0))
    vec = pl.BlockSpec((1, D_MODEL), lambda i: (0, 0))
    return pl.pallas_call(
        _out_kernel,
        grid=(n // tm,),
        in_specs=[
            row_blk(0),
            pl.BlockSpec((None, 1, D_MODEL), lambda i: (cond_row_of_block(i), 0, 2)),
            row_blk(0), row_blk(0), row_blk(0), row_blk(2), row_blk(3), row_blk(4),
            whole, whole, whole, vec, vec,
        ],
        out_specs=row_blk(0),
        out_shape=jax.ShapeDtypeStruct((n, D_MODEL), _F32),
        compiler_params=pltpu.CompilerParams(
            dimension_semantics=("parallel",), vmem_limit_bytes=VMEM_LIMIT),
        name="merge_residual",
    )(x2, mod3, o_attn, rest, o_rnn, rest, rest, rest, wa_bf, wr_bf, wo_bf,
      ln_g.reshape(1, D_MODEL), ln_b.reshape(1, D_MODEL))


def _rope_tables(t_len):
    quarter = HEAD_DIM // 4
    inv_freq = ROPE_BASE ** (-jnp.arange(quarter, dtype=_F32) / quarter)
    pos = jnp.arange(t_len)
    ang_r = (pos // GRID_W).astype(_F32)[:, None] * inv_freq[None, :]
    ang_c = (pos % GRID_W).astype(_F32)[:, None] * inv_freq[None, :]
    cos = jnp.concatenate([jnp.cos(ang_r)] * 2 + [jnp.cos(ang_c)] * 2, axis=1)
    sin = jnp.concatenate([-jnp.sin(ang_r), jnp.sin(ang_r), -jnp.sin(ang_c), jnp.sin(ang_c)], axis=1)
    return cos, sin


def _gate_weights(rg_wa, rg_ba, rg_wx, rg_bx):
    wg = (0.5 * jnp.concatenate([rg_wa[0], rg_wx[0], rg_wa[1], rg_wx[1]], axis=-1)).astype(_BF16)
    blk = lambda v: v.reshape(N_RNN_BLOCKS, 1, RNN_BLOCK)
    bg = 0.5 * jnp.concatenate([blk(rg_ba[0]), blk(rg_bx[0]), blk(rg_ba[1]), blk(rg_bx[1])], axis=-1)
    return wg, bg


def kernel(x_prompt, x_sample, cache_k, cache_v, state_h, c, c_ctx, w_mod, b_mod, w_in,
           attn_sink, conv_w, conv_b, rg_wa, rg_ba, rg_wx, rg_bx, rg_lambda,
           w_br_attn, w_br_rnn, w_out, ln_g, ln_b):
    bp, s_len, _ = x_prompt.shape
    bd, t_len, _ = x_sample.shape
    past = cache_k.shape[2]

    cond8 = jnp.zeros((COND_ROWS, D_MODEL), _F32).at[0].set(c_ctx).at[1:1 + bd].set(c)
    mod = _modulation(cond8, w_mod, b_mod)

    w_in_bf = w_in.astype(_BF16)
    wa_bf = w_br_attn.astype(_BF16)
    wr_bf = w_br_rnn.astype(_BF16)
    wo_bf = w_out.astype(_BF16)
    rope_tabs = _rope_tables(t_len)
    cache_k4 = cache_k.reshape(bd, DEPTH, past, D_KV)
    cache_v4 = cache_v.reshape(bd, DEPTH, past, D_KV)

    tm_proj, tm_out = 512, 512
    ctx_row = lambda i: 0
    lat_row_proj = lambda i: 1 + (i * tm_proj) // t_len
    lat_row_out = lambda i: 1 + (i * tm_out) // t_len

    xp = x_prompt.reshape(bp * s_len, D_MODEL)
    xs = x_sample.reshape(bd * t_len, D_MODEL)
    h0_zero = jnp.zeros((bp, 2, D_RNN), _F32)
    ks, vs, hs = [], [], []
    for l in range(DEPTH):
        mod3 = mod[l].reshape(COND_ROWS, 1, 3 * D_MODEL)
        wg, bg = _gate_weights(rg_wa[l], rg_ba[l], rg_wx[l], rg_bx[l])

        q, k, v, rest = _project(xp, mod3, w_in_bf, l, ctx_row, tm_proj, None, _F32)
        o_attn = _context_attention(q, k, v, attn_sink[l], bp, s_len)
        o_rnn, h_fin = _rnn(rest, h0_zero, conv_w[l], conv_b[l], wg, bg, rg_lambda[l], bp, s_len)
        xp = _merge_residual(xp, mod3, o_attn, o_rnn, rest, wa_bf, wr_bf, wo_bf, l,
                             ln_g[l], ln_b[l], ctx_row, tm_out)
        ks.append(k.reshape(bp, s_len, N_KV_HEADS, HEAD_DIM))
        vs.append(v.reshape(bp, s_len, N_KV_HEADS, HEAD_DIM))
        hs.append(h_fin)

        q, k, v, rest = _project(xs, mod3, w_in_bf, l, lat_row_proj, tm_proj, rope_tabs, _BF16)
        o_attn = _latent_attention(q, k, v, cache_k4, cache_v4, l, attn_sink[l], bd, t_len)
        o_rnn, _ = _rnn(rest, state_h[:, l], conv_w[l], conv_b[l], wg, bg, rg_lambda[l], bd, t_len)
        xs = _merge_residual(xs, mod3, o_attn, o_rnn, rest, wa_bf, wr_bf, wo_bf, l,
                             ln_g[l], ln_b[l], lat_row_out, tm_out)

    y_prompt = xp.reshape(bp, s_len, D_MODEL)
    y_sample = xs.reshape(bd, t_len, D_MODEL)
    return (y_prompt, y_sample, jnp.stack(ks, axis=1), jnp.stack(vs, axis=1), jnp.stack(hs, axis=1))
```

```python
import functools

import jax
import jax.numpy as jnp
from jax import lax
from jax.experimental import pallas as pl
from jax.experimental.pallas import tpu as pltpu

D_MODEL = 1024
DEPTH = 2
GRID_W = 64
N_HEADS = 8
N_KV_HEADS = 2
HEAD_DIM = 128
Q_GROUP = N_HEADS // N_KV_HEADS
D_ATTN = N_HEADS * HEAD_DIM
D_KV = N_KV_HEADS * HEAD_DIM
WINDOW = 128
Q_BLOCK = 128
D_RNN = D_MODEL
N_RNN_BLOCKS = 8
RNN_BLOCK = D_RNN // N_RNN_BLOCKS
CONV_W = 4
CONV_LEFT = 2
LRU_C = 8.0
ROPE_BASE = 10000.0
D_IN = 2 * D_ATTN + 2 * D_KV + 2 * D_RNN + 2 * D_MODEL
DEEPNORM_ALPHA = (2 * DEPTH) ** 0.25
LN_EPS = 1e-6
NEG_INF = -1e30
ATTN_SCALE = HEAD_DIM ** -0.5

LANES = 128
SUBLANES = 8
SEQ_SLOTS = SUBLANES
SLOT_PAD = SUBLANES
CTX_ATTN_SEQS = 4
PASS1_ROWS = 1024
COND_ROWS = 8
PROJ_TN = 512
N_PROJ_BLOCKS = D_IN // PROJ_TN
D_REST = D_IN - D_ATTN - 2 * D_KV
VMEM_LIMIT = 48 * 1024 * 1024
PROJ_VMEM_LIMIT = 56 * 1024 * 1024
RNN_VMEM_LIMIT = 56 * 1024 * 1024
F32_TINY = 1.1754943508222875e-38
LOG2_E = 1.4426950408889634
Q_PRESCALE = ATTN_SCALE * LOG2_E

_BF16 = jnp.bfloat16
_F32 = jnp.float32


def _silu(x):
    return x * jax.nn.sigmoid(x)


def _layer_norm_rows(x):
    mu = jnp.mean(x, axis=-1, keepdims=True)
    xc = x - mu
    var = jnp.mean(xc * xc, axis=-1, keepdims=True)
    return xc * lax.rsqrt(var + LN_EPS)


def _mod_kernel(cond_ref, w_ref, b_ref, o_ref):
    a = _silu(cond_ref[...]).astype(_BF16)
    o_ref[...] = jnp.dot(a, w_ref[...].astype(_BF16), preferred_element_type=_F32) + b_ref[...]


def _modulation(cond8, w_mod, b_mod):
    tn = D_MODEL
    return pl.pallas_call(
        _mod_kernel,
        grid=(DEPTH, 3 * D_MODEL // tn),
        in_specs=[
            pl.BlockSpec((COND_ROWS, D_MODEL), lambda l, j: (0, 0)),
            pl.BlockSpec((None, D_MODEL, tn), lambda l, j: (l, 0, j)),
            pl.BlockSpec((None, 1, tn), lambda l, j: (l, 0, j)),
        ],
        out_specs=pl.BlockSpec((None, COND_ROWS, tn), lambda l, j: (l, 0, j)),
        out_shape=jax.ShapeDtypeStruct((DEPTH, COND_ROWS, 3 * D_MODEL), _F32),
        compiler_params=pltpu.CompilerParams(dimension_semantics=("parallel", "parallel")),
        name="modulation",
    )(cond8, w_mod, b_mod.reshape(DEPTH, 1, 3 * D_MODEL))


def _rope(x, cos, sin_signed):
    lane = lax.broadcasted_iota(jnp.int32, x.shape, 1)
    partner = jnp.where((lane // 32) % 2 == 0,
                        pltpu.roll(x, 3 * 32, axis=1),
                        pltpu.roll(x, 32, axis=1))
    return x * cos + partner * sin_signed


def _proj_kernel(x_ref, shift_ref, scale_ref, w_ref, *rest, rotary):
    rest = list(rest)
    if rotary:
        cos_ref, sin_ref = rest[:2]
        rest = rest[2:]
    q_ref, k_ref, v_ref, r_ref = rest[-4:]
    y = _layer_norm_rows(x_ref[...])
    h = (y * (1.0 + scale_ref[...]) + shift_ref[...]).astype(_BF16)

    def maybe_rope(blk):
        return _rope(blk, cos_ref[...], sin_ref[...]) if rotary else blk

    def put_kv(ref, lanes, val):
        if len(ref.shape) == 2:
            ref[:, lanes] = val.astype(ref.dtype)
        else:
            s_len = ref.shape[1]
            for bi in range(ref.shape[0]):
                ref[bi, :, lanes] = val[bi * s_len:(bi + 1) * s_len].astype(ref.dtype)

    for j in range(N_PROJ_BLOCKS):
        c0 = j * PROJ_TN
        acc = jnp.dot(h, w_ref[:, c0:c0 + PROJ_TN], preferred_element_type=_F32)
        if c0 < D_ATTN:
            for hh in range(PROJ_TN // HEAD_DIM):
                sl = slice(hh * HEAD_DIM, (hh + 1) * HEAD_DIM)
                q_ref[:, c0 + hh * HEAD_DIM:c0 + (hh + 1) * HEAD_DIM] = (
                    (maybe_rope(acc[:, sl]) * Q_PRESCALE).astype(q_ref.dtype))
        elif c0 == D_ATTN:
            for hh in range(N_KV_HEADS):
                sl = slice(hh * HEAD_DIM, (hh + 1) * HEAD_DIM)
                put_kv(k_ref, sl, maybe_rope(acc[:, sl]))
                put_kv(v_ref, sl, acc[:, D_KV + hh * HEAD_DIM:D_KV + (hh + 1) * HEAD_DIM])
        else:
            r0 = c0 - D_ATTN - 2 * D_KV
            r_ref[:, r0:r0 + PROJ_TN] = acc.astype(r_ref.dtype)


def _project(x2, mod3, w_in_bf, layer, cond_row_of_block, tm, rope_tabs=None, cache=None):
    n = x2.shape[0]
    rotary = rope_tabs is not None
    in_specs = [
        pl.BlockSpec((tm, D_MODEL), lambda i: (i, 0)),
        pl.BlockSpec((None, 1, D_MODEL), lambda i: (cond_row_of_block(i), 0, 0)),
        pl.BlockSpec((None, 1, D_MODEL), lambda i: (cond_row_of_block(i), 0, 1)),
        pl.BlockSpec((None, D_MODEL, D_IN), lambda i: (layer, 0, 0), pipeline_mode=pl.Buffered(1)),
    ]
    args = [x2, mod3, mod3, w_in_bf]
    aliases = {}
    if rotary:
        cos, sin = rope_tabs
        blocks_per_seq = cos.shape[0] // tm
        in_specs += [pl.BlockSpec((tm, HEAD_DIM), lambda i: (i % blocks_per_seq, 0))] * 2
        args += [cos, sin]
        kv_spec = pl.BlockSpec((tm, D_KV), lambda i: (i, 0))
        kv_shape = jax.ShapeDtypeStruct((n, D_KV), _BF16)
    else:
        batch, s_len, prev = cache
        kv_spec = pl.BlockSpec((tm // s_len, None, s_len, D_KV), lambda i: (i, layer, 0, 0))
        kv_shape = jax.ShapeDtypeStruct((batch, DEPTH, s_len, D_KV), _F32)
        if prev is not None:
            aliases = {len(args): 1, len(args) + 1: 2}
            in_specs += [pl.BlockSpec(memory_space=pl.ANY)] * 2
            args += list(prev)
    return pl.pallas_call(
        functools.partial(_proj_kernel, rotary=rotary),
        grid=(n // tm,),
        in_specs=in_specs,
        out_specs=[
            pl.BlockSpec((tm, D_ATTN), lambda i: (i, 0)),
            kv_spec,
            kv_spec,
            pl.BlockSpec((tm, D_REST), lambda i: (i, 0)),
        ],
        out_shape=[
            jax.ShapeDtypeStruct((n, D_ATTN), _BF16),
            kv_shape,
            kv_shape,
            jax.ShapeDtypeStruct((n, D_REST), _BF16),
        ],
        input_output_aliases=aliases,
        compiler_params=pltpu.CompilerParams(
            dimension_semantics=("parallel",), vmem_limit_bytes=PROJ_VMEM_LIMIT),
        name="proj_rope" if rotary else "proj",
    )(*args)


def _softmax_pv(score_parts, sink_col, value_parts):
    m = sink_col
    for s in score_parts:
        m = jnp.maximum(m, jnp.max(s, axis=-1, keepdims=True))
    acc = None
    for s, v_aug in zip(score_parts, value_parts):
        part = jnp.dot(jnp.exp2(s - m).astype(_BF16), v_aug, preferred_element_type=_F32)
        acc = part if acc is None else acc + part
    denom = acc[:, HEAD_DIM:HEAD_DIM + 1] + jnp.exp2(sink_col - m)
    return acc[:, :HEAD_DIM] / denom


def _with_ones(v_bf):
    return jnp.concatenate([v_bf, jnp.ones_like(v_bf)], axis=1)


def _sink_column(sink_ref, kvh, rows):
    head = lax.broadcasted_iota(jnp.int32, (Q_GROUP * rows, 1), 0) // rows
    col = jnp.full((Q_GROUP * rows, 1), sink_ref[kvh * Q_GROUP], _F32)
    for g in range(1, Q_GROUP):
        col = jnp.where(head == g, sink_ref[kvh * Q_GROUP + g], col)
    return col * LOG2_E


def _stack_heads(q_rows):
    return jnp.concatenate(
        [q_rows[:, g * HEAD_DIM:(g + 1) * HEAD_DIM] for g in range(Q_GROUP)], axis=0)


def _qk(q, k_bf):
    return lax.dot_general(q, k_bf, (((1,), (1,)), ((), ())), preferred_element_type=_F32)


def _ctx_attn_kernel(sink_ref, q_ref, k_ref, v_ref, o_ref, *, s_len):
    kvh = pl.program_id(1)
    sink_col = _sink_column(sink_ref, kvh, s_len)

    def one_sequence(bi, carry):
        rows = pl.ds(pl.multiple_of(bi * s_len, s_len), s_len)
        q = _stack_heads(q_ref[rows, :])
        k = k_ref[bi].astype(_BF16)
        v_aug = _with_ones(v_ref[bi].astype(_BF16))
        o = _softmax_pv([_qk(q, k)], sink_col, [v_aug])
        for g in range(Q_GROUP):
            o_ref[rows, g * HEAD_DIM:(g + 1) * HEAD_DIM] = o[g * s_len:(g + 1) * s_len].astype(o_ref.dtype)
        return carry

    lax.fori_loop(0, q_ref.shape[0] // s_len, one_sequence, 0)


def _context_attention(q, k_cache, v_cache, layer, sink, batch, s_len):
    n = batch * s_len
    tm = CTX_ATTN_SEQS * s_len
    gw = Q_GROUP * HEAD_DIM
    kv_spec = pl.BlockSpec((CTX_ATTN_SEQS, None, s_len, HEAD_DIM), lambda b, h: (b, layer, 0, h))
    return pl.pallas_call(
        functools.partial(_ctx_attn_kernel, s_len=s_len),
        grid=(n // tm, N_KV_HEADS),
        in_specs=[
            pl.BlockSpec(memory_space=pltpu.SMEM),
            pl.BlockSpec((tm, gw), lambda b, h: (b, h)),
            kv_spec,
            kv_spec,
        ],
        out_specs=pl.BlockSpec((tm, gw), lambda b, h: (b, h)),
        out_shape=jax.ShapeDtypeStruct((n, D_ATTN), _BF16),
        compiler_params=pltpu.CompilerParams(dimension_semantics=("parallel", "parallel")),
        name="ctx_attention",
    )(sink, q, k_cache, v_cache)


def _lat_attn_kernel(sink_ref, q_ref, k_ref, v_ref, ck_ref, cv_ref, o_ref):
    kvh = pl.program_id(1)
    t_len = q_ref.shape[0]
    nb = t_len // Q_BLOCK
    ck = ck_ref[...].astype(_BF16)
    cv_aug = _with_ones(cv_ref[...].astype(_BF16))
    v_aug = _with_ones(v_ref[...])
    sink_col = _sink_column(sink_ref, kvh, Q_BLOCK)
    rows = Q_GROUP * Q_BLOCK
    r = lax.broadcasted_iota(jnp.int32, (rows, Q_BLOCK), 0) % Q_BLOCK
    c = lax.broadcasted_iota(jnp.int32, (rows, Q_BLOCK), 1)
    for jb in range(nb):
        r0 = jb * Q_BLOCK
        q = _stack_heads(q_ref[r0:r0 + Q_BLOCK, :])
        lo = max(r0 - WINDOW, 0)
        hi = min(r0 + Q_BLOCK + WINDOW, t_len)
        s = _qk(q, k_ref[lo:hi, :])
        parts = []
        off = 0
        if jb > 0:
            parts.append(jnp.where(c >= r, s[:, :Q_BLOCK], NEG_INF))
            off = Q_BLOCK
        parts.append(s[:, off:off + Q_BLOCK])
        off += Q_BLOCK
        if jb < nb - 1:
            parts.append(jnp.where(c <= r, s[:, off:], NEG_INF))
        s_local = jnp.concatenate(parts, axis=1)
        o = _softmax_pv([s_local, _qk(q, ck)], sink_col, [v_aug[lo:hi], cv_aug])
        for g in range(Q_GROUP):
            o_ref[r0:r0 + Q_BLOCK, g * HEAD_DIM:(g + 1) * HEAD_DIM] = (
                o[g * Q_BLOCK:(g + 1) * Q_BLOCK].astype(o_ref.dtype))


def _latent_attention(q, k, v, cache_k4, cache_v4, layer, sink, batch, t_len):
    n = batch * t_len
    gw = Q_GROUP * HEAD_DIM
    past = cache_k4.shape[2]
    cache_spec = pl.BlockSpec((None, None, past, HEAD_DIM), lambda b, h: (b, layer, 0, h))
    return pl.pallas_call(
        _lat_attn_kernel,
        grid=(batch, N_KV_HEADS),
        in_specs=[
            pl.BlockSpec(memory_space=pltpu.SMEM),
            pl.BlockSpec((t_len, gw), lambda b, h: (b, h)),
            pl.BlockSpec((t_len, HEAD_DIM), lambda b, h: (b, h)),
            pl.BlockSpec((t_len, HEAD_DIM), lambda b, h: (b, h)),
            cache_spec,
            cache_spec,
        ],
        out_specs=pl.BlockSpec((t_len, gw), lambda b, h: (b, h)),
        out_shape=jax.ShapeDtypeStruct((n, D_ATTN), _BF16),
        compiler_params=pltpu.CompilerParams(
            dimension_semantics=("parallel", "parallel"), vmem_limit_bytes=VMEM_LIMIT),
        name="latent_attention",
    )(sink, q, k, v, cache_k4, cache_v4)


def _shift_rows(x, d, t_len):
    row = lax.broadcasted_iota(jnp.int32, x.shape, 0) % t_len
    rolled = pltpu.roll(x, d % x.shape[0], axis=0)
    valid = (row >= d) if d > 0 else (row < t_len + d)
    return jnp.where(valid, rolled, 0.0)


def _sqrt_nonneg(z):
    return z * lax.rsqrt(jnp.maximum(z, F32_TINY))


def _rnn_kernel(x_ref, cw_ref, cb_ref, wg_ref, bg_ref, lam_ref, h0_ref, y_ref, hl_ref,
                af_scr, bf_scr, ab_scr, bb_scr, hf_scr, hb_scr, *, nb, nc):
    t_len = x_ref.shape[1]
    pitch = t_len + SLOT_PAD
    scr = ((af_scr, bf_scr), (ab_scr, bb_scr))
    kb = min(nb, max(1, PASS1_ROWS // t_len))

    for cl in range(nc):
        lanes = slice(cl * LANES, (cl + 1) * LANES)
        cw = cw_ref[:, lanes]
        cb = cb_ref[:, lanes]
        wg = wg_ref[cl]
        bg = bg_ref[cl]
        lam = lam_ref[:, lanes]
        c_softplus = LRU_C * (jnp.maximum(-lam, 0.0) + jnp.log1p(jnp.exp(-jnp.abs(lam))))

        def per_chunk(it, carry, lanes=lanes, cw=cw, cb=cb, wg=wg, bg=bg,
                      c_softplus=c_softplus, cl=cl):
            x = x_ref[pl.ds(it * kb, kb), :, lanes].astype(_F32).reshape(kb * t_len, LANES)
            xc = cw[0:1] * _shift_rows(x, 2, t_len)
            xc = xc + cw[1:2] * _shift_rows(x, 1, t_len)
            xc = xc + cw[2:3] * x
            xc = xc + cw[3:4] * _shift_rows(x, -1, t_len)
            xc = xc + cb
            half_gates = jnp.dot(xc.astype(_BF16), wg, preferred_element_type=_F32) + bg
            half_xc = 0.5 * xc
            for d in range(2):
                t_r = jnp.tanh(half_gates[:, (2 * d) * LANES:(2 * d + 1) * LANES])
                t_i = jnp.tanh(half_gates[:, (2 * d + 1) * LANES:(2 * d + 2) * LANES])
                k = 0.5 * c_softplus[d:d + 1]
                neg_log_a = t_r * k + k
                a = jnp.exp2(neg_log_a * -LOG2_E)
                bterm = _sqrt_nonneg(jnp.tanh(neg_log_a) * (a * a + 1.0)) * ((t_i + 1.0) * half_xc)
                for jb in range(kb):
                    row0 = pl.multiple_of((cl * nb + it * kb + jb) * pitch, SUBLANES)
                    scr[d][0][pl.ds(row0, t_len), :] = a[jb * t_len:(jb + 1) * t_len]
                    scr[d][1][pl.ds(row0, t_len), :] = bterm[jb * t_len:(jb + 1) * t_len]
            return carry

        lax.fori_loop(0, nb // kb, per_chunk, 0)

    def slot_rows(ref3, d):
        rows = [ref3[b, d:d + 1, cl * LANES:(cl + 1) * LANES] for cl in range(nc) for b in range(nb)]
        return jnp.concatenate(rows, axis=0)

    def advance2(a_scr, b_scr, h_scr, t0, t1, h):
        r0 = pl.ds(t0, SEQ_SLOTS, stride=pitch)
        r1 = pl.ds(t1, SEQ_SLOTS, stride=pitch)
        a0, b0, a1, b1 = a_scr[r0, :], b_scr[r0, :], a_scr[r1, :], b_scr[r1, :]
        h_scr[r0, :] = a0 * h + b0
        h = (a1 * a0) * h + (a1 * b0 + b1)
        h_scr[r1, :] = h
        return h

    hf, hb = slot_rows(h0_ref, 0), slot_rows(h0_ref, 1)
    for t in range(0, t_len, 2):
        hf = advance2(af_scr, bf_scr, hf_scr, t, t + 1, hf)
        hb = advance2(ab_scr, bb_scr, hb_scr, t_len - 1 - t, t_len - 2 - t, hb)

    for cl in range(nc):
        lanes = slice(cl * LANES, (cl + 1) * LANES)
        for b in range(nb):
            s = cl * nb + b
            hl_ref[b, 0:1, lanes] = hf[s:s + 1, :]
            hl_ref[b, 1:2, lanes] = hb[s:s + 1, :]
            rows = pl.ds(s * pitch, t_len)
            y_ref[b, :, lanes] = (hf_scr[rows, :] + hb_scr[rows, :]).astype(y_ref.dtype)


def _rnn(rest, h0, conv_w, conv_b, wg, bg, lam, batch, t_len):
    nb = min(batch, SEQ_SLOTS)
    nc = SEQ_SLOTS // nb
    cw = nc * LANES
    xr_block0 = D_MODEL // cw
    scratch = pltpu.VMEM((SEQ_SLOTS * (t_len + SLOT_PAD), LANES), _F32)
    y, h_last = pl.pallas_call(
        functools.partial(_rnn_kernel, nb=nb, nc=nc),
        grid=(batch // nb, D_RNN // cw),
        in_specs=[
            pl.BlockSpec((nb, t_len, cw), lambda g, c: (g, 0, xr_block0 + c)),
            pl.BlockSpec((CONV_W, cw), lambda g, c: (0, c)),
            pl.BlockSpec((1, cw), lambda g, c: (0, c)),
            pl.BlockSpec((nc, RNN_BLOCK, 4 * LANES), lambda g, c: (c, 0, 0)),
            pl.BlockSpec((nc, 1, 4 * LANES), lambda g, c: (c, 0, 0)),
            pl.BlockSpec((2, cw), lambda g, c: (0, c)),
            pl.BlockSpec((nb, 2, cw), lambda g, c: (g, 0, c)),
        ],
        out_specs=[
            pl.BlockSpec((nb, t_len, cw), lambda g, c: (g, 0, c)),
            pl.BlockSpec((nb, 2, cw), lambda g, c: (g, 0, c)),
        ],
        out_shape=[
            jax.ShapeDtypeStruct((batch, t_len, D_RNN), _BF16),
            jax.ShapeDtypeStruct((batch, 2, D_RNN), _F32),
        ],
        scratch_shapes=[scratch] * 6,
        compiler_params=pltpu.CompilerParams(
            dimension_semantics=("parallel", "parallel"), vmem_limit_bytes=RNN_VMEM_LIMIT),
        name="rnn",
    )(rest.reshape(batch, t_len, D_REST), conv_w, conv_b.reshape(1, D_RNN), wg, bg, lam, h0)
    return y.reshape(batch * t_len, D_RNN), h_last


def _out_kernel(x_ref, gate_ref, oa_ref, ga_ref, or_ref, gr_ref, mga_ref, mgr_ref,
                wa_ref, wr_ref, wo_ref, g_ref, b_ref, o_ref):
    def silu_bf(x):
        h = 0.5 * x
        return h * jnp.tanh(h) + h

    def logistic_f32(x):
        return (0.5 * jnp.tanh(0.5 * x) + 0.5).astype(_F32)

    ya = jnp.dot(oa_ref[...] * silu_bf(ga_ref[...]), wa_ref[...], preferred_element_type=_F32)
    yr = jnp.dot(or_ref[...] * silu_bf(gr_ref[...]), wr_ref[...], preferred_element_type=_F32)
    merged = logistic_f32(mga_ref[...]) * ya + logistic_f32(mgr_ref[...]) * yr
    out = jnp.dot(merged.astype(_BF16), wo_ref[...], preferred_element_type=_F32)
    z = DEEPNORM_ALPHA * x_ref[...] + gate_ref[...] * out
    o_ref[...] = _layer_norm_rows(z) * g_ref[...] + b_ref[...]


def _merge_residual(x2, mod3, o_attn, o_rnn, rest, wa_bf, wr_bf, wo_bf, layer, ln_g, ln_b,
                    cond_row_of_block, tm):
    n = x2.shape[0]
    row_blk = lambda col: pl.BlockSpec((tm, D_MODEL), lambda i, col=col: (i, col))
    whole = pl.BlockSpec((None, D_MODEL, D_MODEL), lambda i: (layer, 0, 0))
    vec = pl.BlockSpec((1, D_MODEL), lambda i: (0, 0))
    return pl.pallas_call(
        _out_kernel,
        grid=(n // tm,),
        in_specs=[
            row_blk(0),
            pl.BlockSpec((None, 1, D_MODEL), lambda i: (cond_row_of_block(i), 0, 2)),
            row_blk(0), row_blk(0), row_blk(0), row_blk(2), row_blk(3), row_blk(4),
            whole, whole, whole, vec, vec,
        ],
        out_specs=row_blk(0),
        out_shape=jax.ShapeDtypeStruct((n, D_MODEL), _F32),
        compiler_params=pltpu.CompilerParams(
            dimension_semantics=("parallel",), vmem_limit_bytes=VMEM_LIMIT),
        name="merge_residual",
    )(x2, mod3, o_attn, rest, o_rnn, rest, rest, rest, wa_bf, wr_bf, wo_bf,
      ln_g.reshape(1, D_MODEL), ln_b.reshape(1, D_MODEL))


def _rope_tables(t_len):
    quarter = HEAD_DIM // 4
    inv_freq = ROPE_BASE ** (-jnp.arange(quarter, dtype=_F32) / quarter)
    pos = jnp.arange(t_len)
    ang_r = (pos // GRID_W).astype(_F32)[:, None] * inv_freq[None, :]
    ang_c = (pos % GRID_W).astype(_F32)[:, None] * inv_freq[None, :]
    cos = jnp.concatenate([jnp.cos(ang_r)] * 2 + [jnp.cos(ang_c)] * 2, axis=1)
    sin = jnp.concatenate([-jnp.sin(ang_r), jnp.sin(ang_r), -jnp.sin(ang_c), jnp.sin(ang_c)], axis=1)
    return cos, sin


def _gate_weights(rg_wa, rg_ba, rg_wx, rg_bx):
    wg = (0.5 * jnp.concatenate([rg_wa[0], rg_wx[0], rg_wa[1], rg_wx[1]], axis=-1)).astype(_BF16)
    blk = lambda v: v.reshape(N_RNN_BLOCKS, 1, RNN_BLOCK)
    bg = 0.5 * jnp.concatenate([blk(rg_ba[0]), blk(rg_bx[0]), blk(rg_ba[1]), blk(rg_bx[1])], axis=-1)
    return wg, bg


def kernel(x_prompt, x_sample, cache_k, cache_v, state_h, c, c_ctx, w_mod, b_mod, w_in,
           attn_sink, conv_w, conv_b, rg_wa, rg_ba, rg_wx, rg_bx, rg_lambda,
           w_br_attn, w_br_rnn, w_out, ln_g, ln_b):
    bp, s_len, _ = x_prompt.shape
    bd, t_len, _ = x_sample.shape
    past = cache_k.shape[2]

    cond8 = jnp.zeros((COND_ROWS, D_MODEL), _F32).at[0].set(c_ctx).at[1:1 + bd].set(c)
    mod = _modulation(cond8, w_mod, b_mod)

    w_in_bf = w_in.astype(_BF16)
    wa_bf = w_br_attn.astype(_BF16)
    wr_bf = w_br_rnn.astype(_BF16)
    wo_bf = w_out.astype(_BF16)
    rope_tabs = _rope_tables(t_len)
    cache_k4 = cache_k.reshape(bd, DEPTH, past, D_KV)
    cache_v4 = cache_v.reshape(bd, DEPTH, past, D_KV)

    tm_proj, tm_out = 512, 512
    ctx_row = lambda i: 0
    lat_row_proj = lambda i: 1 + (i * tm_proj) // t_len
    lat_row_out = lambda i: 1 + (i * tm_out) // t_len

    xp = x_prompt.reshape(bp * s_len, D_MODEL)
    xs = x_sample.reshape(bd * t_len, D_MODEL)
    h0_zero = jnp.zeros((bp, 2, D_RNN), _F32)
    hs = []
    kv_caches = None
    for l in range(DEPTH):
        mod3 = mod[l].reshape(COND_ROWS, 1, 3 * D_MODEL)
        wg, bg = _gate_weights(rg_wa[l], rg_ba[l], rg_wx[l], rg_bx[l])

        q, k_cache, v_cache, rest = _project(xp, mod3, w_in_bf, l, ctx_row, tm_proj,
                                             cache=(bp, s_len, kv_caches))
        kv_caches = (k_cache, v_cache)
        o_attn = _context_attention(q, k_cache, v_cache, l, attn_sink[l], bp, s_len)
        o_rnn, h_fin = _rnn(rest, h0_zero, conv_w[l], conv_b[l], wg, bg, rg_lambda[l], bp, s_len)
        xp = _merge_residual(xp, mod3, o_attn, o_rnn, rest, wa_bf, wr_bf, wo_bf, l,
                             ln_g[l], ln_b[l], ctx_row, tm_out)
        hs.append(h_fin)

        q, k, v, rest = _project(xs, mod3, w_in_bf, l, lat_row_proj, tm_proj, rope_tabs=rope_tabs)
        o_attn = _latent_attention(q, k, v, cache_k4, cache_v4, l, attn_sink[l], bd, t_len)
        o_rnn, _ = _rnn(rest, state_h[:, l], conv_w[l], conv_b[l], wg, bg, rg_lambda[l], bd, t_len)
        xs = _merge_residual(xs, mod3, o_attn, o_rnn, rest, wa_bf, wr_bf, wo_bf, l,
                             ln_g[l], ln_b[l], lat_row_out, tm_out)

    y_prompt = xp.reshape(bp, s_len, D_MODEL)
    y_sample = xs.reshape(bd, t_len, D_MODEL)
    cache_shape = (bp, DEPTH, s_len, N_KV_HEADS, HEAD_DIM)
    return (y_prompt, y_sample, kv_caches[0].reshape(cache_shape), kv_caches[1].reshape(cache_shape),
            jnp.stack(hs, axis=1))
```

```python
import functools

import jax
import jax.numpy as jnp
from jax import lax
from jax.experimental import pallas as pl
from jax.experimental.pallas import tpu as pltpu

D_MODEL = 1024
DEPTH = 2
GRID_W = 64
N_HEADS = 8
N_KV_HEADS = 2
HEAD_DIM = 128
Q_GROUP = N_HEADS // N_KV_HEADS
D_ATTN = N_HEADS * HEAD_DIM
D_KV = N_KV_HEADS * HEAD_DIM
WINDOW = 128
Q_BLOCK = 128
D_RNN = D_MODEL
N_RNN_BLOCKS = 8
RNN_BLOCK = D_RNN // N_RNN_BLOCKS
CONV_W = 4
CONV_LEFT = 2
LRU_C = 8.0
ROPE_BASE = 10000.0
D_IN = 2 * D_ATTN + 2 * D_KV + 2 * D_RNN + 2 * D_MODEL
DEEPNORM_ALPHA = (2 * DEPTH) ** 0.25
LN_EPS = 1e-6
NEG_INF = -1e30
ATTN_SCALE = HEAD_DIM ** -0.5

LANES = 128
SUBLANES = 8
SEQ_SLOTS = SUBLANES
SLOT_PAD = SUBLANES
CTX_ATTN_SEQS = 4
PASS1_ROWS = 1024
COND_ROWS = 8
PROJ_TN = 512
N_PROJ_BLOCKS = D_IN // PROJ_TN
D_REST = D_IN - D_ATTN - 2 * D_KV
VMEM_LIMIT = 48 * 1024 * 1024
PROJ_VMEM_LIMIT = 56 * 1024 * 1024
RNN_VMEM_LIMIT = 56 * 1024 * 1024
F32_TINY = 1.1754943508222875e-38
LOG2_E = 1.4426950408889634
Q_PRESCALE = ATTN_SCALE * LOG2_E

_BF16 = jnp.bfloat16
_F32 = jnp.float32


def _silu(x):
    return x * jax.nn.sigmoid(x)


def _layer_norm_rows(x):
    mu = jnp.mean(x, axis=-1, keepdims=True)
    xc = x - mu
    var = jnp.mean(xc * xc, axis=-1, keepdims=True)
    return xc * lax.rsqrt(var + LN_EPS)


def _mod_kernel(cond_ref, w_ref, b_ref, o_ref):
    a = _silu(cond_ref[...]).astype(_BF16)
    o_ref[...] = jnp.dot(a, w_ref[...].astype(_BF16), preferred_element_type=_F32) + b_ref[...]


def _modulation(cond8, w_mod, b_mod):
    tn = D_MODEL
    return pl.pallas_call(
        _mod_kernel,
        grid=(DEPTH, 3 * D_MODEL // tn),
        in_specs=[
            pl.BlockSpec((COND_ROWS, D_MODEL), lambda l, j: (0, 0)),
            pl.BlockSpec((None, D_MODEL, tn), lambda l, j: (l, 0, j)),
            pl.BlockSpec((None, 1, tn), lambda l, j: (l, 0, j)),
        ],
        out_specs=pl.BlockSpec((None, COND_ROWS, tn), lambda l, j: (l, 0, j)),
        out_shape=jax.ShapeDtypeStruct((DEPTH, COND_ROWS, 3 * D_MODEL), _F32),
        compiler_params=pltpu.CompilerParams(dimension_semantics=("parallel", "parallel")),
        name="modulation",
    )(cond8, w_mod, b_mod.reshape(DEPTH, 1, 3 * D_MODEL))


def _rope(x, cos, sin_signed):
    lane = lax.broadcasted_iota(jnp.int32, x.shape, 1)
    partner = jnp.where((lane // 32) % 2 == 0,
                        pltpu.roll(x, 3 * 32, axis=1),
                        pltpu.roll(x, 32, axis=1))
    return x * cos + partner * sin_signed


def _proj_kernel(x_ref, shift_ref, scale_ref, w_ref, *rest, rotary, layer):
    rest = list(rest)
    if rotary:
        cos_ref, sin_ref = rest[:2]
        rest = rest[2:]
    q_ref, k_ref, v_ref, r_ref = rest[-4:]
    y = _layer_norm_rows(x_ref[...])
    h = (y * (1.0 + scale_ref[...]) + shift_ref[...]).astype(_BF16)

    def maybe_rope(blk):
        return _rope(blk, cos_ref[...], sin_ref[...]) if rotary else blk

    def put_kv(ref, lanes, val):
        if len(ref.shape) == 2:
            ref[:, lanes] = val.astype(ref.dtype)
            return
        s_len = ref.shape[-2]
        for bi in range(ref.shape[0]):
            rows = val[bi * s_len:(bi + 1) * s_len].astype(ref.dtype)
            if len(ref.shape) == 3:
                ref[bi, :, lanes] = rows
            else:
                for l in range(ref.shape[1]):
                    ref[bi, l, :, lanes] = rows if l == layer else jnp.zeros_like(rows)

    for j in range(N_PROJ_BLOCKS):
        c0 = j * PROJ_TN
        acc = jnp.dot(h, w_ref[:, c0:c0 + PROJ_TN], preferred_element_type=_F32)
        if c0 < D_ATTN:
            for hh in range(PROJ_TN // HEAD_DIM):
                sl = slice(hh * HEAD_DIM, (hh + 1) * HEAD_DIM)
                q_ref[:, c0 + hh * HEAD_DIM:c0 + (hh + 1) * HEAD_DIM] = (
                    (maybe_rope(acc[:, sl]) * Q_PRESCALE).astype(q_ref.dtype))
        elif c0 == D_ATTN:
            for hh in range(N_KV_HEADS):
                sl = slice(hh * HEAD_DIM, (hh + 1) * HEAD_DIM)
                put_kv(k_ref, sl, maybe_rope(acc[:, sl]))
                put_kv(v_ref, sl, acc[:, D_KV + hh * HEAD_DIM:D_KV + (hh + 1) * HEAD_DIM])
        else:
            r0 = c0 - D_ATTN - 2 * D_KV
            r_ref[:, r0:r0 + PROJ_TN] = acc.astype(r_ref.dtype)


def _project(x2, mod3, w_in_bf, layer, cond_row_of_block, tm, rope_tabs=None, cache=None):
    n = x2.shape[0]
    rotary = rope_tabs is not None
    in_specs = [
        pl.BlockSpec((tm, D_MODEL), lambda i: (i, 0)),
        pl.BlockSpec((None, 1, D_MODEL), lambda i: (cond_row_of_block(i), 0, 0)),
        pl.BlockSpec((None, 1, D_MODEL), lambda i: (cond_row_of_block(i), 0, 1)),
        pl.BlockSpec((None, D_MODEL, D_IN), lambda i: (layer, 0, 0), pipeline_mode=pl.Buffered(1)),
    ]
    args = [x2, mod3, mod3, w_in_bf]
    aliases = {}
    if rotary:
        cos, sin = rope_tabs
        blocks_per_seq = cos.shape[0] // tm
        in_specs += [pl.BlockSpec((tm, HEAD_DIM), lambda i: (i % blocks_per_seq, 0))] * 2
        args += [cos, sin]
        kv_spec = pl.BlockSpec((tm, D_KV), lambda i: (i, 0))
        kv_shape = jax.ShapeDtypeStruct((n, D_KV), _BF16)
    else:
        batch, s_len, prev = cache
        kv_shape = jax.ShapeDtypeStruct((batch, DEPTH, s_len, D_KV), _F32)
        if prev is None:
            kv_spec = pl.BlockSpec((tm // s_len, DEPTH, s_len, D_KV), lambda i: (i, 0, 0, 0))
        else:
            kv_spec = pl.BlockSpec((tm // s_len, None, s_len, D_KV), lambda i: (i, layer, 0, 0))
            aliases = {len(args): 1, len(args) + 1: 2}
            in_specs += [pl.BlockSpec(memory_space=pl.ANY)] * 2
            args += list(prev)
    return pl.pallas_call(
        functools.partial(_proj_kernel, rotary=rotary, layer=layer),
        grid=(n // tm,),
        in_specs=in_specs,
        out_specs=[
            pl.BlockSpec((tm, D_ATTN), lambda i: (i, 0)),
            kv_spec,
            kv_spec,
            pl.BlockSpec((tm, D_REST), lambda i: (i, 0)),
        ],
        out_shape=[
            jax.ShapeDtypeStruct((n, D_ATTN), _BF16),
            kv_shape,
            kv_shape,
            jax.ShapeDtypeStruct((n, D_REST), _BF16),
        ],
        input_output_aliases=aliases,
        compiler_params=pltpu.CompilerParams(
            dimension_semantics=("parallel",), vmem_limit_bytes=PROJ_VMEM_LIMIT),
        name="proj_rope" if rotary else "proj",
    )(*args)


def _softmax_pv(score_parts, sink_col, value_parts):
    m = sink_col
    for s in score_parts:
        m = jnp.maximum(m, jnp.max(s, axis=-1, keepdims=True))
    acc = None
    for s, v_aug in zip(score_parts, value_parts):
        part = jnp.dot(jnp.exp2(s - m).astype(_BF16), v_aug, preferred_element_type=_F32)
        acc = part if acc is None else acc + part
    denom = acc[:, HEAD_DIM:HEAD_DIM + 1] + jnp.exp2(sink_col - m)
    return acc[:, :HEAD_DIM] / denom


def _with_ones(v_bf):
    return jnp.concatenate([v_bf, jnp.ones_like(v_bf)], axis=1)


def _sink_column(sink_ref, kvh, rows):
    head = lax.broadcasted_iota(jnp.int32, (Q_GROUP * rows, 1), 0) // rows
    col = jnp.full((Q_GROUP * rows, 1), sink_ref[kvh * Q_GROUP], _F32)
    for g in range(1, Q_GROUP):
        col = jnp.where(head == g, sink_ref[kvh * Q_GROUP + g], col)
    return col * LOG2_E


def _stack_heads(q_rows):
    return jnp.concatenate(
        [q_rows[:, g * HEAD_DIM:(g + 1) * HEAD_DIM] for g in range(Q_GROUP)], axis=0)


def _qk(q, k_bf):
    return lax.dot_general(q, k_bf, (((1,), (1,)), ((), ())), preferred_element_type=_F32)


def _ctx_attn_kernel(sink_ref, q_ref, k_ref, v_ref, o_ref, *, s_len):
    kvh = pl.program_id(1)
    sink_col = _sink_column(sink_ref, kvh, s_len)

    def one_sequence(bi, carry):
        rows = pl.ds(pl.multiple_of(bi * s_len, s_len), s_len)
        q = _stack_heads(q_ref[rows, :])
        k = k_ref[bi].astype(_BF16)
        v_aug = _with_ones(v_ref[bi].astype(_BF16))
        o = _softmax_pv([_qk(q, k)], sink_col, [v_aug])
        for g in range(Q_GROUP):
            o_ref[rows, g * HEAD_DIM:(g + 1) * HEAD_DIM] = o[g * s_len:(g + 1) * s_len].astype(o_ref.dtype)
        return carry

    lax.fori_loop(0, q_ref.shape[0] // s_len, one_sequence, 0)


def _context_attention(q, k_cache, v_cache, layer, sink, batch, s_len):
    n = batch * s_len
    tm = CTX_ATTN_SEQS * s_len
    gw = Q_GROUP * HEAD_DIM
    kv_spec = pl.BlockSpec((CTX_ATTN_SEQS, None, s_len, HEAD_DIM), lambda b, h: (b, layer, 0, h))
    return pl.pallas_call(
        functools.partial(_ctx_attn_kernel, s_len=s_len),
        grid=(n // tm, N_KV_HEADS),
        in_specs=[
            pl.BlockSpec(memory_space=pltpu.SMEM),
            pl.BlockSpec((tm, gw), lambda b, h: (b, h)),
            kv_spec,
            kv_spec,
        ],
        out_specs=pl.BlockSpec((tm, gw), lambda b, h: (b, h)),
        out_shape=jax.ShapeDtypeStruct((n, D_ATTN), _BF16),
        compiler_params=pltpu.CompilerParams(dimension_semantics=("parallel", "parallel")),
        name="ctx_attention",
    )(sink, q, k_cache, v_cache)


def _lat_attn_kernel(sink_ref, q_ref, k_ref, v_ref, ck_ref, cv_ref, o_ref):
    kvh = pl.program_id(1)
    t_len = q_ref.shape[0]
    nb = t_len // Q_BLOCK
    ck = ck_ref[...].astype(_BF16)
    cv_aug = _with_ones(cv_ref[...].astype(_BF16))
    v_aug = _with_ones(v_ref[...])
    sink_col = _sink_column(sink_ref, kvh, Q_BLOCK)
    rows = Q_GROUP * Q_BLOCK
    r = lax.broadcasted_iota(jnp.int32, (rows, Q_BLOCK), 0) % Q_BLOCK
    c = lax.broadcasted_iota(jnp.int32, (rows, Q_BLOCK), 1)
    for jb in range(nb):
        r0 = jb * Q_BLOCK
        q = _stack_heads(q_ref[r0:r0 + Q_BLOCK, :])
        lo = max(r0 - WINDOW, 0)
        hi = min(r0 + Q_BLOCK + WINDOW, t_len)
        s = _qk(q, k_ref[lo:hi, :])
        parts = []
        off = 0
        if jb > 0:
            parts.append(jnp.where(c >= r, s[:, :Q_BLOCK], NEG_INF))
            off = Q_BLOCK
        parts.append(s[:, off:off + Q_BLOCK])
        off += Q_BLOCK
        if jb < nb - 1:
            parts.append(jnp.where(c <= r, s[:, off:], NEG_INF))
        s_local = jnp.concatenate(parts, axis=1)
        o = _softmax_pv([s_local, _qk(q, ck)], sink_col, [v_aug[lo:hi], cv_aug])
        for g in range(Q_GROUP):
            o_ref[r0:r0 + Q_BLOCK, g * HEAD_DIM:(g + 1) * HEAD_DIM] = (
                o[g * Q_BLOCK:(g + 1) * Q_BLOCK].astype(o_ref.dtype))


def _latent_attention(q, k, v, cache_k4, cache_v4, layer, sink, batch, t_len):
    n = batch * t_len
    gw = Q_GROUP * HEAD_DIM
    past = cache_k4.shape[2]
    cache_spec = pl.BlockSpec((None, None, past, HEAD_DIM), lambda b, h: (b, layer, 0, h))
    return pl.pallas_call(
        _lat_attn_kernel,
        grid=(batch, N_KV_HEADS),
        in_specs=[
            pl.BlockSpec(memory_space=pltpu.SMEM),
            pl.BlockSpec((t_len, gw), lambda b, h: (b, h)),
            pl.BlockSpec((t_len, HEAD_DIM), lambda b, h: (b, h)),
            pl.BlockSpec((t_len, HEAD_DIM), lambda b, h: (b, h)),
            cache_spec,
            cache_spec,
        ],
        out_specs=pl.BlockSpec((t_len, gw), lambda b, h: (b, h)),
        out_shape=jax.ShapeDtypeStruct((n, D_ATTN), _BF16),
        compiler_params=pltpu.CompilerParams(
            dimension_semantics=("parallel", "parallel"), vmem_limit_bytes=VMEM_LIMIT),
        name="latent_attention",
    )(sink, q, k, v, cache_k4, cache_v4)


def _shift_rows(x, d, t_len):
    row = lax.broadcasted_iota(jnp.int32, x.shape, 0) % t_len
    rolled = pltpu.roll(x, d % x.shape[0], axis=0)
    valid = (row >= d) if d > 0 else (row < t_len + d)
    return jnp.where(valid, rolled, 0.0)


def _sqrt_nonneg(z):
    return z * lax.rsqrt(jnp.maximum(z, F32_TINY))


def _rnn_kernel(x_ref, cw_ref, cb_ref, wg_ref, bg_ref, lam_ref, h0_ref, y_ref, hl_ref,
                af_scr, bf_scr, ab_scr, bb_scr, hf_scr, hb_scr, *, nb, nc):
    t_len = x_ref.shape[1]
    pitch = t_len + SLOT_PAD
    scr = ((af_scr, bf_scr), (ab_scr, bb_scr))
    kb = min(nb, max(1, PASS1_ROWS // t_len))

    for cl in range(nc):
        lanes = slice(cl * LANES, (cl + 1) * LANES)
        cw = cw_ref[:, lanes]
        cb = cb_ref[:, lanes]
        wg = wg_ref[cl]
        bg = bg_ref[cl]
        lam = lam_ref[:, lanes]
        c_softplus = LRU_C * (jnp.maximum(-lam, 0.0) + jnp.log1p(jnp.exp(-jnp.abs(lam))))

        def per_chunk(it, carry, lanes=lanes, cw=cw, cb=cb, wg=wg, bg=bg,
                      c_softplus=c_softplus, cl=cl):
            x = x_ref[pl.ds(it * kb, kb), :, lanes].astype(_F32).reshape(kb * t_len, LANES)
            xc = cw[0:1] * _shift_rows(x, 2, t_len)
            xc = xc + cw[1:2] * _shift_rows(x, 1, t_len)
            xc = xc + cw[2:3] * x
            xc = xc + cw[3:4] * _shift_rows(x, -1, t_len)
            xc = xc + cb
            half_gates = jnp.dot(xc.astype(_BF16), wg, preferred_element_type=_F32) + bg
            half_xc = 0.5 * xc
            for d in range(2):
                t_r = jnp.tanh(half_gates[:, (2 * d) * LANES:(2 * d + 1) * LANES])
                t_i = jnp.tanh(half_gates[:, (2 * d + 1) * LANES:(2 * d + 2) * LANES])
                k = 0.5 * c_softplus[d:d + 1]
                neg_log_a = t_r * k + k
                a = jnp.exp2(neg_log_a * -LOG2_E)
                bterm = _sqrt_nonneg(jnp.tanh(neg_log_a) * (a * a + 1.0)) * ((t_i + 1.0) * half_xc)
                for jb in range(kb):
                    row0 = pl.multiple_of((cl * nb + it * kb + jb) * pitch, SUBLANES)
                    scr[d][0][pl.ds(row0, t_len), :] = a[jb * t_len:(jb + 1) * t_len]
                    scr[d][1][pl.ds(row0, t_len), :] = bterm[jb * t_len:(jb + 1) * t_len]
            return carry

        lax.fori_loop(0, nb // kb, per_chunk, 0)

    def slot_rows(ref3, d):
        rows = [ref3[b, d:d + 1, cl * LANES:(cl + 1) * LANES] for cl in range(nc) for b in range(nb)]
        return jnp.concatenate(rows, axis=0)

    def advance2(a_scr, b_scr, h_scr, t0, t1, h):
        r0 = pl.ds(t0, SEQ_SLOTS, stride=pitch)
        r1 = pl.ds(t1, SEQ_SLOTS, stride=pitch)
        a0, b0, a1, b1 = a_scr[r0, :], b_scr[r0, :], a_scr[r1, :], b_scr[r1, :]
        h_scr[r0, :] = a0 * h + b0
        h = (a1 * a0) * h + (a1 * b0 + b1)
        h_scr[r1, :] = h
        return h

    hf, hb = slot_rows(h0_ref, 0), slot_rows(h0_ref, 1)
    for t in range(0, t_len, 2):
        hf = advance2(af_scr, bf_scr, hf_scr, t, t + 1, hf)
        hb = advance2(ab_scr, bb_scr, hb_scr, t_len - 1 - t, t_len - 2 - t, hb)

    for cl in range(nc):
        lanes = slice(cl * LANES, (cl + 1) * LANES)
        for b in range(nb):
            s = cl * nb + b
            hl_ref[b, 0:1, lanes] = hf[s:s + 1, :]
            hl_ref[b, 1:2, lanes] = hb[s:s + 1, :]
            rows = pl.ds(s * pitch, t_len)
            y_ref[b, :, lanes] = (hf_scr[rows, :] + hb_scr[rows, :]).astype(y_ref.dtype)


def _rnn(rest, h0, conv_w, conv_b, wg, bg, lam, batch, t_len):
    nb = min(batch, SEQ_SLOTS)
    nc = SEQ_SLOTS // nb
    cw = nc * LANES
    xr_block0 = D_MODEL // cw
    scratch = pltpu.VMEM((SEQ_SLOTS * (t_len + SLOT_PAD), LANES), _F32)
    y, h_last = pl.pallas_call(
        functools.partial(_rnn_kernel, nb=nb, nc=nc),
        grid=(batch // nb, D_RNN // cw),
        in_specs=[
            pl.BlockSpec((nb, t_len, cw), lambda g, c: (g, 0, xr_block0 + c)),
            pl.BlockSpec((CONV_W, cw), lambda g, c: (0, c)),
            pl.BlockSpec((1, cw), lambda g, c: (0, c)),
            pl.BlockSpec((nc, RNN_BLOCK, 4 * LANES), lambda g, c: (c, 0, 0)),
            pl.BlockSpec((nc, 1, 4 * LANES), lambda g, c: (c, 0, 0)),
            pl.BlockSpec((2, cw), lambda g, c: (0, c)),
            pl.BlockSpec((nb, 2, cw), lambda g, c: (g, 0, c)),
        ],
        out_specs=[
            pl.BlockSpec((nb, t_len, cw), lambda g, c: (g, 0, c)),
            pl.BlockSpec((nb, 2, cw), lambda g, c: (g, 0, c)),
        ],
        out_shape=[
            jax.ShapeDtypeStruct((batch, t_len, D_RNN), _BF16),
            jax.ShapeDtypeStruct((batch, 2, D_RNN), _F32),
        ],
        scratch_shapes=[scratch] * 6,
        compiler_params=pltpu.CompilerParams(
            dimension_semantics=("parallel", "parallel"), vmem_limit_bytes=RNN_VMEM_LIMIT),
        name="rnn",
    )(rest.reshape(batch, t_len, D_REST), conv_w, conv_b.reshape(1, D_RNN), wg, bg, lam, h0)
    return y.reshape(batch * t_len, D_RNN), h_last


def _out_kernel(x_ref, gate_ref, oa_ref, ga_ref, or_ref, gr_ref, mga_ref, mgr_ref,
                wa_ref, wr_ref, wo_ref, g_ref, b_ref, o_ref):
    def silu_bf(x):
        h = 0.5 * x
        return h * jnp.tanh(h) + h

    def logistic_f32(x):
        return (0.5 * jnp.tanh(0.5 * x) + 0.5).astype(_F32)

    ya = jnp.dot(oa_ref[...] * silu_bf(ga_ref[...]), wa_ref[...], preferred_element_type=_F32)
    yr = jnp.dot(or_ref[...] * silu_bf(gr_ref[...]), wr_ref[...], preferred_element_type=_F32)
    merged = logistic_f32(mga_ref[...]) * ya + logistic_f32(mgr_ref[...]) * yr
    out = jnp.dot(merged.astype(_BF16), wo_ref[...], preferred_element_type=_F32)
    z = DEEPNORM_ALPHA * x_ref[...] + gate_ref[...] * out
    o_ref[...] = _layer_norm_rows(z) * g_ref[...] + b_ref[...]


def _merge_residual(x2, mod3, o_attn, o_rnn, rest, wa_bf, wr_bf, wo_bf, layer, ln_g, ln_b,
                    cond_row_of_block, tm):
    n = x2.shape[0]
    row_blk = lambda col: pl.BlockSpec((tm, D_MODEL), lambda i, col=col: (i, col))
    whole = pl.BlockSpec((None, D_MODEL, D_MODEL), lambda i: (layer, 0, 0))
    vec = pl.BlockSpec((1, D_MODEL), lambda i: (0, 0))
    return pl.pallas_call(
        _out_kernel,
        grid=(n // tm,),
        in_specs=[
            row_blk(0),
            pl.BlockSpec((None, 1, D_MODEL), lambda i: (cond_row_of_block(i), 0, 2)),
            row_blk(0), row_blk(0), row_blk(0), row_blk(2), row_blk(3), row_blk(4),
            whole, whole, whole, vec, vec,
        ],
        out_specs=row_blk(0),
        out_shape=jax.ShapeDtypeStruct((n, D_MODEL), _F32),
        compiler_params=pltpu.CompilerParams(
            dimension_semantics=("parallel",), vmem_limit_bytes=VMEM_LIMIT),
        name="merge_residual",
    )(x2, mod3, o_attn, rest, o_rnn, rest, rest, rest, wa_bf, wr_bf, wo_bf,
      ln_g.reshape(1, D_MODEL), ln_b.reshape(1, D_MODEL))


def _rope_tables(t_len):
    quarter = HEAD_DIM // 4
    inv_freq = ROPE_BASE ** (-jnp.arange(quarter, dtype=_F32) / quarter)
    pos = jnp.arange(t_len)
    ang_r = (pos // GRID_W).astype(_F32)[:, None] * inv_freq[None, :]
    ang_c = (pos % GRID_W).astype(_F32)[:, None] * inv_freq[None, :]
    cos = jnp.concatenate([jnp.cos(ang_r)] * 2 + [jnp.cos(ang_c)] * 2, axis=1)
    sin = jnp.concatenate([-jnp.sin(ang_r), jnp.sin(ang_r), -jnp.sin(ang_c), jnp.sin(ang_c)], axis=1)
    return cos, sin


def _gate_weights(rg_wa, rg_ba, rg_wx, rg_bx):
    wg = (0.5 * jnp.concatenate([rg_wa[0], rg_wx[0], rg_wa[1], rg_wx[1]], axis=-1)).astype(_BF16)
    blk = lambda v: v.reshape(N_RNN_BLOCKS, 1, RNN_BLOCK)
    bg = 0.5 * jnp.concatenate([blk(rg_ba[0]), blk(rg_bx[0]), blk(rg_ba[1]), blk(rg_bx[1])], axis=-1)
    return wg, bg


def kernel(x_prompt, x_sample, cache_k, cache_v, state_h, c, c_ctx, w_mod, b_mod, w_in,
           attn_sink, conv_w, conv_b, rg_wa, rg_ba, rg_wx, rg_bx, rg_lambda,
           w_br_attn, w_br_rnn, w_out, ln_g, ln_b):
    bp, s_len, _ = x_prompt.shape
    bd, t_len, _ = x_sample.shape
    past = cache_k.shape[2]

    cond8 = jnp.zeros((COND_ROWS, D_MODEL), _F32).at[0].set(c_ctx).at[1:1 + bd].set(c)
    mod = _modulation(cond8, w_mod, b_mod)

    w_in_bf = w_in.astype(_BF16)
    wa_bf = w_br_attn.astype(_BF16)
    wr_bf = w_br_rnn.astype(_BF16)
    wo_bf = w_out.astype(_BF16)
    rope_tabs = _rope_tables(t_len)
    cache_k4 = cache_k.reshape(bd, DEPTH, past, D_KV)
    cache_v4 = cache_v.reshape(bd, DEPTH, past, D_KV)

    tm_proj, tm_out = 512, 512
    ctx_row = lambda i: 0
    lat_row_proj = lambda i: 1 + (i * tm_proj) // t_len
    lat_row_out = lambda i: 1 + (i * tm_out) // t_len

    xp = x_prompt.reshape(bp * s_len, D_MODEL)
    xs = x_sample.reshape(bd * t_len, D_MODEL)
    h0_zero = jnp.zeros((bp, 2, D_RNN), _F32)
    hs = []
    kv_caches = None
    for l in range(DEPTH):
        mod3 = mod[l].reshape(COND_ROWS, 1, 3 * D_MODEL)
        wg, bg = _gate_weights(rg_wa[l], rg_ba[l], rg_wx[l], rg_bx[l])

        q, k_cache, v_cache, rest = _project(xp, mod3, w_in_bf, l, ctx_row, tm_proj,
                                             cache=(bp, s_len, kv_caches))
        kv_caches = (k_cache, v_cache)
        o_attn = _context_attention(q, k_cache, v_cache, l, attn_sink[l], bp, s_len)
        o_rnn, h_fin = _rnn(rest, h0_zero, conv_w[l], conv_b[l], wg, bg, rg_lambda[l], bp, s_len)
        xp = _merge_residual(xp, mod3, o_attn, o_rnn, rest, wa_bf, wr_bf, wo_bf, l,
                             ln_g[l], ln_b[l], ctx_row, tm_out)
        hs.append(h_fin)

        q, k, v, rest = _project(xs, mod3, w_in_bf, l, lat_row_proj, tm_proj, rope_tabs=rope_tabs)
        o_attn = _latent_attention(q, k, v, cache_k4, cache_v4, l, attn_sink[l], bd, t_len)
        o_rnn, _ = _rnn(rest, state_h[:, l], conv_w[l], conv_b[l], wg, bg, rg_lambda[l], bd, t_len)
        xs = _merge_residual(xs, mod3, o_attn, o_rnn, rest, wa_bf, wr_bf, wo_bf, l,
                             ln_g[l], ln_b[l], lat_row_out, tm_out)

    y_prompt = xp.reshape(bp, s_len, D_MODEL)
    y_sample = xs.reshape(bd, t_len, D_MODEL)
    cache_shape = (bp, DEPTH, s_len, N_KV_HEADS, HEAD_DIM)
    return (y_prompt, y_sample, kv_caches[0].reshape(cache_shape), kv_caches[1].reshape(cache_shape),
            jnp.stack(hs, axis=1))
```

```python
import functools

import jax
import jax.numpy as jnp
from jax import lax
from jax.experimental import pallas as pl
from jax.experimental.pallas import tpu as pltpu

D_MODEL = 1024
DEPTH = 2
GRID_W = 64
N_HEADS = 8
N_KV_HEADS = 2
HEAD_DIM = 128
Q_GROUP = N_HEADS // N_KV_HEADS
D_ATTN = N_HEADS * HEAD_DIM
D_KV = N_KV_HEADS * HEAD_DIM
WINDOW = 128
Q_BLOCK = 128
D_RNN = D_MODEL
N_RNN_BLOCKS = 8
RNN_BLOCK = D_RNN // N_RNN_BLOCKS
CONV_W = 4
CONV_LEFT = 2
LRU_C = 8.0
ROPE_BASE = 10000.0
D_IN = 2 * D_ATTN + 2 * D_KV + 2 * D_RNN + 2 * D_MODEL
DEEPNORM_ALPHA = (2 * DEPTH) ** 0.25
LN_EPS = 1e-6
NEG_INF = -1e30
ATTN_SCALE = HEAD_DIM ** -0.5

LANES = 128
SUBLANES = 8
SEQ_SLOTS = SUBLANES
SLOT_PAD = 4
CTX_ATTN_SEQS = 4
PASS1_ROWS = 1024
COND_ROWS = 8
PROJ_TN = 512
N_PROJ_BLOCKS = D_IN // PROJ_TN
D_REST = D_IN - D_ATTN - 2 * D_KV
VMEM_LIMIT = 48 * 1024 * 1024
PROJ_VMEM_LIMIT = 56 * 1024 * 1024
RNN_VMEM_LIMIT = 56 * 1024 * 1024
F32_TINY = 1.1754943508222875e-38
LOG2_E = 1.4426950408889634
Q_PRESCALE = ATTN_SCALE * LOG2_E

_BF16 = jnp.bfloat16
_F32 = jnp.float32


def _silu(x):
    return x * jax.nn.sigmoid(x)


def _layer_norm_rows(x):
    mu = jnp.mean(x, axis=-1, keepdims=True)
    xc = x - mu
    var = jnp.mean(xc * xc, axis=-1, keepdims=True)
    return xc * lax.rsqrt(var + LN_EPS)


def _mod_kernel(cond_ref, w_ref, b_ref, o_ref):
    a = _silu(cond_ref[...]).astype(_BF16)
    o_ref[...] = jnp.dot(a, w_ref[...].astype(_BF16), preferred_element_type=_F32) + b_ref[...]


def _modulation(cond8, w_mod, b_mod):
    tn = D_MODEL
    return pl.pallas_call(
        _mod_kernel,
        grid=(DEPTH, 3 * D_MODEL // tn),
        in_specs=[
            pl.BlockSpec((COND_ROWS, D_MODEL), lambda l, j: (0, 0)),
            pl.BlockSpec((None, D_MODEL, tn), lambda l, j: (l, 0, j)),
            pl.BlockSpec((None, 1, tn), lambda l, j: (l, 0, j)),
        ],
        out_specs=pl.BlockSpec((None, COND_ROWS, tn), lambda l, j: (l, 0, j)),
        out_shape=jax.ShapeDtypeStruct((DEPTH, COND_ROWS, 3 * D_MODEL), _F32),
        compiler_params=pltpu.CompilerParams(dimension_semantics=("parallel", "parallel")),
        name="modulation",
    )(cond8, w_mod, b_mod.reshape(DEPTH, 1, 3 * D_MODEL))


def _rope(x, cos, sin_signed):
    lane = lax.broadcasted_iota(jnp.int32, x.shape, 1)
    partner = jnp.where((lane // 32) % 2 == 0,
                        pltpu.roll(x, 3 * 32, axis=1),
                        pltpu.roll(x, 32, axis=1))
    return x * cos + partner * sin_signed


def _proj_kernel(x_ref, shift_ref, scale_ref, w_ref, *rest, rotary, layer):
    rest = list(rest)
    if rotary:
        cos_ref, sin_ref = rest[:2]
        rest = rest[2:]
    q_ref, k_ref, v_ref, r_ref = rest[-4:]
    y = _layer_norm_rows(x_ref[...])
    h = (y * (1.0 + scale_ref[...]) + shift_ref[...]).astype(_BF16)

    def maybe_rope(blk):
        return _rope(blk, cos_ref[...], sin_ref[...]) if rotary else blk

    def put_kv(ref, lanes, val):
        if len(ref.shape) == 2:
            ref[:, lanes] = val.astype(ref.dtype)
            return
        s_len = ref.shape[-2]
        for bi in range(ref.shape[0]):
            rows = val[bi * s_len:(bi + 1) * s_len].astype(ref.dtype)
            if len(ref.shape) == 3:
                ref[bi, :, lanes] = rows
            else:
                for l in range(ref.shape[1]):
                    ref[bi, l, :, lanes] = rows if l == layer else jnp.zeros_like(rows)

    for j in range(N_PROJ_BLOCKS):
        c0 = j * PROJ_TN
        acc = jnp.dot(h, w_ref[:, c0:c0 + PROJ_TN], preferred_element_type=_F32)
        if c0 < D_ATTN:
            for hh in range(PROJ_TN // HEAD_DIM):
                sl = slice(hh * HEAD_DIM, (hh + 1) * HEAD_DIM)
                q_ref[:, c0 + hh * HEAD_DIM:c0 + (hh + 1) * HEAD_DIM] = (
                    (maybe_rope(acc[:, sl]) * Q_PRESCALE).astype(q_ref.dtype))
        elif c0 == D_ATTN:
            for hh in range(N_KV_HEADS):
                sl = slice(hh * HEAD_DIM, (hh + 1) * HEAD_DIM)
                put_kv(k_ref, sl, maybe_rope(acc[:, sl]))
                put_kv(v_ref, sl, acc[:, D_KV + hh * HEAD_DIM:D_KV + (hh + 1) * HEAD_DIM])
        else:
            r0 = c0 - D_ATTN - 2 * D_KV
            r_ref[:, r0:r0 + PROJ_TN] = acc.astype(r_ref.dtype)


def _project(x2, mod3, w_in_bf, layer, cond_row_of_block, tm, rope_tabs=None, cache=None):
    n = x2.shape[0]
    rotary = rope_tabs is not None
    in_specs = [
        pl.BlockSpec((tm, D_MODEL), lambda i: (i, 0)),
        pl.BlockSpec((None, 1, D_MODEL), lambda i: (cond_row_of_block(i), 0, 0)),
        pl.BlockSpec((None, 1, D_MODEL), lambda i: (cond_row_of_block(i), 0, 1)),
        pl.BlockSpec((None, D_MODEL, D_IN), lambda i: (layer, 0, 0), pipeline_mode=pl.Buffered(1)),
    ]
    args = [x2, mod3, mod3, w_in_bf]
    aliases = {}
    if rotary:
        cos, sin = rope_tabs
        blocks_per_seq = cos.shape[0] // tm
        in_specs += [pl.BlockSpec((tm, HEAD_DIM), lambda i: (i % blocks_per_seq, 0))] * 2
        args += [cos, sin]
        kv_spec = pl.BlockSpec((tm, D_KV), lambda i: (i, 0))
        kv_shape = jax.ShapeDtypeStruct((n, D_KV), _BF16)
    else:
        batch, s_len, prev = cache
        kv_shape = jax.ShapeDtypeStruct((batch, DEPTH, s_len, D_KV), _F32)
        if prev is None:
            kv_spec = pl.BlockSpec((tm // s_len, DEPTH, s_len, D_KV), lambda i: (i, 0, 0, 0))
        else:
            kv_spec = pl.BlockSpec((tm // s_len, None, s_len, D_KV), lambda i: (i, layer, 0, 0))
            aliases = {len(args): 1, len(args) + 1: 2}
            in_specs += [pl.BlockSpec(memory_space=pl.ANY)] * 2
            args += list(prev)
    return pl.pallas_call(
        functools.partial(_proj_kernel, rotary=rotary, layer=layer),
        grid=(n // tm,),
        in_specs=in_specs,
        out_specs=[
            pl.BlockSpec((tm, D_ATTN), lambda i: (i, 0)),
            kv_spec,
            kv_spec,
            pl.BlockSpec((tm, D_REST), lambda i: (i, 0)),
        ],
        out_shape=[
            jax.ShapeDtypeStruct((n, D_ATTN), _BF16),
            kv_shape,
            kv_shape,
            jax.ShapeDtypeStruct((n, D_REST), _BF16),
        ],
        input_output_aliases=aliases,
        compiler_params=pltpu.CompilerParams(
            dimension_semantics=("parallel",), vmem_limit_bytes=PROJ_VMEM_LIMIT),
        name="proj_rope" if rotary else "proj",
    )(*args)


def _softmax_pv(score_parts, sink_col, value_parts):
    m = sink_col
    for s in score_parts:
        m = jnp.maximum(m, jnp.max(s, axis=-1, keepdims=True))
    acc = None
    for s, v_aug in zip(score_parts, value_parts):
        part = jnp.dot(jnp.exp2(s - m).astype(_BF16), v_aug, preferred_element_type=_F32)
        acc = part if acc is None else acc + part
    denom = acc[:, HEAD_DIM:HEAD_DIM + 1] + jnp.exp2(sink_col - m)
    return acc[:, :HEAD_DIM] / denom


def _with_ones(v_bf):
    return jnp.concatenate([v_bf, jnp.ones_like(v_bf)], axis=1)


def _sink_column(sink_ref, kvh, rows):
    head = lax.broadcasted_iota(jnp.int32, (Q_GROUP * rows, 1), 0) // rows
    col = jnp.full((Q_GROUP * rows, 1), sink_ref[kvh * Q_GROUP], _F32)
    for g in range(1, Q_GROUP):
        col = jnp.where(head == g, sink_ref[kvh * Q_GROUP + g], col)
    return col * LOG2_E


def _stack_heads(q_rows):
    return jnp.concatenate(
        [q_rows[:, g * HEAD_DIM:(g + 1) * HEAD_DIM] for g in range(Q_GROUP)], axis=0)


def _qk(q, k_bf):
    return lax.dot_general(q, k_bf, (((1,), (1,)), ((), ())), preferred_element_type=_F32)


def _ctx_attn_kernel(sink_ref, q_ref, k_ref, v_ref, o_ref, *, s_len):
    kvh = pl.program_id(1)
    sink_col = _sink_column(sink_ref, kvh, s_len)

    def one_sequence(bi, carry):
        rows = pl.ds(pl.multiple_of(bi * s_len, s_len), s_len)
        q = _stack_heads(q_ref[rows, :])
        k = k_ref[bi].astype(_BF16)
        v_aug = _with_ones(v_ref[bi].astype(_BF16))
        o = _softmax_pv([_qk(q, k)], sink_col, [v_aug])
        for g in range(Q_GROUP):
            o_ref[rows, g * HEAD_DIM:(g + 1) * HEAD_DIM] = o[g * s_len:(g + 1) * s_len].astype(o_ref.dtype)
        return carry

    lax.fori_loop(0, q_ref.shape[0] // s_len, one_sequence, 0)


def _context_attention(q, k_cache, v_cache, layer, sink, batch, s_len):
    n = batch * s_len
    tm = CTX_ATTN_SEQS * s_len
    gw = Q_GROUP * HEAD_DIM
    kv_spec = pl.BlockSpec((CTX_ATTN_SEQS, None, s_len, HEAD_DIM), lambda b, h: (b, layer, 0, h))
    return pl.pallas_call(
        functools.partial(_ctx_attn_kernel, s_len=s_len),
        grid=(n // tm, N_KV_HEADS),
        in_specs=[
            pl.BlockSpec(memory_space=pltpu.SMEM),
            pl.BlockSpec((tm, gw), lambda b, h: (b, h)),
            kv_spec,
            kv_spec,
        ],
        out_specs=pl.BlockSpec((tm, gw), lambda b, h: (b, h)),
        out_shape=jax.ShapeDtypeStruct((n, D_ATTN), _BF16),
        compiler_params=pltpu.CompilerParams(dimension_semantics=("parallel", "parallel")),
        name="ctx_attention",
    )(sink, q, k_cache, v_cache)


def _lat_attn_kernel(sink_ref, q_ref, k_ref, v_ref, ck_ref, cv_ref, o_ref):
    kvh = pl.program_id(1)
    t_len = q_ref.shape[0]
    nb = t_len // Q_BLOCK
    ck = ck_ref[...].astype(_BF16)
    cv_aug = _with_ones(cv_ref[...].astype(_BF16))
    v_aug = _with_ones(v_ref[...])
    sink_col = _sink_column(sink_ref, kvh, Q_BLOCK)
    rows = Q_GROUP * Q_BLOCK
    r = lax.broadcasted_iota(jnp.int32, (rows, Q_BLOCK), 0) % Q_BLOCK
    c = lax.broadcasted_iota(jnp.int32, (rows, Q_BLOCK), 1)
    for jb in range(nb):
        r0 = jb * Q_BLOCK
        q = _stack_heads(q_ref[r0:r0 + Q_BLOCK, :])
        lo = max(r0 - WINDOW, 0)
        hi = min(r0 + Q_BLOCK + WINDOW, t_len)
        s = _qk(q, k_ref[lo:hi, :])
        parts = []
        off = 0
        if jb > 0:
            parts.append(jnp.where(c >= r, s[:, :Q_BLOCK], NEG_INF))
            off = Q_BLOCK
        parts.append(s[:, off:off + Q_BLOCK])
        off += Q_BLOCK
        if jb < nb - 1:
            parts.append(jnp.where(c <= r, s[:, off:], NEG_INF))
        s_local = jnp.concatenate(parts, axis=1)
        o = _softmax_pv([s_local, _qk(q, ck)], sink_col, [v_aug[lo:hi], cv_aug])
        for g in range(Q_GROUP):
            o_ref[r0:r0 + Q_BLOCK, g * HEAD_DIM:(g + 1) * HEAD_DIM] = (
                o[g * Q_BLOCK:(g + 1) * Q_BLOCK].astype(o_ref.dtype))


def _latent_attention(q, k, v, cache_k4, cache_v4, layer, sink, batch, t_len):
    n = batch * t_len
    gw = Q_GROUP * HEAD_DIM
    past = cache_k4.shape[2]
    cache_spec = pl.BlockSpec((None, None, past, HEAD_DIM), lambda b, h: (b, layer, 0, h))
    return pl.pallas_call(
        _lat_attn_kernel,
        grid=(batch, N_KV_HEADS),
        in_specs=[
            pl.BlockSpec(memory_space=pltpu.SMEM),
            pl.BlockSpec((t_len, gw), lambda b, h: (b, h)),
            pl.BlockSpec((t_len, HEAD_DIM), lambda b, h: (b, h)),
            pl.BlockSpec((t_len, HEAD_DIM), lambda b, h: (b, h)),
            cache_spec,
            cache_spec,
        ],
        out_specs=pl.BlockSpec((t_len, gw), lambda b, h: (b, h)),
        out_shape=jax.ShapeDtypeStruct((n, D_ATTN), _BF16),
        compiler_params=pltpu.CompilerParams(
            dimension_semantics=("parallel", "parallel"), vmem_limit_bytes=VMEM_LIMIT),
        name="latent_attention",
    )(sink, q, k, v, cache_k4, cache_v4)


def _shift_rows(x, d, t_len, offset=0):
    n = x.shape[0]
    row = (lax.broadcasted_iota(jnp.int32, x.shape, 0) + (n - offset)) % t_len
    rolled = pltpu.roll(x, d % n, axis=0)
    valid = (row >= d) if d > 0 else (row < t_len + d)
    return jnp.where(valid, rolled, 0.0)


def _sqrt_nonneg(z):
    return z * lax.rsqrt(jnp.maximum(z, F32_TINY))


def _rnn_kernel(x_ref, cw_ref, cb_ref, wg_ref, bg_ref, lam_ref, h0_ref, y_ref, hl_ref,
                af_scr, bf_scr, ab_scr, bb_scr, hf_scr, hb_scr, *, nb, nc):
    t_len = x_ref.shape[1]
    pitch = t_len + SLOT_PAD
    scr = ((af_scr, bf_scr), (ab_scr, bb_scr))
    per_group = SEQ_SLOTS // 2
    kb = min(per_group, max(1, PASS1_ROWS // t_len))

    def seq_of(slot):
        g, u = slot % 2, slot // 2
        return (g * per_group + u, 0) if nc == 1 else (u, g)

    for g in range(2):
        b_first, cl = seq_of(g)
        offset = SLOT_PAD * g
        lanes = slice(cl * LANES, (cl + 1) * LANES)
        cw = cw_ref[:, lanes]
        cb = cb_ref[:, lanes]
        wg = wg_ref[cl]
        bg = bg_ref[cl]
        lam = lam_ref[:, lanes]
        c_softplus = LRU_C * (jnp.maximum(-lam, 0.0) + jnp.log1p(jnp.exp(-jnp.abs(lam))))

        def per_chunk(it, carry, lanes=lanes, cw=cw, cb=cb, wg=wg, bg=bg,
                      c_softplus=c_softplus, g=g, b_first=b_first, offset=offset):
            n_rows = kb * t_len
            x = x_ref[pl.ds(b_first + it * kb, kb), :, lanes].astype(_F32).reshape(n_rows, LANES)
            if offset:
                x = pltpu.roll(x, offset, axis=0)
            xc = cw[0:1] * _shift_rows(x, 2, t_len, offset)
            xc = xc + cw[1:2] * _shift_rows(x, 1, t_len, offset)
            xc = xc + cw[2:3] * x
            xc = xc + cw[3:4] * _shift_rows(x, -1, t_len, offset)
            xc = xc + cb
            half_gates = jnp.dot(xc.astype(_BF16), wg, preferred_element_type=_F32) + bg
            half_xc = 0.5 * xc
            for d in range(2):
                t_r = jnp.tanh(half_gates[:, (2 * d) * LANES:(2 * d + 1) * LANES])
                t_i = jnp.tanh(half_gates[:, (2 * d + 1) * LANES:(2 * d + 2) * LANES])
                k = 0.5 * c_softplus[d:d + 1]
                neg_log_a = t_r * k + k
                a = jnp.exp2(neg_log_a * -LOG2_E)
                bterm = _sqrt_nonneg(jnp.tanh(neg_log_a) * (a * a + 1.0)) * ((t_i + 1.0) * half_xc)
                for jb in range(kb):
                    slot = 2 * (it * kb + jb) + g
                    row0 = pl.multiple_of(slot * pitch - offset, SUBLANES)
                    for dst, val in ((scr[d][0], a), (scr[d][1], bterm)):
                        dst[pl.ds(row0, t_len), :] = val[jb * t_len:(jb + 1) * t_len]
                        if offset:
                            w0 = ((jb + 1) * t_len) % n_rows
                            dst[pl.ds(row0 + t_len, SUBLANES), :] = val[w0:w0 + SUBLANES]
            return carry

        if per_group == kb:
            per_chunk(0, 0)
        else:
            lax.fori_loop(0, per_group // kb, per_chunk, 0)

    def slot_rows(ref3, d):
        rows = []
        for slot in range(SEQ_SLOTS):
            b, cl = seq_of(slot)
            rows.append(ref3[b, d:d + 1, cl * LANES:(cl + 1) * LANES])
        return jnp.concatenate(rows, axis=0)

    for slot in range(SEQ_SLOTS):
        tile = (slot * pitch + t_len) // SUBLANES * SUBLANES
        for h_scr in (hf_scr, hb_scr):
            h_scr[pl.ds(tile, SUBLANES), :] = jnp.zeros((SUBLANES, LANES), _F32)

    def advance2(a_scr, b_scr, h_scr, t0, t1, h):
        r0 = pl.ds(t0, SEQ_SLOTS, stride=pitch)
        r1 = pl.ds(t1, SEQ_SLOTS, stride=pitch)
        a0, b0, a1, b1 = a_scr[r0, :], b_scr[r0, :], a_scr[r1, :], b_scr[r1, :]
        h_scr[r0, :] = a0 * h + b0
        h = (a1 * a0) * h + (a1 * b0 + b1)
        h_scr[r1, :] = h
        return h

    hf, hb = slot_rows(h0_ref, 0), slot_rows(h0_ref, 1)
    for t in range(0, t_len, 2):
        hf = advance2(af_scr, bf_scr, hf_scr, t, t + 1, hf)
        hb = advance2(ab_scr, bb_scr, hb_scr, t_len - 1 - t, t_len - 2 - t, hb)

    for slot in range(SEQ_SLOTS):
        b, cl = seq_of(slot)
        lanes = slice(cl * LANES, (cl + 1) * LANES)
        hl_ref[b, 0:1, lanes] = hf[slot:slot + 1, :]
        hl_ref[b, 1:2, lanes] = hb[slot:slot + 1, :]
        if slot % 2 == 0:
            rows = pl.ds(slot * pitch, t_len)
            y = hf_scr[rows, :] + hb_scr[rows, :]
        else:
            rows = pl.ds(slot * pitch - SLOT_PAD, t_len + SUBLANES)
            y = pltpu.roll(hf_scr[rows, :] + hb_scr[rows, :], t_len + SUBLANES - SLOT_PAD, axis=0)[:t_len]
        y_ref[b, :, lanes] = y.astype(y_ref.dtype)


def _rnn(rest, h0, conv_w, conv_b, wg, bg, lam, batch, t_len):
    nb = min(batch, SEQ_SLOTS)
    nc = SEQ_SLOTS // nb
    assert nc in (1, 2) and batch % nb == 0, "a grid step holds 8 batches x 1 or 4 batches x 2 channel blocks"
    cw = nc * LANES
    xr_block0 = D_MODEL // cw
    scratch = pltpu.VMEM((SEQ_SLOTS * (t_len + SLOT_PAD), LANES), _F32)
    y, h_last = pl.pallas_call(
        functools.partial(_rnn_kernel, nb=nb, nc=nc),
        grid=(batch // nb, D_RNN // cw),
        in_specs=[
            pl.BlockSpec((nb, t_len, cw), lambda g, c: (g, 0, xr_block0 + c)),
            pl.BlockSpec((CONV_W, cw), lambda g, c: (0, c)),
            pl.BlockSpec((1, cw), lambda g, c: (0, c)),
            pl.BlockSpec((nc, RNN_BLOCK, 4 * LANES), lambda g, c: (c, 0, 0)),
            pl.BlockSpec((nc, 1, 4 * LANES), lambda g, c: (c, 0, 0)),
            pl.BlockSpec((2, cw), lambda g, c: (0, c)),
            pl.BlockSpec((nb, 2, cw), lambda g, c: (g, 0, c)),
        ],
        out_specs=[
            pl.BlockSpec((nb, t_len, cw), lambda g, c: (g, 0, c)),
            pl.BlockSpec((nb, 2, cw), lambda g, c: (g, 0, c)),
        ],
        out_shape=[
            jax.ShapeDtypeStruct((batch, t_len, D_RNN), _BF16),
            jax.ShapeDtypeStruct((batch, 2, D_RNN), _F32),
        ],
        scratch_shapes=[scratch] * 6,
        compiler_params=pltpu.CompilerParams(
            dimension_semantics=("parallel", "parallel"), vmem_limit_bytes=RNN_VMEM_LIMIT),
        name="rnn",
    )(rest.reshape(batch, t_len, D_REST), conv_w, conv_b.reshape(1, D_RNN), wg, bg, lam, h0)
    return y.reshape(batch * t_len, D_RNN), h_last


def _out_kernel(x_ref, gate_ref, oa_ref, ga_ref, or_ref, gr_ref, mga_ref, mgr_ref,
                wa_ref, wr_ref, wo_ref, g_ref, b_ref, o_ref):
    def silu_bf(x):
        h = 0.5 * x
        return h * jnp.tanh(h) + h

    def logistic_f32(x):
        return (0.5 * jnp.tanh(0.5 * x) + 0.5).astype(_F32)

    ya = jnp.dot(oa_ref[...] * silu_bf(ga_ref[...]), wa_ref[...], preferred_element_type=_F32)
    yr = jnp.dot(or_ref[...] * silu_bf(gr_ref[...]), wr_ref[...], preferred_element_type=_F32)
    merged = logistic_f32(mga_ref[...]) * ya + logistic_f32(mgr_ref[...]) * yr
    out = jnp.dot(merged.astype(_BF16), wo_ref[...], preferred_element_type=_F32)
    z = DEEPNORM_ALPHA * x_ref[...] + gate_ref[...] * out
    o_ref[...] = _layer_norm_rows(z) * g_ref[...] + b_ref[...]


def _merge_residual(x2, mod3, o_attn, o_rnn, rest, wa_bf, wr_bf, wo_bf, layer, ln_g, ln_b,
                    cond_row_of_block, tm):
    n = x2.shape[0]
    row_blk = lambda col: pl.BlockSpec((tm, D_MODEL), lambda i, col=col: (i, col))
    whole = pl.BlockSpec((None, D_MODEL, D_MODEL), lambda i: (layer, 0, 0))
    vec = pl.BlockSpec((1, D_MODEL), lambda i: (0, 0))
    return pl.pallas_call(
        _out_kernel,
        grid=(n // tm,),
        in_specs=[
            row_blk(0),
            pl.BlockSpec((None, 1, D_MODEL), lambda i: (cond_row_of_block(i), 0, 2)),
            row_blk(0), row_blk(0), row_blk(0), row_blk(2), row_blk(3), row_blk(4),
            whole, whole, whole, vec, vec,
        ],
        out_specs=row_blk(0),
        out_shape=jax.ShapeDtypeStruct((n, D_MODEL), _F32),
        compiler_params=pltpu.CompilerParams(
            dimension_semantics=("parallel",), vmem_limit_bytes=VMEM_LIMIT),
        name="merge_residual",
    )(x2, mod3, o_attn, rest, o_rnn, rest, rest, rest, wa_bf, wr_bf, wo_bf,
      ln_g.reshape(1, D_MODEL), ln_b.reshape(1, D_MODEL))


def _rope_tables(t_len):
    quarter = HEAD_DIM // 4
    inv_freq = ROPE_BASE ** (-jnp.arange(quarter, dtype=_F32) / quarter)
    pos = jnp.arange(t_len)
    ang_r = (pos // GRID_W).astype(_F32)[:, None] * inv_freq[None, :]
    ang_c = (pos % GRID_W).astype(_F32)[:, None] * inv_freq[None, :]
    cos = jnp.concatenate([jnp.cos(ang_r)] * 2 + [jnp.cos(ang_c)] * 2, axis=1)
    sin = jnp.concatenate([-jnp.sin(ang_r), jnp.sin(ang_r), -jnp.sin(ang_c), jnp.sin(ang_c)], axis=1)
    return cos, sin


def _gate_weights(rg_wa, rg_ba, rg_wx, rg_bx):
    wg = (0.5 * jnp.concatenate([rg_wa[0], rg_wx[0], rg_wa[1], rg_wx[1]], axis=-1)).astype(_BF16)
    blk = lambda v: v.reshape(N_RNN_BLOCKS, 1, RNN_BLOCK)
    bg = 0.5 * jnp.concatenate([blk(rg_ba[0]), blk(rg_bx[0]), blk(rg_ba[1]), blk(rg_bx[1])], axis=-1)
    return wg, bg


def kernel(x_prompt, x_sample, cache_k, cache_v, state_h, c, c_ctx, w_mod, b_mod, w_in,
           attn_sink, conv_w, conv_b, rg_wa, rg_ba, rg_wx, rg_bx, rg_lambda,
           w_br_attn, w_br_rnn, w_out, ln_g, ln_b):
    bp, s_len, _ = x_prompt.shape
    bd, t_len, _ = x_sample.shape
    past = cache_k.shape[2]

    cond8 = jnp.zeros((COND_ROWS, D_MODEL), _F32).at[0].set(c_ctx).at[1:1 + bd].set(c)
    mod = _modulation(cond8, w_mod, b_mod)

    w_in_bf = w_in.astype(_BF16)
    wa_bf = w_br_attn.astype(_BF16)
    wr_bf = w_br_rnn.astype(_BF16)
    wo_bf = w_out.astype(_BF16)
    rope_tabs = _rope_tables(t_len)
    cache_k4 = cache_k.reshape(bd, DEPTH, past, D_KV)
    cache_v4 = cache_v.reshape(bd, DEPTH, past, D_KV)

    tm_proj, tm_out = 512, 512
    ctx_row = lambda i: 0
    lat_row_proj = lambda i: 1 + (i * tm_proj) // t_len
    lat_row_out = lambda i: 1 + (i * tm_out) // t_len

    xp = x_prompt.reshape(bp * s_len, D_MODEL)
    xs = x_sample.reshape(bd * t_len, D_MODEL)
    h0_zero = jnp.zeros((bp, 2, D_RNN), _F32)
    hs = []
    kv_caches = None
    for l in range(DEPTH):
        mod3 = mod[l].reshape(COND_ROWS, 1, 3 * D_MODEL)
        wg, bg = _gate_weights(rg_wa[l], rg_ba[l], rg_wx[l], rg_bx[l])

        q, k_cache, v_cache, rest = _project(xp, mod3, w_in_bf, l, ctx_row, tm_proj,
                                             cache=(bp, s_len, kv_caches))
        kv_caches = (k_cache, v_cache)
        o_attn = _context_attention(q, k_cache, v_cache, l, attn_sink[l], bp, s_len)
        o_rnn, h_fin = _rnn(rest, h0_zero, conv_w[l], conv_b[l], wg, bg, rg_lambda[l], bp, s_len)
        xp = _merge_residual(xp, mod3, o_attn, o_rnn, rest, wa_bf, wr_bf, wo_bf, l,
                             ln_g[l], ln_b[l], ctx_row, tm_out)
        hs.append(h_fin)

        q, k, v, rest = _project(xs, mod3, w_in_bf, l, lat_row_proj, tm_proj, rope_tabs=rope_tabs)
        o_attn = _latent_attention(q, k, v, cache_k4, cache_v4, l, attn_sink[l], bd, t_len)
        o_rnn, _ = _rnn(rest, state_h[:, l], conv_w[l], conv_b[l], wg, bg, rg_lambda[l], bd, t_len)
        xs = _merge_residual(xs, mod3, o_attn, o_rnn, rest, wa_bf, wr_bf, wo_bf, l,
                             ln_g[l], ln_b[l], lat_row_out, tm_out)

    y_prompt = xp.reshape(bp, s_len, D_MODEL)
    y_sample = xs.reshape(bd, t_len, D_MODEL)
    cache_shape = (bp, DEPTH, s_len, N_KV_HEADS, HEAD_DIM)
    return (y_prompt, y_sample, kv_caches[0].reshape(cache_shape), kv_caches[1].reshape(cache_shape),
            jnp.stack(hs, axis=1))
```

```python
import functools

import jax
import jax.numpy as jnp
from jax import lax
from jax.experimental import pallas as pl
from jax.experimental.pallas import tpu as pltpu

D_MODEL = 1024
DEPTH = 2
GRID_W = 64
N_HEADS = 8
N_KV_HEADS = 2
HEAD_DIM = 128
Q_GROUP = N_HEADS // N_KV_HEADS
D_ATTN = N_HEADS * HEAD_DIM
D_KV = N_KV_HEADS * HEAD_DIM
WINDOW = 128
Q_BLOCK = 128
D_RNN = D_MODEL
N_RNN_BLOCKS = 8
RNN_BLOCK = D_RNN // N_RNN_BLOCKS
CONV_W = 4
CONV_LEFT = 2
LRU_C = 8.0
ROPE_BASE = 10000.0
D_IN = 2 * D_ATTN + 2 * D_KV + 2 * D_RNN + 2 * D_MODEL
DEEPNORM_ALPHA = (2 * DEPTH) ** 0.25
LN_EPS = 1e-6
NEG_INF = -1e30
ATTN_SCALE = HEAD_DIM ** -0.5

LANES = 128
SUBLANES = 8
SEQ_SLOTS = SUBLANES
SLOT_PAD = 4
CTX_ATTN_SEQS = 4
PASS1_ROWS = 1024
COND_ROWS = 8
PROJ_TN = 512
N_PROJ_BLOCKS = D_IN // PROJ_TN
D_REST = D_IN - D_ATTN - 2 * D_KV
VMEM_LIMIT = 48 * 1024 * 1024
PROJ_VMEM_LIMIT = 56 * 1024 * 1024
MERGE_VMEM_LIMIT = 56 * 1024 * 1024
RNN_VMEM_LIMIT = 56 * 1024 * 1024
F32_TINY = 1.1754943508222875e-38
LOG2_E = 1.4426950408889634
Q_PRESCALE = ATTN_SCALE * LOG2_E

_BF16 = jnp.bfloat16
_F32 = jnp.float32


def _silu(x):
    return x * jax.nn.sigmoid(x)


def _layer_norm_rows(x):
    mu = jnp.mean(x, axis=-1, keepdims=True)
    xc = x - mu
    var = jnp.mean(xc * xc, axis=-1, keepdims=True)
    return xc * lax.rsqrt(var + LN_EPS)


def _mod_kernel(cond_ref, w_ref, b_ref, o_ref):
    a = _silu(cond_ref[...]).astype(_BF16)
    o_ref[...] = jnp.dot(a, w_ref[...].astype(_BF16), preferred_element_type=_F32) + b_ref[...]


def _modulation(cond8, w_mod, b_mod):
    tn = D_MODEL
    return pl.pallas_call(
        _mod_kernel,
        grid=(DEPTH, 3 * D_MODEL // tn),
        in_specs=[
            pl.BlockSpec((COND_ROWS, D_MODEL), lambda l, j: (0, 0)),
            pl.BlockSpec((None, D_MODEL, tn), lambda l, j: (l, 0, j)),
            pl.BlockSpec((None, 1, tn), lambda l, j: (l, 0, j)),
        ],
        out_specs=pl.BlockSpec((None, COND_ROWS, tn), lambda l, j: (l, 0, j)),
        out_shape=jax.ShapeDtypeStruct((DEPTH, COND_ROWS, 3 * D_MODEL), _F32),
        compiler_params=pltpu.CompilerParams(dimension_semantics=("parallel", "parallel")),
        name="modulation",
    )(cond8, w_mod, b_mod.reshape(DEPTH, 1, 3 * D_MODEL))


def _rope(x, cos, sin_signed):
    lane = lax.broadcasted_iota(jnp.int32, x.shape, 1)
    partner = jnp.where((lane // 32) % 2 == 0,
                        pltpu.roll(x, 3 * 32, axis=1),
                        pltpu.roll(x, 32, axis=1))
    return x * cos + partner * sin_signed


def _proj_kernel(x_ref, shift_ref, scale_ref, w_ref, *rest, rotary, layer):
    rest = list(rest)
    if rotary:
        cos_ref, sin_ref = rest[:2]
        rest = rest[2:]
    q_ref, k_ref, v_ref, r_ref = rest[-4:]
    y = _layer_norm_rows(x_ref[...])
    h = (y * (1.0 + scale_ref[...]) + shift_ref[...]).astype(_BF16)

    def maybe_rope(blk):
        return _rope(blk, cos_ref[...], sin_ref[...]) if rotary else blk

    def put_kv(ref, lanes, val):
        if len(ref.shape) == 2:
            ref[:, lanes] = val.astype(ref.dtype)
            return
        s_len = ref.shape[-2]
        for bi in range(ref.shape[0]):
            rows = val[bi * s_len:(bi + 1) * s_len].astype(ref.dtype)
            if len(ref.shape) == 3:
                ref[bi, :, lanes] = rows
            else:
                for l in range(ref.shape[1]):
                    ref[bi, l, :, lanes] = rows if l == layer else jnp.zeros_like(rows)

    for j in range(N_PROJ_BLOCKS):
        c0 = j * PROJ_TN
        acc = jnp.dot(h, w_ref[:, c0:c0 + PROJ_TN], preferred_element_type=_F32)
        if c0 < D_ATTN:
            for hh in range(PROJ_TN // HEAD_DIM):
                sl = slice(hh * HEAD_DIM, (hh + 1) * HEAD_DIM)
                q_ref[:, c0 + hh * HEAD_DIM:c0 + (hh + 1) * HEAD_DIM] = (
                    (maybe_rope(acc[:, sl]) * Q_PRESCALE).astype(q_ref.dtype))
        elif c0 == D_ATTN:
            for hh in range(N_KV_HEADS):
                sl = slice(hh * HEAD_DIM, (hh + 1) * HEAD_DIM)
                put_kv(k_ref, sl, maybe_rope(acc[:, sl]))
                put_kv(v_ref, sl, acc[:, D_KV + hh * HEAD_DIM:D_KV + (hh + 1) * HEAD_DIM])
        else:
            r0 = c0 - D_ATTN - 2 * D_KV
            r_ref[:, r0:r0 + PROJ_TN] = acc.astype(r_ref.dtype)


def _project(x2, mod3, w_in_bf, layer, cond_row_of_block, tm, rope_tabs=None, cache=None):
    n = x2.shape[0]
    rotary = rope_tabs is not None
    in_specs = [
        pl.BlockSpec((tm, D_MODEL), lambda i: (i, 0)),
        pl.BlockSpec((None, 1, D_MODEL), lambda i: (cond_row_of_block(i), 0, 0)),
        pl.BlockSpec((None, 1, D_MODEL), lambda i: (cond_row_of_block(i), 0, 1)),
        pl.BlockSpec((None, D_MODEL, D_IN), lambda i: (layer, 0, 0), pipeline_mode=pl.Buffered(1)),
    ]
    args = [x2, mod3, mod3, w_in_bf]
    aliases = {}
    if rotary:
        cos, sin = rope_tabs
        blocks_per_seq = cos.shape[0] // tm
        in_specs += [pl.BlockSpec((tm, HEAD_DIM), lambda i: (i % blocks_per_seq, 0))] * 2
        args += [cos, sin]
        kv_spec = pl.BlockSpec((tm, D_KV), lambda i: (i, 0))
        kv_shape = jax.ShapeDtypeStruct((n, D_KV), _BF16)
    else:
        batch, s_len, prev = cache
        kv_shape = jax.ShapeDtypeStruct((batch, DEPTH, s_len, D_KV), _F32)
        if prev is None:
            kv_spec = pl.BlockSpec((tm // s_len, DEPTH, s_len, D_KV), lambda i: (i, 0, 0, 0))
        else:
            kv_spec = pl.BlockSpec((tm // s_len, None, s_len, D_KV), lambda i: (i, layer, 0, 0))
            aliases = {len(args): 1, len(args) + 1: 2}
            in_specs += [pl.BlockSpec(memory_space=pl.ANY)] * 2
            args += list(prev)
    return pl.pallas_call(
        functools.partial(_proj_kernel, rotary=rotary, layer=layer),
        grid=(n // tm,),
        in_specs=in_specs,
        out_specs=[
            pl.BlockSpec((tm, D_ATTN), lambda i: (i, 0)),
            kv_spec,
            kv_spec,
            pl.BlockSpec((tm, D_REST), lambda i: (i, 0)),
        ],
        out_shape=[
            jax.ShapeDtypeStruct((n, D_ATTN), _BF16),
            kv_shape,
            kv_shape,
            jax.ShapeDtypeStruct((n, D_REST), _BF16),
        ],
        input_output_aliases=aliases,
        compiler_params=pltpu.CompilerParams(
            dimension_semantics=("parallel",), vmem_limit_bytes=PROJ_VMEM_LIMIT),
        name="proj_rope" if rotary else "proj",
    )(*args)


def _softmax_pv(score_parts, sink_col, value_parts):
    m = sink_col
    for s in score_parts:
        m = jnp.maximum(m, jnp.max(s, axis=-1, keepdims=True))
    acc = None
    for s, v_aug in zip(score_parts, value_parts):
        part = jnp.dot(jnp.exp2(s - m).astype(_BF16), v_aug, preferred_element_type=_F32)
        acc = part if acc is None else acc + part
    denom = acc[:, HEAD_DIM:HEAD_DIM + 1] + jnp.exp2(sink_col - m)
    return acc[:, :HEAD_DIM] / denom


def _with_ones(v_bf):
    return jnp.concatenate([v_bf, jnp.ones_like(v_bf)], axis=1)


def _sink_column(sink_ref, kvh, rows):
    head = lax.broadcasted_iota(jnp.int32, (Q_GROUP * rows, 1), 0) // rows
    col = jnp.full((Q_GROUP * rows, 1), sink_ref[kvh * Q_GROUP], _F32)
    for g in range(1, Q_GROUP):
        col = jnp.where(head == g, sink_ref[kvh * Q_GROUP + g], col)
    return col * LOG2_E


def _stack_heads(q_rows):
    return jnp.concatenate(
        [q_rows[:, g * HEAD_DIM:(g + 1) * HEAD_DIM] for g in range(Q_GROUP)], axis=0)


def _qk(q, k_bf):
    return lax.dot_general(q, k_bf, (((1,), (1,)), ((), ())), preferred_element_type=_F32)


def _ctx_attn_kernel(sink_ref, q_ref, k_ref, v_ref, o_ref, *, s_len):
    kvh = pl.program_id(1)
    sink_col = _sink_column(sink_ref, kvh, s_len)

    def one_sequence(bi, carry):
        rows = pl.ds(pl.multiple_of(bi * s_len, s_len), s_len)
        q = _stack_heads(q_ref[rows, :])
        k = k_ref[bi].astype(_BF16)
        v_aug = _with_ones(v_ref[bi].astype(_BF16))
        o = _softmax_pv([_qk(q, k)], sink_col, [v_aug])
        for g in range(Q_GROUP):
            o_ref[rows, g * HEAD_DIM:(g + 1) * HEAD_DIM] = o[g * s_len:(g + 1) * s_len].astype(o_ref.dtype)
        return carry

    lax.fori_loop(0, q_ref.shape[0] // s_len, one_sequence, 0)


def _context_attention(q, k_cache, v_cache, layer, sink, batch, s_len):
    n = batch * s_len
    tm = CTX_ATTN_SEQS * s_len
    gw = Q_GROUP * HEAD_DIM
    kv_spec = pl.BlockSpec((CTX_ATTN_SEQS, None, s_len, HEAD_DIM), lambda b, h: (b, layer, 0, h))
    return pl.pallas_call(
        functools.partial(_ctx_attn_kernel, s_len=s_len),
        grid=(n // tm, N_KV_HEADS),
        in_specs=[
            pl.BlockSpec(memory_space=pltpu.SMEM),
            pl.BlockSpec((tm, gw), lambda b, h: (b, h)),
            kv_spec,
            kv_spec,
        ],
        out_specs=pl.BlockSpec((tm, gw), lambda b, h: (b, h)),
        out_shape=jax.ShapeDtypeStruct((n, D_ATTN), _BF16),
        compiler_params=pltpu.CompilerParams(dimension_semantics=("parallel", "parallel")),
        name="ctx_attention",
    )(sink, q, k_cache, v_cache)


def _lat_attn_kernel(sink_ref, q_ref, k_ref, v_ref, ck_ref, cv_ref, o_ref):
    kvh = pl.program_id(1)
    t_len = q_ref.shape[0]
    nb = t_len // Q_BLOCK
    ck = ck_ref[...].astype(_BF16)
    cv_aug = _with_ones(cv_ref[...].astype(_BF16))
    v_aug = _with_ones(v_ref[...])
    sink_col = _sink_column(sink_ref, kvh, Q_BLOCK)
    rows = Q_GROUP * Q_BLOCK
    r = lax.broadcasted_iota(jnp.int32, (rows, Q_BLOCK), 0) % Q_BLOCK
    c = lax.broadcasted_iota(jnp.int32, (rows, Q_BLOCK), 1)
    for jb in range(nb):
        r0 = jb * Q_BLOCK
        q = _stack_heads(q_ref[r0:r0 + Q_BLOCK, :])
        lo = max(r0 - WINDOW, 0)
        hi = min(r0 + Q_BLOCK + WINDOW, t_len)
        s = _qk(q, k_ref[lo:hi, :])
        parts = []
        off = 0
        if jb > 0:
            parts.append(jnp.where(c >= r, s[:, :Q_BLOCK], NEG_INF))
            off = Q_BLOCK
        parts.append(s[:, off:off + Q_BLOCK])
        off += Q_BLOCK
        if jb < nb - 1:
            parts.append(jnp.where(c <= r, s[:, off:], NEG_INF))
        s_local = jnp.concatenate(parts, axis=1)
        o = _softmax_pv([s_local, _qk(q, ck)], sink_col, [v_aug[lo:hi], cv_aug])
        for g in range(Q_GROUP):
            o_ref[r0:r0 + Q_BLOCK, g * HEAD_DIM:(g + 1) * HEAD_DIM] = (
                o[g * Q_BLOCK:(g + 1) * Q_BLOCK].astype(o_ref.dtype))


def _latent_attention(q, k, v, cache_k4, cache_v4, layer, sink, batch, t_len):
    n = batch * t_len
    gw = Q_GROUP * HEAD_DIM
    past = cache_k4.shape[2]
    cache_spec = pl.BlockSpec((None, None, past, HEAD_DIM), lambda b, h: (b, layer, 0, h))
    return pl.pallas_call(
        _lat_attn_kernel,
        grid=(batch, N_KV_HEADS),
        in_specs=[
            pl.BlockSpec(memory_space=pltpu.SMEM),
            pl.BlockSpec((t_len, gw), lambda b, h: (b, h)),
            pl.BlockSpec((t_len, HEAD_DIM), lambda b, h: (b, h)),
            pl.BlockSpec((t_len, HEAD_DIM), lambda b, h: (b, h)),
            cache_spec,
            cache_spec,
        ],
        out_specs=pl.BlockSpec((t_len, gw), lambda b, h: (b, h)),
        out_shape=jax.ShapeDtypeStruct((n, D_ATTN), _BF16),
        compiler_params=pltpu.CompilerParams(
            dimension_semantics=("parallel", "parallel"), vmem_limit_bytes=VMEM_LIMIT),
        name="latent_attention",
    )(sink, q, k, v, cache_k4, cache_v4)


def _shift_rows(x, d, t_len, offset=0):
    n = x.shape[0]
    row = (lax.broadcasted_iota(jnp.int32, x.shape, 0) + (n - offset)) % t_len
    rolled = pltpu.roll(x, d % n, axis=0)
    valid = (row >= d) if d > 0 else (row < t_len + d)
    return jnp.where(valid, rolled, 0.0)


def _sqrt_nonneg(z):
    return z * lax.rsqrt(jnp.maximum(z, F32_TINY))


def _rnn_kernel(x_ref, cw_ref, cb_ref, wg_ref, bg_ref, lam_ref, h0_ref, y_ref, hl_ref,
                af_scr, bf_scr, ab_scr, bb_scr, hf_scr, hb_scr, *, nb, nc):
    t_len = x_ref.shape[1]
    pitch = t_len + SLOT_PAD
    scr = ((af_scr, bf_scr), (ab_scr, bb_scr))
    per_group = SEQ_SLOTS // 2
    kb = min(per_group, max(1, PASS1_ROWS // t_len))

    def seq_of(slot):
        g, u = slot % 2, slot // 2
        return (g * per_group + u, 0) if nc == 1 else (u, g)

    for g in range(2):
        b_first, cl = seq_of(g)
        offset = SLOT_PAD * g
        lanes = slice(cl * LANES, (cl + 1) * LANES)
        cw = cw_ref[:, lanes]
        cb = cb_ref[:, lanes]
        wg = wg_ref[cl]
        bg = bg_ref[cl]
        lam = lam_ref[:, lanes]
        c_softplus = LRU_C * (jnp.maximum(-lam, 0.0) + jnp.log1p(jnp.exp(-jnp.abs(lam))))

        def per_chunk(it, carry, lanes=lanes, cw=cw, cb=cb, wg=wg, bg=bg,
                      c_softplus=c_softplus, g=g, b_first=b_first, offset=offset):
            n_rows = kb * t_len
            x = x_ref[pl.ds(b_first + it * kb, kb), :, lanes].astype(_F32).reshape(n_rows, LANES)
            if offset:
                x = pltpu.roll(x, offset, axis=0)
            xc = cw[0:1] * _shift_rows(x, 2, t_len, offset)
            xc = xc + cw[1:2] * _shift_rows(x, 1, t_len, offset)
            xc = xc + cw[2:3] * x
            xc = xc + cw[3:4] * _shift_rows(x, -1, t_len, offset)
            xc = xc + cb
            half_gates = jnp.dot(xc.astype(_BF16), wg, preferred_element_type=_F32) + bg
            half_xc = 0.5 * xc
            for d in range(2):
                t_r = jnp.tanh(half_gates[:, (2 * d) * LANES:(2 * d + 1) * LANES])
                t_i = jnp.tanh(half_gates[:, (2 * d + 1) * LANES:(2 * d + 2) * LANES])
                k = 0.5 * c_softplus[d:d + 1]
                neg_log_a = t_r * k + k
                a = jnp.exp2(neg_log_a * -LOG2_E)
                bterm = _sqrt_nonneg(jnp.tanh(neg_log_a) * (a * a + 1.0)) * ((t_i + 1.0) * half_xc)
                for jb in range(kb):
                    slot = 2 * (it * kb + jb) + g
                    row0 = pl.multiple_of(slot * pitch - offset, SUBLANES)
                    for dst, val in ((scr[d][0], a), (scr[d][1], bterm)):
                        dst[pl.ds(row0, t_len), :] = val[jb * t_len:(jb + 1) * t_len]
                        if offset:
                            w0 = ((jb + 1) * t_len) % n_rows
                            dst[pl.ds(row0 + t_len, SUBLANES), :] = val[w0:w0 + SUBLANES]
            return carry

        if per_group == kb:
            per_chunk(0, 0)
        else:
            lax.fori_loop(0, per_group // kb, per_chunk, 0)

    def slot_rows(ref3, d):
        rows = []
        for slot in range(SEQ_SLOTS):
            b, cl = seq_of(slot)
            rows.append(ref3[b, d:d + 1, cl * LANES:(cl + 1) * LANES])
        return jnp.concatenate(rows, axis=0)

    for slot in range(SEQ_SLOTS):
        tile = (slot * pitch + t_len) // SUBLANES * SUBLANES
        for h_scr in (hf_scr, hb_scr):
            h_scr[pl.ds(tile, SUBLANES), :] = jnp.zeros((SUBLANES, LANES), _F32)

    def advance2(a_scr, b_scr, h_scr, t0, t1, h):
        r0 = pl.ds(t0, SEQ_SLOTS, stride=pitch)
        r1 = pl.ds(t1, SEQ_SLOTS, stride=pitch)
        a0, b0, a1, b1 = a_scr[r0, :], b_scr[r0, :], a_scr[r1, :], b_scr[r1, :]
        h_scr[r0, :] = a0 * h + b0
        h = (a1 * a0) * h + (a1 * b0 + b1)
        h_scr[r1, :] = h
        return h

    hf, hb = slot_rows(h0_ref, 0), slot_rows(h0_ref, 1)
    for t in range(0, t_len, 2):
        hf = advance2(af_scr, bf_scr, hf_scr, t, t + 1, hf)
        hb = advance2(ab_scr, bb_scr, hb_scr, t_len - 1 - t, t_len - 2 - t, hb)

    for slot in range(SEQ_SLOTS):
        b, cl = seq_of(slot)
        lanes = slice(cl * LANES, (cl + 1) * LANES)
        hl_ref[b, 0:1, lanes] = hf[slot:slot + 1, :]
        hl_ref[b, 1:2, lanes] = hb[slot:slot + 1, :]
        if slot % 2 == 0:
            rows = pl.ds(slot * pitch, t_len)
            y = hf_scr[rows, :] + hb_scr[rows, :]
        else:
            rows = pl.ds(slot * pitch - SLOT_PAD, t_len + SUBLANES)
            y = pltpu.roll(hf_scr[rows, :] + hb_scr[rows, :], t_len + SUBLANES - SLOT_PAD, axis=0)[:t_len]
        y_ref[b, :, lanes] = y.astype(y_ref.dtype)


def _rnn(rest, h0, conv_w, conv_b, wg, bg, lam, batch, t_len):
    nb = min(batch, SEQ_SLOTS)
    nc = SEQ_SLOTS // nb
    assert nc in (1, 2) and batch % nb == 0, "a grid step holds 8 batches x 1 or 4 batches x 2 channel blocks"
    cw = nc * LANES
    xr_block0 = D_MODEL // cw
    scratch = pltpu.VMEM((SEQ_SLOTS * (t_len + SLOT_PAD), LANES), _F32)
    y, h_last = pl.pallas_call(
        functools.partial(_rnn_kernel, nb=nb, nc=nc),
        grid=(batch // nb, D_RNN // cw),
        in_specs=[
            pl.BlockSpec((nb, t_len, cw), lambda g, c: (g, 0, xr_block0 + c)),
            pl.BlockSpec((CONV_W, cw), lambda g, c: (0, c)),
            pl.BlockSpec((1, cw), lambda g, c: (0, c)),
            pl.BlockSpec((nc, RNN_BLOCK, 4 * LANES), lambda g, c: (c, 0, 0)),
            pl.BlockSpec((nc, 1, 4 * LANES), lambda g, c: (c, 0, 0)),
            pl.BlockSpec((2, cw), lambda g, c: (0, c)),
            pl.BlockSpec((nb, 2, cw), lambda g, c: (g, 0, c)),
        ],
        out_specs=[
            pl.BlockSpec((nb, t_len, cw), lambda g, c: (g, 0, c)),
            pl.BlockSpec((nb, 2, cw), lambda g, c: (g, 0, c)),
        ],
        out_shape=[
            jax.ShapeDtypeStruct((batch, t_len, D_RNN), _BF16),
            jax.ShapeDtypeStruct((batch, 2, D_RNN), _F32),
        ],
        scratch_shapes=[scratch] * 6,
        compiler_params=pltpu.CompilerParams(
            dimension_semantics=("parallel", "parallel"), vmem_limit_bytes=RNN_VMEM_LIMIT),
        name="rnn",
    )(rest.reshape(batch, t_len, D_REST), conv_w, conv_b.reshape(1, D_RNN), wg, bg, lam, h0)
    return y.reshape(batch * t_len, D_RNN), h_last


def _out_kernel(x_ref, gate_ref, oa_ref, ga_ref, or_ref, gr_ref, mga_ref, mgr_ref,
                wa_ref, wr_ref, wo_ref, g_ref, b_ref, o_ref):
    def silu_bf(x):
        h = 0.5 * x
        return h * jnp.tanh(h) + h

    def logistic_f32(x):
        return (0.5 * jnp.tanh(0.5 * x) + 0.5).astype(_F32)

    ya = jnp.dot(oa_ref[...] * silu_bf(ga_ref[...]), wa_ref[...], preferred_element_type=_F32)
    yr = jnp.dot(or_ref[...] * silu_bf(gr_ref[...]), wr_ref[...], preferred_element_type=_F32)
    merged = logistic_f32(mga_ref[...]) * ya + logistic_f32(mgr_ref[...]) * yr
    out = jnp.dot(merged.astype(_BF16), wo_ref[...], preferred_element_type=_F32)
    z = DEEPNORM_ALPHA * x_ref[...] + gate_ref[...] * out
    o_ref[...] = _layer_norm_rows(z) * g_ref[...] + b_ref[...]


def _merge_residual(x2, mod3, o_attn, o_rnn, rest, wa_bf, wr_bf, wo_bf, layer, ln_g, ln_b,
                    cond_row_of_block, tm):
    n = x2.shape[0]
    row_blk = lambda col: pl.BlockSpec((tm, D_MODEL), lambda i, col=col: (i, col))
    whole = pl.BlockSpec((None, D_MODEL, D_MODEL), lambda i: (layer, 0, 0), pipeline_mode=pl.Buffered(1))
    vec = pl.BlockSpec((1, D_MODEL), lambda i: (0, 0))
    return pl.pallas_call(
        _out_kernel,
        grid=(n // tm,),
        in_specs=[
            row_blk(0),
            pl.BlockSpec((None, 1, D_MODEL), lambda i: (cond_row_of_block(i), 0, 2)),
            row_blk(0), row_blk(0), row_blk(0), row_blk(2), row_blk(3), row_blk(4),
            whole, whole, whole, vec, vec,
        ],
        out_specs=row_blk(0),
        out_shape=jax.ShapeDtypeStruct((n, D_MODEL), _F32),
        compiler_params=pltpu.CompilerParams(
            dimension_semantics=("parallel",), vmem_limit_bytes=MERGE_VMEM_LIMIT),
        name="merge_residual",
    )(x2, mod3, o_attn, rest, o_rnn, rest, rest, rest, wa_bf, wr_bf, wo_bf,
      ln_g.reshape(1, D_MODEL), ln_b.reshape(1, D_MODEL))


def _rope_tables(t_len):
    quarter = HEAD_DIM // 4
    inv_freq = ROPE_BASE ** (-jnp.arange(quarter, dtype=_F32) / quarter)
    pos = jnp.arange(t_len)
    ang_r = (pos // GRID_W).astype(_F32)[:, None] * inv_freq[None, :]
    ang_c = (pos % GRID_W).astype(_F32)[:, None] * inv_freq[None, :]
    cos = jnp.concatenate([jnp.cos(ang_r)] * 2 + [jnp.cos(ang_c)] * 2, axis=1)
    sin = jnp.concatenate([-jnp.sin(ang_r), jnp.sin(ang_r), -jnp.sin(ang_c), jnp.sin(ang_c)], axis=1)
    return cos, sin


def _gate_weights(rg_wa, rg_ba, rg_wx, rg_bx):
    wg = (0.5 * jnp.concatenate([rg_wa[0], rg_wx[0], rg_wa[1], rg_wx[1]], axis=-1)).astype(_BF16)
    blk = lambda v: v.reshape(N_RNN_BLOCKS, 1, RNN_BLOCK)
    bg = 0.5 * jnp.concatenate([blk(rg_ba[0]), blk(rg_bx[0]), blk(rg_ba[1]), blk(rg_bx[1])], axis=-1)
    return wg, bg


def kernel(x_prompt, x_sample, cache_k, cache_v, state_h, c, c_ctx, w_mod, b_mod, w_in,
           attn_sink, conv_w, conv_b, rg_wa, rg_ba, rg_wx, rg_bx, rg_lambda,
           w_br_attn, w_br_rnn, w_out, ln_g, ln_b):
    bp, s_len, _ = x_prompt.shape
    bd, t_len, _ = x_sample.shape
    past = cache_k.shape[2]

    cond8 = jnp.zeros((COND_ROWS, D_MODEL), _F32).at[0].set(c_ctx).at[1:1 + bd].set(c)
    mod = _modulation(cond8, w_mod, b_mod)

    w_in_bf = w_in.astype(_BF16)
    wa_bf = w_br_attn.astype(_BF16)
    wr_bf = w_br_rnn.astype(_BF16)
    wo_bf = w_out.astype(_BF16)
    rope_tabs = _rope_tables(t_len)
    cache_k4 = cache_k.reshape(bd, DEPTH, past, D_KV)
    cache_v4 = cache_v.reshape(bd, DEPTH, past, D_KV)

    tm_proj, tm_out = 1024, 1024
    ctx_row = lambda i: 0
    lat_row_proj = lambda i: 1 + (i * tm_proj) // t_len
    lat_row_out = lambda i: 1 + (i * tm_out) // t_len

    xp = x_prompt.reshape(bp * s_len, D_MODEL)
    xs = x_sample.reshape(bd * t_len, D_MODEL)
    h0_zero = jnp.zeros((bp, 2, D_RNN), _F32)
    hs = []
    kv_caches = None
    for l in range(DEPTH):
        mod3 = mod[l].reshape(COND_ROWS, 1, 3 * D_MODEL)
        wg, bg = _gate_weights(rg_wa[l], rg_ba[l], rg_wx[l], rg_bx[l])

        q, k_cache, v_cache, rest = _project(xp, mod3, w_in_bf, l, ctx_row, tm_proj,
                                             cache=(bp, s_len, kv_caches))
        kv_caches = (k_cache, v_cache)
        o_attn = _context_attention(q, k_cache, v_cache, l, attn_sink[l], bp, s_len)
        o_rnn, h_fin = _rnn(rest, h0_zero, conv_w[l], conv_b[l], wg, bg, rg_lambda[l], bp, s_len)
        xp = _merge_residual(xp, mod3, o_attn, o_rnn, rest, wa_bf, wr_bf, wo_bf, l,
                             ln_g[l], ln_b[l], ctx_row, tm_out)
        hs.append(h_fin)

        q, k, v, rest = _project(xs, mod3, w_in_bf, l, lat_row_proj, tm_proj, rope_tabs=rope_tabs)
        o_attn = _latent_attention(q, k, v, cache_k4, cache_v4, l, attn_sink[l], bd, t_len)
        o_rnn, _ = _rnn(rest, state_h[:, l], conv_w[l], conv_b[l], wg, bg, rg_lambda[l], bd, t_len)
        xs = _merge_residual(xs, mod3, o_attn, o_rnn, rest, wa_bf, wr_bf, wo_bf, l,
                             ln_g[l], ln_b[l], lat_row_out, tm_out)

    y_prompt = xp.reshape(bp, s_len, D_MODEL)
    y_sample = xs.reshape(bd, t_len, D_MODEL)
    cache_shape = (bp, DEPTH, s_len, N_KV_HEADS, HEAD_DIM)
    return (y_prompt, y_sample, kv_caches[0].reshape(cache_shape), kv_caches[1].reshape(cache_shape),
            jnp.stack(hs, axis=1))
```

```python
import functools

import jax
import jax.numpy as jnp
from jax import lax
from jax.experimental import pallas as pl
from jax.experimental.pallas import tpu as pltpu

D_MODEL = 1024
DEPTH = 2
GRID_W = 64
N_HEADS = 8
N_KV_HEADS = 2
HEAD_DIM = 128
Q_GROUP = N_HEADS // N_KV_HEADS
D_ATTN = N_HEADS * HEAD_DIM
D_KV = N_KV_HEADS * HEAD_DIM
WINDOW = 128
Q_BLOCK = 128
D_RNN = D_MODEL
N_RNN_BLOCKS = 8
RNN_BLOCK = D_RNN // N_RNN_BLOCKS
CONV_W = 4
CONV_LEFT = 2
LRU_C = 8.0
ROPE_BASE = 10000.0
D_IN = 2 * D_ATTN + 2 * D_KV + 2 * D_RNN + 2 * D_MODEL
DEEPNORM_ALPHA = (2 * DEPTH) ** 0.25
LN_EPS = 1e-6
NEG_INF = -1e30
ATTN_SCALE = HEAD_DIM ** -0.5

LANES = 128
SUBLANES = 8
SEQ_SLOTS = SUBLANES
SLOT_PAD = 4
CTX_ATTN_SEQS = 4
PASS1_ROWS = 1024
COND_ROWS = 8
PROJ_TN = 512
N_PROJ_BLOCKS = D_IN // PROJ_TN
D_REST = D_IN - D_ATTN - 2 * D_KV
VMEM_LIMIT = 48 * 1024 * 1024
PROJ_VMEM_LIMIT = 56 * 1024 * 1024
RNN_VMEM_LIMIT = 56 * 1024 * 1024
F32_TINY = 1.1754943508222875e-38
LOG2_E = 1.4426950408889634
Q_PRESCALE = ATTN_SCALE * LOG2_E

_BF16 = jnp.bfloat16
_F32 = jnp.float32


def _silu(x):
    return x * jax.nn.sigmoid(x)


def _layer_norm_rows(x):
    mu = jnp.mean(x, axis=-1, keepdims=True)
    xc = x - mu
    var = jnp.mean(xc * xc, axis=-1, keepdims=True)
    return xc * lax.rsqrt(var + LN_EPS)


def _mod_kernel(cond_ref, w_ref, b_ref, o_ref):
    a = _silu(cond_ref[...]).astype(_BF16)
    o_ref[...] = jnp.dot(a, w_ref[...].astype(_BF16), preferred_element_type=_F32) + b_ref[...]


def _modulation(cond8, w_mod, b_mod):
    tn = D_MODEL
    return pl.pallas_call(
        _mod_kernel,
        grid=(DEPTH, 3 * D_MODEL // tn),
        in_specs=[
            pl.BlockSpec((COND_ROWS, D_MODEL), lambda l, j: (0, 0)),
            pl.BlockSpec((None, D_MODEL, tn), lambda l, j: (l, 0, j)),
            pl.BlockSpec((None, 1, tn), lambda l, j: (l, 0, j)),
        ],
        out_specs=pl.BlockSpec((None, COND_ROWS, tn), lambda l, j: (l, 0, j)),
        out_shape=jax.ShapeDtypeStruct((DEPTH, COND_ROWS, 3 * D_MODEL), _F32),
        compiler_params=pltpu.CompilerParams(dimension_semantics=("parallel", "parallel")),
        name="modulation",
    )(cond8, w_mod, b_mod.reshape(DEPTH, 1, 3 * D_MODEL))


def _rope(x, cos, sin_signed):
    lane = lax.broadcasted_iota(jnp.int32, x.shape, 1)
    partner = jnp.where((lane // 32) % 2 == 0,
                        pltpu.roll(x, 3 * 32, axis=1),
                        pltpu.roll(x, 32, axis=1))
    return x * cos + partner * sin_signed


def _proj_kernel(xc_ref, xl_ref, shift_ref, scale_ref, w_hbm, cos_ref, sin_ref, *rest,
                 layer, n_ctx_blocks):
    q_ref, kc_ref, vc_ref, kl_ref, vl_ref, r_ref, w_scr, stage_scr, sem = rest[-9:]
    i = pl.program_id(0)
    is_ctx = i < n_ctx_blocks

    def chunk_copy(j):
        return pltpu.make_async_copy(
            w_hbm.at[layer, :, pl.ds(j * PROJ_TN, PROJ_TN)], stage_scr.at[j % 2], sem.at[j % 2])

    @pl.when(i == 0)
    def _():
        chunk_copy(0).start()
        for j in range(N_PROJ_BLOCKS):
            if j + 1 < N_PROJ_BLOCKS:
                chunk_copy(j + 1).start()
            chunk_copy(j).wait()
            w_scr[:, j * PROJ_TN:(j + 1) * PROJ_TN] = stage_scr[j % 2].astype(_BF16)

    x = jnp.where(is_ctx, xc_ref[...], xl_ref[...])
    y = _layer_norm_rows(x)
    h = (y * (1.0 + scale_ref[...]) + shift_ref[...]).astype(_BF16)

    def rope(blk):
        return _rope(blk, cos_ref[...], sin_ref[...])

    def put_cache(ref, lanes, val):
        s_len = ref.shape[-2]
        for bi in range(ref.shape[0]):
            rows = val[bi * s_len:(bi + 1) * s_len].astype(ref.dtype)
            if len(ref.shape) == 3:
                ref[bi, :, lanes] = rows
            else:
                for l in range(ref.shape[1]):
                    ref[bi, l, :, lanes] = rows if l == layer else jnp.zeros_like(rows)

    for j in range(N_PROJ_BLOCKS):
        c0 = j * PROJ_TN
        acc = jnp.dot(h, w_scr[:, c0:c0 + PROJ_TN], preferred_element_type=_F32)
        if c0 < D_ATTN:
            for hh in range(PROJ_TN // HEAD_DIM):
                sl = slice(hh * HEAD_DIM, (hh + 1) * HEAD_DIM)
                q_ref[:, c0 + hh * HEAD_DIM:c0 + (hh + 1) * HEAD_DIM] = (
                    (rope(acc[:, sl]) * Q_PRESCALE).astype(q_ref.dtype))
        elif c0 == D_ATTN:
            for hh in range(N_KV_HEADS):
                sl = slice(hh * HEAD_DIM, (hh + 1) * HEAD_DIM)
                k_rot = rope(acc[:, sl])
                put_cache(kc_ref, sl, k_rot)
                put_cache(vc_ref, sl, acc[:, D_KV + hh * HEAD_DIM:D_KV + (hh + 1) * HEAD_DIM])
                kl_ref[:, sl] = k_rot.astype(kl_ref.dtype)
            vl_ref[...] = acc[:, D_KV:].astype(vl_ref.dtype)
        else:
            r0 = c0 - D_ATTN - 2 * D_KV
            r_ref[:, r0:r0 + PROJ_TN] = acc.astype(r_ref.dtype)


def _project(x_ctx, x_lat, mod3, w_in, layer, tm, rope_tabs, ctx_dims, lat_dims, prev_caches):
    (bp, s_len), (bd, t_len) = ctx_dims, lat_dims
    n_ctx, n_lat = bp * s_len, bd * t_len
    n = n_ctx + n_lat
    nc_blocks = n_ctx // tm
    ctx_blk = lambda i: jnp.minimum(i, nc_blocks - 1)
    lat_blk = lambda i: jnp.maximum(i - nc_blocks, 0)
    cond_row = lambda i: jnp.where(i < nc_blocks, 0, 1 + (lat_blk(i) * tm) // t_len)
    cos, sin = rope_tabs
    blocks_per_seq = t_len // tm
    cos = jnp.concatenate([jnp.ones((tm, HEAD_DIM), _F32), cos], axis=0)
    sin = jnp.concatenate([jnp.zeros((tm, HEAD_DIM), _F32), sin], axis=0)
    tab_blk = lambda i: jnp.where(i < nc_blocks, 0, 1 + lat_blk(i) % blocks_per_seq)

    in_specs = [
        pl.BlockSpec((tm, D_MODEL), lambda i: (ctx_blk(i), 0)),
        pl.BlockSpec((tm, D_MODEL), lambda i: (lat_blk(i), 0)),
        pl.BlockSpec((None, 1, D_MODEL), lambda i: (cond_row(i), 0, 0)),
        pl.BlockSpec((None, 1, D_MODEL), lambda i: (cond_row(i), 0, 1)),
        pl.BlockSpec(memory_space=pl.ANY),
        pl.BlockSpec((tm, HEAD_DIM), lambda i: (tab_blk(i), 0)),
        pl.BlockSpec((tm, HEAD_DIM), lambda i: (tab_blk(i), 0)),
    ]
    args = [x_ctx, x_lat, mod3, mod3, w_in, cos, sin]
    aliases = {}
    seqs = tm // s_len
    cache_blk = lambda i: jnp.minimum(i, nc_blocks)
    if prev_caches is None:
        cache_spec = pl.BlockSpec((seqs, DEPTH, s_len, D_KV), lambda i: (cache_blk(i), 0, 0, 0))
    else:
        cache_spec = pl.BlockSpec((seqs, None, s_len, D_KV), lambda i: (cache_blk(i), layer, 0, 0))
        aliases = {len(args): 1, len(args) + 1: 2}
        in_specs += [pl.BlockSpec(memory_space=pl.ANY)] * 2
        args += list(prev_caches)
    cache_shape = jax.ShapeDtypeStruct((bp + seqs, DEPTH, s_len, D_KV), _F32)
    lat_kv_spec = pl.BlockSpec((tm, D_KV), lambda i: (lat_blk(i), 0))
    lat_kv_shape = jax.ShapeDtypeStruct((n_lat, D_KV), _BF16)
    q, k_cache, v_cache, k_lat, v_lat, rest = pl.pallas_call(
        functools.partial(_proj_kernel, layer=layer, n_ctx_blocks=nc_blocks),
        grid=(n // tm,),
        in_specs=in_specs,
        out_specs=[
            pl.BlockSpec((tm, D_ATTN), lambda i: (i, 0)),
            cache_spec, cache_spec, lat_kv_spec, lat_kv_spec,
            pl.BlockSpec((tm, D_REST), lambda i: (i, 0)),
        ],
        out_shape=[
            jax.ShapeDtypeStruct((n, D_ATTN), _BF16),
            cache_shape, cache_shape, lat_kv_shape, lat_kv_shape,
            jax.ShapeDtypeStruct((n, D_REST), _BF16),
        ],
        scratch_shapes=[
            pltpu.VMEM((D_MODEL, D_IN), _BF16),
            pltpu.VMEM((2, D_MODEL, PROJ_TN), _F32),
            pltpu.SemaphoreType.DMA((2,)),
        ],
        input_output_aliases=aliases,
        compiler_params=pltpu.CompilerParams(
            dimension_semantics=("arbitrary",), vmem_limit_bytes=PROJ_VMEM_LIMIT),
        name="proj",
    )(*args)
    return q, (k_cache, v_cache), (k_lat, v_lat), rest


def _softmax_pv(score_parts, sink_col, value_parts):
    m = sink_col
    for s in score_parts:
        m = jnp.maximum(m, jnp.max(s, axis=-1, keepdims=True))
    acc = None
    for s, v_aug in zip(score_parts, value_parts):
        part = jnp.dot(jnp.exp2(s - m).astype(_BF16), v_aug, preferred_element_type=_F32)
        acc = part if acc is None else acc + part
    denom = acc[:, HEAD_DIM:HEAD_DIM + 1] + jnp.exp2(sink_col - m)
    return acc[:, :HEAD_DIM] / denom


def _with_ones(v_bf):
    return jnp.concatenate([v_bf, jnp.ones_like(v_bf)], axis=1)


def _sink_column(sink_ref, kvh, rows):
    head = lax.broadcasted_iota(jnp.int32, (Q_GROUP * rows, 1), 0) // rows
    col = jnp.full((Q_GROUP * rows, 1), sink_ref[kvh * Q_GROUP], _F32)
    for g in range(1, Q_GROUP):
        col = jnp.where(head == g, sink_ref[kvh * Q_GROUP + g], col)
    return col * LOG2_E


def _stack_heads(q_rows):
    return jnp.concatenate(
        [q_rows[:, g * HEAD_DIM:(g + 1) * HEAD_DIM] for g in range(Q_GROUP)], axis=0)


def _qk(q, k_bf):
    return lax.dot_general(q, k_bf, (((1,), (1,)), ((), ())), preferred_element_type=_F32)


def _ctx_attn_kernel(sink_ref, q_ref, k_ref, v_ref, o_ref, *, s_len):
    kvh = pl.program_id(1)
    sink_col = _sink_column(sink_ref, kvh, s_len)

    def one_sequence(bi, carry):
        rows = pl.ds(pl.multiple_of(bi * s_len, s_len), s_len)
        q = _stack_heads(q_ref[rows, :])
        k = k_ref[bi].astype(_BF16)
        v_aug = _with_ones(v_ref[bi].astype(_BF16))
        o = _softmax_pv([_qk(q, k)], sink_col, [v_aug])
        for g in range(Q_GROUP):
            o_ref[rows, g * HEAD_DIM:(g + 1) * HEAD_DIM] = o[g * s_len:(g + 1) * s_len].astype(o_ref.dtype)
        return carry

    lax.fori_loop(0, q_ref.shape[0] // s_len, one_sequence, 0)


def _context_attention(q, k_cache, v_cache, layer, sink, batch, s_len):
    n = batch * s_len
    tm = CTX_ATTN_SEQS * s_len
    gw = Q_GROUP * HEAD_DIM
    kv_spec = pl.BlockSpec((CTX_ATTN_SEQS, None, s_len, HEAD_DIM), lambda b, h: (b, layer, 0, h))
    return pl.pallas_call(
        functools.partial(_ctx_attn_kernel, s_len=s_len),
        grid=(n // tm, N_KV_HEADS),
        in_specs=[
            pl.BlockSpec(memory_space=pltpu.SMEM),
            pl.BlockSpec((tm, gw), lambda b, h: (b, h)),
            kv_spec,
            kv_spec,
        ],
        out_specs=pl.BlockSpec((tm, gw), lambda b, h: (b, h)),
        out_shape=jax.ShapeDtypeStruct((n, D_ATTN), _BF16),
        compiler_params=pltpu.CompilerParams(dimension_semantics=("parallel", "parallel")),
        name="ctx_attention",
    )(sink, q, k_cache, v_cache)


def _lat_attn_kernel(sink_ref, q_ref, k_ref, v_ref, ck_ref, cv_ref, o_ref):
    kvh = pl.program_id(1)
    t_len = q_ref.shape[0]
    nb = t_len // Q_BLOCK
    ck = ck_ref[...].astype(_BF16)
    cv_aug = _with_ones(cv_ref[...].astype(_BF16))
    v_aug = _with_ones(v_ref[...])
    sink_col = _sink_column(sink_ref, kvh, Q_BLOCK)
    rows = Q_GROUP * Q_BLOCK
    r = lax.broadcasted_iota(jnp.int32, (rows, Q_BLOCK), 0) % Q_BLOCK
    c = lax.broadcasted_iota(jnp.int32, (rows, Q_BLOCK), 1)
    for jb in range(nb):
        r0 = jb * Q_BLOCK
        q = _stack_heads(q_ref[r0:r0 + Q_BLOCK, :])
        lo = max(r0 - WINDOW, 0)
        hi = min(r0 + Q_BLOCK + WINDOW, t_len)
        s = _qk(q, k_ref[lo:hi, :])
        parts = []
        off = 0
        if jb > 0:
            parts.append(jnp.where(c >= r, s[:, :Q_BLOCK], NEG_INF))
            off = Q_BLOCK
        parts.append(s[:, off:off + Q_BLOCK])
        off += Q_BLOCK
        if jb < nb - 1:
            parts.append(jnp.where(c <= r, s[:, off:], NEG_INF))
        s_local = jnp.concatenate(parts, axis=1)
        o = _softmax_pv([s_local, _qk(q, ck)], sink_col, [v_aug[lo:hi], cv_aug])
        for g in range(Q_GROUP):
            o_ref[r0:r0 + Q_BLOCK, g * HEAD_DIM:(g + 1) * HEAD_DIM] = (
                o[g * Q_BLOCK:(g + 1) * Q_BLOCK].astype(o_ref.dtype))


def _latent_attention(q, q_row0, k, v, cache_k4, cache_v4, layer, sink, batch, t_len):
    n = batch * t_len
    gw = Q_GROUP * HEAD_DIM
    past = cache_k4.shape[2]
    q_blk0 = q_row0 // t_len
    cache_spec = pl.BlockSpec((None, None, past, HEAD_DIM), lambda b, h: (b, layer, 0, h))
    return pl.pallas_call(
        _lat_attn_kernel,
        grid=(batch, N_KV_HEADS),
        in_specs=[
            pl.BlockSpec(memory_space=pltpu.SMEM),
            pl.BlockSpec((t_len, gw), lambda b, h: (q_blk0 + b, h)),
            pl.BlockSpec((t_len, HEAD_DIM), lambda b, h: (b, h)),
            pl.BlockSpec((t_len, HEAD_DIM), lambda b, h: (b, h)),
            cache_spec,
            cache_spec,
        ],
        out_specs=pl.BlockSpec((t_len, gw), lambda b, h: (b, h)),
        out_shape=jax.ShapeDtypeStruct((n, D_ATTN), _BF16),
        compiler_params=pltpu.CompilerParams(
            dimension_semantics=("parallel", "parallel"), vmem_limit_bytes=VMEM_LIMIT),
        name="latent_attention",
    )(sink, q, k, v, cache_k4, cache_v4)


def _shift_rows(x, d, t_len, offset=0):
    n = x.shape[0]
    row = (lax.broadcasted_iota(jnp.int32, x.shape, 0) + (n - offset)) % t_len
    rolled = pltpu.roll(x, d % n, axis=0)
    valid = (row >= d) if d > 0 else (row < t_len + d)
    return jnp.where(valid, rolled, 0.0)


def _sqrt_nonneg(z):
    return z * lax.rsqrt(jnp.maximum(z, F32_TINY))


def _rnn_kernel(x_ref, cw_ref, cb_ref, wg_ref, bg_ref, lam_ref, h0_ref, y_ref, hl_ref,
                af_scr, bf_scr, ab_scr, bb_scr, hf_scr, hb_scr, *, nb, nc):
    t_len = x_ref.shape[1]
    pitch = t_len + SLOT_PAD
    scr = ((af_scr, bf_scr), (ab_scr, bb_scr))
    per_group = SEQ_SLOTS // 2
    kb = min(per_group, max(1, PASS1_ROWS // t_len))

    def seq_of(slot):
        g, u = slot % 2, slot // 2
        return (g * per_group + u, 0) if nc == 1 else (u, g)

    for g in range(2):
        b_first, cl = seq_of(g)
        offset = SLOT_PAD * g
        lanes = slice(cl * LANES, (cl + 1) * LANES)
        cw = cw_ref[:, lanes]
        cb = cb_ref[:, lanes]
        wg = wg_ref[cl]
        bg = bg_ref[cl]
        lam = lam_ref[:, lanes]
        c_softplus = LRU_C * (jnp.maximum(-lam, 0.0) + jnp.log1p(jnp.exp(-jnp.abs(lam))))

        def per_chunk(it, carry, lanes=lanes, cw=cw, cb=cb, wg=wg, bg=bg,
                      c_softplus=c_softplus, g=g, b_first=b_first, offset=offset):
            n_rows = kb * t_len
            x = x_ref[pl.ds(b_first + it * kb, kb), :, lanes].astype(_F32).reshape(n_rows, LANES)
            if offset:
                x = pltpu.roll(x, offset, axis=0)
            xc = cw[0:1] * _shift_rows(x, 2, t_len, offset)
            xc = xc + cw[1:2] * _shift_rows(x, 1, t_len, offset)
            xc = xc + cw[2:3] * x
            xc = xc + cw[3:4] * _shift_rows(x, -1, t_len, offset)
            xc = xc + cb
            half_gates = jnp.dot(xc.astype(_BF16), wg, preferred_element_type=_F32) + bg
            half_xc = 0.5 * xc
            for d in range(2):
                t_r = jnp.tanh(half_gates[:, (2 * d) * LANES:(2 * d + 1) * LANES])
                t_i = jnp.tanh(half_gates[:, (2 * d + 1) * LANES:(2 * d + 2) * LANES])
                k = 0.5 * c_softplus[d:d + 1]
                neg_log_a = t_r * k + k
                a = jnp.exp2(neg_log_a * -LOG2_E)
                bterm = _sqrt_nonneg(jnp.tanh(neg_log_a) * (a * a + 1.0)) * ((t_i + 1.0) * half_xc)
                for jb in range(kb):
                    slot = 2 * (it * kb + jb) + g
                    row0 = pl.multiple_of(slot * pitch - offset, SUBLANES)
                    for dst, val in ((scr[d][0], a), (scr[d][1], bterm)):
                        dst[pl.ds(row0, t_len), :] = val[jb * t_len:(jb + 1) * t_len]
                        if offset:
                            w0 = ((jb + 1) * t_len) % n_rows
                            dst[pl.ds(row0 + t_len, SUBLANES), :] = val[w0:w0 + SUBLANES]
            return carry

        if per_group == kb:
            per_chunk(0, 0)
        else:
            lax.fori_loop(0, per_group // kb, per_chunk, 0)

    def slot_rows(ref3, d):
        rows = []
        for slot in range(SEQ_SLOTS):
            b, cl = seq_of(slot)
            rows.append(ref3[b, d:d + 1, cl * LANES:(cl + 1) * LANES])
        return jnp.concatenate(rows, axis=0)

    for slot in range(SEQ_SLOTS):
        tile = (slot * pitch + t_len) // SUBLANES * SUBLANES
        for h_scr in (hf_scr, hb_scr):
            h_scr[pl.ds(tile, SUBLANES), :] = jnp.zeros((SUBLANES, LANES), _F32)

    def advance2(a_scr, b_scr, h_scr, t0, t1, h):
        r0 = pl.ds(t0, SEQ_SLOTS, stride=pitch)
        r1 = pl.ds(t1, SEQ_SLOTS, stride=pitch)
        a0, b0, a1, b1 = a_scr[r0, :], b_scr[r0, :], a_scr[r1, :], b_scr[r1, :]
        h_scr[r0, :] = a0 * h + b0
        h = (a1 * a0) * h + (a1 * b0 + b1)
        h_scr[r1, :] = h
        return h

    hf, hb = slot_rows(h0_ref, 0), slot_rows(h0_ref, 1)
    for t in range(0, t_len, 2):
        hf = advance2(af_scr, bf_scr, hf_scr, t, t + 1, hf)
        hb = advance2(ab_scr, bb_scr, hb_scr, t_len - 1 - t, t_len - 2 - t, hb)

    for slot in range(SEQ_SLOTS):
        b, cl = seq_of(slot)
        lanes = slice(cl * LANES, (cl + 1) * LANES)
        hl_ref[b, 0:1, lanes] = hf[slot:slot + 1, :]
        hl_ref[b, 1:2, lanes] = hb[slot:slot + 1, :]
        if slot % 2 == 0:
            rows = pl.ds(slot * pitch, t_len)
            y = hf_scr[rows, :] + hb_scr[rows, :]
        else:
            rows = pl.ds(slot * pitch - SLOT_PAD, t_len + SUBLANES)
            y = pltpu.roll(hf_scr[rows, :] + hb_scr[rows, :], t_len + SUBLANES - SLOT_PAD, axis=0)[:t_len]
        y_ref[b, :, lanes] = y.astype(y_ref.dtype)


def _rnn(rest, row0, h0, conv_w, conv_b, wg, bg, lam, batch, t_len):
    nb = min(batch, SEQ_SLOTS)
    blk0 = row0 // (nb * t_len)
    nc = SEQ_SLOTS // nb
    assert nc in (1, 2) and batch % nb == 0, "a grid step holds 8 batches x 1 or 4 batches x 2 channel blocks"
    cw = nc * LANES
    xr_block0 = D_MODEL // cw
    scratch = pltpu.VMEM((SEQ_SLOTS * (t_len + SLOT_PAD), LANES), _F32)
    y, h_last = pl.pallas_call(
        functools.partial(_rnn_kernel, nb=nb, nc=nc),
        grid=(batch // nb, D_RNN // cw),
        in_specs=[
            pl.BlockSpec((nb, t_len, cw), lambda g, c: (blk0 + g, 0, xr_block0 + c)),
            pl.BlockSpec((CONV_W, cw), lambda g, c: (0, c)),
            pl.BlockSpec((1, cw), lambda g, c: (0, c)),
            pl.BlockSpec((nc, RNN_BLOCK, 4 * LANES), lambda g, c: (c, 0, 0)),
            pl.BlockSpec((nc, 1, 4 * LANES), lambda g, c: (c, 0, 0)),
            pl.BlockSpec((2, cw), lambda g, c: (0, c)),
            pl.BlockSpec((nb, 2, cw), lambda g, c: (g, 0, c)),
        ],
        out_specs=[
            pl.BlockSpec((nb, t_len, cw), lambda g, c: (g, 0, c)),
            pl.BlockSpec((nb, 2, cw), lambda g, c: (g, 0, c)),
        ],
        out_shape=[
            jax.ShapeDtypeStruct((batch, t_len, D_RNN), _BF16),
            jax.ShapeDtypeStruct((batch, 2, D_RNN), _F32),
        ],
        scratch_shapes=[scratch] * 6,
        compiler_params=pltpu.CompilerParams(
            dimension_semantics=("parallel", "parallel"), vmem_limit_bytes=RNN_VMEM_LIMIT),
        name="rnn",
    )(rest.reshape(rest.shape[0] // t_len, t_len, D_REST), conv_w, conv_b.reshape(1, D_RNN), wg, bg, lam, h0)
    return y.reshape(batch * t_len, D_RNN), h_last


def _out_kernel(x_ref, gate_ref, oa_ref, ga_ref, or_ref, gr_ref, mga_ref, mgr_ref,
                wa_ref, wr_ref, wo_ref, g_ref, b_ref, o_ref):
    def silu_bf(x):
        h = 0.5 * x
        return h * jnp.tanh(h) + h

    def logistic_f32(x):
        return (0.5 * jnp.tanh(0.5 * x) + 0.5).astype(_F32)

    ya = jnp.dot(oa_ref[...] * silu_bf(ga_ref[...]), wa_ref[...], preferred_element_type=_F32)
    yr = jnp.dot(or_ref[...] * silu_bf(gr_ref[...]), wr_ref[...], preferred_element_type=_F32)
    merged = logistic_f32(mga_ref[...]) * ya + logistic_f32(mgr_ref[...]) * yr
    out = jnp.dot(merged.astype(_BF16), wo_ref[...], preferred_element_type=_F32)
    z = DEEPNORM_ALPHA * x_ref[...] + gate_ref[...] * out
    o_ref[...] = _layer_norm_rows(z) * g_ref[...] + b_ref[...]


def _merge_residual(x2, mod3, o_attn, o_rnn, rest, rest_row0, wa_bf, wr_bf, wo_bf, layer, ln_g, ln_b,
                    cond_row_of_block, tm):
    n = x2.shape[0]
    row_blk = lambda col: pl.BlockSpec((tm, D_MODEL), lambda i, col=col: (i, col))
    rest_blk = lambda col: pl.BlockSpec((tm, D_MODEL), lambda i, col=col: (rest_row0 // tm + i, col))
    whole = pl.BlockSpec((None, D_MODEL, D_MODEL), lambda i: (layer, 0, 0))
    vec = pl.BlockSpec((1, D_MODEL), lambda i: (0, 0))
    return pl.pallas_call(
        _out_kernel,
        grid=(n // tm,),
        in_specs=[
            row_blk(0),
            pl.BlockSpec((None, 1, D_MODEL), lambda i: (cond_row_of_block(i), 0, 2)),
            row_blk(0), rest_blk(0), row_blk(0), rest_blk(2), rest_blk(3), rest_blk(4),
            whole, whole, whole, vec, vec,
        ],
        out_specs=row_blk(0),
        out_shape=jax.ShapeDtypeStruct((n, D_MODEL), _F32),
        compiler_params=pltpu.CompilerParams(
            dimension_semantics=("parallel",), vmem_limit_bytes=VMEM_LIMIT),
        name="merge_residual",
    )(x2, mod3, o_attn, rest, o_rnn, rest, rest, rest, wa_bf, wr_bf, wo_bf,
      ln_g.reshape(1, D_MODEL), ln_b.reshape(1, D_MODEL))


def _rope_tables(t_len):
    quarter = HEAD_DIM // 4
    inv_freq = ROPE_BASE ** (-jnp.arange(quarter, dtype=_F32) / quarter)
    pos = jnp.arange(t_len)
    ang_r = (pos // GRID_W).astype(_F32)[:, None] * inv_freq[None, :]
    ang_c = (pos % GRID_W).astype(_F32)[:, None] * inv_freq[None, :]
    cos = jnp.concatenate([jnp.cos(ang_r)] * 2 + [jnp.cos(ang_c)] * 2, axis=1)
    sin = jnp.concatenate([-jnp.sin(ang_r), jnp.sin(ang_r), -jnp.sin(ang_c), jnp.sin(ang_c)], axis=1)
    return cos, sin


def _gate_weights(rg_wa, rg_ba, rg_wx, rg_bx):
    wg = (0.5 * jnp.concatenate([rg_wa[0], rg_wx[0], rg_wa[1], rg_wx[1]], axis=-1)).astype(_BF16)
    blk = lambda v: v.reshape(N_RNN_BLOCKS, 1, RNN_BLOCK)
    bg = 0.5 * jnp.concatenate([blk(rg_ba[0]), blk(rg_bx[0]), blk(rg_ba[1]), blk(rg_bx[1])], axis=-1)
    return wg, bg


def kernel(x_prompt, x_sample, cache_k, cache_v, state_h, c, c_ctx, w_mod, b_mod, w_in,
           attn_sink, conv_w, conv_b, rg_wa, rg_ba, rg_wx, rg_bx, rg_lambda,
           w_br_attn, w_br_rnn, w_out, ln_g, ln_b):
    bp, s_len, _ = x_prompt.shape
    bd, t_len, _ = x_sample.shape
    past = cache_k.shape[2]

    cond8 = jnp.zeros((COND_ROWS, D_MODEL), _F32).at[0].set(c_ctx).at[1:1 + bd].set(c)
    mod = _modulation(cond8, w_mod, b_mod)

    wa_bf = w_br_attn.astype(_BF16)
    wr_bf = w_br_rnn.astype(_BF16)
    wo_bf = w_out.astype(_BF16)
    rope_tabs = _rope_tables(t_len)
    cache_k4 = cache_k.reshape(bd, DEPTH, past, D_KV)
    cache_v4 = cache_v.reshape(bd, DEPTH, past, D_KV)

    tm_proj, tm_out = 512, 512
    ctx_row = lambda i: 0
    lat_row_out = lambda i: 1 + (i * tm_out) // t_len
    n_ctx = bp * s_len

    xp = x_prompt.reshape(n_ctx, D_MODEL)
    xs = x_sample.reshape(bd * t_len, D_MODEL)
    h0_zero = jnp.zeros((bp, 2, D_RNN), _F32)
    hs = []
    kv_caches = None
    for l in range(DEPTH):
        mod3 = mod[l].reshape(COND_ROWS, 1, 3 * D_MODEL)
        wg, bg = _gate_weights(rg_wa[l], rg_ba[l], rg_wx[l], rg_bx[l])
        q, kv_caches, (k_lat, v_lat), rest = _project(
            xp, xs, mod3, w_in, l, tm_proj, rope_tabs, (bp, s_len), (bd, t_len), kv_caches)

        o_attn = _context_attention(q, kv_caches[0], kv_caches[1], l, attn_sink[l], bp, s_len)
        o_rnn, h_fin = _rnn(rest, 0, h0_zero, conv_w[l], conv_b[l], wg, bg, rg_lambda[l], bp, s_len)
        xp = _merge_residual(xp, mod3, o_attn, o_rnn, rest, 0, wa_bf, wr_bf, wo_bf, l,
                             ln_g[l], ln_b[l], ctx_row, tm_out)
        hs.append(h_fin)

        o_attn = _latent_attention(q, n_ctx, k_lat, v_lat, cache_k4, cache_v4, l, attn_sink[l], bd, t_len)
        o_rnn, _ = _rnn(rest, n_ctx, state_h[:, l], conv_w[l], conv_b[l], wg, bg, rg_lambda[l], bd, t_len)
        xs = _merge_residual(xs, mod3, o_attn, o_rnn, rest, n_ctx, wa_bf, wr_bf, wo_bf, l,
                             ln_g[l], ln_b[l], lat_row_out, tm_out)

    y_prompt = xp.reshape(bp, s_len, D_MODEL)
    y_sample = xs.reshape(bd, t_len, D_MODEL)
    cache_shape = (bp, DEPTH, s_len, N_KV_HEADS, HEAD_DIM)
    new_k, new_v = (cache[:bp].reshape(cache_shape) for cache in kv_caches)
    return (y_prompt, y_sample, new_k, new_v, jnp.stack(hs, axis=1))
```

```python
import functools

import jax
import jax.numpy as jnp
from jax import lax
from jax.experimental import pallas as pl
from jax.experimental.pallas import tpu as pltpu

D_MODEL = 1024
DEPTH = 2
GRID_W = 64
N_HEADS = 8
N_KV_HEADS = 2
HEAD_DIM = 128
Q_GROUP = N_HEADS // N_KV_HEADS
D_ATTN = N_HEADS * HEAD_DIM
D_KV = N_KV_HEADS * HEAD_DIM
WINDOW = 128
Q_BLOCK = 128
D_RNN = D_MODEL
N_RNN_BLOCKS = 8
RNN_BLOCK = D_RNN // N_RNN_BLOCKS
CONV_W = 4
CONV_LEFT = 2
LRU_C = 8.0
ROPE_BASE = 10000.0
D_IN = 2 * D_ATTN + 2 * D_KV + 2 * D_RNN + 2 * D_MODEL
DEEPNORM_ALPHA = (2 * DEPTH) ** 0.25
LN_EPS = 1e-6
NEG_INF = -1e30
ATTN_SCALE = HEAD_DIM ** -0.5

LANES = 128
SUBLANES = 8
SEQ_SLOTS = SUBLANES
SLOT_PAD = 4
CTX_ATTN_SEQS = 4
PASS1_ROWS = 1024
COND_ROWS = 8
PROJ_TN = 512
N_PROJ_BLOCKS = D_IN // PROJ_TN
D_REST = D_IN - D_ATTN - 2 * D_KV
VMEM_LIMIT = 48 * 1024 * 1024
PROJ_VMEM_LIMIT = 56 * 1024 * 1024
RNN_VMEM_LIMIT = 56 * 1024 * 1024
F32_TINY = 1.1754943508222875e-38
LOG2_E = 1.4426950408889634
Q_PRESCALE = ATTN_SCALE * LOG2_E

_BF16 = jnp.bfloat16
_F32 = jnp.float32


def _silu(x):
    return x * jax.nn.sigmoid(x)


def _layer_norm_rows(x):
    mu = jnp.mean(x, axis=-1, keepdims=True)
    xc = x - mu
    var = jnp.mean(xc * xc, axis=-1, keepdims=True)
    return xc * lax.rsqrt(var + LN_EPS)


def _mod_kernel(cond_ref, w_ref, b_ref, o_ref):
    a = _silu(cond_ref[...]).astype(_BF16)
    o_ref[...] = jnp.dot(a, w_ref[...].astype(_BF16), preferred_element_type=_F32) + b_ref[...]


def _modulation(cond8, w_mod, b_mod):
    tn = D_MODEL
    return pl.pallas_call(
        _mod_kernel,
        grid=(DEPTH, 3 * D_MODEL // tn),
        in_specs=[
            pl.BlockSpec((COND_ROWS, D_MODEL), lambda l, j: (0, 0)),
            pl.BlockSpec((None, D_MODEL, tn), lambda l, j: (l, 0, j)),
            pl.BlockSpec((None, 1, tn), lambda l, j: (l, 0, j)),
        ],
        out_specs=pl.BlockSpec((None, COND_ROWS, tn), lambda l, j: (l, 0, j)),
        out_shape=jax.ShapeDtypeStruct((DEPTH, COND_ROWS, 3 * D_MODEL), _F32),
        compiler_params=pltpu.CompilerParams(dimension_semantics=("parallel", "parallel")),
        name="modulation",
    )(cond8, w_mod, b_mod.reshape(DEPTH, 1, 3 * D_MODEL))


def _rope(x, cos, sin_signed):
    lane = lax.broadcasted_iota(jnp.int32, x.shape, 1)
    partner = jnp.where((lane // 32) % 2 == 0,
                        pltpu.roll(x, 3 * 32, axis=1),
                        pltpu.roll(x, 32, axis=1))
    return x * cos + partner * sin_signed


def _proj_kernel(xl_ref, xc_ref, shift_ref, scale_ref, w_hbm, cos_ref, sin_ref, *rest,
                 layer, n_lat_blocks):
    q_ref, kc_ref, vc_ref, kl_ref, vl_ref, r_ref, w_scr, stage_scr, sem = rest[-9:]
    i = pl.program_id(0)

    def chunk_copy(j):
        return pltpu.make_async_copy(
            w_hbm.at[layer, :, pl.ds(j * PROJ_TN, PROJ_TN)], stage_scr.at[j % 2], sem.at[j % 2])

    def fetch_chunk(j):
        if j == 0:
            chunk_copy(0).start()
        if j + 1 < N_PROJ_BLOCKS:
            chunk_copy(j + 1).start()
        chunk_copy(j).wait()
        w_scr[:, j * PROJ_TN:(j + 1) * PROJ_TN] = stage_scr[j % 2].astype(_BF16)

    x = jnp.where(i < n_lat_blocks, xl_ref[...], xc_ref[...])
    y = _layer_norm_rows(x)
    h = (y * (1.0 + scale_ref[...]) + shift_ref[...]).astype(_BF16)

    def rope(blk):
        return _rope(blk, cos_ref[...], sin_ref[...])

    def put_cache(ref, lanes, val):
        s_len = ref.shape[-2]
        for bi in range(ref.shape[0]):
            rows = val[bi * s_len:(bi + 1) * s_len].astype(ref.dtype)
            if len(ref.shape) == 3:
                ref[bi, :, lanes] = rows
            else:
                for l in range(ref.shape[1]):
                    ref[bi, l, :, lanes] = rows if l == layer else jnp.zeros_like(rows)

    def project_block(stream_weights):
        for j in range(N_PROJ_BLOCKS):
            c0 = j * PROJ_TN
            if stream_weights:
                fetch_chunk(j)
            acc = jnp.dot(h, w_scr[:, c0:c0 + PROJ_TN], preferred_element_type=_F32)
            if c0 < D_ATTN:
                for hh in range(PROJ_TN // HEAD_DIM):
                    sl = slice(hh * HEAD_DIM, (hh + 1) * HEAD_DIM)
                    q_ref[:, c0 + hh * HEAD_DIM:c0 + (hh + 1) * HEAD_DIM] = (
                        (rope(acc[:, sl]) * Q_PRESCALE).astype(q_ref.dtype))
            elif c0 == D_ATTN:
                for hh in range(N_KV_HEADS):
                    sl = slice(hh * HEAD_DIM, (hh + 1) * HEAD_DIM)
                    k_rot = rope(acc[:, sl])
                    put_cache(kc_ref, sl, k_rot)
                    put_cache(vc_ref, sl, acc[:, D_KV + hh * HEAD_DIM:D_KV + (hh + 1) * HEAD_DIM])
                    kl_ref[:, sl] = k_rot.astype(kl_ref.dtype)
                vl_ref[...] = acc[:, D_KV:].astype(vl_ref.dtype)
            else:
                r0 = c0 - D_ATTN - 2 * D_KV
                r_ref[:, r0:r0 + PROJ_TN] = acc.astype(r_ref.dtype)

    pl.when(i == 0)(functools.partial(project_block, True))
    pl.when(i != 0)(functools.partial(project_block, False))


def _project(x_lat, x_ctx, mod3, w_in, layer, tm, rope_tabs, lat_dims, ctx_dims, prev_caches):
    (bd, t_len), (bp, s_len) = lat_dims, ctx_dims
    n_lat, n_ctx = bd * t_len, bp * s_len
    n = n_lat + n_ctx
    nl_blocks = n_lat // tm
    is_lat = lambda i: i < nl_blocks
    ctx_blk = lambda i: jnp.maximum(i - nl_blocks, 0)
    cond_row = lambda i: jnp.where(is_lat(i), 1 + (i * tm) // t_len, 0)
    cos, sin = rope_tabs
    blocks_per_seq = t_len // tm
    cos = jnp.concatenate([jnp.ones((tm, HEAD_DIM), _F32), cos], axis=0)
    sin = jnp.concatenate([jnp.zeros((tm, HEAD_DIM), _F32), sin], axis=0)
    tab_blk = lambda i: jnp.where(is_lat(i), 1 + i % blocks_per_seq, 0)

    in_specs = [
        pl.BlockSpec((tm, D_MODEL), lambda i: (jnp.minimum(i, nl_blocks - 1), 0)),
        pl.BlockSpec((tm, D_MODEL), lambda i: (ctx_blk(i), 0)),
        pl.BlockSpec((None, 1, D_MODEL), lambda i: (cond_row(i), 0, 0)),
        pl.BlockSpec((None, 1, D_MODEL), lambda i: (cond_row(i), 0, 1)),
        pl.BlockSpec(memory_space=pl.ANY),
        pl.BlockSpec((tm, HEAD_DIM), lambda i: (tab_blk(i), 0)),
        pl.BlockSpec((tm, HEAD_DIM), lambda i: (tab_blk(i), 0)),
    ]
    args = [x_lat, x_ctx, mod3, mod3, w_in, cos, sin]
    aliases = {}
    seqs = tm // s_len
    if prev_caches is None:
        cache_spec = pl.BlockSpec((seqs, DEPTH, s_len, D_KV), lambda i: (ctx_blk(i), 0, 0, 0))
    else:
        cache_spec = pl.BlockSpec((seqs, None, s_len, D_KV), lambda i: (ctx_blk(i), layer, 0, 0))
        aliases = {len(args): 1, len(args) + 1: 2}
        in_specs += [pl.BlockSpec(memory_space=pl.ANY)] * 2
        args += list(prev_caches)
    cache_shape = jax.ShapeDtypeStruct((bp, DEPTH, s_len, D_KV), _F32)
    lat_kv_spec = pl.BlockSpec((tm, D_KV), lambda i: (jnp.minimum(i, nl_blocks), 0))
    lat_kv_shape = jax.ShapeDtypeStruct((n_lat + tm, D_KV), _BF16)
    q, k_cache, v_cache, k_lat, v_lat, rest = pl.pallas_call(
        functools.partial(_proj_kernel, layer=layer, n_lat_blocks=nl_blocks),
        grid=(n // tm,),
        in_specs=in_specs,
        out_specs=[
            pl.BlockSpec((tm, D_ATTN), lambda i: (i, 0)),
            cache_spec, cache_spec, lat_kv_spec, lat_kv_spec,
            pl.BlockSpec((tm, D_REST), lambda i: (i, 0)),
        ],
        out_shape=[
            jax.ShapeDtypeStruct((n, D_ATTN), _BF16),
            cache_shape, cache_shape, lat_kv_shape, lat_kv_shape,
            jax.ShapeDtypeStruct((n, D_REST), _BF16),
        ],
        scratch_shapes=[
            pltpu.VMEM((D_MODEL, D_IN), _BF16),
            pltpu.VMEM((2, D_MODEL, PROJ_TN), _F32),
            pltpu.SemaphoreType.DMA((2,)),
        ],
        input_output_aliases=aliases,
        compiler_params=pltpu.CompilerParams(
            dimension_semantics=("arbitrary",), vmem_limit_bytes=PROJ_VMEM_LIMIT),
        name="proj",
    )(*args)
    return q, (k_cache, v_cache), (k_lat, v_lat), rest


def _softmax_pv(score_parts, sink_col, value_parts):
    m = sink_col
    for s in score_parts:
        m = jnp.maximum(m, jnp.max(s, axis=-1, keepdims=True))
    acc = None
    for s, v_aug in zip(score_parts, value_parts):
        part = jnp.dot(jnp.exp2(s - m).astype(_BF16), v_aug, preferred_element_type=_F32)
        acc = part if acc is None else acc + part
    denom = acc[:, HEAD_DIM:HEAD_DIM + 1] + jnp.exp2(sink_col - m)
    return acc[:, :HEAD_DIM] / denom


def _with_ones(v_bf):
    return jnp.concatenate([v_bf, jnp.ones_like(v_bf)], axis=1)


def _sink_column(sink_ref, kvh, rows):
    head = lax.broadcasted_iota(jnp.int32, (Q_GROUP * rows, 1), 0) // rows
    col = jnp.full((Q_GROUP * rows, 1), sink_ref[kvh * Q_GROUP], _F32)
    for g in range(1, Q_GROUP):
        col = jnp.where(head == g, sink_ref[kvh * Q_GROUP + g], col)
    return col * LOG2_E


def _stack_heads(q_rows):
    return jnp.concatenate(
        [q_rows[:, g * HEAD_DIM:(g + 1) * HEAD_DIM] for g in range(Q_GROUP)], axis=0)


def _qk(q, k_bf):
    return lax.dot_general(q, k_bf, (((1,), (1,)), ((), ())), preferred_element_type=_F32)


def _ctx_attn_kernel(sink_ref, q_ref, k_ref, v_ref, o_ref, *, s_len):
    kvh = pl.program_id(1)
    sink_col = _sink_column(sink_ref, kvh, s_len)

    def one_sequence(bi, carry):
        rows = pl.ds(pl.multiple_of(bi * s_len, s_len), s_len)
        q = _stack_heads(q_ref[rows, :])
        k = k_ref[bi].astype(_BF16)
        v_aug = _with_ones(v_ref[bi].astype(_BF16))
        o = _softmax_pv([_qk(q, k)], sink_col, [v_aug])
        for g in range(Q_GROUP):
            o_ref[rows, g * HEAD_DIM:(g + 1) * HEAD_DIM] = o[g * s_len:(g + 1) * s_len].astype(o_ref.dtype)
        return carry

    lax.fori_loop(0, q_ref.shape[0] // s_len, one_sequence, 0)


def _context_attention(q, q_row0, k_cache, v_cache, layer, sink, batch, s_len):
    n = batch * s_len
    tm = CTX_ATTN_SEQS * s_len
    gw = Q_GROUP * HEAD_DIM
    q_blk0 = q_row0 // tm
    kv_spec = pl.BlockSpec((CTX_ATTN_SEQS, None, s_len, HEAD_DIM), lambda b, h: (b, layer, 0, h))
    return pl.pallas_call(
        functools.partial(_ctx_attn_kernel, s_len=s_len),
        grid=(n // tm, N_KV_HEADS),
        in_specs=[
            pl.BlockSpec(memory_space=pltpu.SMEM),
            pl.BlockSpec((tm, gw), lambda b, h: (q_blk0 + b, h)),
            kv_spec,
            kv_spec,
        ],
        out_specs=pl.BlockSpec((tm, gw), lambda b, h: (b, h)),
        out_shape=jax.ShapeDtypeStruct((n, D_ATTN), _BF16),
        compiler_params=pltpu.CompilerParams(dimension_semantics=("parallel", "parallel")),
        name="ctx_attention",
    )(sink, q, k_cache, v_cache)


def _lat_attn_kernel(sink_ref, q_ref, k_ref, v_ref, ck_ref, cv_ref, o_ref):
    kvh = pl.program_id(1)
    t_len = q_ref.shape[0]
    nb = t_len // Q_BLOCK
    ck = ck_ref[...].astype(_BF16)
    cv_aug = _with_ones(cv_ref[...].astype(_BF16))
    v_aug = _with_ones(v_ref[...])
    sink_col = _sink_column(sink_ref, kvh, Q_BLOCK)
    rows = Q_GROUP * Q_BLOCK
    r = lax.broadcasted_iota(jnp.int32, (rows, Q_BLOCK), 0) % Q_BLOCK
    c = lax.broadcasted_iota(jnp.int32, (rows, Q_BLOCK), 1)
    for jb in range(nb):
        r0 = jb * Q_BLOCK
        q = _stack_heads(q_ref[r0:r0 + Q_BLOCK, :])
        lo = max(r0 - WINDOW, 0)
        hi = min(r0 + Q_BLOCK + WINDOW, t_len)
        s = _qk(q, k_ref[lo:hi, :])
        parts = []
        off = 0
        if jb > 0:
            parts.append(jnp.where(c >= r, s[:, :Q_BLOCK], NEG_INF))
            off = Q_BLOCK
        parts.append(s[:, off:off + Q_BLOCK])
        off += Q_BLOCK
        if jb < nb - 1:
            parts.append(jnp.where(c <= r, s[:, off:], NEG_INF))
        s_local = jnp.concatenate(parts, axis=1)
        o = _softmax_pv([s_local, _qk(q, ck)], sink_col, [v_aug[lo:hi], cv_aug])
        for g in range(Q_GROUP):
            o_ref[r0:r0 + Q_BLOCK, g * HEAD_DIM:(g + 1) * HEAD_DIM] = (
                o[g * Q_BLOCK:(g + 1) * Q_BLOCK].astype(o_ref.dtype))


def _latent_attention(q, q_row0, k, v, cache_k4, cache_v4, layer, sink, batch, t_len):
    n = batch * t_len
    gw = Q_GROUP * HEAD_DIM
    past = cache_k4.shape[2]
    q_blk0 = q_row0 // t_len
    cache_spec = pl.BlockSpec((None, None, past, HEAD_DIM), lambda b, h: (b, layer, 0, h))
    return pl.pallas_call(
        _lat_attn_kernel,
        grid=(batch, N_KV_HEADS),
        in_specs=[
            pl.BlockSpec(memory_space=pltpu.SMEM),
            pl.BlockSpec((t_len, gw), lambda b, h: (q_blk0 + b, h)),
            pl.BlockSpec((t_len, HEAD_DIM), lambda b, h: (b, h)),
            pl.BlockSpec((t_len, HEAD_DIM), lambda b, h: (b, h)),
            cache_spec,
            cache_spec,
        ],
        out_specs=pl.BlockSpec((t_len, gw), lambda b, h: (b, h)),
        out_shape=jax.ShapeDtypeStruct((n, D_ATTN), _BF16),
        compiler_params=pltpu.CompilerParams(
            dimension_semantics=("parallel", "parallel"), vmem_limit_bytes=VMEM_LIMIT),
        name="latent_attention",
    )(sink, q, k, v, cache_k4, cache_v4)


def _shift_rows(x, d, t_len, offset=0):
    n = x.shape[0]
    row = (lax.broadcasted_iota(jnp.int32, x.shape, 0) + (n - offset)) % t_len
    rolled = pltpu.roll(x, d % n, axis=0)
    valid = (row >= d) if d > 0 else (row < t_len + d)
    return jnp.where(valid, rolled, 0.0)


def _sqrt_nonneg(z):
    return z * lax.rsqrt(jnp.maximum(z, F32_TINY))


def _rnn_kernel(x_ref, cw_ref, cb_ref, wg_ref, bg_ref, lam_ref, h0_ref, y_ref, hl_ref,
                af_scr, bf_scr, ab_scr, bb_scr, hf_scr, hb_scr, *, nb, nc):
    t_len = x_ref.shape[1]
    pitch = t_len + SLOT_PAD
    scr = ((af_scr, bf_scr), (ab_scr, bb_scr))
    per_group = SEQ_SLOTS // 2
    kb = min(per_group, max(1, PASS1_ROWS // t_len))

    def seq_of(slot):
        g, u = slot % 2, slot // 2
        return (g * per_group + u, 0) if nc == 1 else (u, g)

    for g in range(2):
        b_first, cl = seq_of(g)
        offset = SLOT_PAD * g
        lanes = slice(cl * LANES, (cl + 1) * LANES)
        cw = cw_ref[:, lanes]
        cb = cb_ref[:, lanes]
        wg = wg_ref[cl]
        bg = bg_ref[cl]
        lam = lam_ref[:, lanes]
        c_softplus = LRU_C * (jnp.maximum(-lam, 0.0) + jnp.log1p(jnp.exp(-jnp.abs(lam))))

        def per_chunk(it, carry, lanes=lanes, cw=cw, cb=cb, wg=wg, bg=bg,
                      c_softplus=c_softplus, g=g, b_first=b_first, offset=offset):
            n_rows = kb * t_len
            x = x_ref[pl.ds(b_first + it * kb, kb), :, lanes].astype(_F32).reshape(n_rows, LANES)
            if offset:
                x = pltpu.roll(x, offset, axis=0)
            xc = cw[0:1] * _shift_rows(x, 2, t_len, offset)
            xc = xc + cw[1:2] * _shift_rows(x, 1, t_len, offset)
            xc = xc + cw[2:3] * x
            xc = xc + cw[3:4] * _shift_rows(x, -1, t_len, offset)
            xc = xc + cb
            half_gates = jnp.dot(xc.astype(_BF16), wg, preferred_element_type=_F32) + bg
            half_xc = 0.5 * xc
            for d in range(2):
                t_r = jnp.tanh(half_gates[:, (2 * d) * LANES:(2 * d + 1) * LANES])
                t_i = jnp.tanh(half_gates[:, (2 * d + 1) * LANES:(2 * d + 2) * LANES])
                k = 0.5 * c_softplus[d:d + 1]
                neg_log_a = t_r * k + k
                a = jnp.exp2(neg_log_a * -LOG2_E)
                bterm = _sqrt_nonneg(jnp.tanh(neg_log_a) * (a * a + 1.0)) * ((t_i + 1.0) * half_xc)
                for jb in range(kb):
                    slot = 2 * (it * kb + jb) + g
                    row0 = pl.multiple_of(slot * pitch - offset, SUBLANES)
                    for dst, val in ((scr[d][0], a), (scr[d][1], bterm)):
                        dst[pl.ds(row0, t_len), :] = val[jb * t_len:(jb + 1) * t_len]
                        if offset:
                            w0 = ((jb + 1) * t_len) % n_rows
                            dst[pl.ds(row0 + t_len, SUBLANES), :] = val[w0:w0 + SUBLANES]
            return carry

        if per_group == kb:
            per_chunk(0, 0)
        else:
            lax.fori_loop(0, per_group // kb, per_chunk, 0)

    def slot_rows(ref3, d):
        rows = []
        for slot in range(SEQ_SLOTS):
            b, cl = seq_of(slot)
            rows.append(ref3[b, d:d + 1, cl * LANES:(cl + 1) * LANES])
        return jnp.concatenate(rows, axis=0)

    for slot in range(SEQ_SLOTS):
        tile = (slot * pitch + t_len) // SUBLANES * SUBLANES
        for h_scr in (hf_scr, hb_scr):
            h_scr[pl.ds(tile, SUBLANES), :] = jnp.zeros((SUBLANES, LANES), _F32)

    def advance2(a_scr, b_scr, h_scr, t0, t1, h):
        r0 = pl.ds(t0, SEQ_SLOTS, stride=pitch)
        r1 = pl.ds(t1, SEQ_SLOTS, stride=pitch)
        a0, b0, a1, b1 = a_scr[r0, :], b_scr[r0, :], a_scr[r1, :], b_scr[r1, :]
        h_scr[r0, :] = a0 * h + b0
        h = (a1 * a0) * h + (a1 * b0 + b1)
        h_scr[r1, :] = h
        return h

    hf, hb = slot_rows(h0_ref, 0), slot_rows(h0_ref, 1)
    for t in range(0, t_len, 2):
        hf = advance2(af_scr, bf_scr, hf_scr, t, t + 1, hf)
        hb = advance2(ab_scr, bb_scr, hb_scr, t_len - 1 - t, t_len - 2 - t, hb)

    for slot in range(SEQ_SLOTS):
        b, cl = seq_of(slot)
        lanes = slice(cl * LANES, (cl + 1) * LANES)
        hl_ref[b, 0:1, lanes] = hf[slot:slot + 1, :]
        hl_ref[b, 1:2, lanes] = hb[slot:slot + 1, :]
        if slot % 2 == 0:
            rows = pl.ds(slot * pitch, t_len)
            y = hf_scr[rows, :] + hb_scr[rows, :]
        else:
            rows = pl.ds(slot * pitch - SLOT_PAD, t_len + SUBLANES)
            y = pltpu.roll(hf_scr[rows, :] + hb_scr[rows, :], t_len + SUBLANES - SLOT_PAD, axis=0)[:t_len]
        y_ref[b, :, lanes] = y.astype(y_ref.dtype)


def _rnn(rest, row0, h0, conv_w, conv_b, wg, bg, lam, batch, t_len):
    nb = min(batch, SEQ_SLOTS)
    blk0 = row0 // (nb * t_len)
    nc = SEQ_SLOTS // nb
    assert nc in (1, 2) and batch % nb == 0, "a grid step holds 8 batches x 1 or 4 batches x 2 channel blocks"
    cw = nc * LANES
    xr_block0 = D_MODEL // cw
    scratch = pltpu.VMEM((SEQ_SLOTS * (t_len + SLOT_PAD), LANES), _F32)
    y, h_last = pl.pallas_call(
        functools.partial(_rnn_kernel, nb=nb, nc=nc),
        grid=(batch // nb, D_RNN // cw),
        in_specs=[
            pl.BlockSpec((nb, t_len, cw), lambda g, c: (blk0 + g, 0, xr_block0 + c)),
            pl.BlockSpec((CONV_W, cw), lambda g, c: (0, c)),
            pl.BlockSpec((1, cw), lambda g, c: (0, c)),
            pl.BlockSpec((nc, RNN_BLOCK, 4 * LANES), lambda g, c: (c, 0, 0)),
            pl.BlockSpec((nc, 1, 4 * LANES), lambda g, c: (c, 0, 0)),
            pl.BlockSpec((2, cw), lambda g, c: (0, c)),
            pl.BlockSpec((nb, 2, cw), lambda g, c: (g, 0, c)),
        ],
        out_specs=[
            pl.BlockSpec((nb, t_len, cw), lambda g, c: (g, 0, c)),
            pl.BlockSpec((nb, 2, cw), lambda g, c: (g, 0, c)),
        ],
        out_shape=[
            jax.ShapeDtypeStruct((batch, t_len, D_RNN), _BF16),
            jax.ShapeDtypeStruct((batch, 2, D_RNN), _F32),
        ],
        scratch_shapes=[scratch] * 6,
        compiler_params=pltpu.CompilerParams(
            dimension_semantics=("parallel", "parallel"), vmem_limit_bytes=RNN_VMEM_LIMIT),
        name="rnn",
    )(rest.reshape(rest.shape[0] // t_len, t_len, D_REST), conv_w, conv_b.reshape(1, D_RNN), wg, bg, lam, h0)
    return y.reshape(batch * t_len, D_RNN), h_last


def _out_kernel(x_ref, gate_ref, oa_ref, ga_ref, or_ref, gr_ref, mga_ref, mgr_ref,
                wa_ref, wr_ref, wo_ref, g_ref, b_ref, o_ref):
    def silu_bf(x):
        h = 0.5 * x
        return h * jnp.tanh(h) + h

    def logistic_f32(x):
        return (0.5 * jnp.tanh(0.5 * x) + 0.5).astype(_F32)

    ya = jnp.dot(oa_ref[...] * silu_bf(ga_ref[...]), wa_ref[...], preferred_element_type=_F32)
    yr = jnp.dot(or_ref[...] * silu_bf(gr_ref[...]), wr_ref[...], preferred_element_type=_F32)
    merged = logistic_f32(mga_ref[...]) * ya + logistic_f32(mgr_ref[...]) * yr
    out = jnp.dot(merged.astype(_BF16), wo_ref[...], preferred_element_type=_F32)
    z = DEEPNORM_ALPHA * x_ref[...] + gate_ref[...] * out
    o_ref[...] = _layer_norm_rows(z) * g_ref[...] + b_ref[...]


def _merge_residual(x2, mod3, o_attn, o_rnn, rest, rest_row0, wa_bf, wr_bf, wo_bf, layer, ln_g, ln_b,
                    cond_row_of_block, tm):
    n = x2.shape[0]
    row_blk = lambda col: pl.BlockSpec((tm, D_MODEL), lambda i, col=col: (i, col))
    rest_blk = lambda col: pl.BlockSpec((tm, D_MODEL), lambda i, col=col: (rest_row0 // tm + i, col))
    whole = pl.BlockSpec((None, D_MODEL, D_MODEL), lambda i: (layer, 0, 0))
    vec = pl.BlockSpec((1, D_MODEL), lambda i: (0, 0))
    return pl.pallas_call(
        _out_kernel,
        grid=(n // tm,),
        in_specs=[
            row_blk(0),
            pl.BlockSpec((None, 1, D_MODEL), lambda i: (cond_row_of_block(i), 0, 2)),
            row_blk(0), rest_blk(0), row_blk(0), rest_blk(2), rest_blk(3), rest_blk(4),
            whole, whole, whole, vec, vec,
        ],
        out_specs=row_blk(0),
        out_shape=jax.ShapeDtypeStruct((n, D_MODEL), _F32),
        compiler_params=pltpu.CompilerParams(
            dimension_semantics=("parallel",), vmem_limit_bytes=VMEM_LIMIT),
        name="merge_residual",
    )(x2, mod3, o_attn, rest, o_rnn, rest, rest, rest, wa_bf, wr_bf, wo_bf,
      ln_g.reshape(1, D_MODEL), ln_b.reshape(1, D_MODEL))


def _rope_tables(t_len):
    quarter = HEAD_DIM // 4
    inv_freq = ROPE_BASE ** (-jnp.arange(quarter, dtype=_F32) / quarter)
    pos = jnp.arange(t_len)
    ang_r = (pos // GRID_W).astype(_F32)[:, None] * inv_freq[None, :]
    ang_c = (pos % GRID_W).astype(_F32)[:, None] * inv_freq[None, :]
    cos = jnp.concatenate([jnp.cos(ang_r)] * 2 + [jnp.cos(ang_c)] * 2, axis=1)
    sin = jnp.concatenate([-jnp.sin(ang_r), jnp.sin(ang_r), -jnp.sin(ang_c), jnp.sin(ang_c)], axis=1)
    return cos, sin


def _gate_weights(rg_wa, rg_ba, rg_wx, rg_bx):
    wg = (0.5 * jnp.concatenate([rg_wa[0], rg_wx[0], rg_wa[1], rg_wx[1]], axis=-1)).astype(_BF16)
    blk = lambda v: v.reshape(N_RNN_BLOCKS, 1, RNN_BLOCK)
    bg = 0.5 * jnp.concatenate([blk(rg_ba[0]), blk(rg_bx[0]), blk(rg_ba[1]), blk(rg_bx[1])], axis=-1)
    return wg, bg


def kernel(x_prompt, x_sample, cache_k, cache_v, state_h, c, c_ctx, w_mod, b_mod, w_in,
           attn_sink, conv_w, conv_b, rg_wa, rg_ba, rg_wx, rg_bx, rg_lambda,
           w_br_attn, w_br_rnn, w_out, ln_g, ln_b):
    bp, s_len, _ = x_prompt.shape
    bd, t_len, _ = x_sample.shape
    past = cache_k.shape[2]

    cond8 = jnp.zeros((COND_ROWS, D_MODEL), _F32).at[0].set(c_ctx).at[1:1 + bd].set(c)
    mod = _modulation(cond8, w_mod, b_mod)

    wa_bf = w_br_attn.astype(_BF16)
    wr_bf = w_br_rnn.astype(_BF16)
    wo_bf = w_out.astype(_BF16)
    rope_tabs = _rope_tables(t_len)
    cache_k4 = cache_k.reshape(bd, DEPTH, past, D_KV)
    cache_v4 = cache_v.reshape(bd, DEPTH, past, D_KV)

    tm_proj, tm_out = 512, 512
    ctx_row = lambda i: 0
    lat_row_out = lambda i: 1 + (i * tm_out) // t_len
    n_lat = bd * t_len

    xp = x_prompt.reshape(bp * s_len, D_MODEL)
    xs = x_sample.reshape(n_lat, D_MODEL)
    h0_zero = jnp.zeros((bp, 2, D_RNN), _F32)
    hs = []
    kv_caches = None
    for l in range(DEPTH):
        mod3 = mod[l].reshape(COND_ROWS, 1, 3 * D_MODEL)
        wg, bg = _gate_weights(rg_wa[l], rg_ba[l], rg_wx[l], rg_bx[l])
        q, kv_caches, (k_lat, v_lat), rest = _project(
            xs, xp, mod3, w_in, l, tm_proj, rope_tabs, (bd, t_len), (bp, s_len), kv_caches)

        o_attn = _context_attention(q, n_lat, kv_caches[0], kv_caches[1], l, attn_sink[l], bp, s_len)
        o_rnn, h_fin = _rnn(rest, n_lat, h0_zero, conv_w[l], conv_b[l], wg, bg, rg_lambda[l], bp, s_len)
        xp = _merge_residual(xp, mod3, o_attn, o_rnn, rest, n_lat, wa_bf, wr_bf, wo_bf, l,
                             ln_g[l], ln_b[l], ctx_row, tm_out)
        hs.append(h_fin)

        o_attn = _latent_attention(q, 0, k_lat, v_lat, cache_k4, cache_v4, l, attn_sink[l], bd, t_len)
        o_rnn, _ = _rnn(rest, 0, state_h[:, l], conv_w[l], conv_b[l], wg, bg, rg_lambda[l], bd, t_len)
        xs = _merge_residual(xs, mod3, o_attn, o_rnn, rest, 0, wa_bf, wr_bf, wo_bf, l,
                             ln_g[l], ln_b[l], lat_row_out, tm_out)

    y_prompt = xp.reshape(bp, s_len, D_MODEL)
    y_sample = xs.reshape(bd, t_len, D_MODEL)
    cache_shape = (bp, DEPTH, s_len, N_KV_HEADS, HEAD_DIM)
    new_k, new_v = (cache.reshape(cache_shape) for cache in kv_caches)
    return (y_prompt, y_sample, new_k, new_v, jnp.stack(hs, axis=1))
```

```python
import functools

import jax
import jax.numpy as jnp
from jax import lax
from jax.experimental import pallas as pl
from jax.experimental.pallas import tpu as pltpu

D_MODEL = 1024
DEPTH = 2
GRID_W = 64
N_HEADS = 8
N_KV_HEADS = 2
HEAD_DIM = 128
Q_GROUP = N_HEADS // N_KV_HEADS
D_ATTN = N_HEADS * HEAD_DIM
D_KV = N_KV_HEADS * HEAD_DIM
WINDOW = 128
Q_BLOCK = 128
D_RNN = D_MODEL
N_RNN_BLOCKS = 8
RNN_BLOCK = D_RNN // N_RNN_BLOCKS
CONV_W = 4
CONV_LEFT = 2
LRU_C = 8.0
ROPE_BASE = 10000.0
D_IN = 2 * D_ATTN + 2 * D_KV + 2 * D_RNN + 2 * D_MODEL
DEEPNORM_ALPHA = (2 * DEPTH) ** 0.25
LN_EPS = 1e-6
NEG_INF = -1e30
ATTN_SCALE = HEAD_DIM ** -0.5

LANES = 128
SUBLANES = 8
SEQ_SLOTS = SUBLANES
SLOT_PAD = 4
CTX_ATTN_SEQS = 4
PASS1_ROWS = 1024
COND_ROWS = 8
PROJ_TN = 512
N_PROJ_BLOCKS = D_IN // PROJ_TN
D_REST = D_IN - D_ATTN - 2 * D_KV
VMEM_LIMIT = 48 * 1024 * 1024
PROJ_VMEM_LIMIT = 56 * 1024 * 1024
RNN_VMEM_LIMIT = 56 * 1024 * 1024
F32_TINY = 1.1754943508222875e-38
LOG2_E = 1.4426950408889634
Q_PRESCALE = ATTN_SCALE * LOG2_E

_BF16 = jnp.bfloat16
_F32 = jnp.float32


def _silu(x):
    return x * jax.nn.sigmoid(x)


def _layer_norm_rows(x):
    mu = jnp.mean(x, axis=-1, keepdims=True)
    xc = x - mu
    var = jnp.mean(xc * xc, axis=-1, keepdims=True)
    return xc * lax.rsqrt(var + LN_EPS)


def _mod_kernel(cond_ref, w_ref, b_ref, o_ref):
    a = _silu(cond_ref[...]).astype(_BF16)
    o_ref[...] = jnp.dot(a, w_ref[...].astype(_BF16), preferred_element_type=_F32) + b_ref[...]


def _modulation(cond8, w_mod, b_mod):
    tn = D_MODEL
    return pl.pallas_call(
        _mod_kernel,
        grid=(DEPTH, 3 * D_MODEL // tn),
        in_specs=[
            pl.BlockSpec((COND_ROWS, D_MODEL), lambda l, j: (0, 0)),
            pl.BlockSpec((None, D_MODEL, tn), lambda l, j: (l, 0, j)),
            pl.BlockSpec((None, 1, tn), lambda l, j: (l, 0, j)),
        ],
        out_specs=pl.BlockSpec((None, COND_ROWS, tn), lambda l, j: (l, 0, j)),
        out_shape=jax.ShapeDtypeStruct((DEPTH, COND_ROWS, 3 * D_MODEL), _F32),
        compiler_params=pltpu.CompilerParams(dimension_semantics=("parallel", "parallel")),
        name="modulation",
    )(cond8, w_mod, b_mod.reshape(DEPTH, 1, 3 * D_MODEL))


def _rope(x, cos, sin_signed):
    lane = lax.broadcasted_iota(jnp.int32, x.shape, 1)
    partner = jnp.where((lane // 32) % 2 == 0,
                        pltpu.roll(x, 3 * 32, axis=1),
                        pltpu.roll(x, 32, axis=1))
    return x * cos + partner * sin_signed


def _proj_kernel(xl_ref, xc_ref, shift_ref, scale_ref, w_hbm, cos_ref, sin_ref, *rest,
                 layer, n_lat_blocks):
    q_ref, kc_ref, vc_ref, kl_ref, vl_ref, r_ref, w_scr, stage_scr, h_scr, sem = rest[-10:]
    s = pl.program_id(0)

    def chunk_copy(j):
        return pltpu.make_async_copy(
            w_hbm.at[layer, :, pl.ds(j * PROJ_TN, PROJ_TN)], stage_scr.at[j % 2], sem.at[j % 2])

    def fetch_chunk(j):
        if j == 0:
            chunk_copy(0).start()
        if j + 1 < N_PROJ_BLOCKS:
            chunk_copy(j + 1).start()
        chunk_copy(j).wait()
        w_scr[:, j * PROJ_TN:(j + 1) * PROJ_TN] = stage_scr[j % 2].astype(_BF16)

    def normalise_block():
        x = jnp.where(s < n_lat_blocks, xl_ref[...], xc_ref[...])
        y = _layer_norm_rows(x)
        h_scr[s % 2] = (y * (1.0 + scale_ref[...]) + shift_ref[...]).astype(_BF16)

    def rope(blk):
        return _rope(blk, cos_ref[...], sin_ref[...])

    def put_cache(ref, lanes, val):
        s_len = ref.shape[-2]
        for bi in range(ref.shape[0]):
            rows = val[bi * s_len:(bi + 1) * s_len].astype(ref.dtype)
            if len(ref.shape) == 3:
                ref[bi, :, lanes] = rows
            else:
                for l in range(ref.shape[1]):
                    ref[bi, l, :, lanes] = rows if l == layer else jnp.zeros_like(rows)

    def project_block(stream_weights):
        h = h_scr[(s + 1) % 2]
        normalise_block()
        for j in range(N_PROJ_BLOCKS):
            c0 = j * PROJ_TN
            if stream_weights:
                fetch_chunk(j)
            acc = jnp.dot(h, w_scr[:, c0:c0 + PROJ_TN], preferred_element_type=_F32)
            if c0 < D_ATTN:
                for hh in range(PROJ_TN // HEAD_DIM):
                    sl = slice(hh * HEAD_DIM, (hh + 1) * HEAD_DIM)
                    q_ref[:, c0 + hh * HEAD_DIM:c0 + (hh + 1) * HEAD_DIM] = (
                        (rope(acc[:, sl]) * Q_PRESCALE).astype(q_ref.dtype))
            elif c0 == D_ATTN:
                for hh in range(N_KV_HEADS):
                    sl = slice(hh * HEAD_DIM, (hh + 1) * HEAD_DIM)
                    k_rot = rope(acc[:, sl])
                    put_cache(kc_ref, sl, k_rot)
                    put_cache(vc_ref, sl, acc[:, D_KV + hh * HEAD_DIM:D_KV + (hh + 1) * HEAD_DIM])
                    kl_ref[:, sl] = k_rot.astype(kl_ref.dtype)
                vl_ref[...] = acc[:, D_KV:].astype(vl_ref.dtype)
            else:
                r0 = c0 - D_ATTN - 2 * D_KV
                r_ref[:, r0:r0 + PROJ_TN] = acc.astype(r_ref.dtype)

    pl.when(s == 0)(normalise_block)
    pl.when(s == 1)(functools.partial(project_block, True))
    pl.when(s > 1)(functools.partial(project_block, False))


def _project(x_lat, x_ctx, mod3, w_in, layer, tm, rope_tabs, lat_dims, ctx_dims, prev_caches):
    (bd, t_len), (bp, s_len) = lat_dims, ctx_dims
    n_lat, n_ctx = bd * t_len, bp * s_len
    n = n_lat + n_ctx
    nl_blocks = n_lat // tm
    n_blocks = n // tm
    norm_blk = lambda s: jnp.minimum(s, n_blocks - 1)
    mul_blk = lambda s: jnp.maximum(s - 1, 0)
    is_lat = lambda i: i < nl_blocks
    ctx_blk = lambda i: jnp.maximum(i - nl_blocks, 0)
    cond_row = lambda i: jnp.where(is_lat(i), 1 + (i * tm) // t_len, 0)
    cos, sin = rope_tabs
    blocks_per_seq = t_len // tm
    cos = jnp.concatenate([jnp.ones((tm, HEAD_DIM), _F32), cos], axis=0)
    sin = jnp.concatenate([jnp.zeros((tm, HEAD_DIM), _F32), sin], axis=0)
    tab_blk = lambda i: jnp.where(is_lat(i), 1 + i % blocks_per_seq, 0)

    in_specs = [
        pl.BlockSpec((tm, D_MODEL), lambda s: (jnp.minimum(s, nl_blocks - 1), 0)),
        pl.BlockSpec((tm, D_MODEL), lambda s: (ctx_blk(norm_blk(s)), 0)),
        pl.BlockSpec((None, 1, D_MODEL), lambda s: (cond_row(norm_blk(s)), 0, 0)),
        pl.BlockSpec((None, 1, D_MODEL), lambda s: (cond_row(norm_blk(s)), 0, 1)),
        pl.BlockSpec(memory_space=pl.ANY),
        pl.BlockSpec((tm, HEAD_DIM), lambda s: (tab_blk(mul_blk(s)), 0)),
        pl.BlockSpec((tm, HEAD_DIM), lambda s: (tab_blk(mul_blk(s)), 0)),
    ]
    args = [x_lat, x_ctx, mod3, mod3, w_in, cos, sin]
    aliases = {}
    seqs = tm // s_len
    if prev_caches is None:
        cache_spec = pl.BlockSpec((seqs, DEPTH, s_len, D_KV), lambda s: (ctx_blk(mul_blk(s)), 0, 0, 0))
    else:
        cache_spec = pl.BlockSpec((seqs, None, s_len, D_KV), lambda s: (ctx_blk(mul_blk(s)), layer, 0, 0))
        aliases = {len(args): 1, len(args) + 1: 2}
        in_specs += [pl.BlockSpec(memory_space=pl.ANY)] * 2
        args += list(prev_caches)
    cache_shape = jax.ShapeDtypeStruct((bp, DEPTH, s_len, D_KV), _F32)
    lat_kv_spec = pl.BlockSpec((tm, D_KV), lambda s: (jnp.minimum(mul_blk(s), nl_blocks), 0))
    lat_kv_shape = jax.ShapeDtypeStruct((n_lat + tm, D_KV), _BF16)
    q, k_cache, v_cache, k_lat, v_lat, rest = pl.pallas_call(
        functools.partial(_proj_kernel, layer=layer, n_lat_blocks=nl_blocks),
        grid=(n_blocks + 1,),
        in_specs=in_specs,
        out_specs=[
            pl.BlockSpec((tm, D_ATTN), lambda s: (mul_blk(s), 0)),
            cache_spec, cache_spec, lat_kv_spec, lat_kv_spec,
            pl.BlockSpec((tm, D_REST), lambda s: (mul_blk(s), 0)),
        ],
        out_shape=[
            jax.ShapeDtypeStruct((n, D_ATTN), _BF16),
            cache_shape, cache_shape, lat_kv_shape, lat_kv_shape,
            jax.ShapeDtypeStruct((n, D_REST), _BF16),
        ],
        scratch_shapes=[
            pltpu.VMEM((D_MODEL, D_IN), _BF16),
            pltpu.VMEM((2, D_MODEL, PROJ_TN), _F32),
            pltpu.VMEM((2, tm, D_MODEL), _BF16),
            pltpu.SemaphoreType.DMA((2,)),
        ],
        input_output_aliases=aliases,
        compiler_params=pltpu.CompilerParams(
            dimension_semantics=("arbitrary",), vmem_limit_bytes=PROJ_VMEM_LIMIT),
        name="proj",
    )(*args)
    return q, (k_cache, v_cache), (k_lat, v_lat), rest


def _softmax_pv(score_parts, sink_col, value_parts):
    m = sink_col
    for s in score_parts:
        m = jnp.maximum(m, jnp.max(s, axis=-1, keepdims=True))
    acc = None
    for s, v_aug in zip(score_parts, value_parts):
        part = jnp.dot(jnp.exp2(s - m).astype(_BF16), v_aug, preferred_element_type=_F32)
        acc = part if acc is None else acc + part
    denom = acc[:, HEAD_DIM:HEAD_DIM + 1] + jnp.exp2(sink_col - m)
    return acc[:, :HEAD_DIM] / denom


def _with_ones(v_bf):
    return jnp.concatenate([v_bf, jnp.ones_like(v_bf)], axis=1)


def _sink_column(sink_ref, kvh, rows):
    head = lax.broadcasted_iota(jnp.int32, (Q_GROUP * rows, 1), 0) // rows
    col = jnp.full((Q_GROUP * rows, 1), sink_ref[kvh * Q_GROUP], _F32)
    for g in range(1, Q_GROUP):
        col = jnp.where(head == g, sink_ref[kvh * Q_GROUP + g], col)
    return col * LOG2_E


def _stack_heads(q_rows):
    return jnp.concatenate(
        [q_rows[:, g * HEAD_DIM:(g + 1) * HEAD_DIM] for g in range(Q_GROUP)], axis=0)


def _qk(q, k_bf):
    return lax.dot_general(q, k_bf, (((1,), (1,)), ((), ())), preferred_element_type=_F32)


def _ctx_attn_kernel(sink_ref, q_ref, k_ref, v_ref, o_ref, *, s_len):
    kvh = pl.program_id(1)
    sink_col = _sink_column(sink_ref, kvh, s_len)

    def one_sequence(bi, carry):
        rows = pl.ds(pl.multiple_of(bi * s_len, s_len), s_len)
        q = _stack_heads(q_ref[rows, :])
        k = k_ref[bi].astype(_BF16)
        v_aug = _with_ones(v_ref[bi].astype(_BF16))
        o = _softmax_pv([_qk(q, k)], sink_col, [v_aug])
        for g in range(Q_GROUP):
            o_ref[rows, g * HEAD_DIM:(g + 1) * HEAD_DIM] = o[g * s_len:(g + 1) * s_len].astype(o_ref.dtype)
        return carry

    lax.fori_loop(0, q_ref.shape[0] // s_len, one_sequence, 0)


def _context_attention(q, q_row0, k_cache, v_cache, layer, sink, batch, s_len):
    n = batch * s_len
    tm = CTX_ATTN_SEQS * s_len
    gw = Q_GROUP * HEAD_DIM
    q_blk0 = q_row0 // tm
    kv_spec = pl.BlockSpec((CTX_ATTN_SEQS, None, s_len, HEAD_DIM), lambda b, h: (b, layer, 0, h))
    return pl.pallas_call(
        functools.partial(_ctx_attn_kernel, s_len=s_len),
        grid=(n // tm, N_KV_HEADS),
        in_specs=[
            pl.BlockSpec(memory_space=pltpu.SMEM),
            pl.BlockSpec((tm, gw), lambda b, h: (q_blk0 + b, h)),
            kv_spec,
            kv_spec,
        ],
        out_specs=pl.BlockSpec((tm, gw), lambda b, h: (b, h)),
        out_shape=jax.ShapeDtypeStruct((n, D_ATTN), _BF16),
        compiler_params=pltpu.CompilerParams(dimension_semantics=("parallel", "parallel")),
        name="ctx_attention",
    )(sink, q, k_cache, v_cache)


def _lat_attn_kernel(sink_ref, q_ref, k_ref, v_ref, ck_ref, cv_ref, o_ref):
    kvh = pl.program_id(1)
    t_len = q_ref.shape[0]
    nb = t_len // Q_BLOCK
    ck = ck_ref[...].astype(_BF16)
    cv_aug = _with_ones(cv_ref[...].astype(_BF16))
    v_aug = _with_ones(v_ref[...])
    sink_col = _sink_column(sink_ref, kvh, Q_BLOCK)
    rows = Q_GROUP * Q_BLOCK
    r = lax.broadcasted_iota(jnp.int32, (rows, Q_BLOCK), 0) % Q_BLOCK
    c = lax.broadcasted_iota(jnp.int32, (rows, Q_BLOCK), 1)
    for jb in range(nb):
        r0 = jb * Q_BLOCK
        q = _stack_heads(q_ref[r0:r0 + Q_BLOCK, :])
        lo = max(r0 - WINDOW, 0)
        hi = min(r0 + Q_BLOCK + WINDOW, t_len)
        s = _qk(q, k_ref[lo:hi, :])
        parts = []
        off = 0
        if jb > 0:
            parts.append(jnp.where(c >= r, s[:, :Q_BLOCK], NEG_INF))
            off = Q_BLOCK
        parts.append(s[:, off:off + Q_BLOCK])
        off += Q_BLOCK
        if jb < nb - 1:
            parts.append(jnp.where(c <= r, s[:, off:], NEG_INF))
        s_local = jnp.concatenate(parts, axis=1)
        o = _softmax_pv([s_local, _qk(q, ck)], sink_col, [v_aug[lo:hi], cv_aug])
        for g in range(Q_GROUP):
            o_ref[r0:r0 + Q_BLOCK, g * HEAD_DIM:(g + 1) * HEAD_DIM] = (
                o[g * Q_BLOCK:(g + 1) * Q_BLOCK].astype(o_ref.dtype))


def _latent_attention(q, q_row0, k, v, cache_k4, cache_v4, layer, sink, batch, t_len):
    n = batch * t_len
    gw = Q_GROUP * HEAD_DIM
    past = cache_k4.shape[2]
    q_blk0 = q_row0 // t_len
    cache_spec = pl.BlockSpec((None, None, past, HEAD_DIM), lambda b, h: (b, layer, 0, h))
    return pl.pallas_call(
        _lat_attn_kernel,
        grid=(batch, N_KV_HEADS),
        in_specs=[
            pl.BlockSpec(memory_space=pltpu.SMEM),
            pl.BlockSpec((t_len, gw), lambda b, h: (q_blk0 + b, h)),
            pl.BlockSpec((t_len, HEAD_DIM), lambda b, h: (b, h)),
            pl.BlockSpec((t_len, HEAD_DIM), lambda b, h: (b, h)),
            cache_spec,
            cache_spec,
        ],
        out_specs=pl.BlockSpec((t_len, gw), lambda b, h: (b, h)),
        out_shape=jax.ShapeDtypeStruct((n, D_ATTN), _BF16),
        compiler_params=pltpu.CompilerParams(
            dimension_semantics=("parallel", "parallel"), vmem_limit_bytes=VMEM_LIMIT),
        name="latent_attention",
    )(sink, q, k, v, cache_k4, cache_v4)


def _shift_rows(x, d, t_len, offset=0):
    n = x.shape[0]
    row = (lax.broadcasted_iota(jnp.int32, x.shape, 0) + (n - offset)) % t_len
    rolled = pltpu.roll(x, d % n, axis=0)
    valid = (row >= d) if d > 0 else (row < t_len + d)
    return jnp.where(valid, rolled, 0.0)


def _sqrt_nonneg(z):
    return z * lax.rsqrt(jnp.maximum(z, F32_TINY))


def _rnn_kernel(x_ref, cw_ref, cb_ref, wg_ref, bg_ref, lam_ref, h0_ref, y_ref, hl_ref,
                af_scr, bf_scr, ab_scr, bb_scr, hf_scr, hb_scr, *, nb, nc):
    t_len = x_ref.shape[1]
    pitch = t_len + SLOT_PAD
    scr = ((af_scr, bf_scr), (ab_scr, bb_scr))
    per_group = SEQ_SLOTS // 2
    kb = min(per_group, max(1, PASS1_ROWS // t_len))

    def seq_of(slot):
        g, u = slot % 2, slot // 2
        return (g * per_group + u, 0) if nc == 1 else (u, g)

    for g in range(2):
        b_first, cl = seq_of(g)
        offset = SLOT_PAD * g
        lanes = slice(cl * LANES, (cl + 1) * LANES)
        cw = cw_ref[:, lanes]
        cb = cb_ref[:, lanes]
        wg = wg_ref[cl]
        bg = bg_ref[cl]
        lam = lam_ref[:, lanes]
        c_softplus = LRU_C * (jnp.maximum(-lam, 0.0) + jnp.log1p(jnp.exp(-jnp.abs(lam))))

        def per_chunk(it, carry, lanes=lanes, cw=cw, cb=cb, wg=wg, bg=bg,
                      c_softplus=c_softplus, g=g, b_first=b_first, offset=offset):
            n_rows = kb * t_len
            x = x_ref[pl.ds(b_first + it * kb, kb), :, lanes].astype(_F32).reshape(n_rows, LANES)
            if offset:
                x = pltpu.roll(x, offset, axis=0)
            xc = cw[0:1] * _shift_rows(x, 2, t_len, offset)
            xc = xc + cw[1:2] * _shift_rows(x, 1, t_len, offset)
            xc = xc + cw[2:3] * x
            xc = xc + cw[3:4] * _shift_rows(x, -1, t_len, offset)
            xc = xc + cb
            half_gates = jnp.dot(xc.astype(_BF16), wg, preferred_element_type=_F32) + bg
            half_xc = 0.5 * xc
            for d in range(2):
                t_r = jnp.tanh(half_gates[:, (2 * d) * LANES:(2 * d + 1) * LANES])
                t_i = jnp.tanh(half_gates[:, (2 * d + 1) * LANES:(2 * d + 2) * LANES])
                k = 0.5 * c_softplus[d:d + 1]
                neg_log_a = t_r * k + k
                a = jnp.exp2(neg_log_a * -LOG2_E)
                bterm = _sqrt_nonneg(jnp.tanh(neg_log_a) * (a * a + 1.0)) * ((t_i + 1.0) * half_xc)
                for jb in range(kb):
                    slot = 2 * (it * kb + jb) + g
                    row0 = pl.multiple_of(slot * pitch - offset, SUBLANES)
                    for dst, val in ((scr[d][0], a), (scr[d][1], bterm)):
                        dst[pl.ds(row0, t_len), :] = val[jb * t_len:(jb + 1) * t_len]
                        if offset:
                            w0 = ((jb + 1) * t_len) % n_rows
                            dst[pl.ds(row0 + t_len, SUBLANES), :] = val[w0:w0 + SUBLANES]
            return carry

        if per_group == kb:
            per_chunk(0, 0)
        else:
            lax.fori_loop(0, per_group // kb, per_chunk, 0)

    def slot_rows(ref3, d):
        rows = []
        for slot in range(SEQ_SLOTS):
            b, cl = seq_of(slot)
            rows.append(ref3[b, d:d + 1, cl * LANES:(cl + 1) * LANES])
        return jnp.concatenate(rows, axis=0)

    for slot in range(SEQ_SLOTS):
        tile = (slot * pitch + t_len) // SUBLANES * SUBLANES
        for h_scr in (hf_scr, hb_scr):
            h_scr[pl.ds(tile, SUBLANES), :] = jnp.zeros((SUBLANES, LANES), _F32)

    def advance2(a_scr, b_scr, h_scr, t0, t1, h):
        r0 = pl.ds(t0, SEQ_SLOTS, stride=pitch)
        r1 = pl.ds(t1, SEQ_SLOTS, stride=pitch)
        a0, b0, a1, b1 = a_scr[r0, :], b_scr[r0, :], a_scr[r1, :], b_scr[r1, :]
        h_scr[r0, :] = a0 * h + b0
        h = (a1 * a0) * h + (a1 * b0 + b1)
        h_scr[r1, :] = h
        return h

    hf, hb = slot_rows(h0_ref, 0), slot_rows(h0_ref, 1)
    for t in range(0, t_len, 2):
        hf = advance2(af_scr, bf_scr, hf_scr, t, t + 1, hf)
        hb = advance2(ab_scr, bb_scr, hb_scr, t_len - 1 - t, t_len - 2 - t, hb)

    for slot in range(SEQ_SLOTS):
        b, cl = seq_of(slot)
        lanes = slice(cl * LANES, (cl + 1) * LANES)
        hl_ref[b, 0:1, lanes] = hf[slot:slot + 1, :]
        hl_ref[b, 1:2, lanes] = hb[slot:slot + 1, :]
        if slot % 2 == 0:
            rows = pl.ds(slot * pitch, t_len)
            y = hf_scr[rows, :] + hb_scr[rows, :]
        else:
            rows = pl.ds(slot * pitch - SLOT_PAD, t_len + SUBLANES)
            y = pltpu.roll(hf_scr[rows, :] + hb_scr[rows, :], t_len + SUBLANES - SLOT_PAD, axis=0)[:t_len]
        y_ref[b, :, lanes] = y.astype(y_ref.dtype)


def _rnn(rest, row0, h0, conv_w, conv_b, wg, bg, lam, batch, t_len):
    nb = min(batch, SEQ_SLOTS)
    blk0 = row0 // (nb * t_len)
    nc = SEQ_SLOTS // nb
    assert nc in (1, 2) and batch % nb == 0, "a grid step holds 8 batches x 1 or 4 batches x 2 channel blocks"
    cw = nc * LANES
    xr_block0 = D_MODEL // cw
    scratch = pltpu.VMEM((SEQ_SLOTS * (t_len + SLOT_PAD), LANES), _F32)
    y, h_last = pl.pallas_call(
        functools.partial(_rnn_kernel, nb=nb, nc=nc),
        grid=(batch // nb, D_RNN // cw),
        in_specs=[
            pl.BlockSpec((nb, t_len, cw), lambda g, c: (blk0 + g, 0, xr_block0 + c)),
            pl.BlockSpec((CONV_W, cw), lambda g, c: (0, c)),
            pl.BlockSpec((1, cw), lambda g, c: (0, c)),
            pl.BlockSpec((nc, RNN_BLOCK, 4 * LANES), lambda g, c: (c, 0, 0)),
            pl.BlockSpec((nc, 1, 4 * LANES), lambda g, c: (c, 0, 0)),
            pl.BlockSpec((2, cw), lambda g, c: (0, c)),
            pl.BlockSpec((nb, 2, cw), lambda g, c: (g, 0, c)),
        ],
        out_specs=[
            pl.BlockSpec((nb, t_len, cw), lambda g, c: (g, 0, c)),
            pl.BlockSpec((nb, 2, cw), lambda g, c: (g, 0, c)),
        ],
        out_shape=[
            jax.ShapeDtypeStruct((batch, t_len, D_RNN), _BF16),
            jax.ShapeDtypeStruct((batch, 2, D_RNN), _F32),
        ],
        scratch_shapes=[scratch] * 6,
        compiler_params=pltpu.CompilerParams(
            dimension_semantics=("parallel", "parallel"), vmem_limit_bytes=RNN_VMEM_LIMIT),
        name="rnn",
    )(rest.reshape(rest.shape[0] // t_len, t_len, D_REST), conv_w, conv_b.reshape(1, D_RNN), wg, bg, lam, h0)
    return y.reshape(batch * t_len, D_RNN), h_last


def _out_kernel(x_ref, gate_ref, oa_ref, ga_ref, or_ref, gr_ref, mga_ref, mgr_ref,
                wa_ref, wr_ref, wo_ref, g_ref, b_ref, o_ref):
    def silu_bf(x):
        h = 0.5 * x
        return h * jnp.tanh(h) + h

    def logistic_f32(x):
        return (0.5 * jnp.tanh(0.5 * x) + 0.5).astype(_F32)

    ya = jnp.dot(oa_ref[...] * silu_bf(ga_ref[...]), wa_ref[...], preferred_element_type=_F32)
    yr = jnp.dot(or_ref[...] * silu_bf(gr_ref[...]), wr_ref[...], preferred_element_type=_F32)
    merged = logistic_f32(mga_ref[...]) * ya + logistic_f32(mgr_ref[...]) * yr
    out = jnp.dot(merged.astype(_BF16), wo_ref[...], preferred_element_type=_F32)
    z = DEEPNORM_ALPHA * x_ref[...] + gate_ref[...] * out
    o_ref[...] = _layer_norm_rows(z) * g_ref[...] + b_ref[...]


def _merge_residual(x2, mod3, o_attn, o_rnn, rest, rest_row0, wa_bf, wr_bf, wo_bf, layer, ln_g, ln_b,
                    cond_row_of_block, tm):
    n = x2.shape[0]
    row_blk = lambda col: pl.BlockSpec((tm, D_MODEL), lambda i, col=col: (i, col))
    rest_blk = lambda col: pl.BlockSpec((tm, D_MODEL), lambda i, col=col: (rest_row0 // tm + i, col))
    whole = pl.BlockSpec((None, D_MODEL, D_MODEL), lambda i: (layer, 0, 0))
    vec = pl.BlockSpec((1, D_MODEL), lambda i: (0, 0))
    return pl.pallas_call(
        _out_kernel,
        grid=(n // tm,),
        in_specs=[
            row_blk(0),
            pl.BlockSpec((None, 1, D_MODEL), lambda i: (cond_row_of_block(i), 0, 2)),
            row_blk(0), rest_blk(0), row_blk(0), rest_blk(2), rest_blk(3), rest_blk(4),
            whole, whole, whole, vec, vec,
        ],
        out_specs=row_blk(0),
        out_shape=jax.ShapeDtypeStruct((n, D_MODEL), _F32),
        compiler_params=pltpu.CompilerParams(
            dimension_semantics=("parallel",), vmem_limit_bytes=VMEM_LIMIT),
        name="merge_residual",
    )(x2, mod3, o_attn, rest, o_rnn, rest, rest, rest, wa_bf, wr_bf, wo_bf,
      ln_g.reshape(1, D_MODEL), ln_b.reshape(1, D_MODEL))


def _rope_tables(t_len):
    quarter = HEAD_DIM // 4
    inv_freq = ROPE_BASE ** (-jnp.arange(quarter, dtype=_F32) / quarter)
    pos = jnp.arange(t_len)
    ang_r = (pos // GRID_W).astype(_F32)[:, None] * inv_freq[None, :]
    ang_c = (pos % GRID_W).astype(_F32)[:, None] * inv_freq[None, :]
    cos = jnp.concatenate([jnp.cos(ang_r)] * 2 + [jnp.cos(ang_c)] * 2, axis=1)
    sin = jnp.concatenate([-jnp.sin(ang_r), jnp.sin(ang_r), -jnp.sin(ang_c), jnp.sin(ang_c)], axis=1)
    return cos, sin


def _gate_weights(rg_wa, rg_ba, rg_wx, rg_bx):
    wg = (0.5 * jnp.concatenate([rg_wa[0], rg_wx[0], rg_wa[1], rg_wx[1]], axis=-1)).astype(_BF16)
    blk = lambda v: v.reshape(N_RNN_BLOCKS, 1, RNN_BLOCK)
    bg = 0.5 * jnp.concatenate([blk(rg_ba[0]), blk(rg_bx[0]), blk(rg_ba[1]), blk(rg_bx[1])], axis=-1)
    return wg, bg


def kernel(x_prompt, x_sample, cache_k, cache_v, state_h, c, c_ctx, w_mod, b_mod, w_in,
           attn_sink, conv_w, conv_b, rg_wa, rg_ba, rg_wx, rg_bx, rg_lambda,
           w_br_attn, w_br_rnn, w_out, ln_g, ln_b):
    bp, s_len, _ = x_prompt.shape
    bd, t_len, _ = x_sample.shape
    past = cache_k.shape[2]

    cond8 = jnp.zeros((COND_ROWS, D_MODEL), _F32).at[0].set(c_ctx).at[1:1 + bd].set(c)
    mod = _modulation(cond8, w_mod, b_mod)

    wa_bf = w_br_attn.astype(_BF16)
    wr_bf = w_br_rnn.astype(_BF16)
    wo_bf = w_out.astype(_BF16)
    rope_tabs = _rope_tables(t_len)
    cache_k4 = cache_k.reshape(bd, DEPTH, past, D_KV)
    cache_v4 = cache_v.reshape(bd, DEPTH, past, D_KV)

    tm_proj, tm_out = 512, 512
    ctx_row = lambda i: 0
    lat_row_out = lambda i: 1 + (i * tm_out) // t_len
    n_lat = bd * t_len

    xp = x_prompt.reshape(bp * s_len, D_MODEL)
    xs = x_sample.reshape(n_lat, D_MODEL)
    h0_zero = jnp.zeros((bp, 2, D_RNN), _F32)
    hs = []
    kv_caches = None
    for l in range(DEPTH):
        mod3 = mod[l].reshape(COND_ROWS, 1, 3 * D_MODEL)
        wg, bg = _gate_weights(rg_wa[l], rg_ba[l], rg_wx[l], rg_bx[l])
        q, kv_caches, (k_lat, v_lat), rest = _project(
            xs, xp, mod3, w_in, l, tm_proj, rope_tabs, (bd, t_len), (bp, s_len), kv_caches)

        o_attn = _context_attention(q, n_lat, kv_caches[0], kv_caches[1], l, attn_sink[l], bp, s_len)
        o_rnn, h_fin = _rnn(rest, n_lat, h0_zero, conv_w[l], conv_b[l], wg, bg, rg_lambda[l], bp, s_len)
        xp = _merge_residual(xp, mod3, o_attn, o_rnn, rest, n_lat, wa_bf, wr_bf, wo_bf, l,
                             ln_g[l], ln_b[l], ctx_row, tm_out)
        hs.append(h_fin)

        o_attn = _latent_attention(q, 0, k_lat, v_lat, cache_k4, cache_v4, l, attn_sink[l], bd, t_len)
        o_rnn, _ = _rnn(rest, 0, state_h[:, l], conv_w[l], conv_b[l], wg, bg, rg_lambda[l], bd, t_len)
        xs = _merge_residual(xs, mod3, o_attn, o_rnn, rest, 0, wa_bf, wr_bf, wo_bf, l,
                             ln_g[l], ln_b[l], lat_row_out, tm_out)

    y_prompt = xp.reshape(bp, s_len, D_MODEL)
    y_sample = xs.reshape(bd, t_len, D_MODEL)
    cache_shape = (bp, DEPTH, s_len, N_KV_HEADS, HEAD_DIM)
    new_k, new_v = (cache.reshape(cache_shape) for cache in kv_caches)
    return (y_prompt, y_sample, new_k, new_v, jnp.stack(hs, axis=1))
```

```python
import functools

import jax
import jax.numpy as jnp
from jax import lax
from jax.experimental import pallas as pl
from jax.experimental.pallas import tpu as pltpu

D_MODEL = 1024
DEPTH = 2
GRID_W = 64
N_HEADS = 8
N_KV_HEADS = 2
HEAD_DIM = 128
Q_GROUP = N_HEADS // N_KV_HEADS
D_ATTN = N_HEADS * HEAD_DIM
D_KV = N_KV_HEADS * HEAD_DIM
WINDOW = 128
Q_BLOCK = 128
D_RNN = D_MODEL
N_RNN_BLOCKS = 8
RNN_BLOCK = D_RNN // N_RNN_BLOCKS
CONV_W = 4
CONV_LEFT = 2
LRU_C = 8.0
ROPE_BASE = 10000.0
D_IN = 2 * D_ATTN + 2 * D_KV + 2 * D_RNN + 2 * D_MODEL
DEEPNORM_ALPHA = (2 * DEPTH) ** 0.25
LN_EPS = 1e-6
NEG_INF = -1e30
ATTN_SCALE = HEAD_DIM ** -0.5

LANES = 128
SUBLANES = 8
SEQ_SLOTS = SUBLANES
SLOT_PAD = 4
CTX_ATTN_SEQS = 4
PASS1_ROWS = 1024
COND_ROWS = 8
PROJ_TN = 512
N_PROJ_BLOCKS = D_IN // PROJ_TN
D_REST = D_IN - D_ATTN - 2 * D_KV
VMEM_LIMIT = 48 * 1024 * 1024
PROJ_VMEM_LIMIT = 56 * 1024 * 1024
RNN_VMEM_LIMIT = 56 * 1024 * 1024
F32_TINY = 1.1754943508222875e-38
LOG2_E = 1.4426950408889634
Q_PRESCALE = ATTN_SCALE * LOG2_E

_BF16 = jnp.bfloat16
_F32 = jnp.float32


def _silu(x):
    return x * jax.nn.sigmoid(x)


def _layer_norm_rows(x):
    mu = jnp.mean(x, axis=-1, keepdims=True)
    xc = x - mu
    var = jnp.mean(xc * xc, axis=-1, keepdims=True)
    return xc * lax.rsqrt(var + LN_EPS)


def _head_rows(head, n_rows):
    return pl.ds(head, n_rows, stride=N_KV_HEADS)


def _mod_kernel(cond_ref, w_ref, b_ref, o_ref):
    a = _silu(cond_ref[...]).astype(_BF16)
    o_ref[...] = jnp.dot(a, w_ref[...].astype(_BF16), preferred_element_type=_F32) + b_ref[...]


def _modulation(cond8, w_mod, b_mod):
    tn = D_MODEL
    return pl.pallas_call(
        _mod_kernel,
        grid=(DEPTH, 3 * D_MODEL // tn),
        in_specs=[
            pl.BlockSpec((COND_ROWS, D_MODEL), lambda l, j: (0, 0)),
            pl.BlockSpec((None, D_MODEL, tn), lambda l, j: (l, 0, j)),
            pl.BlockSpec((None, 1, tn), lambda l, j: (l, 0, j)),
        ],
        out_specs=pl.BlockSpec((None, COND_ROWS, tn), lambda l, j: (l, 0, j)),
        out_shape=jax.ShapeDtypeStruct((DEPTH, COND_ROWS, 3 * D_MODEL), _F32),
        compiler_params=pltpu.CompilerParams(dimension_semantics=("parallel", "parallel")),
        name="modulation",
    )(cond8, w_mod, b_mod.reshape(DEPTH, 1, 3 * D_MODEL))


def _rope(x, cos, sin_signed):
    lane = lax.broadcasted_iota(jnp.int32, x.shape, 1)
    partner = jnp.where((lane // 32) % 2 == 0,
                        pltpu.roll(x, 3 * 32, axis=1),
                        pltpu.roll(x, 32, axis=1))
    return x * cos + partner * sin_signed


def _proj_kernel(xl_ref, xc_ref, shift_ref, scale_ref, w_hbm, cos_ref, sin_ref, *rest,
                 layer, n_lat_blocks):
    q_ref, kc_ref, vc_ref, kl_ref, vl_ref, r_ref, w_scr, stage_scr, sem = rest[-9:]
    i = pl.program_id(0)

    def chunk_copy(j):
        return pltpu.make_async_copy(
            w_hbm.at[layer, :, pl.ds(j * PROJ_TN, PROJ_TN)], stage_scr.at[j % 2], sem.at[j % 2])

    def fetch_chunk(j):
        if j == 0:
            chunk_copy(0).start()
        if j + 1 < N_PROJ_BLOCKS:
            chunk_copy(j + 1).start()
        chunk_copy(j).wait()
        w_scr[:, j * PROJ_TN:(j + 1) * PROJ_TN] = stage_scr[j % 2].astype(_BF16)

    x = jnp.where(i < n_lat_blocks, xl_ref[...], xc_ref[...])
    y = _layer_norm_rows(x)
    h = (y * (1.0 + scale_ref[...]) + shift_ref[...]).astype(_BF16)

    def rope(blk):
        return _rope(blk, cos_ref[...], sin_ref[...])

    def put_cache(ref, head, val):
        s_len = ref.shape[-2] // N_KV_HEADS
        for bi in range(ref.shape[0]):
            rows = val[bi * s_len:(bi + 1) * s_len].astype(ref.dtype)
            if len(ref.shape) == 3:
                ref[bi, _head_rows(head, s_len), :] = rows
            else:
                for l in range(ref.shape[1]):
                    ref[bi, l, _head_rows(head, s_len), :] = rows if l == layer else jnp.zeros_like(rows)

    def project_block(stream_weights):
        for j in range(N_PROJ_BLOCKS):
            c0 = j * PROJ_TN
            if stream_weights:
                fetch_chunk(j)
            acc = jnp.dot(h, w_scr[:, c0:c0 + PROJ_TN], preferred_element_type=_F32)
            if c0 < D_ATTN:
                for hh in range(PROJ_TN // HEAD_DIM):
                    sl = slice(hh * HEAD_DIM, (hh + 1) * HEAD_DIM)
                    q_ref[:, c0 + hh * HEAD_DIM:c0 + (hh + 1) * HEAD_DIM] = (
                        (rope(acc[:, sl]) * Q_PRESCALE).astype(q_ref.dtype))
            elif c0 == D_ATTN:
                for hh in range(N_KV_HEADS):
                    sl = slice(hh * HEAD_DIM, (hh + 1) * HEAD_DIM)
                    k_rot = rope(acc[:, sl])
                    put_cache(kc_ref, hh, k_rot)
                    put_cache(vc_ref, hh, acc[:, D_KV + hh * HEAD_DIM:D_KV + (hh + 1) * HEAD_DIM])
                    kl_ref[:, sl] = k_rot.astype(kl_ref.dtype)
                vl_ref[...] = acc[:, D_KV:].astype(vl_ref.dtype)
            else:
                r0 = c0 - D_ATTN - 2 * D_KV
                r_ref[:, r0:r0 + PROJ_TN] = acc.astype(r_ref.dtype)

    pl.when(i == 0)(functools.partial(project_block, True))
    pl.when(i != 0)(functools.partial(project_block, False))


def _project(x_lat, x_ctx, mod3, w_in, layer, tm, rope_tabs, lat_dims, ctx_dims, prev_caches):
    (bd, t_len), (bp, s_len) = lat_dims, ctx_dims
    n_lat, n_ctx = bd * t_len, bp * s_len
    n = n_lat + n_ctx
    nl_blocks = n_lat // tm
    is_lat = lambda i: i < nl_blocks
    ctx_blk = lambda i: jnp.maximum(i - nl_blocks, 0)
    cond_row = lambda i: jnp.where(is_lat(i), 1 + (i * tm) // t_len, 0)
    cos, sin = rope_tabs
    blocks_per_seq = t_len // tm
    cos = jnp.concatenate([jnp.ones((tm, HEAD_DIM), _F32), cos], axis=0)
    sin = jnp.concatenate([jnp.zeros((tm, HEAD_DIM), _F32), sin], axis=0)
    tab_blk = lambda i: jnp.where(is_lat(i), 1 + i % blocks_per_seq, 0)

    in_specs = [
        pl.BlockSpec((tm, D_MODEL), lambda i: (jnp.minimum(i, nl_blocks - 1), 0)),
        pl.BlockSpec((tm, D_MODEL), lambda i: (ctx_blk(i), 0)),
        pl.BlockSpec((None, 1, D_MODEL), lambda i: (cond_row(i), 0, 0)),
        pl.BlockSpec((None, 1, D_MODEL), lambda i: (cond_row(i), 0, 1)),
        pl.BlockSpec(memory_space=pl.ANY),
        pl.BlockSpec((tm, HEAD_DIM), lambda i: (tab_blk(i), 0)),
        pl.BlockSpec((tm, HEAD_DIM), lambda i: (tab_blk(i), 0)),
    ]
    args = [x_lat, x_ctx, mod3, mod3, w_in, cos, sin]
    aliases = {}
    seqs = tm // s_len
    cache_rows = N_KV_HEADS * s_len
    if prev_caches is None:
        cache_spec = pl.BlockSpec((seqs, DEPTH, cache_rows, HEAD_DIM), lambda i: (ctx_blk(i), 0, 0, 0))
    else:
        cache_spec = pl.BlockSpec((seqs, None, cache_rows, HEAD_DIM), lambda i: (ctx_blk(i), layer, 0, 0))
        aliases = {len(args): 1, len(args) + 1: 2}
        in_specs += [pl.BlockSpec(memory_space=pl.ANY)] * 2
        args += list(prev_caches)
    cache_shape = jax.ShapeDtypeStruct((bp, DEPTH, cache_rows, HEAD_DIM), _F32)
    lat_kv_spec = pl.BlockSpec((tm, D_KV), lambda i: (jnp.minimum(i, nl_blocks), 0))
    lat_kv_shape = jax.ShapeDtypeStruct((n_lat + tm, D_KV), _BF16)
    q, k_cache, v_cache, k_lat, v_lat, rest = pl.pallas_call(
        functools.partial(_proj_kernel, layer=layer, n_lat_blocks=nl_blocks),
        grid=(n // tm,),
        in_specs=in_specs,
        out_specs=[
            pl.BlockSpec((tm, D_ATTN), lambda i: (i, 0)),
            cache_spec, cache_spec, lat_kv_spec, lat_kv_spec,
            pl.BlockSpec((tm, D_REST), lambda i: (i, 0)),
        ],
        out_shape=[
            jax.ShapeDtypeStruct((n, D_ATTN), _BF16),
            cache_shape, cache_shape, lat_kv_shape, lat_kv_shape,
            jax.ShapeDtypeStruct((n, D_REST), _BF16),
        ],
        scratch_shapes=[
            pltpu.VMEM((D_MODEL, D_IN), _BF16),
            pltpu.VMEM((2, D_MODEL, PROJ_TN), _F32),
            pltpu.SemaphoreType.DMA((2,)),
        ],
        input_output_aliases=aliases,
        compiler_params=pltpu.CompilerParams(
            dimension_semantics=("arbitrary",), vmem_limit_bytes=PROJ_VMEM_LIMIT),
        name="proj",
    )(*args)
    return q, (k_cache, v_cache), (k_lat, v_lat), rest


def _softmax_pv(score_parts, sink_col, value_parts):
    m = sink_col
    for s in score_parts:
        m = jnp.maximum(m, jnp.max(s, axis=-1, keepdims=True))
    acc = None
    for s, v_aug in zip(score_parts, value_parts):
        part = jnp.dot(jnp.exp2(s - m).astype(_BF16), v_aug, preferred_element_type=_F32)
        acc = part if acc is None else acc + part
    denom = acc[:, HEAD_DIM:HEAD_DIM + 1] + jnp.exp2(sink_col - m)
    return acc[:, :HEAD_DIM] / denom


def _with_ones(v_bf):
    return jnp.concatenate([v_bf, jnp.ones_like(v_bf)], axis=1)


def _sink_column(sink_ref, kvh, rows):
    head = lax.broadcasted_iota(jnp.int32, (Q_GROUP * rows, 1), 0) // rows
    col = jnp.full((Q_GROUP * rows, 1), sink_ref[kvh * Q_GROUP], _F32)
    for g in range(1, Q_GROUP):
        col = jnp.where(head == g, sink_ref[kvh * Q_GROUP + g], col)
    return col * LOG2_E


def _stack_heads(q_rows):
    return jnp.concatenate(
        [q_rows[:, g * HEAD_DIM:(g + 1) * HEAD_DIM] for g in range(Q_GROUP)], axis=0)


def _qk(q, k_bf):
    return lax.dot_general(q, k_bf, (((1,), (1,)), ((), ())), preferred_element_type=_F32)


def _ctx_attn_kernel(sink_ref, q_ref, k_ref, v_ref, o_ref, *, s_len):
    kvh = pl.program_id(1)
    sink_col = _sink_column(sink_ref, kvh, s_len)

    def one_sequence(bi, carry):
        rows = pl.ds(pl.multiple_of(bi * s_len, s_len), s_len)
        q = _stack_heads(q_ref[rows, :])
        k = k_ref[bi, _head_rows(kvh, s_len), :].astype(_BF16)
        v_aug = _with_ones(v_ref[bi, _head_rows(kvh, s_len), :].astype(_BF16))
        o = _softmax_pv([_qk(q, k)], sink_col, [v_aug])
        for g in range(Q_GROUP):
            o_ref[rows, g * HEAD_DIM:(g + 1) * HEAD_DIM] = o[g * s_len:(g + 1) * s_len].astype(o_ref.dtype)
        return carry

    lax.fori_loop(0, q_ref.shape[0] // s_len, one_sequence, 0)


def _context_attention(q, q_row0, k_cache, v_cache, layer, sink, batch, s_len):
    n = batch * s_len
    tm = CTX_ATTN_SEQS * s_len
    gw = Q_GROUP * HEAD_DIM
    q_blk0 = q_row0 // tm
    kv_spec = pl.BlockSpec((CTX_ATTN_SEQS, None, N_KV_HEADS * s_len, HEAD_DIM), lambda b, h: (b, layer, 0, 0))
    return pl.pallas_call(
        functools.partial(_ctx_attn_kernel, s_len=s_len),
        grid=(n // tm, N_KV_HEADS),
        in_specs=[
            pl.BlockSpec(memory_space=pltpu.SMEM),
            pl.BlockSpec((tm, gw), lambda b, h: (q_blk0 + b, h)),
            kv_spec,
            kv_spec,
        ],
        out_specs=pl.BlockSpec((tm, gw), lambda b, h: (b, h)),
        out_shape=jax.ShapeDtypeStruct((n, D_ATTN), _BF16),
        compiler_params=pltpu.CompilerParams(dimension_semantics=("parallel", "parallel")),
        name="ctx_attention",
    )(sink, q, k_cache, v_cache)


def _lat_attn_kernel(sink_ref, q_ref, k_ref, v_ref, ck_ref, cv_ref, o_ref):
    kvh = pl.program_id(1)
    t_len = q_ref.shape[0]
    nb = t_len // Q_BLOCK
    past = ck_ref.shape[0] // N_KV_HEADS
    ck = ck_ref[_head_rows(kvh, past), :].astype(_BF16)
    cv_aug = _with_ones(cv_ref[_head_rows(kvh, past), :].astype(_BF16))
    v_aug = _with_ones(v_ref[...])
    sink_col = _sink_column(sink_ref, kvh, Q_BLOCK)
    rows = Q_GROUP * Q_BLOCK
    r = lax.broadcasted_iota(jnp.int32, (rows, Q_BLOCK), 0) % Q_BLOCK
    c = lax.broadcasted_iota(jnp.int32, (rows, Q_BLOCK), 1)
    for jb in range(nb):
        r0 = jb * Q_BLOCK
        q = _stack_heads(q_ref[r0:r0 + Q_BLOCK, :])
        lo = max(r0 - WINDOW, 0)
        hi = min(r0 + Q_BLOCK + WINDOW, t_len)
        s = _qk(q, k_ref[lo:hi, :])
        parts = []
        off = 0
        if jb > 0:
            parts.append(jnp.where(c >= r, s[:, :Q_BLOCK], NEG_INF))
            off = Q_BLOCK
        parts.append(s[:, off:off + Q_BLOCK])
        off += Q_BLOCK
        if jb < nb - 1:
            parts.append(jnp.where(c <= r, s[:, off:], NEG_INF))
        s_local = jnp.concatenate(parts, axis=1)
        o = _softmax_pv([s_local, _qk(q, ck)], sink_col, [v_aug[lo:hi], cv_aug])
        for g in range(Q_GROUP):
            o_ref[r0:r0 + Q_BLOCK, g * HEAD_DIM:(g + 1) * HEAD_DIM] = (
                o[g * Q_BLOCK:(g + 1) * Q_BLOCK].astype(o_ref.dtype))


def _latent_attention(q, q_row0, k, v, cache_k4, cache_v4, layer, sink, batch, t_len):
    n = batch * t_len
    gw = Q_GROUP * HEAD_DIM
    q_blk0 = q_row0 // t_len
    cache_spec = pl.BlockSpec((None, None, cache_k4.shape[2], HEAD_DIM), lambda b, h: (b, layer, 0, 0))
    return pl.pallas_call(
        _lat_attn_kernel,
        grid=(batch, N_KV_HEADS),
        in_specs=[
            pl.BlockSpec(memory_space=pltpu.SMEM),
            pl.BlockSpec((t_len, gw), lambda b, h: (q_blk0 + b, h)),
            pl.BlockSpec((t_len, HEAD_DIM), lambda b, h: (b, h)),
            pl.BlockSpec((t_len, HEAD_DIM), lambda b, h: (b, h)),
            cache_spec,
            cache_spec,
        ],
        out_specs=pl.BlockSpec((t_len, gw), lambda b, h: (b, h)),
        out_shape=jax.ShapeDtypeStruct((n, D_ATTN), _BF16),
        compiler_params=pltpu.CompilerParams(
            dimension_semantics=("parallel", "parallel"), vmem_limit_bytes=VMEM_LIMIT),
        name="latent_attention",
    )(sink, q, k, v, cache_k4, cache_v4)


def _shift_rows(x, d, t_len, offset=0):
    n = x.shape[0]
    row = (lax.broadcasted_iota(jnp.int32, x.shape, 0) + (n - offset)) % t_len
    rolled = pltpu.roll(x, d % n, axis=0)
    valid = (row >= d) if d > 0 else (row < t_len + d)
    return jnp.where(valid, rolled, 0.0)


def _sqrt_nonneg(z):
    return z * lax.rsqrt(jnp.maximum(z, F32_TINY))


def _rnn_kernel(x_ref, cw_ref, cb_ref, wg_ref, bg_ref, lam_ref, h0_ref, y_ref, hl_ref,
                af_scr, bf_scr, ab_scr, bb_scr, hf_scr, hb_scr, *, nb, nc):
    t_len = x_ref.shape[1]
    pitch = t_len + SLOT_PAD
    scr = ((af_scr, bf_scr), (ab_scr, bb_scr))
    per_group = SEQ_SLOTS // 2
    kb = min(per_group, max(1, PASS1_ROWS // t_len))

    def seq_of(slot):
        g, u = slot % 2, slot // 2
        return (g * per_group + u, 0) if nc == 1 else (u, g)

    for g in range(2):
        b_first, cl = seq_of(g)
        offset = SLOT_PAD * g
        lanes = slice(cl * LANES, (cl + 1) * LANES)
        cw = cw_ref[:, lanes]
        cb = cb_ref[:, lanes]
        wg = wg_ref[cl]
        bg = bg_ref[cl]
        lam = lam_ref[:, lanes]
        c_softplus = LRU_C * (jnp.maximum(-lam, 0.0) + jnp.log1p(jnp.exp(-jnp.abs(lam))))

        def per_chunk(it, carry, lanes=lanes, cw=cw, cb=cb, wg=wg, bg=bg,
                      c_softplus=c_softplus, g=g, b_first=b_first, offset=offset):
            n_rows = kb * t_len
            x = x_ref[pl.ds(b_first + it * kb, kb), :, lanes].astype(_F32).reshape(n_rows, LANES)
            if offset:
                x = pltpu.roll(x, offset, axis=0)
            xc = cw[0:1] * _shift_rows(x, 2, t_len, offset)
            xc = xc + cw[1:2] * _shift_rows(x, 1, t_len, offset)
            xc = xc + cw[2:3] * x
            xc = xc + cw[3:4] * _shift_rows(x, -1, t_len, offset)
            xc = xc + cb
            half_gates = jnp.dot(xc.astype(_BF16), wg, preferred_element_type=_F32) + bg
            half_xc = 0.5 * xc
            for d in range(2):
                t_r = jnp.tanh(half_gates[:, (2 * d) * LANES:(2 * d + 1) * LANES])
                t_i = jnp.tanh(half_gates[:, (2 * d + 1) * LANES:(2 * d + 2) * LANES])
                k = 0.5 * c_softplus[d:d + 1]
                neg_log_a = t_r * k + k
                a = jnp.exp2(neg_log_a * -LOG2_E)
                bterm = _sqrt_nonneg(jnp.tanh(neg_log_a) * (a * a + 1.0)) * ((t_i + 1.0) * half_xc)
                for jb in range(kb):
                    slot = 2 * (it * kb + jb) + g
                    row0 = pl.multiple_of(slot * pitch - offset, SUBLANES)
                    for dst, val in ((scr[d][0], a), (scr[d][1], bterm)):
                        dst[pl.ds(row0, t_len), :] = val[jb * t_len:(jb + 1) * t_len]
                        if offset:
                            w0 = ((jb + 1) * t_len) % n_rows
                            dst[pl.ds(row0 + t_len, SUBLANES), :] = val[w0:w0 + SUBLANES]
            return carry

        if per_group == kb:
            per_chunk(0, 0)
        else:
            lax.fori_loop(0, per_group // kb, per_chunk, 0)

    def slot_rows(ref3, d):
        rows = []
        for slot in range(SEQ_SLOTS):
            b, cl = seq_of(slot)
            rows.append(ref3[b, d:d + 1, cl * LANES:(cl + 1) * LANES])
        return jnp.concatenate(rows, axis=0)

    for slot in range(SEQ_SLOTS):
        tile = (slot * pitch + t_len) // SUBLANES * SUBLANES
        for h_scr in (hf_scr, hb_scr):
            h_scr[pl.ds(tile, SUBLANES), :] = jnp.zeros((SUBLANES, LANES), _F32)

    def advance2(a_scr, b_scr, h_scr, t0, t1, h):
        r0 = pl.ds(t0, SEQ_SLOTS, stride=pitch)
        r1 = pl.ds(t1, SEQ_SLOTS, stride=pitch)
        a0, b0, a1, b1 = a_scr[r0, :], b_scr[r0, :], a_scr[r1, :], b_scr[r1, :]
        h_scr[r0, :] = a0 * h + b0
        h = (a1 * a0) * h + (a1 * b0 + b1)
        h_scr[r1, :] = h
        return h

    hf, hb = slot_rows(h0_ref, 0), slot_rows(h0_ref, 1)
    for t in range(0, t_len, 2):
        hf = advance2(af_scr, bf_scr, hf_scr, t, t + 1, hf)
        hb = advance2(ab_scr, bb_scr, hb_scr, t_len - 1 - t, t_len - 2 - t, hb)

    for slot in range(SEQ_SLOTS):
        b, cl = seq_of(slot)
        lanes = slice(cl * LANES, (cl + 1) * LANES)
        hl_ref[b, 0:1, lanes] = hf[slot:slot + 1, :]
        hl_ref[b, 1:2, lanes] = hb[slot:slot + 1, :]
        if slot % 2 == 0:
            rows = pl.ds(slot * pitch, t_len)
            y = hf_scr[rows, :] + hb_scr[rows, :]
        else:
            rows = pl.ds(slot * pitch - SLOT_PAD, t_len + SUBLANES)
            y = pltpu.roll(hf_scr[rows, :] + hb_scr[rows, :], t_len + SUBLANES - SLOT_PAD, axis=0)[:t_len]
        y_ref[b, :, lanes] = y.astype(y_ref.dtype)


def _rnn(rest, row0, h0, conv_w, conv_b, wg, bg, lam, batch, t_len):
    nb = min(batch, SEQ_SLOTS)
    blk0 = row0 // (nb * t_len)
    nc = SEQ_SLOTS // nb
    assert nc in (1, 2) and batch % nb == 0, "a grid step holds 8 batches x 1 or 4 batches x 2 channel blocks"
    cw = nc * LANES
    xr_block0 = D_MODEL // cw
    scratch = pltpu.VMEM((SEQ_SLOTS * (t_len + SLOT_PAD), LANES), _F32)
    y, h_last = pl.pallas_call(
        functools.partial(_rnn_kernel, nb=nb, nc=nc),
        grid=(batch // nb, D_RNN // cw),
        in_specs=[
            pl.BlockSpec((nb, t_len, cw), lambda g, c: (blk0 + g, 0, xr_block0 + c)),
            pl.BlockSpec((CONV_W, cw), lambda g, c: (0, c)),
            pl.BlockSpec((1, cw), lambda g, c: (0, c)),
            pl.BlockSpec((nc, RNN_BLOCK, 4 * LANES), lambda g, c: (c, 0, 0)),
            pl.BlockSpec((nc, 1, 4 * LANES), lambda g, c: (c, 0, 0)),
            pl.BlockSpec((2, cw), lambda g, c: (0, c)),
            pl.BlockSpec((nb, 2, cw), lambda g, c: (g, 0, c)),
        ],
        out_specs=[
            pl.BlockSpec((nb, t_len, cw), lambda g, c: (g, 0, c)),
            pl.BlockSpec((nb, 2, cw), lambda g, c: (g, 0, c)),
        ],
        out_shape=[
            jax.ShapeDtypeStruct((batch, t_len, D_RNN), _BF16),
            jax.ShapeDtypeStruct((batch, 2, D_RNN), _F32),
        ],
        scratch_shapes=[scratch] * 6,
        compiler_params=pltpu.CompilerParams(
            dimension_semantics=("parallel", "parallel"), vmem_limit_bytes=RNN_VMEM_LIMIT),
        name="rnn",
    )(rest.reshape(rest.shape[0] // t_len, t_len, D_REST), conv_w, conv_b.reshape(1, D_RNN), wg, bg, lam, h0)
    return y.reshape(batch * t_len, D_RNN), h_last


def _out_kernel(x_ref, gate_ref, oa_ref, ga_ref, or_ref, gr_ref, mga_ref, mgr_ref,
                wa_ref, wr_ref, wo_ref, g_ref, b_ref, o_ref):
    def silu_bf(x):
        h = 0.5 * x
        return h * jnp.tanh(h) + h

    def logistic_f32(x):
        return (0.5 * jnp.tanh(0.5 * x) + 0.5).astype(_F32)

    ya = jnp.dot(oa_ref[...] * silu_bf(ga_ref[...]), wa_ref[...], preferred_element_type=_F32)
    yr = jnp.dot(or_ref[...] * silu_bf(gr_ref[...]), wr_ref[...], preferred_element_type=_F32)
    merged = logistic_f32(mga_ref[...]) * ya + logistic_f32(mgr_ref[...]) * yr
    out = jnp.dot(merged.astype(_BF16), wo_ref[...], preferred_element_type=_F32)
    z = DEEPNORM_ALPHA * x_ref[...] + gate_ref[...] * out
    o_ref[...] = _layer_norm_rows(z) * g_ref[...] + b_ref[...]


def _merge_residual(x2, mod3, o_attn, o_rnn, rest, rest_row0, wa_bf, wr_bf, wo_bf, layer, ln_g, ln_b,
                    cond_row_of_block, tm):
    n = x2.shape[0]
    row_blk = lambda col: pl.BlockSpec((tm, D_MODEL), lambda i, col=col: (i, col))
    rest_blk = lambda col: pl.BlockSpec((tm, D_MODEL), lambda i, col=col: (rest_row0 // tm + i, col))
    whole = pl.BlockSpec((None, D_MODEL, D_MODEL), lambda i: (layer, 0, 0))
    vec = pl.BlockSpec((1, D_MODEL), lambda i: (0, 0))
    return pl.pallas_call(
        _out_kernel,
        grid=(n // tm,),
        in_specs=[
            row_blk(0),
            pl.BlockSpec((None, 1, D_MODEL), lambda i: (cond_row_of_block(i), 0, 2)),
            row_blk(0), rest_blk(0), row_blk(0), rest_blk(2), rest_blk(3), rest_blk(4),
            whole, whole, whole, vec, vec,
        ],
        out_specs=row_blk(0),
        out_shape=jax.ShapeDtypeStruct((n, D_MODEL), _F32),
        compiler_params=pltpu.CompilerParams(
            dimension_semantics=("parallel",), vmem_limit_bytes=VMEM_LIMIT),
        name="merge_residual",
    )(x2, mod3, o_attn, rest, o_rnn, rest, rest, rest, wa_bf, wr_bf, wo_bf,
      ln_g.reshape(1, D_MODEL), ln_b.reshape(1, D_MODEL))


def _rope_tables(t_len):
    quarter = HEAD_DIM // 4
    inv_freq = ROPE_BASE ** (-jnp.arange(quarter, dtype=_F32) / quarter)
    pos = jnp.arange(t_len)
    ang_r = (pos // GRID_W).astype(_F32)[:, None] * inv_freq[None, :]
    ang_c = (pos % GRID_W).astype(_F32)[:, None] * inv_freq[None, :]
    cos = jnp.concatenate([jnp.cos(ang_r)] * 2 + [jnp.cos(ang_c)] * 2, axis=1)
    sin = jnp.concatenate([-jnp.sin(ang_r), jnp.sin(ang_r), -jnp.sin(ang_c), jnp.sin(ang_c)], axis=1)
    return cos, sin


def _gate_weights(rg_wa, rg_ba, rg_wx, rg_bx):
    wg = (0.5 * jnp.concatenate([rg_wa[0], rg_wx[0], rg_wa[1], rg_wx[1]], axis=-1)).astype(_BF16)
    blk = lambda v: v.reshape(N_RNN_BLOCKS, 1, RNN_BLOCK)
    bg = 0.5 * jnp.concatenate([blk(rg_ba[0]), blk(rg_bx[0]), blk(rg_ba[1]), blk(rg_bx[1])], axis=-1)
    return wg, bg


def kernel(x_prompt, x_sample, cache_k, cache_v, state_h, c, c_ctx, w_mod, b_mod, w_in,
           attn_sink, conv_w, conv_b, rg_wa, rg_ba, rg_wx, rg_bx, rg_lambda,
           w_br_attn, w_br_rnn, w_out, ln_g, ln_b):
    bp, s_len, _ = x_prompt.shape
    bd, t_len, _ = x_sample.shape
    past = cache_k.shape[2]

    cond8 = jnp.zeros((COND_ROWS, D_MODEL), _F32).at[0].set(c_ctx).at[1:1 + bd].set(c)
    mod = _modulation(cond8, w_mod, b_mod)

    wa_bf = w_br_attn.astype(_BF16)
    wr_bf = w_br_rnn.astype(_BF16)
    wo_bf = w_out.astype(_BF16)
    rope_tabs = _rope_tables(t_len)
    cache_k4 = cache_k.reshape(bd, DEPTH, past * N_KV_HEADS, HEAD_DIM)
    cache_v4 = cache_v.reshape(bd, DEPTH, past * N_KV_HEADS, HEAD_DIM)

    tm_proj, tm_out = 512, 512
    ctx_row = lambda i: 0
    lat_row_out = lambda i: 1 + (i * tm_out) // t_len
    n_lat = bd * t_len

    xp = x_prompt.reshape(bp * s_len, D_MODEL)
    xs = x_sample.reshape(n_lat, D_MODEL)
    h0_zero = jnp.zeros((bp, 2, D_RNN), _F32)
    hs = []
    kv_caches = None
    for l in range(DEPTH):
        mod3 = mod[l].reshape(COND_ROWS, 1, 3 * D_MODEL)
        wg, bg = _gate_weights(rg_wa[l], rg_ba[l], rg_wx[l], rg_bx[l])
        q, kv_caches, (k_lat, v_lat), rest = _project(
            xs, xp, mod3, w_in, l, tm_proj, rope_tabs, (bd, t_len), (bp, s_len), kv_caches)

        o_attn = _context_attention(q, n_lat, kv_caches[0], kv_caches[1], l, attn_sink[l], bp, s_len)
        o_rnn, h_fin = _rnn(rest, n_lat, h0_zero, conv_w[l], conv_b[l], wg, bg, rg_lambda[l], bp, s_len)
        xp = _merge_residual(xp, mod3, o_attn, o_rnn, rest, n_lat, wa_bf, wr_bf, wo_bf, l,
                             ln_g[l], ln_b[l], ctx_row, tm_out)
        hs.append(h_fin)

        o_attn = _latent_attention(q, 0, k_lat, v_lat, cache_k4, cache_v4, l, attn_sink[l], bd, t_len)
        o_rnn, _ = _rnn(rest, 0, state_h[:, l], conv_w[l], conv_b[l], wg, bg, rg_lambda[l], bd, t_len)
        xs = _merge_residual(xs, mod3, o_attn, o_rnn, rest, 0, wa_bf, wr_bf, wo_bf, l,
                             ln_g[l], ln_b[l], lat_row_out, tm_out)

    y_prompt = xp.reshape(bp, s_len, D_MODEL)
    y_sample = xs.reshape(bd, t_len, D_MODEL)
    cache_shape = (bp, DEPTH, s_len, N_KV_HEADS, HEAD_DIM)
    new_k, new_v = (cache.reshape(cache_shape) for cache in kv_caches)
    return (y_prompt, y_sample, new_k, new_v, jnp.stack(hs, axis=1))
```

```python
import functools

import jax
import jax.numpy as jnp
import numpy as np
from jax import lax
from jax.experimental import pallas as pl
from jax.experimental.pallas import tpu as pltpu

D_MODEL = 1024
DEPTH = 2
GRID_W = 64
N_HEADS = 8
N_KV_HEADS = 2
HEAD_DIM = 128
Q_GROUP = N_HEADS // N_KV_HEADS
D_ATTN = N_HEADS * HEAD_DIM
D_KV = N_KV_HEADS * HEAD_DIM
WINDOW = 128
Q_BLOCK = 128
D_RNN = D_MODEL
N_RNN_BLOCKS = 8
RNN_BLOCK = D_RNN // N_RNN_BLOCKS
CONV_W = 4
CONV_LEFT = 2
LRU_C = 8.0
ROPE_BASE = 10000.0
D_IN = 2 * D_ATTN + 2 * D_KV + 2 * D_RNN + 2 * D_MODEL
DEEPNORM_ALPHA = (2 * DEPTH) ** 0.25
LN_EPS = 1e-6
NEG_INF = -1e30
ATTN_SCALE = HEAD_DIM ** -0.5

LANES = 128
SUBLANES = 8
SEQ_SLOTS = SUBLANES
SLOT_PAD = 4
CTX_ATTN_SEQS = 4
PASS1_ROWS = 1024
COND_ROWS = 8
PROJ_TN = 512
N_PROJ_BLOCKS = D_IN // PROJ_TN
D_REST = D_IN - D_ATTN - 2 * D_KV
_REST_COL = (0, 4, 1, 2, 3)
D_GATES = 4 * D_MODEL
XR_COL0 = 4 * D_MODEL
VMEM_LIMIT = 48 * 1024 * 1024
PROJ_VMEM_LIMIT = 56 * 1024 * 1024
RNN_VMEM_LIMIT = 56 * 1024 * 1024
F32_TINY = 1.1754943508222875e-38
LOG2_E = 1.4426950408889634
Q_PRESCALE = ATTN_SCALE * LOG2_E

_BF16 = jnp.bfloat16
_F32 = jnp.float32


def _silu(x):
    return x * jax.nn.sigmoid(x)


def _layer_norm_rows(x):
    mu = jnp.mean(x, axis=-1, keepdims=True)
    xc = x - mu
    var = jnp.mean(xc * xc, axis=-1, keepdims=True)
    return xc * lax.rsqrt(var + LN_EPS)


def _head_rows(head, n_rows):
    return pl.ds(head, n_rows, stride=N_KV_HEADS)


def _mod_kernel(cond_ref, w_ref, b_ref, o_ref):
    a = _silu(cond_ref[...]).astype(_BF16)
    o_ref[...] = jnp.dot(a, w_ref[...].astype(_BF16), preferred_element_type=_F32) + b_ref[...]


def _modulation(cond8, w_mod, b_mod):
    tn = D_MODEL
    return pl.pallas_call(
        _mod_kernel,
        grid=(DEPTH, 3 * D_MODEL // tn),
        in_specs=[
            pl.BlockSpec((COND_ROWS, D_MODEL), lambda l, j: (0, 0)),
            pl.BlockSpec((None, D_MODEL, tn), lambda l, j: (l, 0, j)),
            pl.BlockSpec((None, 1, tn), lambda l, j: (l, 0, j)),
        ],
        out_specs=pl.BlockSpec((None, COND_ROWS, tn), lambda l, j: (l, 0, j)),
        out_shape=jax.ShapeDtypeStruct((DEPTH, COND_ROWS, 3 * D_MODEL), _F32),
        compiler_params=pltpu.CompilerParams(dimension_semantics=("parallel", "parallel")),
        name="modulation",
    )(cond8, w_mod, b_mod.reshape(DEPTH, 1, 3 * D_MODEL))


def _rope(x, cos, sin_signed):
    lane = lax.broadcasted_iota(jnp.int32, x.shape, 1)
    partner = jnp.where((lane // 32) % 2 == 0,
                        pltpu.roll(x, 3 * 32, axis=1),
                        pltpu.roll(x, 32, axis=1))
    return x * cos + partner * sin_signed


def _proj_kernel(xl_ref, xc_ref, shift_ref, scale_ref, w_hbm, cos_ref, sin_ref, *rest,
                 layer, n_lat_blocks):
    q_ref, kc_ref, vc_ref, kl_ref, vl_ref, r_ref, w_scr, stage_scr, sem = rest[-9:]
    i = pl.program_id(0)

    def chunk_copy(j):
        return pltpu.make_async_copy(
            w_hbm.at[layer, :, pl.ds(j * PROJ_TN, PROJ_TN)], stage_scr.at[j % 2], sem.at[j % 2])

    def fetch_chunk(j):
        if j == 0:
            chunk_copy(0).start()
        if j + 1 < N_PROJ_BLOCKS:
            chunk_copy(j + 1).start()
        chunk_copy(j).wait()
        w_scr[:, j * PROJ_TN:(j + 1) * PROJ_TN] = stage_scr[j % 2].astype(_BF16)

    x = jnp.where(i < n_lat_blocks, xl_ref[...], xc_ref[...])
    y = _layer_norm_rows(x)
    h = (y * (1.0 + scale_ref[...]) + shift_ref[...]).astype(_BF16)

    def rope(blk):
        return _rope(blk, cos_ref[...], sin_ref[...])

    def put_cache(ref, head, val):
        s_len = ref.shape[-2] // N_KV_HEADS
        for bi in range(ref.shape[0]):
            rows = val[bi * s_len:(bi + 1) * s_len].astype(ref.dtype)
            if len(ref.shape) == 3:
                ref[bi, _head_rows(head, s_len), :] = rows
            else:
                for l in range(ref.shape[1]):
                    ref[bi, l, _head_rows(head, s_len), :] = rows if l == layer else jnp.zeros_like(rows)

    def project_block(stream_weights):
        for j in range(N_PROJ_BLOCKS):
            c0 = j * PROJ_TN
            if stream_weights:
                fetch_chunk(j)
            acc = jnp.dot(h, w_scr[:, c0:c0 + PROJ_TN], preferred_element_type=_F32)
            if c0 < D_ATTN:
                for hh in range(PROJ_TN // HEAD_DIM):
                    sl = slice(hh * HEAD_DIM, (hh + 1) * HEAD_DIM)
                    q_ref[:, c0 + hh * HEAD_DIM:c0 + (hh + 1) * HEAD_DIM] = (
                        (rope(acc[:, sl]) * Q_PRESCALE).astype(q_ref.dtype))
            elif c0 == D_ATTN:
                for hh in range(N_KV_HEADS):
                    sl = slice(hh * HEAD_DIM, (hh + 1) * HEAD_DIM)
                    k_rot = rope(acc[:, sl])
                    put_cache(kc_ref, hh, k_rot)
                    put_cache(vc_ref, hh, acc[:, D_KV + hh * HEAD_DIM:D_KV + (hh + 1) * HEAD_DIM])
                    kl_ref[:, sl] = k_rot.astype(kl_ref.dtype)
                vl_ref[...] = acc[:, D_KV:].astype(vl_ref.dtype)
            else:
                off = c0 - D_ATTN - 2 * D_KV
                r0 = _REST_COL[off // D_MODEL] * D_MODEL + off % D_MODEL
                r_ref[:, r0:r0 + PROJ_TN] = acc.astype(r_ref.dtype)

    pl.when(i == 0)(functools.partial(project_block, True))
    pl.when(i != 0)(functools.partial(project_block, False))


def _project(x_lat, x_ctx, mod3, w_in, layer, tm, rope_tabs, lat_dims, ctx_dims, prev_caches):
    (bd, t_len), (bp, s_len) = lat_dims, ctx_dims
    n_lat, n_ctx = bd * t_len, bp * s_len
    n = n_lat + n_ctx
    nl_blocks = n_lat // tm
    is_lat = lambda i: i < nl_blocks
    ctx_blk = lambda i: jnp.maximum(i - nl_blocks, 0)
    cond_row = lambda i: jnp.where(is_lat(i), 1 + (i * tm) // t_len, 0)
    cos, sin = rope_tabs
    blocks_per_seq = t_len // tm
    cos = jnp.asarray(np.concatenate([np.ones((tm, HEAD_DIM), np.float32), cos], axis=0))
    sin = jnp.asarray(np.concatenate([np.zeros((tm, HEAD_DIM), np.float32), sin], axis=0))
    tab_blk = lambda i: jnp.where(is_lat(i), 1 + i % blocks_per_seq, 0)

    in_specs = [
        pl.BlockSpec((tm, D_MODEL), lambda i: (jnp.minimum(i, nl_blocks - 1), 0)),
        pl.BlockSpec((tm, D_MODEL), lambda i: (ctx_blk(i), 0)),
        pl.BlockSpec((None, 1, D_MODEL), lambda i: (cond_row(i), 0, 0)),
        pl.BlockSpec((None, 1, D_MODEL), lambda i: (cond_row(i), 0, 1)),
        pl.BlockSpec(memory_space=pl.ANY),
        pl.BlockSpec((tm, HEAD_DIM), lambda i: (tab_blk(i), 0)),
        pl.BlockSpec((tm, HEAD_DIM), lambda i: (tab_blk(i), 0)),
    ]
    args = [x_lat, x_ctx, mod3, mod3, w_in, cos, sin]
    aliases = {}
    seqs = tm // s_len
    cache_rows = N_KV_HEADS * s_len
    if prev_caches is None:
        cache_spec = pl.BlockSpec((seqs, DEPTH, cache_rows, HEAD_DIM), lambda i: (ctx_blk(i), 0, 0, 0))
    else:
        cache_spec = pl.BlockSpec((seqs, None, cache_rows, HEAD_DIM), lambda i: (ctx_blk(i), layer, 0, 0))
        aliases = {len(args): 1, len(args) + 1: 2}
        in_specs += [pl.BlockSpec(memory_space=pl.ANY)] * 2
        args += list(prev_caches)
    cache_shape = jax.ShapeDtypeStruct((bp, DEPTH, cache_rows, HEAD_DIM), _F32)
    lat_kv_spec = pl.BlockSpec((tm, D_KV), lambda i: (jnp.minimum(i, nl_blocks), 0))
    lat_kv_shape = jax.ShapeDtypeStruct((n_lat + tm, D_KV), _BF16)
    q, k_cache, v_cache, k_lat, v_lat, rest = pl.pallas_call(
        functools.partial(_proj_kernel, layer=layer, n_lat_blocks=nl_blocks),
        grid=(n // tm,),
        in_specs=in_specs,
        out_specs=[
            pl.BlockSpec((tm, D_ATTN), lambda i: (i, 0)),
            cache_spec, cache_spec, lat_kv_spec, lat_kv_spec,
            pl.BlockSpec((tm, D_REST), lambda i: (i, 0)),
        ],
        out_shape=[
            jax.ShapeDtypeStruct((n, D_ATTN), _BF16),
            cache_shape, cache_shape, lat_kv_shape, lat_kv_shape,
            jax.ShapeDtypeStruct((n, D_REST), _BF16),
        ],
        scratch_shapes=[
            pltpu.VMEM((D_MODEL, D_IN), _BF16),
            pltpu.VMEM((2, D_MODEL, PROJ_TN), _F32),
            pltpu.SemaphoreType.DMA((2,)),
        ],
        input_output_aliases=aliases,
        compiler_params=pltpu.CompilerParams(
            dimension_semantics=("arbitrary",), vmem_limit_bytes=PROJ_VMEM_LIMIT),
        name="proj",
    )(*args)
    return q, (k_cache, v_cache), (k_lat, v_lat), rest


def _softmax_pv(score_parts, sink_col, value_parts):
    m = sink_col
    for s in score_parts:
        m = jnp.maximum(m, jnp.max(s, axis=-1, keepdims=True))
    acc = None
    for s, v_aug in zip(score_parts, value_parts):
        part = jnp.dot(jnp.exp2(s - m).astype(_BF16), v_aug, preferred_element_type=_F32)
        acc = part if acc is None else acc + part
    denom = acc[:, HEAD_DIM:HEAD_DIM + 1] + jnp.exp2(sink_col - m)
    return acc[:, :HEAD_DIM] / denom


def _with_ones(v_bf):
    return jnp.concatenate([v_bf, jnp.ones_like(v_bf)], axis=1)


def _sink_column(sink_ref, kvh, rows):
    head = lax.broadcasted_iota(jnp.int32, (Q_GROUP * rows, 1), 0) // rows
    col = jnp.full((Q_GROUP * rows, 1), sink_ref[kvh * Q_GROUP], _F32)
    for g in range(1, Q_GROUP):
        col = jnp.where(head == g, sink_ref[kvh * Q_GROUP + g], col)
    return col * LOG2_E


def _stack_heads(q_rows):
    return jnp.concatenate(
        [q_rows[:, g * HEAD_DIM:(g + 1) * HEAD_DIM] for g in range(Q_GROUP)], axis=0)


def _qk(q, k_bf):
    return lax.dot_general(q, k_bf, (((1,), (1,)), ((), ())), preferred_element_type=_F32)


def _ctx_attn_kernel(sink_ref, q_ref, k_ref, v_ref, o_ref, *, s_len):
    kvh = pl.program_id(1)
    sink_col = _sink_column(sink_ref, kvh, s_len)

    def one_sequence(bi, carry):
        rows = pl.ds(pl.multiple_of(bi * s_len, s_len), s_len)
        q = _stack_heads(q_ref[rows, :])
        k = k_ref[bi, _head_rows(kvh, s_len), :].astype(_BF16)
        v_aug = _with_ones(v_ref[bi, _head_rows(kvh, s_len), :].astype(_BF16))
        o = _softmax_pv([_qk(q, k)], sink_col, [v_aug])
        for g in range(Q_GROUP):
            o_ref[rows, g * HEAD_DIM:(g + 1) * HEAD_DIM] = o[g * s_len:(g + 1) * s_len].astype(o_ref.dtype)
        return carry

    lax.fori_loop(0, q_ref.shape[0] // s_len, one_sequence, 0)


def _context_attention(q, q_row0, k_cache, v_cache, layer, sink, batch, s_len):
    n = batch * s_len
    tm = CTX_ATTN_SEQS * s_len
    gw = Q_GROUP * HEAD_DIM
    q_blk0 = q_row0 // tm
    kv_spec = pl.BlockSpec((CTX_ATTN_SEQS, None, N_KV_HEADS * s_len, HEAD_DIM), lambda b, h: (b, layer, 0, 0))
    return pl.pallas_call(
        functools.partial(_ctx_attn_kernel, s_len=s_len),
        grid=(n // tm, N_KV_HEADS),
        in_specs=[
            pl.BlockSpec(memory_space=pltpu.SMEM),
            pl.BlockSpec((tm, gw), lambda b, h: (q_blk0 + b, h)),
            kv_spec,
            kv_spec,
        ],
        out_specs=pl.BlockSpec((tm, gw), lambda b, h: (b, h)),
        out_shape=jax.ShapeDtypeStruct((n, D_ATTN), _BF16),
        compiler_params=pltpu.CompilerParams(dimension_semantics=("parallel", "parallel")),
        name="ctx_attention",
    )(sink, q, k_cache, v_cache)


def _lat_attn_kernel(sink_ref, q_ref, k_ref, v_ref, ck_ref, cv_ref, o_ref):
    kvh = pl.program_id(1)
    t_len = q_ref.shape[0]
    nb = t_len // Q_BLOCK
    past = ck_ref.shape[0] // N_KV_HEADS
    ck = ck_ref[_head_rows(kvh, past), :].astype(_BF16)
    cv_aug = _with_ones(cv_ref[_head_rows(kvh, past), :].astype(_BF16))
    v_aug = _with_ones(v_ref[...])
    sink_col = _sink_column(sink_ref, kvh, Q_BLOCK)
    rows = Q_GROUP * Q_BLOCK
    r = lax.broadcasted_iota(jnp.int32, (rows, Q_BLOCK), 0) % Q_BLOCK
    c = lax.broadcasted_iota(jnp.int32, (rows, Q_BLOCK), 1)
    for jb in range(nb):
        r0 = jb * Q_BLOCK
        q = _stack_heads(q_ref[r0:r0 + Q_BLOCK, :])
        lo = max(r0 - WINDOW, 0)
        hi = min(r0 + Q_BLOCK + WINDOW, t_len)
        s = _qk(q, k_ref[lo:hi, :])
        parts = []
        off = 0
        if jb > 0:
            parts.append(jnp.where(c >= r, s[:, :Q_BLOCK], NEG_INF))
            off = Q_BLOCK
        parts.append(s[:, off:off + Q_BLOCK])
        off += Q_BLOCK
        if jb < nb - 1:
            parts.append(jnp.where(c <= r, s[:, off:], NEG_INF))
        s_local = jnp.concatenate(parts, axis=1)
        o = _softmax_pv([s_local, _qk(q, ck)], sink_col, [v_aug[lo:hi], cv_aug])
        for g in range(Q_GROUP):
            o_ref[r0:r0 + Q_BLOCK, g * HEAD_DIM:(g + 1) * HEAD_DIM] = (
                o[g * Q_BLOCK:(g + 1) * Q_BLOCK].astype(o_ref.dtype))


def _latent_attention(q, q_row0, k, v, cache_k4, cache_v4, layer, sink, batch, t_len):
    n = batch * t_len
    gw = Q_GROUP * HEAD_DIM
    q_blk0 = q_row0 // t_len
    cache_spec = pl.BlockSpec((None, None, cache_k4.shape[2], HEAD_DIM), lambda b, h: (b, layer, 0, 0))
    return pl.pallas_call(
        _lat_attn_kernel,
        grid=(batch, N_KV_HEADS),
        in_specs=[
            pl.BlockSpec(memory_space=pltpu.SMEM),
            pl.BlockSpec((t_len, gw), lambda b, h: (q_blk0 + b, h)),
            pl.BlockSpec((t_len, HEAD_DIM), lambda b, h: (b, h)),
            pl.BlockSpec((t_len, HEAD_DIM), lambda b, h: (b, h)),
            cache_spec,
            cache_spec,
        ],
        out_specs=pl.BlockSpec((t_len, gw), lambda b, h: (b, h)),
        out_shape=jax.ShapeDtypeStruct((n, D_ATTN), _BF16),
        compiler_params=pltpu.CompilerParams(
            dimension_semantics=("parallel", "parallel"), vmem_limit_bytes=VMEM_LIMIT),
        name="latent_attention",
    )(sink, q, k, v, cache_k4, cache_v4)


def _shift_rows(x, d, t_len, offset=0):
    n = x.shape[0]
    row = (lax.broadcasted_iota(jnp.int32, x.shape, 0) + (n - offset)) % t_len
    rolled = pltpu.roll(x, d % n, axis=0)
    valid = (row >= d) if d > 0 else (row < t_len + d)
    return jnp.where(valid, rolled, 0.0)


def _sqrt_nonneg(z):
    return z * lax.rsqrt(jnp.maximum(z, F32_TINY))


def _rnn_kernel(x_ref, cw_ref, cb_ref, wg_ref, bg_ref, lam_ref, h0_ref, y_ref, hl_ref,
                af_scr, bf_scr, ab_scr, bb_scr, hf_scr, hb_scr, *, nb, nc):
    t_len = x_ref.shape[1]
    pitch = t_len + SLOT_PAD
    scr = ((af_scr, bf_scr), (ab_scr, bb_scr))
    per_group = SEQ_SLOTS // 2
    kb = min(per_group, max(1, PASS1_ROWS // t_len))

    def seq_of(slot):
        g, u = slot % 2, slot // 2
        return (g * per_group + u, 0) if nc == 1 else (u, g)

    for g in range(2):
        b_first, cl = seq_of(g)
        offset = SLOT_PAD * g
        lanes = slice(cl * LANES, (cl + 1) * LANES)
        cw = cw_ref[:, lanes]
        cb = cb_ref[:, lanes]
        wg = wg_ref[cl]
        bg = bg_ref[cl]
        lam = lam_ref[:, lanes]
        c_softplus = LRU_C * (jnp.maximum(-lam, 0.0) + jnp.log1p(jnp.exp(-jnp.abs(lam))))

        def per_chunk(it, carry, lanes=lanes, cw=cw, cb=cb, wg=wg, bg=bg,
                      c_softplus=c_softplus, g=g, b_first=b_first, offset=offset):
            n_rows = kb * t_len
            x = x_ref[pl.ds(b_first + it * kb, kb), :, lanes].astype(_F32).reshape(n_rows, LANES)
            if offset:
                x = pltpu.roll(x, offset, axis=0)
            xc = cw[0:1] * _shift_rows(x, 2, t_len, offset)
            xc = xc + cw[1:2] * _shift_rows(x, 1, t_len, offset)
            xc = xc + cw[2:3] * x
            xc = xc + cw[3:4] * _shift_rows(x, -1, t_len, offset)
            xc = xc + cb
            half_gates = jnp.dot(xc.astype(_BF16), wg, preferred_element_type=_F32) + bg
            half_xc = 0.5 * xc
            for d in range(2):
                t_r = jnp.tanh(half_gates[:, (2 * d) * LANES:(2 * d + 1) * LANES])
                t_i = jnp.tanh(half_gates[:, (2 * d + 1) * LANES:(2 * d + 2) * LANES])
                k = 0.5 * c_softplus[d:d + 1]
                neg_log_a = t_r * k + k
                a = jnp.exp2(neg_log_a * -LOG2_E)
                bterm = _sqrt_nonneg(jnp.tanh(neg_log_a) * (a * a + 1.0)) * ((t_i + 1.0) * half_xc)
                for jb in range(kb):
                    slot = 2 * (it * kb + jb) + g
                    row0 = pl.multiple_of(slot * pitch - offset, SUBLANES)
                    for dst, val in ((scr[d][0], a), (scr[d][1], bterm)):
                        dst[pl.ds(row0, t_len), :] = val[jb * t_len:(jb + 1) * t_len]
                        if offset:
                            w0 = ((jb + 1) * t_len) % n_rows
                            dst[pl.ds(row0 + t_len, SUBLANES), :] = val[w0:w0 + SUBLANES]
            return carry

        if per_group == kb:
            per_chunk(0, 0)
        else:
            lax.fori_loop(0, per_group // kb, per_chunk, 0)

    def slot_rows(ref3, d):
        rows = []
        for slot in range(SEQ_SLOTS):
            b, cl = seq_of(slot)
            rows.append(ref3[b, d:d + 1, cl * LANES:(cl + 1) * LANES])
        return jnp.concatenate(rows, axis=0)

    for slot in range(SEQ_SLOTS):
        tile = (slot * pitch + t_len) // SUBLANES * SUBLANES
        for h_scr in (hf_scr, hb_scr):
            h_scr[pl.ds(tile, SUBLANES), :] = jnp.zeros((SUBLANES, LANES), _F32)

    def advance2(a_scr, b_scr, h_scr, t0, t1, h):
        r0 = pl.ds(t0, SEQ_SLOTS, stride=pitch)
        r1 = pl.ds(t1, SEQ_SLOTS, stride=pitch)
        a0, b0, a1, b1 = a_scr[r0, :], b_scr[r0, :], a_scr[r1, :], b_scr[r1, :]
        h_scr[r0, :] = a0 * h + b0
        h = (a1 * a0) * h + (a1 * b0 + b1)
        h_scr[r1, :] = h
        return h

    hf, hb = slot_rows(h0_ref, 0), slot_rows(h0_ref, 1)
    for t in range(0, t_len, 2):
        hf = advance2(af_scr, bf_scr, hf_scr, t, t + 1, hf)
        hb = advance2(ab_scr, bb_scr, hb_scr, t_len - 1 - t, t_len - 2 - t, hb)

    for slot in range(SEQ_SLOTS):
        b, cl = seq_of(slot)
        lanes = slice(cl * LANES, (cl + 1) * LANES)
        hl_ref[b, 0:1, lanes] = hf[slot:slot + 1, :]
        hl_ref[b, 1:2, lanes] = hb[slot:slot + 1, :]
        if slot % 2 == 0:
            rows = pl.ds(slot * pitch, t_len)
            y = hf_scr[rows, :] + hb_scr[rows, :]
        else:
            rows = pl.ds(slot * pitch - SLOT_PAD, t_len + SUBLANES)
            y = pltpu.roll(hf_scr[rows, :] + hb_scr[rows, :], t_len + SUBLANES - SLOT_PAD, axis=0)[:t_len]
        y_ref[b, :, lanes] = y.astype(y_ref.dtype)


def _rnn(rest, row0, h0, conv_w, conv_b, wg, bg, lam, batch, t_len):
    nb = min(batch, SEQ_SLOTS)
    blk0 = row0 // (nb * t_len)
    nc = SEQ_SLOTS // nb
    assert nc in (1, 2) and batch % nb == 0, "a grid step holds 8 batches x 1 or 4 batches x 2 channel blocks"
    cw = nc * LANES
    xr_block0 = XR_COL0 // cw
    scratch = pltpu.VMEM((SEQ_SLOTS * (t_len + SLOT_PAD), LANES), _F32)
    y, h_last = pl.pallas_call(
        functools.partial(_rnn_kernel, nb=nb, nc=nc),
        grid=(batch // nb, D_RNN // cw),
        in_specs=[
            pl.BlockSpec((nb, t_len, cw), lambda g, c: (blk0 + g, 0, xr_block0 + c)),
            pl.BlockSpec((CONV_W, cw), lambda g, c: (0, c)),
            pl.BlockSpec((1, cw), lambda g, c: (0, c)),
            pl.BlockSpec((nc, RNN_BLOCK, 4 * LANES), lambda g, c: (c, 0, 0)),
            pl.BlockSpec((nc, 1, 4 * LANES), lambda g, c: (c, 0, 0)),
            pl.BlockSpec((2, cw), lambda g, c: (0, c)),
            pl.BlockSpec((nb, 2, cw), lambda g, c: (g, 0, c)),
        ],
        out_specs=[
            pl.BlockSpec((nb, t_len, cw), lambda g, c: (g, 0, c)),
            pl.BlockSpec((nb, 2, cw), lambda g, c: (g, 0, c)),
        ],
        out_shape=[
            jax.ShapeDtypeStruct((batch, t_len, D_RNN), _BF16),
            jax.ShapeDtypeStruct((batch, 2, D_RNN), _F32),
        ],
        scratch_shapes=[scratch] * 6,
        compiler_params=pltpu.CompilerParams(
            dimension_semantics=("parallel", "parallel"), vmem_limit_bytes=RNN_VMEM_LIMIT),
        name="rnn",
    )(rest.reshape(rest.shape[0] // t_len, t_len, D_REST), conv_w, conv_b.reshape(1, D_RNN), wg, bg, lam, h0)
    return y.reshape(batch * t_len, D_RNN), h_last


def _out_kernel(x_ref, gate_ref, oa_ref, or_ref, gates_ref, wa_ref, wr_ref, wo_ref, g_ref, b_ref, o_ref):
    ga_ref, gr_ref, mga_ref, mgr_ref = (
        gates_ref.at[:, p * D_MODEL:(p + 1) * D_MODEL] for p in range(4))
    def silu_bf(x):
        h = 0.5 * x
        return h * jnp.tanh(h) + h

    def logistic_f32(x):
        return (0.5 * jnp.tanh(0.5 * x) + 0.5).astype(_F32)

    ya = jnp.dot(oa_ref[...] * silu_bf(ga_ref[...]), wa_ref[...], preferred_element_type=_F32)
    yr = jnp.dot(or_ref[...] * silu_bf(gr_ref[...]), wr_ref[...], preferred_element_type=_F32)
    merged = logistic_f32(mga_ref[...]) * ya + logistic_f32(mgr_ref[...]) * yr
    out = jnp.dot(merged.astype(_BF16), wo_ref[...], preferred_element_type=_F32)
    z = DEEPNORM_ALPHA * x_ref[...] + gate_ref[...] * out
    o_ref[...] = _layer_norm_rows(z) * g_ref[...] + b_ref[...]


def _merge_residual(x2, mod3, o_attn, o_rnn, rest, rest_row0, wa_bf, wr_bf, wo_bf, layer, ln_g, ln_b,
                    cond_row_of_block, tm):
    n = x2.shape[0]
    row_blk = pl.BlockSpec((tm, D_MODEL), lambda i: (i, 0))
    gates_blk = pl.BlockSpec((tm, D_GATES), lambda i: (rest_row0 // tm + i, 0))
    whole = pl.BlockSpec((None, D_MODEL, D_MODEL), lambda i: (layer, 0, 0))
    vec = pl.BlockSpec((1, D_MODEL), lambda i: (0, 0))
    return pl.pallas_call(
        _out_kernel,
        grid=(n // tm,),
        in_specs=[
            row_blk,
            pl.BlockSpec((None, 1, D_MODEL), lambda i: (cond_row_of_block(i), 0, 2)),
            row_blk, row_blk, gates_blk,
            whole, whole, whole, vec, vec,
        ],
        out_specs=row_blk,
        out_shape=jax.ShapeDtypeStruct((n, D_MODEL), _F32),
        compiler_params=pltpu.CompilerParams(
            dimension_semantics=("parallel",), vmem_limit_bytes=VMEM_LIMIT),
        name="merge_residual",
    )(x2, mod3, o_attn, o_rnn, rest, wa_bf, wr_bf, wo_bf,
      ln_g.reshape(1, D_MODEL), ln_b.reshape(1, D_MODEL))


def _rope_tables(t_len):
    quarter = HEAD_DIM // 4
    inv_freq = ROPE_BASE ** (-np.arange(quarter, dtype=np.float64) / quarter)
    pos = np.arange(t_len)
    ang_r = (pos // GRID_W).astype(np.float64)[:, None] * inv_freq[None, :]
    ang_c = (pos % GRID_W).astype(np.float64)[:, None] * inv_freq[None, :]
    cos = np.concatenate([np.cos(ang_r)] * 2 + [np.cos(ang_c)] * 2, axis=1)
    sin = np.concatenate([-np.sin(ang_r), np.sin(ang_r), -np.sin(ang_c), np.sin(ang_c)], axis=1)
    return cos.astype(np.float32), sin.astype(np.float32)


def _gate_weights(rg_wa, rg_ba, rg_wx, rg_bx):
    wg = (0.5 * jnp.concatenate([rg_wa[0], rg_wx[0], rg_wa[1], rg_wx[1]], axis=-1)).astype(_BF16)
    blk = lambda v: v.reshape(N_RNN_BLOCKS, 1, RNN_BLOCK)
    bg = 0.5 * jnp.concatenate([blk(rg_ba[0]), blk(rg_bx[0]), blk(rg_ba[1]), blk(rg_bx[1])], axis=-1)
    return wg, bg


def kernel(x_prompt, x_sample, cache_k, cache_v, state_h, c, c_ctx, w_mod, b_mod, w_in,
           attn_sink, conv_w, conv_b, rg_wa, rg_ba, rg_wx, rg_bx, rg_lambda,
           w_br_attn, w_br_rnn, w_out, ln_g, ln_b):
    bp, s_len, _ = x_prompt.shape
    bd, t_len, _ = x_sample.shape
    past = cache_k.shape[2]

    cond8 = jnp.zeros((COND_ROWS, D_MODEL), _F32).at[0].set(c_ctx).at[1:1 + bd].set(c)
    mod = _modulation(cond8, w_mod, b_mod)

    wa_bf = w_br_attn.astype(_BF16)
    wr_bf = w_br_rnn.astype(_BF16)
    wo_bf = w_out.astype(_BF16)
    rope_tabs = _rope_tables(t_len)
    cache_k4 = cache_k.reshape(bd, DEPTH, past * N_KV_HEADS, HEAD_DIM)
    cache_v4 = cache_v.reshape(bd, DEPTH, past * N_KV_HEADS, HEAD_DIM)

    tm_proj, tm_out = 512, 512
    ctx_row = lambda i: 0
    lat_row_out = lambda i: 1 + (i * tm_out) // t_len
    n_lat = bd * t_len

    xp = x_prompt.reshape(bp * s_len, D_MODEL)
    xs = x_sample.reshape(n_lat, D_MODEL)
    h0_zero = jnp.zeros((bp, 2, D_RNN), _F32)
    hs = []
    kv_caches = None
    for l in range(DEPTH):
        mod3 = mod[l].reshape(COND_ROWS, 1, 3 * D_MODEL)
        wg, bg = _gate_weights(rg_wa[l], rg_ba[l], rg_wx[l], rg_bx[l])
        q, kv_caches, (k_lat, v_lat), rest = _project(
            xs, xp, mod3, w_in, l, tm_proj, rope_tabs, (bd, t_len), (bp, s_len), kv_caches)

        o_attn = _context_attention(q, n_lat, kv_caches[0], kv_caches[1], l, attn_sink[l], bp, s_len)
        o_rnn, h_fin = _rnn(rest, n_lat, h0_zero, conv_w[l], conv_b[l], wg, bg, rg_lambda[l], bp, s_len)
        xp = _merge_residual(xp, mod3, o_attn, o_rnn, rest, n_lat, wa_bf, wr_bf, wo_bf, l,
                             ln_g[l], ln_b[l], ctx_row, tm_out)
        hs.append(h_fin)

        o_attn = _latent_attention(q, 0, k_lat, v_lat, cache_k4, cache_v4, l, attn_sink[l], bd, t_len)
        o_rnn, _ = _rnn(rest, 0, state_h[:, l], conv_w[l], conv_b[l], wg, bg, rg_lambda[l], bd, t_len)
        xs = _merge_residual(xs, mod3, o_attn, o_rnn, rest, 0, wa_bf, wr_bf, wo_bf, l,
                             ln_g[l], ln_b[l], lat_row_out, tm_out)

    y_prompt = xp.reshape(bp, s_len, D_MODEL)
    y_sample = xs.reshape(bd, t_len, D_MODEL)
    cache_shape = (bp, DEPTH, s_len, N_KV_HEADS, HEAD_DIM)
    new_k, new_v = (cache.reshape(cache_shape) for cache in kv_caches)
    return (y_prompt, y_sample, new_k, new_v, jnp.stack(hs, axis=1))
```

```python
import functools

import jax
import jax.numpy as jnp
import numpy as np
from jax import lax
from jax.experimental import pallas as pl
from jax.experimental.pallas import tpu as pltpu

D_MODEL = 1024
DEPTH = 2
GRID_W = 64
N_HEADS = 8
N_KV_HEADS = 2
HEAD_DIM = 128
Q_GROUP = N_HEADS // N_KV_HEADS
D_ATTN = N_HEADS * HEAD_DIM
D_KV = N_KV_HEADS * HEAD_DIM
WINDOW = 128
Q_BLOCK = 128
D_RNN = D_MODEL
N_RNN_BLOCKS = 8
RNN_BLOCK = D_RNN // N_RNN_BLOCKS
CONV_W = 4
CONV_LEFT = 2
LRU_C = 8.0
ROPE_BASE = 10000.0
D_IN = 2 * D_ATTN + 2 * D_KV + 2 * D_RNN + 2 * D_MODEL
DEEPNORM_ALPHA = (2 * DEPTH) ** 0.25
LN_EPS = 1e-6
NEG_INF = -1e30
ATTN_SCALE = HEAD_DIM ** -0.5

LANES = 128
SUBLANES = 8
SEQ_SLOTS = SUBLANES
SLOT_PAD = 4
CTX_ATTN_SEQS = 8
PASS1_ROWS = 1024
COND_ROWS = 8
PROJ_TN = 512
N_PROJ_BLOCKS = D_IN // PROJ_TN
D_REST = D_IN - D_ATTN - 2 * D_KV
_REST_COL = (0, 4, 1, 2, 3)
D_GATES = 4 * D_MODEL
XR_COL0 = 4 * D_MODEL
VMEM_LIMIT = 48 * 1024 * 1024
PROJ_VMEM_LIMIT = 56 * 1024 * 1024
RNN_VMEM_LIMIT = 56 * 1024 * 1024
F32_TINY = 1.1754943508222875e-38
LOG2_E = 1.4426950408889634
Q_PRESCALE = ATTN_SCALE * LOG2_E

_BF16 = jnp.bfloat16
_F32 = jnp.float32


def _silu(x):
    return x * jax.nn.sigmoid(x)


def _layer_norm_rows(x):
    mu = jnp.mean(x, axis=-1, keepdims=True)
    xc = x - mu
    var = jnp.mean(xc * xc, axis=-1, keepdims=True)
    return xc * lax.rsqrt(var + LN_EPS)


def _head_rows(head, n_rows):
    return pl.ds(head, n_rows, stride=N_KV_HEADS)


def _mod_kernel(cond_ref, w_ref, b_ref, o_ref):
    a = _silu(cond_ref[...]).astype(_BF16)
    o_ref[...] = jnp.dot(a, w_ref[...].astype(_BF16), preferred_element_type=_F32) + b_ref[...]


def _modulation(cond8, w_mod, b_mod):
    tn = D_MODEL
    return pl.pallas_call(
        _mod_kernel,
        grid=(DEPTH, 3 * D_MODEL // tn),
        in_specs=[
            pl.BlockSpec((COND_ROWS, D_MODEL), lambda l, j: (0, 0)),
            pl.BlockSpec((None, D_MODEL, tn), lambda l, j: (l, 0, j)),
            pl.BlockSpec((None, 1, tn), lambda l, j: (l, 0, j)),
        ],
        out_specs=pl.BlockSpec((None, COND_ROWS, tn), lambda l, j: (l, 0, j)),
        out_shape=jax.ShapeDtypeStruct((DEPTH, COND_ROWS, 3 * D_MODEL), _F32),
        compiler_params=pltpu.CompilerParams(dimension_semantics=("parallel", "parallel")),
        name="modulation",
    )(cond8, w_mod, b_mod.reshape(DEPTH, 1, 3 * D_MODEL))


def _rope(x, cos, sin_signed):
    lane = lax.broadcasted_iota(jnp.int32, x.shape, 1)
    partner = jnp.where((lane // 32) % 2 == 0,
                        pltpu.roll(x, 3 * 32, axis=1),
                        pltpu.roll(x, 32, axis=1))
    return x * cos + partner * sin_signed


def _proj_kernel(xl_ref, xc_ref, shift_ref, scale_ref, w_hbm, cos_ref, sin_ref, *rest,
                 layer, n_lat_blocks):
    q_ref, kc_ref, vc_ref, kl_ref, vl_ref, r_ref, w_scr, stage_scr, sem = rest[-9:]
    i = pl.program_id(0)

    def chunk_copy(j):
        return pltpu.make_async_copy(
            w_hbm.at[layer, :, pl.ds(j * PROJ_TN, PROJ_TN)], stage_scr.at[j % 2], sem.at[j % 2])

    def fetch_chunk(j):
        if j == 0:
            chunk_copy(0).start()
        if j + 1 < N_PROJ_BLOCKS:
            chunk_copy(j + 1).start()
        chunk_copy(j).wait()
        w_scr[:, j * PROJ_TN:(j + 1) * PROJ_TN] = stage_scr[j % 2].astype(_BF16)

    x = jnp.where(i < n_lat_blocks, xl_ref[...], xc_ref[...])
    y = _layer_norm_rows(x)
    h = (y * (1.0 + scale_ref[...]) + shift_ref[...]).astype(_BF16)

    def rope(blk):
        return _rope(blk, cos_ref[...], sin_ref[...])

    def put_cache(ref, head, val):
        s_len = ref.shape[-2] // N_KV_HEADS
        for bi in range(ref.shape[0]):
            rows = val[bi * s_len:(bi + 1) * s_len].astype(ref.dtype)
            if len(ref.shape) == 3:
                ref[bi, _head_rows(head, s_len), :] = rows
            else:
                for l in range(ref.shape[1]):
                    ref[bi, l, _head_rows(head, s_len), :] = rows if l == layer else jnp.zeros_like(rows)

    def project_block(stream_weights):
        for j in range(N_PROJ_BLOCKS):
            c0 = j * PROJ_TN
            if stream_weights:
                fetch_chunk(j)
            acc = jnp.dot(h, w_scr[:, c0:c0 + PROJ_TN], preferred_element_type=_F32)
            if c0 < D_ATTN:
                for hh in range(PROJ_TN // HEAD_DIM):
                    sl = slice(hh * HEAD_DIM, (hh + 1) * HEAD_DIM)
                    q_ref[:, c0 + hh * HEAD_DIM:c0 + (hh + 1) * HEAD_DIM] = (
                        (rope(acc[:, sl]) * Q_PRESCALE).astype(q_ref.dtype))
            elif c0 == D_ATTN:
                for hh in range(N_KV_HEADS):
                    sl = slice(hh * HEAD_DIM, (hh + 1) * HEAD_DIM)
                    k_rot = rope(acc[:, sl])
                    put_cache(kc_ref, hh, k_rot)
                    put_cache(vc_ref, hh, acc[:, D_KV + hh * HEAD_DIM:D_KV + (hh + 1) * HEAD_DIM])
                    kl_ref[:, sl] = k_rot.astype(kl_ref.dtype)
                vl_ref[...] = acc[:, D_KV:].astype(vl_ref.dtype)
            else:
                off = c0 - D_ATTN - 2 * D_KV
                r0 = _REST_COL[off // D_MODEL] * D_MODEL + off % D_MODEL
                r_ref[:, r0:r0 + PROJ_TN] = acc.astype(r_ref.dtype)

    pl.when(i == 0)(functools.partial(project_block, True))
    pl.when(i != 0)(functools.partial(project_block, False))


def _project(x_lat, x_ctx, mod3, w_in, layer, tm, rope_tabs, lat_dims, ctx_dims, prev_caches):
    (bd, t_len), (bp, s_len) = lat_dims, ctx_dims
    n_lat, n_ctx = bd * t_len, bp * s_len
    n = n_lat + n_ctx
    nl_blocks = n_lat // tm
    is_lat = lambda i: i < nl_blocks
    ctx_blk = lambda i: jnp.maximum(i - nl_blocks, 0)
    cond_row = lambda i: jnp.where(is_lat(i), 1 + (i * tm) // t_len, 0)
    cos, sin = rope_tabs
    blocks_per_seq = t_len // tm
    cos = jnp.asarray(np.concatenate([np.ones((tm, HEAD_DIM), np.float32), cos], axis=0))
    sin = jnp.asarray(np.concatenate([np.zeros((tm, HEAD_DIM), np.float32), sin], axis=0))
    tab_blk = lambda i: jnp.where(is_lat(i), 1 + i % blocks_per_seq, 0)

    in_specs = [
        pl.BlockSpec((tm, D_MODEL), lambda i: (jnp.minimum(i, nl_blocks - 1), 0)),
        pl.BlockSpec((tm, D_MODEL), lambda i: (ctx_blk(i), 0)),
        pl.BlockSpec((None, 1, D_MODEL), lambda i: (cond_row(i), 0, 0)),
        pl.BlockSpec((None, 1, D_MODEL), lambda i: (cond_row(i), 0, 1)),
        pl.BlockSpec(memory_space=pl.ANY),
        pl.BlockSpec((tm, HEAD_DIM), lambda i: (tab_blk(i), 0)),
        pl.BlockSpec((tm, HEAD_DIM), lambda i: (tab_blk(i), 0)),
    ]
    args = [x_lat, x_ctx, mod3, mod3, w_in, cos, sin]
    aliases = {}
    seqs = tm // s_len
    cache_rows = N_KV_HEADS * s_len
    if prev_caches is None:
        cache_spec = pl.BlockSpec((seqs, DEPTH, cache_rows, HEAD_DIM), lambda i: (ctx_blk(i), 0, 0, 0))
    else:
        cache_spec = pl.BlockSpec((seqs, None, cache_rows, HEAD_DIM), lambda i: (ctx_blk(i), layer, 0, 0))
        aliases = {len(args): 1, len(args) + 1: 2}
        in_specs += [pl.BlockSpec(memory_space=pl.ANY)] * 2
        args += list(prev_caches)
    cache_shape = jax.ShapeDtypeStruct((bp, DEPTH, cache_rows, HEAD_DIM), _F32)
    lat_kv_spec = pl.BlockSpec((tm, D_KV), lambda i: (jnp.minimum(i, nl_blocks), 0))
    lat_kv_shape = jax.ShapeDtypeStruct((n_lat + tm, D_KV), _BF16)
    q, k_cache, v_cache, k_lat, v_lat, rest = pl.pallas_call(
        functools.partial(_proj_kernel, layer=layer, n_lat_blocks=nl_blocks),
        grid=(n // tm,),
        in_specs=in_specs,
        out_specs=[
            pl.BlockSpec((tm, D_ATTN), lambda i: (i, 0)),
            cache_spec, cache_spec, lat_kv_spec, lat_kv_spec,
            pl.BlockSpec((tm, D_REST), lambda i: (i, 0)),
        ],
        out_shape=[
            jax.ShapeDtypeStruct((n, D_ATTN), _BF16),
            cache_shape, cache_shape, lat_kv_shape, lat_kv_shape,
            jax.ShapeDtypeStruct((n, D_REST), _BF16),
        ],
        scratch_shapes=[
            pltpu.VMEM((D_MODEL, D_IN), _BF16),
            pltpu.VMEM((2, D_MODEL, PROJ_TN), _F32),
            pltpu.SemaphoreType.DMA((2,)),
        ],
        input_output_aliases=aliases,
        compiler_params=pltpu.CompilerParams(
            dimension_semantics=("arbitrary",), vmem_limit_bytes=PROJ_VMEM_LIMIT),
        name="proj",
    )(*args)
    return q, (k_cache, v_cache), (k_lat, v_lat), rest


def _softmax_pv(score_parts, sink_col, value_parts):
    m = sink_col
    for s in score_parts:
        m = jnp.maximum(m, jnp.max(s, axis=-1, keepdims=True))
    acc = None
    for s, v_aug in zip(score_parts, value_parts):
        part = jnp.dot(jnp.exp2(s - m).astype(_BF16), v_aug, preferred_element_type=_F32)
        acc = part if acc is None else acc + part
    denom = acc[:, HEAD_DIM:HEAD_DIM + 1] + jnp.exp2(sink_col - m)
    return acc[:, :HEAD_DIM] / denom


def _with_ones(v_bf):
    return jnp.concatenate([v_bf, jnp.ones_like(v_bf)], axis=1)


def _sink_column(sink_ref, kvh, rows):
    head = lax.broadcasted_iota(jnp.int32, (Q_GROUP * rows, 1), 0) // rows
    col = jnp.full((Q_GROUP * rows, 1), sink_ref[kvh * Q_GROUP], _F32)
    for g in range(1, Q_GROUP):
        col = jnp.where(head == g, sink_ref[kvh * Q_GROUP + g], col)
    return col * LOG2_E


def _stack_heads(q_rows):
    return jnp.concatenate(
        [q_rows[:, g * HEAD_DIM:(g + 1) * HEAD_DIM] for g in range(Q_GROUP)], axis=0)


def _qk(q, k_bf):
    return lax.dot_general(q, k_bf, (((1,), (1,)), ((), ())), preferred_element_type=_F32)


def _ctx_attn_kernel(sink_ref, q_ref, k_ref, v_ref, o_ref, *, s_len):
    kvh = pl.program_id(1)
    n_q = Q_GROUP * s_len
    head = lax.broadcasted_iota(jnp.int32, (1, n_q), 1) // s_len
    sink_row = jnp.full((1, n_q), sink_ref[kvh * Q_GROUP], _F32)
    for g in range(1, Q_GROUP):
        sink_row = jnp.where(head == g, sink_ref[kvh * Q_GROUP + g], sink_row)
    sink_row = sink_row * LOG2_E


    def one_sequence(bi, carry):
        rows = pl.ds(pl.multiple_of(bi * s_len, s_len), s_len)
        q = _stack_heads(q_ref[rows, :])
        k = k_ref[bi, _head_rows(kvh, s_len), :].astype(_BF16)
        v_aug = _with_ones(v_ref[bi, _head_rows(kvh, s_len), :].astype(_BF16))
        s_t = _qk(k, q)
        m = jnp.maximum(jnp.max(s_t, axis=0, keepdims=True), sink_row)
        e_t = jnp.exp2(s_t - m).astype(_BF16)
        acc_t = lax.dot_general(v_aug, e_t, (((0,), (0,)), ((), ())), preferred_element_type=_F32)
        denom = acc_t[HEAD_DIM:HEAD_DIM + 1, :] + jnp.exp2(sink_row - m)
        o_t = acc_t[:HEAD_DIM, :] / denom
        for g in range(Q_GROUP):
            o_ref[rows, g * HEAD_DIM:(g + 1) * HEAD_DIM] = (
                o_t[:, g * s_len:(g + 1) * s_len].T.astype(o_ref.dtype))
        return carry

    for bi in range(q_ref.shape[0] // s_len):
        one_sequence(bi, 0)


def _context_attention(q, q_row0, k_cache, v_cache, layer, sink, batch, s_len):
    n = batch * s_len
    tm = CTX_ATTN_SEQS * s_len
    gw = Q_GROUP * HEAD_DIM
    q_blk0 = q_row0 // tm
    kv_spec = pl.BlockSpec((CTX_ATTN_SEQS, None, N_KV_HEADS * s_len, HEAD_DIM), lambda b, h: (b, layer, 0, 0))
    return pl.pallas_call(
        functools.partial(_ctx_attn_kernel, s_len=s_len),
        grid=(n // tm, N_KV_HEADS),
        in_specs=[
            pl.BlockSpec(memory_space=pltpu.SMEM),
            pl.BlockSpec((tm, gw), lambda b, h: (q_blk0 + b, h)),
            kv_spec,
            kv_spec,
        ],
        out_specs=pl.BlockSpec((tm, gw), lambda b, h: (b, h)),
        out_shape=jax.ShapeDtypeStruct((n, D_ATTN), _BF16),
        compiler_params=pltpu.CompilerParams(dimension_semantics=("parallel", "parallel")),
        name="ctx_attention",
    )(sink, q, k_cache, v_cache)


def _lat_attn_kernel(sink_ref, q_ref, k_ref, v_ref, ck_ref, cv_ref, o_ref):
    kvh = pl.program_id(1)
    t_len = q_ref.shape[0]
    nb = t_len // Q_BLOCK
    past = ck_ref.shape[0] // N_KV_HEADS
    ck = ck_ref[_head_rows(kvh, past), :].astype(_BF16)
    cv_aug = _with_ones(cv_ref[_head_rows(kvh, past), :].astype(_BF16))
    v_aug = _with_ones(v_ref[...])
    sink_col = _sink_column(sink_ref, kvh, Q_BLOCK)
    rows = Q_GROUP * Q_BLOCK
    r = lax.broadcasted_iota(jnp.int32, (rows, Q_BLOCK), 0) % Q_BLOCK
    c = lax.broadcasted_iota(jnp.int32, (rows, Q_BLOCK), 1)
    for jb in range(nb):
        r0 = jb * Q_BLOCK
        q = _stack_heads(q_ref[r0:r0 + Q_BLOCK, :])
        lo = max(r0 - WINDOW, 0)
        hi = min(r0 + Q_BLOCK + WINDOW, t_len)
        s = _qk(q, k_ref[lo:hi, :])
        parts = []
        off = 0
        if jb > 0:
            parts.append(jnp.where(c >= r, s[:, :Q_BLOCK], NEG_INF))
            off = Q_BLOCK
        parts.append(s[:, off:off + Q_BLOCK])
        off += Q_BLOCK
        if jb < nb - 1:
            parts.append(jnp.where(c <= r, s[:, off:], NEG_INF))
        s_local = jnp.concatenate(parts, axis=1)
        o = _softmax_pv([s_local, _qk(q, ck)], sink_col, [v_aug[lo:hi], cv_aug])
        for g in range(Q_GROUP):
            o_ref[r0:r0 + Q_BLOCK, g * HEAD_DIM:(g + 1) * HEAD_DIM] = (
                o[g * Q_BLOCK:(g + 1) * Q_BLOCK].astype(o_ref.dtype))


def _latent_attention(q, q_row0, k, v, cache_k4, cache_v4, layer, sink, batch, t_len):
    n = batch * t_len
    gw = Q_GROUP * HEAD_DIM
    q_blk0 = q_row0 // t_len
    cache_spec = pl.BlockSpec((None, None, cache_k4.shape[2], HEAD_DIM), lambda b, h: (b, layer, 0, 0))
    return pl.pallas_call(
        _lat_attn_kernel,
        grid=(batch, N_KV_HEADS),
        in_specs=[
            pl.BlockSpec(memory_space=pltpu.SMEM),
            pl.BlockSpec((t_len, gw), lambda b, h: (q_blk0 + b, h)),
            pl.BlockSpec((t_len, HEAD_DIM), lambda b, h: (b, h)),
            pl.BlockSpec((t_len, HEAD_DIM), lambda b, h: (b, h)),
            cache_spec,
            cache_spec,
        ],
        out_specs=pl.BlockSpec((t_len, gw), lambda b, h: (b, h)),
        out_shape=jax.ShapeDtypeStruct((n, D_ATTN), _BF16),
        compiler_params=pltpu.CompilerParams(
            dimension_semantics=("parallel", "parallel"), vmem_limit_bytes=VMEM_LIMIT),
        name="latent_attention",
    )(sink, q, k, v, cache_k4, cache_v4)


def _shift_rows(x, d, t_len, offset=0):
    n = x.shape[0]
    row = (lax.broadcasted_iota(jnp.int32, x.shape, 0) + (n - offset)) % t_len
    rolled = pltpu.roll(x, d % n, axis=0)
    valid = (row >= d) if d > 0 else (row < t_len + d)
    return jnp.where(valid, rolled, 0.0)


def _sqrt_nonneg(z):
    return z * lax.rsqrt(jnp.maximum(z, F32_TINY))


def _rnn_kernel(x_ref, cw_ref, cb_ref, wg_ref, bg_ref, lam_ref, h0_ref, y_ref, hl_ref,
                af_scr, bf_scr, ab_scr, bb_scr, hf_scr, hb_scr, *, nb, nc):
    t_len = x_ref.shape[1]
    pitch = t_len + SLOT_PAD
    scr = ((af_scr, bf_scr), (ab_scr, bb_scr))
    per_group = SEQ_SLOTS // 2
    kb = min(per_group, max(1, PASS1_ROWS // t_len))

    def seq_of(slot):
        g, u = slot % 2, slot // 2
        return (g * per_group + u, 0) if nc == 1 else (u, g)

    for g in range(2):
        b_first, cl = seq_of(g)
        offset = SLOT_PAD * g
        lanes = slice(cl * LANES, (cl + 1) * LANES)
        cw = cw_ref[:, lanes]
        cb = cb_ref[:, lanes]
        wg = wg_ref[cl]
        bg = bg_ref[cl]
        lam = lam_ref[:, lanes]
        c_softplus = LRU_C * (jnp.maximum(-lam, 0.0) + jnp.log1p(jnp.exp(-jnp.abs(lam))))

        def per_chunk(it, carry, lanes=lanes, cw=cw, cb=cb, wg=wg, bg=bg,
                      c_softplus=c_softplus, g=g, b_first=b_first, offset=offset):
            n_rows = kb * t_len
            x = x_ref[pl.ds(b_first + it * kb, kb), :, lanes].astype(_F32).reshape(n_rows, LANES)
            if offset:
                x = pltpu.roll(x, offset, axis=0)
            xc = cw[0:1] * _shift_rows(x, 2, t_len, offset)
            xc = xc + cw[1:2] * _shift_rows(x, 1, t_len, offset)
            xc = xc + cw[2:3] * x
            xc = xc + cw[3:4] * _shift_rows(x, -1, t_len, offset)
            xc = xc + cb
            half_gates = jnp.dot(xc.astype(_BF16), wg, preferred_element_type=_F32) + bg
            half_xc = 0.5 * xc
            for d in range(2):
                t_r = jnp.tanh(half_gates[:, (2 * d) * LANES:(2 * d + 1) * LANES])
                t_i = jnp.tanh(half_gates[:, (2 * d + 1) * LANES:(2 * d + 2) * LANES])
                k = 0.5 * c_softplus[d:d + 1]
                neg_log_a = t_r * k + k
                a = jnp.exp2(neg_log_a * -LOG2_E)
                bterm = _sqrt_nonneg(jnp.tanh(neg_log_a) * (a * a + 1.0)) * ((t_i + 1.0) * half_xc)
                for jb in range(kb):
                    slot = 2 * (it * kb + jb) + g
                    row0 = pl.multiple_of(slot * pitch - offset, SUBLANES)
                    for dst, val in ((scr[d][0], a), (scr[d][1], bterm)):
                        dst[pl.ds(row0, t_len), :] = val[jb * t_len:(jb + 1) * t_len]
                        if offset:
                            w0 = ((jb + 1) * t_len) % n_rows
                            dst[pl.ds(row0 + t_len, SUBLANES), :] = val[w0:w0 + SUBLANES]
            return carry

        if per_group == kb:
            per_chunk(0, 0)
        else:
            lax.fori_loop(0, per_group // kb, per_chunk, 0)

    def slot_rows(ref3, d):
        rows = []
        for slot in range(SEQ_SLOTS):
            b, cl = seq_of(slot)
            rows.append(ref3[b, d:d + 1, cl * LANES:(cl + 1) * LANES])
        return jnp.concatenate(rows, axis=0)

    for slot in range(SEQ_SLOTS):
        tile = (slot * pitch + t_len) // SUBLANES * SUBLANES
        for h_scr in (hf_scr, hb_scr):
            h_scr[pl.ds(tile, SUBLANES), :] = jnp.zeros((SUBLANES, LANES), _F32)

    def advance2(a_scr, b_scr, h_scr, t0, t1, h):
        r0 = pl.ds(t0, SEQ_SLOTS, stride=pitch)
        r1 = pl.ds(t1, SEQ_SLOTS, stride=pitch)
        a0, b0, a1, b1 = a_scr[r0, :], b_scr[r0, :], a_scr[r1, :], b_scr[r1, :]
        h_scr[r0, :] = a0 * h + b0
        h = (a1 * a0) * h + (a1 * b0 + b1)
        h_scr[r1, :] = h
        return h

    hf, hb = slot_rows(h0_ref, 0), slot_rows(h0_ref, 1)
    for t in range(0, t_len, 2):
        hf = advance2(af_scr, bf_scr, hf_scr, t, t + 1, hf)
        hb = advance2(ab_scr, bb_scr, hb_scr, t_len - 1 - t, t_len - 2 - t, hb)

    for slot in range(SEQ_SLOTS):
        b, cl = seq_of(slot)
        lanes = slice(cl * LANES, (cl + 1) * LANES)
        hl_ref[b, 0:1, lanes] = hf[slot:slot + 1, :]
        hl_ref[b, 1:2, lanes] = hb[slot:slot + 1, :]
        if slot % 2 == 0:
            rows = pl.ds(slot * pitch, t_len)
            y = hf_scr[rows, :] + hb_scr[rows, :]
        else:
            rows = pl.ds(slot * pitch - SLOT_PAD, t_len + SUBLANES)
            y = pltpu.roll(hf_scr[rows, :] + hb_scr[rows, :], t_len + SUBLANES - SLOT_PAD, axis=0)[:t_len]
        y_ref[b, :, lanes] = y.astype(y_ref.dtype)


def _rnn(rest, row0, h0, conv_w, conv_b, wg, bg, lam, batch, t_len):
    nb = min(batch, SEQ_SLOTS)
    blk0 = row0 // (nb * t_len)
    nc = SEQ_SLOTS // nb
    assert nc in (1, 2) and batch % nb == 0, "a grid step holds 8 batches x 1 or 4 batches x 2 channel blocks"
    cw = nc * LANES
    xr_block0 = XR_COL0 // cw
    scratch = pltpu.VMEM((SEQ_SLOTS * (t_len + SLOT_PAD), LANES), _F32)
    y, h_last = pl.pallas_call(
        functools.partial(_rnn_kernel, nb=nb, nc=nc),
        grid=(batch // nb, D_RNN // cw),
        in_specs=[
            pl.BlockSpec((nb, t_len, cw), lambda g, c: (blk0 + g, 0, xr_block0 + c)),
            pl.BlockSpec((CONV_W, cw), lambda g, c: (0, c)),
            pl.BlockSpec((1, cw), lambda g, c: (0, c)),
            pl.BlockSpec((nc, RNN_BLOCK, 4 * LANES), lambda g, c: (c, 0, 0)),
            pl.BlockSpec((nc, 1, 4 * LANES), lambda g, c: (c, 0, 0)),
            pl.BlockSpec((2, cw), lambda g, c: (0, c)),
            pl.BlockSpec((nb, 2, cw), lambda g, c: (g, 0, c)),
        ],
        out_specs=[
            pl.BlockSpec((nb, t_len, cw), lambda g, c: (g, 0, c)),
            pl.BlockSpec((nb, 2, cw), lambda g, c: (g, 0, c)),
        ],
        out_shape=[
            jax.ShapeDtypeStruct((batch, t_len, D_RNN), _BF16),
            jax.ShapeDtypeStruct((batch, 2, D_RNN), _F32),
        ],
        scratch_shapes=[scratch] * 6,
        compiler_params=pltpu.CompilerParams(
            dimension_semantics=("parallel", "parallel"), vmem_limit_bytes=RNN_VMEM_LIMIT),
        name="rnn",
    )(rest.reshape(rest.shape[0] // t_len, t_len, D_REST), conv_w, conv_b.reshape(1, D_RNN), wg, bg, lam, h0)
    return y.reshape(batch * t_len, D_RNN), h_last


def _out_kernel(x_ref, gate_ref, oa_ref, or_ref, gates_ref, wa_ref, wr_ref, wo_ref, g_ref, b_ref, o_ref):
    ga_ref, gr_ref, mga_ref, mgr_ref = (
        gates_ref.at[:, p * D_MODEL:(p + 1) * D_MODEL] for p in range(4))
    def silu_bf(x):
        h = 0.5 * x
        return h * jnp.tanh(h) + h

    def logistic_f32(x):
        return (0.5 * jnp.tanh(0.5 * x) + 0.5).astype(_F32)

    ya = jnp.dot(oa_ref[...] * silu_bf(ga_ref[...]), wa_ref[...], preferred_element_type=_F32)
    yr = jnp.dot(or_ref[...] * silu_bf(gr_ref[...]), wr_ref[...], preferred_element_type=_F32)
    merged = logistic_f32(mga_ref[...]) * ya + logistic_f32(mgr_ref[...]) * yr
    out = jnp.dot(merged.astype(_BF16), wo_ref[...], preferred_element_type=_F32)
    z = DEEPNORM_ALPHA * x_ref[...] + gate_ref[...] * out
    o_ref[...] = _layer_norm_rows(z) * g_ref[...] + b_ref[...]


def _merge_residual(x2, mod3, o_attn, o_rnn, rest, rest_row0, wa_bf, wr_bf, wo_bf, layer, ln_g, ln_b,
                    cond_row_of_block, tm):
    n = x2.shape[0]
    row_blk = pl.BlockSpec((tm, D_MODEL), lambda i: (i, 0))
    gates_blk = pl.BlockSpec((tm, D_GATES), lambda i: (rest_row0 // tm + i, 0))
    whole = pl.BlockSpec((None, D_MODEL, D_MODEL), lambda i: (layer, 0, 0))
    vec = pl.BlockSpec((1, D_MODEL), lambda i: (0, 0))
    return pl.pallas_call(
        _out_kernel,
        grid=(n // tm,),
        in_specs=[
            row_blk,
            pl.BlockSpec((None, 1, D_MODEL), lambda i: (cond_row_of_block(i), 0, 2)),
            row_blk, row_blk, gates_blk,
            whole, whole, whole, vec, vec,
        ],
        out_specs=row_blk,
        out_shape=jax.ShapeDtypeStruct((n, D_MODEL), _F32),
        compiler_params=pltpu.CompilerParams(
            dimension_semantics=("parallel",), vmem_limit_bytes=VMEM_LIMIT),
        name="merge_residual",
    )(x2, mod3, o_attn, o_rnn, rest, wa_bf, wr_bf, wo_bf,
      ln_g.reshape(1, D_MODEL), ln_b.reshape(1, D_MODEL))


def _rope_tables(t_len):
    quarter = HEAD_DIM // 4
    inv_freq = ROPE_BASE ** (-np.arange(quarter, dtype=np.float64) / quarter)
    pos = np.arange(t_len)
    ang_r = (pos // GRID_W).astype(np.float64)[:, None] * inv_freq[None, :]
    ang_c = (pos % GRID_W).astype(np.float64)[:, None] * inv_freq[None, :]
    cos = np.concatenate([np.cos(ang_r)] * 2 + [np.cos(ang_c)] * 2, axis=1)
    sin = np.concatenate([-np.sin(ang_r), np.sin(ang_r), -np.sin(ang_c), np.sin(ang_c)], axis=1)
    return cos.astype(np.float32), sin.astype(np.float32)


def _gate_weights(rg_wa, rg_ba, rg_wx, rg_bx):
    wg = (0.5 * jnp.concatenate([rg_wa[0], rg_wx[0], rg_wa[1], rg_wx[1]], axis=-1)).astype(_BF16)
    blk = lambda v: v.reshape(N_RNN_BLOCKS, 1, RNN_BLOCK)
    bg = 0.5 * jnp.concatenate([blk(rg_ba[0]), blk(rg_bx[0]), blk(rg_ba[1]), blk(rg_bx[1])], axis=-1)
    return wg, bg


def kernel(x_prompt, x_sample, cache_k, cache_v, state_h, c, c_ctx, w_mod, b_mod, w_in,
           attn_sink, conv_w, conv_b, rg_wa, rg_ba, rg_wx, rg_bx, rg_lambda,
           w_br_attn, w_br_rnn, w_out, ln_g, ln_b):
    bp, s_len, _ = x_prompt.shape
    bd, t_len, _ = x_sample.shape
    past = cache_k.shape[2]

    cond8 = jnp.zeros((COND_ROWS, D_MODEL), _F32).at[0].set(c_ctx).at[1:1 + bd].set(c)
    mod = _modulation(cond8, w_mod, b_mod)

    wa_bf = w_br_attn.astype(_BF16)
    wr_bf = w_br_rnn.astype(_BF16)
    wo_bf = w_out.astype(_BF16)
    rope_tabs = _rope_tables(t_len)
    cache_k4 = cache_k.reshape(bd, DEPTH, past * N_KV_HEADS, HEAD_DIM)
    cache_v4 = cache_v.reshape(bd, DEPTH, past * N_KV_HEADS, HEAD_DIM)

    tm_proj, tm_out = 512, 512
    ctx_row = lambda i: 0
    lat_row_out = lambda i: 1 + (i * tm_out) // t_len
    n_lat = bd * t_len

    xp = x_prompt.reshape(bp * s_len, D_MODEL)
    xs = x_sample.reshape(n_lat, D_MODEL)
    h0_zero = jnp.zeros((bp, 2, D_RNN), _F32)
    hs = []
    kv_caches = None
    for l in range(DEPTH):
        mod3 = mod[l].reshape(COND_ROWS, 1, 3 * D_MODEL)
        wg, bg = _gate_weights(rg_wa[l], rg_ba[l], rg_wx[l], rg_bx[l])
        q, kv_caches, (k_lat, v_lat), rest = _project(
            xs, xp, mod3, w_in, l, tm_proj, rope_tabs, (bd, t_len), (bp, s_len), kv_caches)

        o_attn = _context_attention(q, n_lat, kv_caches[0], kv_caches[1], l, attn_sink[l], bp, s_len)
        o_rnn, h_fin = _rnn(rest, n_lat, h0_zero, conv_w[l], conv_b[l], wg, bg, rg_lambda[l], bp, s_len)
        xp = _merge_residual(xp, mod3, o_attn, o_rnn, rest, n_lat, wa_bf, wr_bf, wo_bf, l,
                             ln_g[l], ln_b[l], ctx_row, tm_out)
        hs.append(h_fin)

        o_attn = _latent_attention(q, 0, k_lat, v_lat, cache_k4, cache_v4, l, attn_sink[l], bd, t_len)
        o_rnn, _ = _rnn(rest, 0, state_h[:, l], conv_w[l], conv_b[l], wg, bg, rg_lambda[l], bd, t_len)
        xs = _merge_residual(xs, mod3, o_attn, o_rnn, rest, 0, wa_bf, wr_bf, wo_bf, l,
                             ln_g[l], ln_b[l], lat_row_out, tm_out)

    y_prompt = xp.reshape(bp, s_len, D_MODEL)
    y_sample = xs.reshape(bd, t_len, D_MODEL)
    cache_shape = (bp, DEPTH, s_len, N_KV_HEADS, HEAD_DIM)
    new_k, new_v = (cache.reshape(cache_shape) for cache in kv_caches)
    return (y_prompt, y_sample, new_k, new_v, jnp.stack(hs, axis=1))
```

```python
import functools

import jax
import jax.numpy as jnp
import numpy as np
from jax import lax
from jax.experimental import pallas as pl
from jax.experimental.pallas import tpu as pltpu

D_MODEL = 1024
DEPTH = 2
GRID_W = 64
N_HEADS = 8
N_KV_HEADS = 2
HEAD_DIM = 128
Q_GROUP = N_HEADS // N_KV_HEADS
D_ATTN = N_HEADS * HEAD_DIM
D_KV = N_KV_HEADS * HEAD_DIM
WINDOW = 128
Q_BLOCK = 128
D_RNN = D_MODEL
N_RNN_BLOCKS = 8
RNN_BLOCK = D_RNN // N_RNN_BLOCKS
CONV_W = 4
CONV_LEFT = 2
LRU_C = 8.0
ROPE_BASE = 10000.0
D_IN = 2 * D_ATTN + 2 * D_KV + 2 * D_RNN + 2 * D_MODEL
DEEPNORM_ALPHA = (2 * DEPTH) ** 0.25
LN_EPS = 1e-6
NEG_INF = -1e30
ATTN_SCALE = HEAD_DIM ** -0.5

LANES = 128
SUBLANES = 8
SEQ_SLOTS = SUBLANES
SLOT_PAD = 4
CTX_ATTN_SEQS = 8
PASS1_ROWS = 1024
COND_ROWS = 8
PROJ_TN = 512
N_PROJ_BLOCKS = D_IN // PROJ_TN
D_REST = D_IN - D_ATTN - 2 * D_KV
_REST_COL = (0, 4, 1, 2, 3)
D_GATES = 4 * D_MODEL
XR_COL0 = 4 * D_MODEL
VMEM_LIMIT = 48 * 1024 * 1024
PROJ_VMEM_LIMIT = 56 * 1024 * 1024
RNN_VMEM_LIMIT = 56 * 1024 * 1024
F32_TINY = 1.1754943508222875e-38
LOG2_E = 1.4426950408889634
Q_PRESCALE = ATTN_SCALE * LOG2_E

_BF16 = jnp.bfloat16
_F32 = jnp.float32


def _silu(x):
    return x * jax.nn.sigmoid(x)


def _layer_norm_rows(x):
    mu = jnp.mean(x, axis=-1, keepdims=True)
    xc = x - mu
    var = jnp.mean(xc * xc, axis=-1, keepdims=True)
    return xc * lax.rsqrt(var + LN_EPS)


def _head_rows(head, n_rows):
    return pl.ds(head, n_rows, stride=N_KV_HEADS)


def _mod_kernel(cond_ref, w_ref, b_ref, o_ref):
    a = _silu(cond_ref[...]).astype(_BF16)
    o_ref[...] = jnp.dot(a, w_ref[...].astype(_BF16), preferred_element_type=_F32) + b_ref[...]


def _modulation(cond8, w_mod, b_mod):
    tn = D_MODEL
    return pl.pallas_call(
        _mod_kernel,
        grid=(DEPTH, 3 * D_MODEL // tn),
        in_specs=[
            pl.BlockSpec((COND_ROWS, D_MODEL), lambda l, j: (0, 0)),
            pl.BlockSpec((None, D_MODEL, tn), lambda l, j: (l, 0, j)),
            pl.BlockSpec((None, 1, tn), lambda l, j: (l, 0, j)),
        ],
        out_specs=pl.BlockSpec((None, COND_ROWS, tn), lambda l, j: (l, 0, j)),
        out_shape=jax.ShapeDtypeStruct((DEPTH, COND_ROWS, 3 * D_MODEL), _F32),
        compiler_params=pltpu.CompilerParams(dimension_semantics=("parallel", "parallel")),
        name="modulation",
    )(cond8, w_mod, b_mod.reshape(DEPTH, 1, 3 * D_MODEL))


def _rope(x, cos, sin_signed):
    lane = lax.broadcasted_iota(jnp.int32, x.shape, 1)
    partner = jnp.where((lane // 32) % 2 == 0,
                        pltpu.roll(x, 3 * 32, axis=1),
                        pltpu.roll(x, 32, axis=1))
    return x * cos + partner * sin_signed


def _proj_kernel(xl_ref, xc_ref, shift_ref, scale_ref, w_hbm, cos_ref, sin_ref, *rest,
                 layer, n_lat_blocks):
    q_ref, kc_ref, vc_ref, kl_ref, vl_ref, r_ref, w_scr, stage_scr, sem = rest[-9:]
    i = pl.program_id(0)

    def chunk_copy(j):
        return pltpu.make_async_copy(
            w_hbm.at[layer, :, pl.ds(j * PROJ_TN, PROJ_TN)], stage_scr.at[j % 2], sem.at[j % 2])

    def fetch_chunk(j):
        if j == 0:
            chunk_copy(0).start()
        if j + 1 < N_PROJ_BLOCKS:
            chunk_copy(j + 1).start()
        chunk_copy(j).wait()
        w_scr[:, j * PROJ_TN:(j + 1) * PROJ_TN] = stage_scr[j % 2].astype(_BF16)

    x = jnp.where(i < n_lat_blocks, xl_ref[...], xc_ref[...])
    y = _layer_norm_rows(x)
    h = (y * (1.0 + scale_ref[...]) + shift_ref[...]).astype(_BF16)

    def rope(blk):
        return _rope(blk, cos_ref[...], sin_ref[...])

    def put_cache(ref, head, val):
        s_len = ref.shape[-2] // N_KV_HEADS
        for bi in range(ref.shape[0]):
            rows = val[bi * s_len:(bi + 1) * s_len].astype(ref.dtype)
            if len(ref.shape) == 3:
                ref[bi, _head_rows(head, s_len), :] = rows
            else:
                for l in range(ref.shape[1]):
                    ref[bi, l, _head_rows(head, s_len), :] = rows if l == layer else jnp.zeros_like(rows)

    def project_block(stream_weights):
        for j in range(N_PROJ_BLOCKS):
            c0 = j * PROJ_TN
            if stream_weights:
                fetch_chunk(j)
            acc = jnp.dot(h, w_scr[:, c0:c0 + PROJ_TN], preferred_element_type=_F32)
            if c0 < D_ATTN:
                for hh in range(PROJ_TN // HEAD_DIM):
                    sl = slice(hh * HEAD_DIM, (hh + 1) * HEAD_DIM)
                    q_ref[:, c0 + hh * HEAD_DIM:c0 + (hh + 1) * HEAD_DIM] = (
                        (rope(acc[:, sl]) * Q_PRESCALE).astype(q_ref.dtype))
            elif c0 == D_ATTN:
                for hh in range(N_KV_HEADS):
                    sl = slice(hh * HEAD_DIM, (hh + 1) * HEAD_DIM)
                    k_rot = rope(acc[:, sl])
                    put_cache(kc_ref, hh, k_rot)
                    put_cache(vc_ref, hh, acc[:, D_KV + hh * HEAD_DIM:D_KV + (hh + 1) * HEAD_DIM])
                    kl_ref[:, sl] = k_rot.astype(kl_ref.dtype)
                vl_ref[...] = acc[:, D_KV:].astype(vl_ref.dtype)
            else:
                off = c0 - D_ATTN - 2 * D_KV
                r0 = _REST_COL[off // D_MODEL] * D_MODEL + off % D_MODEL
                r_ref[:, r0:r0 + PROJ_TN] = acc.astype(r_ref.dtype)

    pl.when(i == 0)(functools.partial(project_block, True))
    pl.when(i != 0)(functools.partial(project_block, False))


def _project(x_lat, x_ctx, mod3, w_in, layer, tm, rope_tabs, lat_dims, ctx_dims, prev_caches):
    (bd, t_len), (bp, s_len) = lat_dims, ctx_dims
    n_lat, n_ctx = bd * t_len, bp * s_len
    n = n_lat + n_ctx
    nl_blocks = n_lat // tm
    is_lat = lambda i: i < nl_blocks
    ctx_blk = lambda i: jnp.maximum(i - nl_blocks, 0)
    cond_row = lambda i: jnp.where(is_lat(i), 1 + (i * tm) // t_len, 0)
    cos, sin = rope_tabs
    blocks_per_seq = t_len // tm
    cos = jnp.asarray(np.concatenate([np.ones((tm, HEAD_DIM), np.float32), cos], axis=0))
    sin = jnp.asarray(np.concatenate([np.zeros((tm, HEAD_DIM), np.float32), sin], axis=0))
    tab_blk = lambda i: jnp.where(is_lat(i), 1 + i % blocks_per_seq, 0)

    in_specs = [
        pl.BlockSpec((tm, D_MODEL), lambda i: (jnp.minimum(i, nl_blocks - 1), 0)),
        pl.BlockSpec((tm, D_MODEL), lambda i: (ctx_blk(i), 0)),
        pl.BlockSpec((None, 1, D_MODEL), lambda i: (cond_row(i), 0, 0)),
        pl.BlockSpec((None, 1, D_MODEL), lambda i: (cond_row(i), 0, 1)),
        pl.BlockSpec(memory_space=pl.ANY),
        pl.BlockSpec((tm, HEAD_DIM), lambda i: (tab_blk(i), 0)),
        pl.BlockSpec((tm, HEAD_DIM), lambda i: (tab_blk(i), 0)),
    ]
    args = [x_lat, x_ctx, mod3, mod3, w_in, cos, sin]
    aliases = {}
    seqs = tm // s_len
    cache_rows = N_KV_HEADS * s_len
    if prev_caches is None:
        cache_spec = pl.BlockSpec((seqs, DEPTH, cache_rows, HEAD_DIM), lambda i: (ctx_blk(i), 0, 0, 0))
    else:
        cache_spec = pl.BlockSpec((seqs, None, cache_rows, HEAD_DIM), lambda i: (ctx_blk(i), layer, 0, 0))
        aliases = {len(args): 1, len(args) + 1: 2}
        in_specs += [pl.BlockSpec(memory_space=pl.ANY)] * 2
        args += list(prev_caches)
    cache_shape = jax.ShapeDtypeStruct((bp, DEPTH, cache_rows, HEAD_DIM), _F32)
    lat_kv_spec = pl.BlockSpec((tm, D_KV), lambda i: (jnp.minimum(i, nl_blocks), 0))
    lat_kv_shape = jax.ShapeDtypeStruct((n_lat + tm, D_KV), _BF16)
    q, k_cache, v_cache, k_lat, v_lat, rest = pl.pallas_call(
        functools.partial(_proj_kernel, layer=layer, n_lat_blocks=nl_blocks),
        grid=(n // tm,),
        in_specs=in_specs,
        out_specs=[
            pl.BlockSpec((tm, D_ATTN), lambda i: (i, 0)),
            cache_spec, cache_spec, lat_kv_spec, lat_kv_spec,
            pl.BlockSpec((tm, D_REST), lambda i: (i, 0)),
        ],
        out_shape=[
            jax.ShapeDtypeStruct((n, D_ATTN), _BF16),
            cache_shape, cache_shape, lat_kv_shape, lat_kv_shape,
            jax.ShapeDtypeStruct((n, D_REST), _BF16),
        ],
        scratch_shapes=[
            pltpu.VMEM((D_MODEL, D_IN), _BF16),
            pltpu.VMEM((2, D_MODEL, PROJ_TN), _F32),
            pltpu.SemaphoreType.DMA((2,)),
        ],
        input_output_aliases=aliases,
        compiler_params=pltpu.CompilerParams(
            dimension_semantics=("arbitrary",), vmem_limit_bytes=PROJ_VMEM_LIMIT),
        name="proj",
    )(*args)
    return q, (k_cache, v_cache), (k_lat, v_lat), rest


def _softmax_pv_t(score_parts, sink_row, value_parts):
    m = sink_row
    for s in score_parts:
        m = jnp.maximum(m, jnp.max(s, axis=0, keepdims=True))
    acc = None
    for s, v_aug in zip(score_parts, value_parts):
        part = lax.dot_general(v_aug, jnp.exp2(s - m).astype(_BF16), (((0,), (0,)), ((), ())),
                               preferred_element_type=_F32)
        acc = part if acc is None else acc + part
    denom = acc[HEAD_DIM:HEAD_DIM + 1, :] + jnp.exp2(sink_row - m)
    return acc[:HEAD_DIM, :] / denom


def _softmax_pv(score_parts, sink_col, value_parts):
    m = sink_col
    for s in score_parts:
        m = jnp.maximum(m, jnp.max(s, axis=-1, keepdims=True))
    acc = None
    for s, v_aug in zip(score_parts, value_parts):
        part = jnp.dot(jnp.exp2(s - m).astype(_BF16), v_aug, preferred_element_type=_F32)
        acc = part if acc is None else acc + part
    denom = acc[:, HEAD_DIM:HEAD_DIM + 1] + jnp.exp2(sink_col - m)
    return acc[:, :HEAD_DIM] / denom


def _sink_column(sink_ref, kvh, rows):
    head = lax.broadcasted_iota(jnp.int32, (Q_GROUP * rows, 1), 0) // rows
    col = jnp.full((Q_GROUP * rows, 1), sink_ref[kvh * Q_GROUP], _F32)
    for g in range(1, Q_GROUP):
        col = jnp.where(head == g, sink_ref[kvh * Q_GROUP + g], col)
    return col * LOG2_E


def _with_ones(v_bf):
    return jnp.concatenate([v_bf, jnp.ones_like(v_bf)], axis=1)


def _sink_row(sink_ref, kvh, n_per_head):
    head = lax.broadcasted_iota(jnp.int32, (1, Q_GROUP * n_per_head), 1) // n_per_head
    row = jnp.full((1, Q_GROUP * n_per_head), sink_ref[kvh * Q_GROUP], _F32)
    for g in range(1, Q_GROUP):
        row = jnp.where(head == g, sink_ref[kvh * Q_GROUP + g], row)
    return row * LOG2_E


def _stack_heads(q_rows):
    return jnp.concatenate(
        [q_rows[:, g * HEAD_DIM:(g + 1) * HEAD_DIM] for g in range(Q_GROUP)], axis=0)


def _qk(q, k_bf):
    return lax.dot_general(q, k_bf, (((1,), (1,)), ((), ())), preferred_element_type=_F32)


def _ctx_attn_kernel(sink_ref, q_ref, k_ref, v_ref, o_ref, *, s_len):
    kvh = pl.program_id(1)
    sink_row = _sink_row(sink_ref, kvh, s_len)
    for bi in range(q_ref.shape[0] // s_len):
        rows = slice(bi * s_len, (bi + 1) * s_len)
        q = _stack_heads(q_ref[rows, :])
        k = k_ref[bi, _head_rows(kvh, s_len), :].astype(_BF16)
        v_aug = _with_ones(v_ref[bi, _head_rows(kvh, s_len), :].astype(_BF16))
        o_t = _softmax_pv_t([_qk(k, q)], sink_row, [v_aug])
        for g in range(Q_GROUP):
            o_ref[rows, g * HEAD_DIM:(g + 1) * HEAD_DIM] = (
                o_t[:, g * s_len:(g + 1) * s_len].T.astype(o_ref.dtype))


def _context_attention(q, q_row0, k_cache, v_cache, layer, sink, batch, s_len):
    n = batch * s_len
    tm = CTX_ATTN_SEQS * s_len
    gw = Q_GROUP * HEAD_DIM
    q_blk0 = q_row0 // tm
    kv_spec = pl.BlockSpec((CTX_ATTN_SEQS, None, N_KV_HEADS * s_len, HEAD_DIM), lambda b, h: (b, layer, 0, 0))
    return pl.pallas_call(
        functools.partial(_ctx_attn_kernel, s_len=s_len),
        grid=(n // tm, N_KV_HEADS),
        in_specs=[
            pl.BlockSpec(memory_space=pltpu.SMEM),
            pl.BlockSpec((tm, gw), lambda b, h: (q_blk0 + b, h)),
            kv_spec,
            kv_spec,
        ],
        out_specs=pl.BlockSpec((tm, gw), lambda b, h: (b, h)),
        out_shape=jax.ShapeDtypeStruct((n, D_ATTN), _BF16),
        compiler_params=pltpu.CompilerParams(dimension_semantics=("parallel", "parallel")),
        name="ctx_attention",
    )(sink, q, k_cache, v_cache)


def _lat_attn_kernel(sink_ref, q_ref, k_ref, v_ref, ck_ref, cv_ref, o_ref):
    kvh = pl.program_id(1)
    t_len = q_ref.shape[0]
    nb = t_len // Q_BLOCK
    past = ck_ref.shape[0] // N_KV_HEADS
    ck = ck_ref[_head_rows(kvh, past), :].astype(_BF16)
    cv_aug = _with_ones(cv_ref[_head_rows(kvh, past), :].astype(_BF16))
    v_aug = _with_ones(v_ref[...])
    sink_col = _sink_column(sink_ref, kvh, Q_BLOCK)
    rows = Q_GROUP * Q_BLOCK
    r = lax.broadcasted_iota(jnp.int32, (rows, Q_BLOCK), 0) % Q_BLOCK
    c = lax.broadcasted_iota(jnp.int32, (rows, Q_BLOCK), 1)
    for jb in range(nb):
        r0 = jb * Q_BLOCK
        q = _stack_heads(q_ref[r0:r0 + Q_BLOCK, :])
        lo = max(r0 - WINDOW, 0)
        hi = min(r0 + Q_BLOCK + WINDOW, t_len)
        s = _qk(q, k_ref[lo:hi, :])
        parts = []
        off = 0
        if jb > 0:
            parts.append(jnp.where(c >= r, s[:, :Q_BLOCK], NEG_INF))
            off = Q_BLOCK
        parts.append(s[:, off:off + Q_BLOCK])
        off += Q_BLOCK
        if jb < nb - 1:
            parts.append(jnp.where(c <= r, s[:, off:], NEG_INF))
        s_local = jnp.concatenate(parts, axis=1)
        o = _softmax_pv([s_local, _qk(q, ck)], sink_col, [v_aug[lo:hi], cv_aug])
        for g in range(Q_GROUP):
            o_ref[r0:r0 + Q_BLOCK, g * HEAD_DIM:(g + 1) * HEAD_DIM] = (
                o[g * Q_BLOCK:(g + 1) * Q_BLOCK].astype(o_ref.dtype))


def _latent_attention(q, q_row0, k, v, cache_k4, cache_v4, layer, sink, batch, t_len):
    n = batch * t_len
    gw = Q_GROUP * HEAD_DIM
    q_blk0 = q_row0 // t_len
    cache_spec = pl.BlockSpec((None, None, cache_k4.shape[2], HEAD_DIM), lambda b, h: (b, layer, 0, 0))
    return pl.pallas_call(
        _lat_attn_kernel,
        grid=(batch, N_KV_HEADS),
        in_specs=[
            pl.BlockSpec(memory_space=pltpu.SMEM),
            pl.BlockSpec((t_len, gw), lambda b, h: (q_blk0 + b, h)),
            pl.BlockSpec((t_len, HEAD_DIM), lambda b, h: (b, h)),
            pl.BlockSpec((t_len, HEAD_DIM), lambda b, h: (b, h)),
            cache_spec,
            cache_spec,
        ],
        out_specs=pl.BlockSpec((t_len, gw), lambda b, h: (b, h)),
        out_shape=jax.ShapeDtypeStruct((n, D_ATTN), _BF16),
        compiler_params=pltpu.CompilerParams(
            dimension_semantics=("parallel", "parallel"), vmem_limit_bytes=VMEM_LIMIT),
        name="latent_attention",
    )(sink, q, k, v, cache_k4, cache_v4)


def _shift_rows(x, d, t_len, offset=0):
    n = x.shape[0]
    row = (lax.broadcasted_iota(jnp.int32, x.shape, 0) + (n - offset)) % t_len
    rolled = pltpu.roll(x, d % n, axis=0)
    valid = (row >= d) if d > 0 else (row < t_len + d)
    return jnp.where(valid, rolled, 0.0)


def _sqrt_nonneg(z):
    return z * lax.rsqrt(jnp.maximum(z, F32_TINY))


def _rnn_kernel(x_ref, cw_ref, cb_ref, wg_ref, bg_ref, lam_ref, h0_ref, y_ref, hl_ref,
                af_scr, bf_scr, ab_scr, bb_scr, hf_scr, hb_scr, *, nb, nc):
    t_len = x_ref.shape[1]
    pitch = t_len + SLOT_PAD
    scr = ((af_scr, bf_scr), (ab_scr, bb_scr))
    per_group = SEQ_SLOTS // 2
    kb = min(per_group, max(1, PASS1_ROWS // t_len))

    def seq_of(slot):
        g, u = slot % 2, slot // 2
        return (g * per_group + u, 0) if nc == 1 else (u, g)

    for g in range(2):
        b_first, cl = seq_of(g)
        offset = SLOT_PAD * g
        lanes = slice(cl * LANES, (cl + 1) * LANES)
        cw = cw_ref[:, lanes]
        cb = cb_ref[:, lanes]
        wg = wg_ref[cl]
        bg = bg_ref[cl]
        lam = lam_ref[:, lanes]
        c_softplus = LRU_C * (jnp.maximum(-lam, 0.0) + jnp.log1p(jnp.exp(-jnp.abs(lam))))

        def per_chunk(it, carry, lanes=lanes, cw=cw, cb=cb, wg=wg, bg=bg,
                      c_softplus=c_softplus, g=g, b_first=b_first, offset=offset):
            n_rows = kb * t_len
            x = x_ref[pl.ds(b_first + it * kb, kb), :, lanes].astype(_F32).reshape(n_rows, LANES)
            if offset:
                x = pltpu.roll(x, offset, axis=0)
            xc = cw[0:1] * _shift_rows(x, 2, t_len, offset)
            xc = xc + cw[1:2] * _shift_rows(x, 1, t_len, offset)
            xc = xc + cw[2:3] * x
            xc = xc + cw[3:4] * _shift_rows(x, -1, t_len, offset)
            xc = xc + cb
            half_gates = jnp.dot(xc.astype(_BF16), wg, preferred_element_type=_F32) + bg
            half_xc = 0.5 * xc
            for d in range(2):
                t_r = jnp.tanh(half_gates[:, (2 * d) * LANES:(2 * d + 1) * LANES])
                t_i = jnp.tanh(half_gates[:, (2 * d + 1) * LANES:(2 * d + 2) * LANES])
                k = 0.5 * c_softplus[d:d + 1]
                neg_log_a = t_r * k + k
                a = jnp.exp2(neg_log_a * -LOG2_E)
                bterm = _sqrt_nonneg(jnp.tanh(neg_log_a) * (a * a + 1.0)) * ((t_i + 1.0) * half_xc)
                for jb in range(kb):
                    slot = 2 * (it * kb + jb) + g
                    row0 = pl.multiple_of(slot * pitch - offset, SUBLANES)
                    for dst, val in ((scr[d][0], a), (scr[d][1], bterm)):
                        dst[pl.ds(row0, t_len), :] = val[jb * t_len:(jb + 1) * t_len]
                        if offset:
                            w0 = ((jb + 1) * t_len) % n_rows
                            dst[pl.ds(row0 + t_len, SUBLANES), :] = val[w0:w0 + SUBLANES]
            return carry

        if per_group == kb:
            per_chunk(0, 0)
        else:
            lax.fori_loop(0, per_group // kb, per_chunk, 0)

    def slot_rows(ref3, d):
        rows = []
        for slot in range(SEQ_SLOTS):
            b, cl = seq_of(slot)
            rows.append(ref3[b, d:d + 1, cl * LANES:(cl + 1) * LANES])
        return jnp.concatenate(rows, axis=0)

    for slot in range(SEQ_SLOTS):
        tile = (slot * pitch + t_len) // SUBLANES * SUBLANES
        for h_scr in (hf_scr, hb_scr):
            h_scr[pl.ds(tile, SUBLANES), :] = jnp.zeros((SUBLANES, LANES), _F32)

    def advance2(a_scr, b_scr, h_scr, t0, t1, h):
        r0 = pl.ds(t0, SEQ_SLOTS, stride=pitch)
        r1 = pl.ds(t1, SEQ_SLOTS, stride=pitch)
        a0, b0, a1, b1 = a_scr[r0, :], b_scr[r0, :], a_scr[r1, :], b_scr[r1, :]
        h_scr[r0, :] = a0 * h + b0
        h = (a1 * a0) * h + (a1 * b0 + b1)
        h_scr[r1, :] = h
        return h

    hf, hb = slot_rows(h0_ref, 0), slot_rows(h0_ref, 1)
    for t in range(0, t_len, 2):
        hf = advance2(af_scr, bf_scr, hf_scr, t, t + 1, hf)
        hb = advance2(ab_scr, bb_scr, hb_scr, t_len - 1 - t, t_len - 2 - t, hb)

    for slot in range(SEQ_SLOTS):
        b, cl = seq_of(slot)
        lanes = slice(cl * LANES, (cl + 1) * LANES)
        hl_ref[b, 0:1, lanes] = hf[slot:slot + 1, :]
        hl_ref[b, 1:2, lanes] = hb[slot:slot + 1, :]
        if slot % 2 == 0:
            rows = pl.ds(slot * pitch, t_len)
            y = hf_scr[rows, :] + hb_scr[rows, :]
        else:
            rows = pl.ds(slot * pitch - SLOT_PAD, t_len + SUBLANES)
            y = pltpu.roll(hf_scr[rows, :] + hb_scr[rows, :], t_len + SUBLANES - SLOT_PAD, axis=0)[:t_len]
        y_ref[b, :, lanes] = y.astype(y_ref.dtype)


def _rnn(rest, row0, h0, conv_w, conv_b, wg, bg, lam, batch, t_len):
    nb = min(batch, SEQ_SLOTS)
    blk0 = row0 // (nb * t_len)
    nc = SEQ_SLOTS // nb
    assert nc in (1, 2) and batch % nb == 0, "a grid step holds 8 batches x 1 or 4 batches x 2 channel blocks"
    cw = nc * LANES
    xr_block0 = XR_COL0 // cw
    scratch = pltpu.VMEM((SEQ_SLOTS * (t_len + SLOT_PAD), LANES), _F32)
    y, h_last = pl.pallas_call(
        functools.partial(_rnn_kernel, nb=nb, nc=nc),
        grid=(batch // nb, D_RNN // cw),
        in_specs=[
            pl.BlockSpec((nb, t_len, cw), lambda g, c: (blk0 + g, 0, xr_block0 + c)),
            pl.BlockSpec((CONV_W, cw), lambda g, c: (0, c)),
            pl.BlockSpec((1, cw), lambda g, c: (0, c)),
            pl.BlockSpec((nc, RNN_BLOCK, 4 * LANES), lambda g, c: (c, 0, 0)),
            pl.BlockSpec((nc, 1, 4 * LANES), lambda g, c: (c, 0, 0)),
            pl.BlockSpec((2, cw), lambda g, c: (0, c)),
            pl.BlockSpec((nb, 2, cw), lambda g, c: (g, 0, c)),
        ],
        out_specs=[
            pl.BlockSpec((nb, t_len, cw), lambda g, c: (g, 0, c)),
            pl.BlockSpec((nb, 2, cw), lambda g, c: (g, 0, c)),
        ],
        out_shape=[
            jax.ShapeDtypeStruct((batch, t_len, D_RNN), _BF16),
            jax.ShapeDtypeStruct((batch, 2, D_RNN), _F32),
        ],
        scratch_shapes=[scratch] * 6,
        compiler_params=pltpu.CompilerParams(
            dimension_semantics=("parallel", "parallel"), vmem_limit_bytes=RNN_VMEM_LIMIT),
        name="rnn",
    )(rest.reshape(rest.shape[0] // t_len, t_len, D_REST), conv_w, conv_b.reshape(1, D_RNN), wg, bg, lam, h0)
    return y.reshape(batch * t_len, D_RNN), h_last


def _out_kernel(x_ref, gate_ref, oa_ref, or_ref, gates_ref, wa_ref, wr_ref, wo_ref, g_ref, b_ref, o_ref):
    ga_ref, gr_ref, mga_ref, mgr_ref = (
        gates_ref.at[:, p * D_MODEL:(p + 1) * D_MODEL] for p in range(4))
    def silu_bf(x):
        h = 0.5 * x
        return h * jnp.tanh(h) + h

    def logistic_f32(x):
        return (0.5 * jnp.tanh(0.5 * x) + 0.5).astype(_F32)

    ya = jnp.dot(oa_ref[...] * silu_bf(ga_ref[...]), wa_ref[...], preferred_element_type=_F32)
    yr = jnp.dot(or_ref[...] * silu_bf(gr_ref[...]), wr_ref[...], preferred_element_type=_F32)
    merged = logistic_f32(mga_ref[...]) * ya + logistic_f32(mgr_ref[...]) * yr
    out = jnp.dot(merged.astype(_BF16), wo_ref[...], preferred_element_type=_F32)
    z = DEEPNORM_ALPHA * x_ref[...] + gate_ref[...] * out
    o_ref[...] = _layer_norm_rows(z) * g_ref[...] + b_ref[...]


def _merge_residual(x2, mod3, o_attn, o_rnn, rest, rest_row0, wa_bf, wr_bf, wo_bf, layer, ln_g, ln_b,
                    cond_row_of_block, tm):
    n = x2.shape[0]
    row_blk = pl.BlockSpec((tm, D_MODEL), lambda i: (i, 0))
    gates_blk = pl.BlockSpec((tm, D_GATES), lambda i: (rest_row0 // tm + i, 0))
    whole = pl.BlockSpec((None, D_MODEL, D_MODEL), lambda i: (layer, 0, 0))
    vec = pl.BlockSpec((1, D_MODEL), lambda i: (0, 0))
    return pl.pallas_call(
        _out_kernel,
        grid=(n // tm,),
        in_specs=[
            row_blk,
            pl.BlockSpec((None, 1, D_MODEL), lambda i: (cond_row_of_block(i), 0, 2)),
            row_blk, row_blk, gates_blk,
            whole, whole, whole, vec, vec,
        ],
        out_specs=row_blk,
        out_shape=jax.ShapeDtypeStruct((n, D_MODEL), _F32),
        compiler_params=pltpu.CompilerParams(
            dimension_semantics=("parallel",), vmem_limit_bytes=VMEM_LIMIT),
        name="merge_residual",
    )(x2, mod3, o_attn, o_rnn, rest, wa_bf, wr_bf, wo_bf,
      ln_g.reshape(1, D_MODEL), ln_b.reshape(1, D_MODEL))


def _rope_tables(t_len):
    quarter = HEAD_DIM // 4
    inv_freq = ROPE_BASE ** (-np.arange(quarter, dtype=np.float64) / quarter)
    pos = np.arange(t_len)
    ang_r = (pos // GRID_W).astype(np.float64)[:, None] * inv_freq[None, :]
    ang_c = (pos % GRID_W).astype(np.float64)[:, None] * inv_freq[None, :]
    cos = np.concatenate([np.cos(ang_r)] * 2 + [np.cos(ang_c)] * 2, axis=1)
    sin = np.concatenate([-np.sin(ang_r), np.sin(ang_r), -np.sin(ang_c), np.sin(ang_c)], axis=1)
    return cos.astype(np.float32), sin.astype(np.float32)


def _gate_weights(rg_wa, rg_ba, rg_wx, rg_bx):
    wg = (0.5 * jnp.concatenate([rg_wa[0], rg_wx[0], rg_wa[1], rg_wx[1]], axis=-1)).astype(_BF16)
    blk = lambda v: v.reshape(N_RNN_BLOCKS, 1, RNN_BLOCK)
    bg = 0.5 * jnp.concatenate([blk(rg_ba[0]), blk(rg_bx[0]), blk(rg_ba[1]), blk(rg_bx[1])], axis=-1)
    return wg, bg


def kernel(x_prompt, x_sample, cache_k, cache_v, state_h, c, c_ctx, w_mod, b_mod, w_in,
           attn_sink, conv_w, conv_b, rg_wa, rg_ba, rg_wx, rg_bx, rg_lambda,
           w_br_attn, w_br_rnn, w_out, ln_g, ln_b):
    bp, s_len, _ = x_prompt.shape
    bd, t_len, _ = x_sample.shape
    past = cache_k.shape[2]

    cond8 = jnp.zeros((COND_ROWS, D_MODEL), _F32).at[0].set(c_ctx).at[1:1 + bd].set(c)
    mod = _modulation(cond8, w_mod, b_mod)

    wa_bf = w_br_attn.astype(_BF16)
    wr_bf = w_br_rnn.astype(_BF16)
    wo_bf = w_out.astype(_BF16)
    rope_tabs = _rope_tables(t_len)
    cache_k4 = cache_k.reshape(bd, DEPTH, past * N_KV_HEADS, HEAD_DIM)
    cache_v4 = cache_v.reshape(bd, DEPTH, past * N_KV_HEADS, HEAD_DIM)

    tm_proj, tm_out = 512, 512
    ctx_row = lambda i: 0
    lat_row_out = lambda i: 1 + (i * tm_out) // t_len
    n_lat = bd * t_len

    xp = x_prompt.reshape(bp * s_len, D_MODEL)
    xs = x_sample.reshape(n_lat, D_MODEL)
    h0_zero = jnp.zeros((bp, 2, D_RNN), _F32)
    hs = []
    kv_caches = None
    for l in range(DEPTH):
        mod3 = mod[l].reshape(COND_ROWS, 1, 3 * D_MODEL)
        wg, bg = _gate_weights(rg_wa[l], rg_ba[l], rg_wx[l], rg_bx[l])
        q, kv_caches, (k_lat, v_lat), rest = _project(
            xs, xp, mod3, w_in, l, tm_proj, rope_tabs, (bd, t_len), (bp, s_len), kv_caches)

        o_attn = _context_attention(q, n_lat, kv_caches[0], kv_caches[1], l, attn_sink[l], bp, s_len)
        o_rnn, h_fin = _rnn(rest, n_lat, h0_zero, conv_w[l], conv_b[l], wg, bg, rg_lambda[l], bp, s_len)
        xp = _merge_residual(xp, mod3, o_attn, o_rnn, rest, n_lat, wa_bf, wr_bf, wo_bf, l,
                             ln_g[l], ln_b[l], ctx_row, tm_out)
        hs.append(h_fin)

        o_attn = _latent_attention(q, 0, k_lat, v_lat, cache_k4, cache_v4, l, attn_sink[l], bd, t_len)
        o_rnn, _ = _rnn(rest, 0, state_h[:, l], conv_w[l], conv_b[l], wg, bg, rg_lambda[l], bd, t_len)
        xs = _merge_residual(xs, mod3, o_attn, o_rnn, rest, 0, wa_bf, wr_bf, wo_bf, l,
                             ln_g[l], ln_b[l], lat_row_out, tm_out)

    y_prompt = xp.reshape(bp, s_len, D_MODEL)
    y_sample = xs.reshape(bd, t_len, D_MODEL)
    cache_shape = (bp, DEPTH, s_len, N_KV_HEADS, HEAD_DIM)
    new_k, new_v = (cache.reshape(cache_shape) for cache in kv_caches)
    return (y_prompt, y_sample, new_k, new_v, jnp.stack(hs, axis=1))
```

```python
import functools

import jax
import jax.numpy as jnp
import numpy as np
from jax import lax
from jax.experimental import pallas as pl
from jax.experimental.pallas import tpu as pltpu

D_MODEL = 1024
DEPTH = 2
GRID_W = 64
N_HEADS = 8
N_KV_HEADS = 2
HEAD_DIM = 128
Q_GROUP = N_HEADS // N_KV_HEADS
D_ATTN = N_HEADS * HEAD_DIM
D_KV = N_KV_HEADS * HEAD_DIM
WINDOW = 128
Q_BLOCK = 128
D_RNN = D_MODEL
N_RNN_BLOCKS = 8
RNN_BLOCK = D_RNN // N_RNN_BLOCKS
CONV_W = 4
CONV_LEFT = 2
LRU_C = 8.0
ROPE_BASE = 10000.0
D_IN = 2 * D_ATTN + 2 * D_KV + 2 * D_RNN + 2 * D_MODEL
DEEPNORM_ALPHA = (2 * DEPTH) ** 0.25
LN_EPS = 1e-6
NEG_INF = -1e30
ATTN_SCALE = HEAD_DIM ** -0.5

LANES = 128
SUBLANES = 8
SEQ_SLOTS = SUBLANES
SLOT_PAD = 4
CTX_ATTN_SEQS = 8
PASS1_ROWS = 1024
COND_ROWS = 8
PROJ_TN = 512
N_PROJ_BLOCKS = D_IN // PROJ_TN
D_REST = D_IN - D_ATTN - 2 * D_KV
_REST_COL = (0, 4, 1, 2, 3)
D_GATES = 4 * D_MODEL
XR_COL0 = 4 * D_MODEL
VMEM_LIMIT = 48 * 1024 * 1024
PROJ_VMEM_LIMIT = 56 * 1024 * 1024
RNN_VMEM_LIMIT = 56 * 1024 * 1024
F32_TINY = 1.1754943508222875e-38
LOG2_E = 1.4426950408889634
Q_PRESCALE = ATTN_SCALE * LOG2_E

_BF16 = jnp.bfloat16
_F32 = jnp.float32


def _silu(x):
    return x * jax.nn.sigmoid(x)


def _layer_norm_rows(x):
    mu = jnp.mean(x, axis=-1, keepdims=True)
    xc = x - mu
    var = jnp.mean(xc * xc, axis=-1, keepdims=True)
    return xc * lax.rsqrt(var + LN_EPS)


def _head_rows(head, n_rows):
    return pl.ds(head, n_rows, stride=N_KV_HEADS)


def _mod_kernel(cond_ref, w_ref, b_ref, o_ref):
    a = _silu(cond_ref[...]).astype(_BF16)
    o_ref[...] = jnp.dot(a, w_ref[...].astype(_BF16), preferred_element_type=_F32) + b_ref[...]


def _modulation(cond8, w_mod, b_mod):
    tn = D_MODEL
    return pl.pallas_call(
        _mod_kernel,
        grid=(DEPTH, 3 * D_MODEL // tn),
        in_specs=[
            pl.BlockSpec((COND_ROWS, D_MODEL), lambda l, j: (0, 0)),
            pl.BlockSpec((None, D_MODEL, tn), lambda l, j: (l, 0, j)),
            pl.BlockSpec((None, 1, tn), lambda l, j: (l, 0, j)),
        ],
        out_specs=pl.BlockSpec((None, COND_ROWS, tn), lambda l, j: (l, 0, j)),
        out_shape=jax.ShapeDtypeStruct((DEPTH, COND_ROWS, 3 * D_MODEL), _F32),
        compiler_params=pltpu.CompilerParams(dimension_semantics=("parallel", "parallel")),
        name="modulation",
    )(cond8, w_mod, b_mod.reshape(DEPTH, 1, 3 * D_MODEL))


def _rope(x, cos, sin_signed):
    quarter = HEAD_DIM // 4
    lane = lax.broadcasted_iota(jnp.int32, x.shape, 1)
    partner = jnp.where((lane // quarter) % 2 == 0,
                        pltpu.roll(x, HEAD_DIM - quarter, axis=1),
                        pltpu.roll(x, quarter, axis=1))
    return x * cos + partner * sin_signed


def _proj_kernel(xl_ref, xc_ref, shift_ref, scale_ref, w_hbm, cos_ref, sin_ref, *rest,
                 layer, n_lat_blocks):
    q_ref, kc_ref, vc_ref, kl_ref, vl_ref, r_ref, w_scr, stage_scr, sem = rest[-9:]
    i = pl.program_id(0)

    def chunk_copy(j):
        return pltpu.make_async_copy(
            w_hbm.at[layer, :, pl.ds(j * PROJ_TN, PROJ_TN)], stage_scr.at[j % 2], sem.at[j % 2])

    def fetch_chunk(j):
        if j == 0:
            chunk_copy(0).start()
        if j + 1 < N_PROJ_BLOCKS:
            chunk_copy(j + 1).start()
        chunk_copy(j).wait()
        w_scr[:, j * PROJ_TN:(j + 1) * PROJ_TN] = stage_scr[j % 2].astype(_BF16)

    x = jnp.where(i < n_lat_blocks, xl_ref[...], xc_ref[...])
    y = _layer_norm_rows(x)
    h = (y * (1.0 + scale_ref[...]) + shift_ref[...]).astype(_BF16)

    def rope(blk):
        return _rope(blk, cos_ref[...], sin_ref[...])

    def put_cache(ref, head, val):
        s_len = ref.shape[-2] // N_KV_HEADS
        for bi in range(ref.shape[0]):
            rows = val[bi * s_len:(bi + 1) * s_len].astype(ref.dtype)
            if len(ref.shape) == 3:
                ref[bi, _head_rows(head, s_len), :] = rows
            else:
                for l in range(ref.shape[1]):
                    ref[bi, l, _head_rows(head, s_len), :] = rows if l == layer else jnp.zeros_like(rows)

    def project_block(stream_weights):
        for j in range(N_PROJ_BLOCKS):
            c0 = j * PROJ_TN
            if stream_weights:
                fetch_chunk(j)
            acc = jnp.dot(h, w_scr[:, c0:c0 + PROJ_TN], preferred_element_type=_F32)
            if c0 < D_ATTN:
                for hh in range(PROJ_TN // HEAD_DIM):
                    sl = slice(hh * HEAD_DIM, (hh + 1) * HEAD_DIM)
                    q_ref[:, c0 + hh * HEAD_DIM:c0 + (hh + 1) * HEAD_DIM] = (
                        (rope(acc[:, sl]) * Q_PRESCALE).astype(q_ref.dtype))
            elif c0 == D_ATTN:
                for hh in range(N_KV_HEADS):
                    sl = slice(hh * HEAD_DIM, (hh + 1) * HEAD_DIM)
                    k_rot = rope(acc[:, sl])
                    put_cache(kc_ref, hh, k_rot)
                    put_cache(vc_ref, hh, acc[:, D_KV + hh * HEAD_DIM:D_KV + (hh + 1) * HEAD_DIM])
                    kl_ref[:, sl] = k_rot.astype(kl_ref.dtype)
                vl_ref[...] = acc[:, D_KV:].astype(vl_ref.dtype)
            else:
                off = c0 - D_ATTN - 2 * D_KV
                r0 = _REST_COL[off // D_MODEL] * D_MODEL + off % D_MODEL
                r_ref[:, r0:r0 + PROJ_TN] = acc.astype(r_ref.dtype)

    pl.when(i == 0)(functools.partial(project_block, True))
    pl.when(i != 0)(functools.partial(project_block, False))


def _project(x_lat, x_ctx, mod3, w_in, layer, tm, rope_tabs, lat_dims, ctx_dims, prev_caches):
    (bd, t_len), (bp, s_len) = lat_dims, ctx_dims
    n_lat, n_ctx = bd * t_len, bp * s_len
    n = n_lat + n_ctx
    nl_blocks = n_lat // tm
    is_lat = lambda i: i < nl_blocks
    ctx_blk = lambda i: jnp.maximum(i - nl_blocks, 0)
    cond_row = lambda i: jnp.where(is_lat(i), 1 + (i * tm) // t_len, 0)
    cos, sin = rope_tabs
    blocks_per_seq = t_len // tm
    cos = jnp.asarray(np.concatenate([np.ones((tm, HEAD_DIM), np.float32), cos], axis=0))
    sin = jnp.asarray(np.concatenate([np.zeros((tm, HEAD_DIM), np.float32), sin], axis=0))
    tab_blk = lambda i: jnp.where(is_lat(i), 1 + i % blocks_per_seq, 0)

    in_specs = [
        pl.BlockSpec((tm, D_MODEL), lambda i: (jnp.minimum(i, nl_blocks - 1), 0)),
        pl.BlockSpec((tm, D_MODEL), lambda i: (ctx_blk(i), 0)),
        pl.BlockSpec((None, 1, D_MODEL), lambda i: (cond_row(i), 0, 0)),
        pl.BlockSpec((None, 1, D_MODEL), lambda i: (cond_row(i), 0, 1)),
        pl.BlockSpec(memory_space=pl.ANY),
        pl.BlockSpec((tm, HEAD_DIM), lambda i: (tab_blk(i), 0)),
        pl.BlockSpec((tm, HEAD_DIM), lambda i: (tab_blk(i), 0)),
    ]
    args = [x_lat, x_ctx, mod3, mod3, w_in, cos, sin]
    aliases = {}
    seqs = tm // s_len
    cache_rows = N_KV_HEADS * s_len
    if prev_caches is None:
        cache_spec = pl.BlockSpec((seqs, DEPTH, cache_rows, HEAD_DIM), lambda i: (ctx_blk(i), 0, 0, 0))
    else:
        cache_spec = pl.BlockSpec((seqs, None, cache_rows, HEAD_DIM), lambda i: (ctx_blk(i), layer, 0, 0))
        aliases = {len(args): 1, len(args) + 1: 2}
        in_specs += [pl.BlockSpec(memory_space=pl.ANY)] * 2
        args += list(prev_caches)
    cache_shape = jax.ShapeDtypeStruct((bp, DEPTH, cache_rows, HEAD_DIM), _F32)
    lat_kv_spec = pl.BlockSpec((tm, D_KV), lambda i: (jnp.minimum(i, nl_blocks), 0))
    lat_kv_shape = jax.ShapeDtypeStruct((n_lat + tm, D_KV), _BF16)
    q, k_cache, v_cache, k_lat, v_lat, rest = pl.pallas_call(
        functools.partial(_proj_kernel, layer=layer, n_lat_blocks=nl_blocks),
        grid=(n // tm,),
        in_specs=in_specs,
        out_specs=[
            pl.BlockSpec((tm, D_ATTN), lambda i: (i, 0)),
            cache_spec, cache_spec, lat_kv_spec, lat_kv_spec,
            pl.BlockSpec((tm, D_REST), lambda i: (i, 0)),
        ],
        out_shape=[
            jax.ShapeDtypeStruct((n, D_ATTN), _BF16),
            cache_shape, cache_shape, lat_kv_shape, lat_kv_shape,
            jax.ShapeDtypeStruct((n, D_REST), _BF16),
        ],
        scratch_shapes=[
            pltpu.VMEM((D_MODEL, D_IN), _BF16),
            pltpu.VMEM((2, D_MODEL, PROJ_TN), _F32),
            pltpu.SemaphoreType.DMA((2,)),
        ],
        input_output_aliases=aliases,
        compiler_params=pltpu.CompilerParams(
            dimension_semantics=("arbitrary",), vmem_limit_bytes=PROJ_VMEM_LIMIT),
        name="proj",
    )(*args)
    return q, (k_cache, v_cache), (k_lat, v_lat), rest


def _softmax_pv_t(score_parts, sink_row, value_parts):
    m = sink_row
    for s in score_parts:
        m = jnp.maximum(m, jnp.max(s, axis=0, keepdims=True))
    acc = None
    for s, v_aug in zip(score_parts, value_parts):
        part = lax.dot_general(v_aug, jnp.exp2(s - m).astype(_BF16), (((0,), (0,)), ((), ())),
                               preferred_element_type=_F32)
        acc = part if acc is None else acc + part
    denom = acc[HEAD_DIM:HEAD_DIM + 1, :] + jnp.exp2(sink_row - m)
    return acc[:HEAD_DIM, :] / denom


def _softmax_pv(score_parts, sink_col, value_parts):
    m = sink_col
    for s in score_parts:
        m = jnp.maximum(m, jnp.max(s, axis=-1, keepdims=True))
    acc = None
    for s, v_aug in zip(score_parts, value_parts):
        part = jnp.dot(jnp.exp2(s - m).astype(_BF16), v_aug, preferred_element_type=_F32)
        acc = part if acc is None else acc + part
    denom = acc[:, HEAD_DIM:HEAD_DIM + 1] + jnp.exp2(sink_col - m)
    return acc[:, :HEAD_DIM] / denom


def _sink_column(sink_ref, kvh, rows):
    head = lax.broadcasted_iota(jnp.int32, (Q_GROUP * rows, 1), 0) // rows
    col = jnp.full((Q_GROUP * rows, 1), sink_ref[kvh * Q_GROUP], _F32)
    for g in range(1, Q_GROUP):
        col = jnp.where(head == g, sink_ref[kvh * Q_GROUP + g], col)
    return col * LOG2_E


def _with_ones(v_bf):
    return jnp.concatenate([v_bf, jnp.ones_like(v_bf)], axis=1)


def _sink_row(sink_ref, kvh, n_per_head):
    head = lax.broadcasted_iota(jnp.int32, (1, Q_GROUP * n_per_head), 1) // n_per_head
    row = jnp.full((1, Q_GROUP * n_per_head), sink_ref[kvh * Q_GROUP], _F32)
    for g in range(1, Q_GROUP):
        row = jnp.where(head == g, sink_ref[kvh * Q_GROUP + g], row)
    return row * LOG2_E


def _stack_heads(q_rows):
    return jnp.concatenate(
        [q_rows[:, g * HEAD_DIM:(g + 1) * HEAD_DIM] for g in range(Q_GROUP)], axis=0)


def _qk(q, k_bf):
    return lax.dot_general(q, k_bf, (((1,), (1,)), ((), ())), preferred_element_type=_F32)


def _ctx_attn_kernel(sink_ref, q_ref, k_ref, v_ref, o_ref, *, s_len):
    kvh = pl.program_id(1)
    sink_row = _sink_row(sink_ref, kvh, s_len)
    for bi in range(q_ref.shape[0] // s_len):
        rows = slice(bi * s_len, (bi + 1) * s_len)
        q = _stack_heads(q_ref[rows, :])
        k = k_ref[bi, _head_rows(kvh, s_len), :].astype(_BF16)
        v_aug = _with_ones(v_ref[bi, _head_rows(kvh, s_len), :].astype(_BF16))
        o_t = _softmax_pv_t([_qk(k, q)], sink_row, [v_aug])
        for g in range(Q_GROUP):
            o_ref[rows, g * HEAD_DIM:(g + 1) * HEAD_DIM] = (
                o_t[:, g * s_len:(g + 1) * s_len].T.astype(o_ref.dtype))


def _context_attention(q, q_row0, k_cache, v_cache, layer, sink, batch, s_len):
    n = batch * s_len
    tm = CTX_ATTN_SEQS * s_len
    gw = Q_GROUP * HEAD_DIM
    q_blk0 = q_row0 // tm
    kv_spec = pl.BlockSpec((CTX_ATTN_SEQS, None, N_KV_HEADS * s_len, HEAD_DIM), lambda b, h: (b, layer, 0, 0))
    return pl.pallas_call(
        functools.partial(_ctx_attn_kernel, s_len=s_len),
        grid=(n // tm, N_KV_HEADS),
        in_specs=[
            pl.BlockSpec(memory_space=pltpu.SMEM),
            pl.BlockSpec((tm, gw), lambda b, h: (q_blk0 + b, h)),
            kv_spec,
            kv_spec,
        ],
        out_specs=pl.BlockSpec((tm, gw), lambda b, h: (b, h)),
        out_shape=jax.ShapeDtypeStruct((n, D_ATTN), _BF16),
        compiler_params=pltpu.CompilerParams(dimension_semantics=("parallel", "parallel")),
        name="ctx_attention",
    )(sink, q, k_cache, v_cache)


def _lat_attn_kernel(sink_ref, q_ref, k_ref, v_ref, ck_ref, cv_ref, o_ref):
    kvh = pl.program_id(1)
    t_len = q_ref.shape[0]
    nb = t_len // Q_BLOCK
    past = ck_ref.shape[0] // N_KV_HEADS
    ck = ck_ref[_head_rows(kvh, past), :].astype(_BF16)
    cv_aug = _with_ones(cv_ref[_head_rows(kvh, past), :].astype(_BF16))
    v_aug = _with_ones(v_ref[...])
    sink_col = _sink_column(sink_ref, kvh, Q_BLOCK)
    rows = Q_GROUP * Q_BLOCK
    r = lax.broadcasted_iota(jnp.int32, (rows, Q_BLOCK), 0) % Q_BLOCK
    c = lax.broadcasted_iota(jnp.int32, (rows, Q_BLOCK), 1)
    for jb in range(nb):
        r0 = jb * Q_BLOCK
        q = _stack_heads(q_ref[r0:r0 + Q_BLOCK, :])
        lo = max(r0 - WINDOW, 0)
        hi = min(r0 + Q_BLOCK + WINDOW, t_len)
        s = _qk(q, k_ref[lo:hi, :])
        parts = []
        off = 0
        if jb > 0:
            parts.append(jnp.where(c >= r, s[:, :Q_BLOCK], NEG_INF))
            off = Q_BLOCK
        parts.append(s[:, off:off + Q_BLOCK])
        off += Q_BLOCK
        if jb < nb - 1:
            parts.append(jnp.where(c <= r, s[:, off:], NEG_INF))
        s_local = jnp.concatenate(parts, axis=1)
        o = _softmax_pv([s_local, _qk(q, ck)], sink_col, [v_aug[lo:hi], cv_aug])
        for g in range(Q_GROUP):
            o_ref[r0:r0 + Q_BLOCK, g * HEAD_DIM:(g + 1) * HEAD_DIM] = (
                o[g * Q_BLOCK:(g + 1) * Q_BLOCK].astype(o_ref.dtype))


def _latent_attention(q, q_row0, k, v, cache_k4, cache_v4, layer, sink, batch, t_len):
    n = batch * t_len
    gw = Q_GROUP * HEAD_DIM
    q_blk0 = q_row0 // t_len
    cache_spec = pl.BlockSpec((None, None, cache_k4.shape[2], HEAD_DIM), lambda b, h: (b, layer, 0, 0))
    return pl.pallas_call(
        _lat_attn_kernel,
        grid=(batch, N_KV_HEADS),
        in_specs=[
            pl.BlockSpec(memory_space=pltpu.SMEM),
            pl.BlockSpec((t_len, gw), lambda b, h: (q_blk0 + b, h)),
            pl.BlockSpec((t_len, HEAD_DIM), lambda b, h: (b, h)),
            pl.BlockSpec((t_len, HEAD_DIM), lambda b, h: (b, h)),
            cache_spec,
            cache_spec,
        ],
        out_specs=pl.BlockSpec((t_len, gw), lambda b, h: (b, h)),
        out_shape=jax.ShapeDtypeStruct((n, D_ATTN), _BF16),
        compiler_params=pltpu.CompilerParams(
            dimension_semantics=("parallel", "parallel"), vmem_limit_bytes=VMEM_LIMIT),
        name="latent_attention",
    )(sink, q, k, v, cache_k4, cache_v4)


def _shift_rows(x, d, t_len, offset=0):
    n = x.shape[0]
    row = (lax.broadcasted_iota(jnp.int32, x.shape, 0) + (n - offset)) % t_len
    rolled = pltpu.roll(x, d % n, axis=0)
    valid = (row >= d) if d > 0 else (row < t_len + d)
    return jnp.where(valid, rolled, 0.0)


def _sqrt_nonneg(z):
    return z * lax.rsqrt(jnp.maximum(z, F32_TINY))


def _rnn_kernel(x_ref, cw_ref, cb_ref, wg_ref, bg_ref, lam_ref, h0_ref, y_ref, hl_ref,
                af_scr, bf_scr, ab_scr, bb_scr, hf_scr, hb_scr, *, nb, nc):
    t_len = x_ref.shape[1]
    pitch = t_len + SLOT_PAD
    scr = ((af_scr, bf_scr), (ab_scr, bb_scr))
    per_group = SEQ_SLOTS // 2
    kb = min(per_group, max(1, PASS1_ROWS // t_len))

    def seq_of(slot):
        g, u = slot % 2, slot // 2
        return (g * per_group + u, 0) if nc == 1 else (u, g)

    for g in range(2):
        b_first, cl = seq_of(g)
        offset = SLOT_PAD * g
        lanes = slice(cl * LANES, (cl + 1) * LANES)
        cw = cw_ref[:, lanes]
        cb = cb_ref[:, lanes]
        wg = wg_ref[cl]
        bg = bg_ref[cl]
        lam = lam_ref[:, lanes]
        c_softplus = LRU_C * (jnp.maximum(-lam, 0.0) + jnp.log1p(jnp.exp(-jnp.abs(lam))))

        def per_chunk(it, carry, lanes=lanes, cw=cw, cb=cb, wg=wg, bg=bg,
                      c_softplus=c_softplus, g=g, b_first=b_first, offset=offset):
            n_rows = kb * t_len
            x = x_ref[pl.ds(b_first + it * kb, kb), :, lanes].astype(_F32).reshape(n_rows, LANES)
            if offset:
                x = pltpu.roll(x, offset, axis=0)
            xc = None
            for j in range(CONV_W):
                d = CONV_LEFT - j
                tap = cw[j:j + 1] * (_shift_rows(x, d, t_len, offset) if d else x)
                xc = tap if xc is None else xc + tap
            xc = xc + cb
            half_gates = jnp.dot(xc.astype(_BF16), wg, preferred_element_type=_F32) + bg
            half_xc = 0.5 * xc
            for d in range(2):
                t_r = jnp.tanh(half_gates[:, (2 * d) * LANES:(2 * d + 1) * LANES])
                t_i = jnp.tanh(half_gates[:, (2 * d + 1) * LANES:(2 * d + 2) * LANES])
                k = 0.5 * c_softplus[d:d + 1]
                neg_log_a = t_r * k + k
                a = jnp.exp2(neg_log_a * -LOG2_E)
                bterm = _sqrt_nonneg(jnp.tanh(neg_log_a) * (a * a + 1.0)) * ((t_i + 1.0) * half_xc)
                for jb in range(kb):
                    slot = 2 * (it * kb + jb) + g
                    row0 = pl.multiple_of(slot * pitch - offset, SUBLANES)
                    for dst, val in ((scr[d][0], a), (scr[d][1], bterm)):
                        dst[pl.ds(row0, t_len), :] = val[jb * t_len:(jb + 1) * t_len]
                        if offset:
                            w0 = ((jb + 1) * t_len) % n_rows
                            dst[pl.ds(row0 + t_len, SUBLANES), :] = val[w0:w0 + SUBLANES]
            return carry

        if per_group == kb:
            per_chunk(0, 0)
        else:
            lax.fori_loop(0, per_group // kb, per_chunk, 0)

    def slot_rows(ref3, d):
        rows = []
        for slot in range(SEQ_SLOTS):
            b, cl = seq_of(slot)
            rows.append(ref3[b, d:d + 1, cl * LANES:(cl + 1) * LANES])
        return jnp.concatenate(rows, axis=0)

    for slot in range(SEQ_SLOTS):
        tile = (slot * pitch + t_len) // SUBLANES * SUBLANES
        for h_scr in (hf_scr, hb_scr):
            h_scr[pl.ds(tile, SUBLANES), :] = jnp.zeros((SUBLANES, LANES), _F32)

    def advance2(a_scr, b_scr, h_scr, t0, t1, h):
        r0 = pl.ds(t0, SEQ_SLOTS, stride=pitch)
        r1 = pl.ds(t1, SEQ_SLOTS, stride=pitch)
        a0, b0, a1, b1 = a_scr[r0, :], b_scr[r0, :], a_scr[r1, :], b_scr[r1, :]
        h_scr[r0, :] = a0 * h + b0
        h = (a1 * a0) * h + (a1 * b0 + b1)
        h_scr[r1, :] = h
        return h

    hf, hb = slot_rows(h0_ref, 0), slot_rows(h0_ref, 1)
    for t in range(0, t_len, 2):
        hf = advance2(af_scr, bf_scr, hf_scr, t, t + 1, hf)
        hb = advance2(ab_scr, bb_scr, hb_scr, t_len - 1 - t, t_len - 2 - t, hb)

    for slot in range(SEQ_SLOTS):
        b, cl = seq_of(slot)
        lanes = slice(cl * LANES, (cl + 1) * LANES)
        hl_ref[b, 0:1, lanes] = hf[slot:slot + 1, :]
        hl_ref[b, 1:2, lanes] = hb[slot:slot + 1, :]
        if slot % 2 == 0:
            rows = pl.ds(slot * pitch, t_len)
            y = hf_scr[rows, :] + hb_scr[rows, :]
        else:
            rows = pl.ds(slot * pitch - SLOT_PAD, t_len + SUBLANES)
            y = pltpu.roll(hf_scr[rows, :] + hb_scr[rows, :], t_len + SUBLANES - SLOT_PAD, axis=0)[:t_len]
        y_ref[b, :, lanes] = y.astype(y_ref.dtype)


def _rnn(rest, row0, h0, conv_w, conv_b, wg, bg, lam, batch, t_len):
    nb = min(batch, SEQ_SLOTS)
    blk0 = row0 // (nb * t_len)
    nc = SEQ_SLOTS // nb
    assert nc in (1, 2) and batch % nb == 0, "a grid step holds 8 batches x 1 or 4 batches x 2 channel blocks"
    cw = nc * LANES
    xr_block0 = XR_COL0 // cw
    scratch = pltpu.VMEM((SEQ_SLOTS * (t_len + SLOT_PAD), LANES), _F32)
    y, h_last = pl.pallas_call(
        functools.partial(_rnn_kernel, nb=nb, nc=nc),
        grid=(batch // nb, D_RNN // cw),
        in_specs=[
            pl.BlockSpec((nb, t_len, cw), lambda g, c: (blk0 + g, 0, xr_block0 + c)),
            pl.BlockSpec((CONV_W, cw), lambda g, c: (0, c)),
            pl.BlockSpec((1, cw), lambda g, c: (0, c)),
            pl.BlockSpec((nc, RNN_BLOCK, 4 * LANES), lambda g, c: (c, 0, 0)),
            pl.BlockSpec((nc, 1, 4 * LANES), lambda g, c: (c, 0, 0)),
            pl.BlockSpec((2, cw), lambda g, c: (0, c)),
            pl.BlockSpec((nb, 2, cw), lambda g, c: (g, 0, c)),
        ],
        out_specs=[
            pl.BlockSpec((nb, t_len, cw), lambda g, c: (g, 0, c)),
            pl.BlockSpec((nb, 2, cw), lambda g, c: (g, 0, c)),
        ],
        out_shape=[
            jax.ShapeDtypeStruct((batch, t_len, D_RNN), _BF16),
            jax.ShapeDtypeStruct((batch, 2, D_RNN), _F32),
        ],
        scratch_shapes=[scratch] * 6,
        compiler_params=pltpu.CompilerParams(
            dimension_semantics=("parallel", "parallel"), vmem_limit_bytes=RNN_VMEM_LIMIT),
        name="rnn",
    )(rest.reshape(rest.shape[0] // t_len, t_len, D_REST), conv_w, conv_b.reshape(1, D_RNN), wg, bg, lam, h0)
    return y.reshape(batch * t_len, D_RNN), h_last


def _out_kernel(x_ref, gate_ref, oa_ref, or_ref, gates_ref, wa_ref, wr_ref, wo_ref, g_ref, b_ref, o_ref):
    ga_ref, gr_ref, mga_ref, mgr_ref = (
        gates_ref.at[:, p * D_MODEL:(p + 1) * D_MODEL] for p in range(4))
    def silu_bf(x):
        h = 0.5 * x
        return h * jnp.tanh(h) + h

    def logistic_f32(x):
        return (0.5 * jnp.tanh(0.5 * x) + 0.5).astype(_F32)

    ya = jnp.dot(oa_ref[...] * silu_bf(ga_ref[...]), wa_ref[...], preferred_element_type=_F32)
    yr = jnp.dot(or_ref[...] * silu_bf(gr_ref[...]), wr_ref[...], preferred_element_type=_F32)
    merged = logistic_f32(mga_ref[...]) * ya + logistic_f32(mgr_ref[...]) * yr
    out = jnp.dot(merged.astype(_BF16), wo_ref[...], preferred_element_type=_F32)
    z = DEEPNORM_ALPHA * x_ref[...] + gate_ref[...] * out
    o_ref[...] = _layer_norm_rows(z) * g_ref[...] + b_ref[...]


def _merge_residual(x2, mod3, o_attn, o_rnn, rest, rest_row0, wa_bf, wr_bf, wo_bf, layer, ln_g, ln_b,
                    cond_row_of_block, tm):
    n = x2.shape[0]
    row_blk = pl.BlockSpec((tm, D_MODEL), lambda i: (i, 0))
    gates_blk = pl.BlockSpec((tm, D_GATES), lambda i: (rest_row0 // tm + i, 0))
    whole = pl.BlockSpec((None, D_MODEL, D_MODEL), lambda i: (layer, 0, 0))
    vec = pl.BlockSpec((1, D_MODEL), lambda i: (0, 0))
    return pl.pallas_call(
        _out_kernel,
        grid=(n // tm,),
        in_specs=[
            row_blk,
            pl.BlockSpec((None, 1, D_MODEL), lambda i: (cond_row_of_block(i), 0, 2)),
            row_blk, row_blk, gates_blk,
            whole, whole, whole, vec, vec,
        ],
        out_specs=row_blk,
        out_shape=jax.ShapeDtypeStruct((n, D_MODEL), _F32),
        compiler_params=pltpu.CompilerParams(
            dimension_semantics=("parallel",), vmem_limit_bytes=VMEM_LIMIT),
        name="merge_residual",
    )(x2, mod3, o_attn, o_rnn, rest, wa_bf, wr_bf, wo_bf,
      ln_g.reshape(1, D_MODEL), ln_b.reshape(1, D_MODEL))


def _rope_tables(t_len):
    quarter = HEAD_DIM // 4
    inv_freq = ROPE_BASE ** (-np.arange(quarter, dtype=np.float64) / quarter)
    pos = np.arange(t_len)
    ang_r = (pos // GRID_W).astype(np.float64)[:, None] * inv_freq[None, :]
    ang_c = (pos % GRID_W).astype(np.float64)[:, None] * inv_freq[None, :]
    cos = np.concatenate([np.cos(ang_r)] * 2 + [np.cos(ang_c)] * 2, axis=1)
    sin = np.concatenate([-np.sin(ang_r), np.sin(ang_r), -np.sin(ang_c), np.sin(ang_c)], axis=1)
    return cos.astype(np.float32), sin.astype(np.float32)


def _gate_weights(rg_wa, rg_ba, rg_wx, rg_bx):
    wg = (0.5 * jnp.concatenate([rg_wa[0], rg_wx[0], rg_wa[1], rg_wx[1]], axis=-1)).astype(_BF16)
    blk = lambda v: v.reshape(N_RNN_BLOCKS, 1, RNN_BLOCK)
    bg = 0.5 * jnp.concatenate([blk(rg_ba[0]), blk(rg_bx[0]), blk(rg_ba[1]), blk(rg_bx[1])], axis=-1)
    return wg, bg


def kernel(x_prompt, x_sample, cache_k, cache_v, state_h, c, c_ctx, w_mod, b_mod, w_in,
           attn_sink, conv_w, conv_b, rg_wa, rg_ba, rg_wx, rg_bx, rg_lambda,
           w_br_attn, w_br_rnn, w_out, ln_g, ln_b):
    bp, s_len, _ = x_prompt.shape
    bd, t_len, _ = x_sample.shape
    past = cache_k.shape[2]

    cond8 = jnp.zeros((COND_ROWS, D_MODEL), _F32).at[0].set(c_ctx).at[1:1 + bd].set(c)
    mod = _modulation(cond8, w_mod, b_mod)

    wa_bf = w_br_attn.astype(_BF16)
    wr_bf = w_br_rnn.astype(_BF16)
    wo_bf = w_out.astype(_BF16)
    rope_tabs = _rope_tables(t_len)
    cache_k4 = cache_k.reshape(bd, DEPTH, past * N_KV_HEADS, HEAD_DIM)
    cache_v4 = cache_v.reshape(bd, DEPTH, past * N_KV_HEADS, HEAD_DIM)

    tm_proj, tm_out = 512, 512
    ctx_row = lambda i: 0
    lat_row_out = lambda i: 1 + (i * tm_out) // t_len
    n_lat = bd * t_len

    xp = x_prompt.reshape(bp * s_len, D_MODEL)
    xs = x_sample.reshape(n_lat, D_MODEL)
    h0_zero = jnp.zeros((bp, 2, D_RNN), _F32)
    hs = []
    kv_caches = None
    for l in range(DEPTH):
        mod3 = mod[l].reshape(COND_ROWS, 1, 3 * D_MODEL)
        wg, bg = _gate_weights(rg_wa[l], rg_ba[l], rg_wx[l], rg_bx[l])
        q, kv_caches, (k_lat, v_lat), rest = _project(
            xs, xp, mod3, w_in, l, tm_proj, rope_tabs, (bd, t_len), (bp, s_len), kv_caches)

        o_attn = _context_attention(q, n_lat, kv_caches[0], kv_caches[1], l, attn_sink[l], bp, s_len)
        o_rnn, h_fin = _rnn(rest, n_lat, h0_zero, conv_w[l], conv_b[l], wg, bg, rg_lambda[l], bp, s_len)
        xp = _merge_residual(xp, mod3, o_attn, o_rnn, rest, n_lat, wa_bf, wr_bf, wo_bf, l,
                             ln_g[l], ln_b[l], ctx_row, tm_out)
        hs.append(h_fin)

        o_attn = _latent_attention(q, 0, k_lat, v_lat, cache_k4, cache_v4, l, attn_sink[l], bd, t_len)
        o_rnn, _ = _rnn(rest, 0, state_h[:, l], conv_w[l], conv_b[l], wg, bg, rg_lambda[l], bd, t_len)
        xs = _merge_residual(xs, mod3, o_attn, o_rnn, rest, 0, wa_bf, wr_bf, wo_bf, l,
                             ln_g[l], ln_b[l], lat_row_out, tm_out)

    y_prompt = xp.reshape(bp, s_len, D_MODEL)
    y_sample = xs.reshape(bd, t_len, D_MODEL)
    cache_shape = (bp, DEPTH, s_len, N_KV_HEADS, HEAD_DIM)
    new_k, new_v = (cache.reshape(cache_shape) for cache in kv_caches)
    return (y_prompt, y_sample, new_k, new_v, jnp.stack(hs, axis=1))
```

```python
import functools

import jax
import jax.numpy as jnp
import numpy as np
from jax import lax
from jax.experimental import pallas as pl
from jax.experimental.pallas import tpu as pltpu

D_MODEL = 1024
DEPTH = 2
GRID_W = 64
N_HEADS = 8
N_KV_HEADS = 2
HEAD_DIM = 128
Q_GROUP = N_HEADS // N_KV_HEADS
D_ATTN = N_HEADS * HEAD_DIM
D_KV = N_KV_HEADS * HEAD_DIM
WINDOW = 128
Q_BLOCK = 128
D_RNN = D_MODEL
N_RNN_BLOCKS = 8
RNN_BLOCK = D_RNN // N_RNN_BLOCKS
CONV_W = 4
CONV_LEFT = 2
LRU_C = 8.0
ROPE_BASE = 10000.0
D_IN = 2 * D_ATTN + 2 * D_KV + 2 * D_RNN + 2 * D_MODEL
DEEPNORM_ALPHA = (2 * DEPTH) ** 0.25
LN_EPS = 1e-6
NEG_INF = -1e30
ATTN_SCALE = HEAD_DIM ** -0.5

LANES = 128
SUBLANES = 8
SEQ_SLOTS = SUBLANES
SLOT_PAD = 4
CTX_ATTN_SEQS = 8
PASS1_ROWS = 1024
COND_ROWS = 8
PROJ_TN = 512
N_PROJ_BLOCKS = D_IN // PROJ_TN
_REST_COL = (0, 4, 1, 2, 3)
D_GATES = 4 * D_MODEL
VMEM_LIMIT = 48 * 1024 * 1024
PROJ_VMEM_LIMIT = 56 * 1024 * 1024
RNN_VMEM_LIMIT = 56 * 1024 * 1024
F32_TINY = 1.1754943508222875e-38
LOG2_E = 1.4426950408889634
Q_PRESCALE = ATTN_SCALE * LOG2_E

_BF16 = jnp.bfloat16
_F32 = jnp.float32


def _silu(x):
    return x * jax.nn.sigmoid(x)


def _layer_norm_rows(x):
    mu = jnp.mean(x, axis=-1, keepdims=True)
    xc = x - mu
    var = jnp.mean(xc * xc, axis=-1, keepdims=True)
    return xc * lax.rsqrt(var + LN_EPS)


def _head_rows(head, n_rows):
    return pl.ds(head, n_rows, stride=N_KV_HEADS)


def _mod_kernel(cond_ref, w_ref, b_ref, o_ref):
    a = _silu(cond_ref[...]).astype(_BF16)
    o_ref[...] = jnp.dot(a, w_ref[...].astype(_BF16), preferred_element_type=_F32) + b_ref[...]


def _modulation(cond8, w_mod, b_mod):
    tn = D_MODEL
    return pl.pallas_call(
        _mod_kernel,
        grid=(DEPTH, 3 * D_MODEL // tn),
        in_specs=[
            pl.BlockSpec((COND_ROWS, D_MODEL), lambda l, j: (0, 0)),
            pl.BlockSpec((None, D_MODEL, tn), lambda l, j: (l, 0, j)),
            pl.BlockSpec((None, 1, tn), lambda l, j: (l, 0, j)),
        ],
        out_specs=pl.BlockSpec((None, COND_ROWS, tn), lambda l, j: (l, 0, j)),
        out_shape=jax.ShapeDtypeStruct((DEPTH, COND_ROWS, 3 * D_MODEL), _F32),
        compiler_params=pltpu.CompilerParams(dimension_semantics=("parallel", "parallel")),
        name="modulation",
    )(cond8, w_mod, b_mod.reshape(DEPTH, 1, 3 * D_MODEL))


def _rope(x, cos, sin_signed):
    quarter = HEAD_DIM // 4
    lane = lax.broadcasted_iota(jnp.int32, x.shape, 1)
    partner = jnp.where((lane // quarter) % 2 == 0,
                        pltpu.roll(x, HEAD_DIM - quarter, axis=1),
                        pltpu.roll(x, quarter, axis=1))
    return x * cos + partner * sin_signed


def _proj_kernel(xl_ref, xc_ref, shift_ref, scale_ref, w_hbm, cos_ref, sin_ref, *rest,
                 layer, n_lat_blocks):
    q_ref, kc_ref, vc_ref, kl_ref, vl_ref, g_ref, xr_ref, w_scr, stage_scr, sem = rest[-10:]
    i = pl.program_id(0)

    def chunk_copy(j):
        return pltpu.make_async_copy(
            w_hbm.at[layer, :, pl.ds(j * PROJ_TN, PROJ_TN)], stage_scr.at[j % 2], sem.at[j % 2])

    def fetch_chunk(j):
        if j == 0:
            chunk_copy(0).start()
        if j + 1 < N_PROJ_BLOCKS:
            chunk_copy(j + 1).start()
        chunk_copy(j).wait()
        w_scr[:, j * PROJ_TN:(j + 1) * PROJ_TN] = stage_scr[j % 2].astype(_BF16)

    x = jnp.where(i < n_lat_blocks, xl_ref[...], xc_ref[...])
    y = _layer_norm_rows(x)
    h = (y * (1.0 + scale_ref[...]) + shift_ref[...]).astype(_BF16)

    def rope(blk):
        return _rope(blk, cos_ref[...], sin_ref[...])

    def put_cache(ref, head, val):
        s_len = ref.shape[-2] // N_KV_HEADS
        for bi in range(ref.shape[0]):
            rows = val[bi * s_len:(bi + 1) * s_len].astype(ref.dtype)
            if len(ref.shape) == 3:
                ref[bi, _head_rows(head, s_len), :] = rows
            else:
                for l in range(ref.shape[1]):
                    ref[bi, l, _head_rows(head, s_len), :] = rows if l == layer else jnp.zeros_like(rows)

    def project_block(stream_weights):
        for j in range(N_PROJ_BLOCKS):
            c0 = j * PROJ_TN
            if stream_weights:
                fetch_chunk(j)
            acc = jnp.dot(h, w_scr[:, c0:c0 + PROJ_TN], preferred_element_type=_F32)
            if c0 < D_ATTN:
                for hh in range(PROJ_TN // HEAD_DIM):
                    sl = slice(hh * HEAD_DIM, (hh + 1) * HEAD_DIM)
                    q_ref[:, c0 + hh * HEAD_DIM:c0 + (hh + 1) * HEAD_DIM] = (
                        (rope(acc[:, sl]) * Q_PRESCALE).astype(q_ref.dtype))
            elif c0 == D_ATTN:
                for hh in range(N_KV_HEADS):
                    sl = slice(hh * HEAD_DIM, (hh + 1) * HEAD_DIM)
                    k_rot = rope(acc[:, sl])
                    put_cache(kc_ref, hh, k_rot)
                    put_cache(vc_ref, hh, acc[:, D_KV + hh * HEAD_DIM:D_KV + (hh + 1) * HEAD_DIM])
                    kl_ref[:, sl] = k_rot.astype(kl_ref.dtype)
                vl_ref[...] = acc[:, D_KV:].astype(vl_ref.dtype)
            else:
                off = c0 - D_ATTN - 2 * D_KV
                r0 = _REST_COL[off // D_MODEL] * D_MODEL + off % D_MODEL
                if r0 < D_GATES:
                    g_ref[:, r0:r0 + PROJ_TN] = acc.astype(g_ref.dtype)
                else:
                    xr_ref[:, r0 - D_GATES:r0 - D_GATES + PROJ_TN] = acc.astype(xr_ref.dtype)

    pl.when(i == 0)(functools.partial(project_block, True))
    pl.when(i != 0)(functools.partial(project_block, False))


def _project(x_lat, x_ctx, mod3, w_in, layer, tm, rope_tabs, lat_dims, ctx_dims, prev_caches):
    (bd, t_len), (bp, s_len) = lat_dims, ctx_dims
    n_lat, n_ctx = bd * t_len, bp * s_len
    n = n_lat + n_ctx
    nl_blocks = n_lat // tm
    is_lat = lambda i: i < nl_blocks
    ctx_blk = lambda i: jnp.maximum(i - nl_blocks, 0)
    cond_row = lambda i: jnp.where(is_lat(i), 1 + (i * tm) // t_len, 0)
    cos, sin = rope_tabs
    blocks_per_seq = t_len // tm
    cos = jnp.asarray(np.concatenate([np.ones((tm, HEAD_DIM), np.float32), cos], axis=0))
    sin = jnp.asarray(np.concatenate([np.zeros((tm, HEAD_DIM), np.float32), sin], axis=0))
    tab_blk = lambda i: jnp.where(is_lat(i), 1 + i % blocks_per_seq, 0)

    in_specs = [
        pl.BlockSpec((tm, D_MODEL), lambda i: (jnp.minimum(i, nl_blocks - 1), 0)),
        pl.BlockSpec((tm, D_MODEL), lambda i: (ctx_blk(i), 0)),
        pl.BlockSpec((None, 1, D_MODEL), lambda i: (cond_row(i), 0, 0)),
        pl.BlockSpec((None, 1, D_MODEL), lambda i: (cond_row(i), 0, 1)),
        pl.BlockSpec(memory_space=pl.ANY),
        pl.BlockSpec((tm, HEAD_DIM), lambda i: (tab_blk(i), 0)),
        pl.BlockSpec((tm, HEAD_DIM), lambda i: (tab_blk(i), 0)),
    ]
    args = [x_lat, x_ctx, mod3, mod3, w_in, cos, sin]
    aliases = {}
    seqs = tm // s_len
    cache_rows = N_KV_HEADS * s_len
    if prev_caches is None:
        cache_spec = pl.BlockSpec((seqs, DEPTH, cache_rows, HEAD_DIM), lambda i: (ctx_blk(i), 0, 0, 0))
    else:
        cache_spec = pl.BlockSpec((seqs, None, cache_rows, HEAD_DIM), lambda i: (ctx_blk(i), layer, 0, 0))
        aliases = {len(args): 1, len(args) + 1: 2}
        in_specs += [pl.BlockSpec(memory_space=pl.ANY)] * 2
        args += list(prev_caches)
    cache_shape = jax.ShapeDtypeStruct((bp, DEPTH, cache_rows, HEAD_DIM), _F32)
    lat_kv_spec = pl.BlockSpec((tm, D_KV), lambda i: (jnp.minimum(i, nl_blocks), 0))
    lat_kv_shape = jax.ShapeDtypeStruct((n_lat + tm, D_KV), _BF16)
    q, k_cache, v_cache, k_lat, v_lat, gates, xr = pl.pallas_call(
        functools.partial(_proj_kernel, layer=layer, n_lat_blocks=nl_blocks),
        grid=(n // tm,),
        in_specs=in_specs,
        out_specs=[
            pl.BlockSpec((tm, D_ATTN), lambda i: (i, 0)),
            cache_spec, cache_spec, lat_kv_spec, lat_kv_spec,
            pl.BlockSpec((tm, D_GATES), lambda i: (i, 0)),
            pl.BlockSpec((tm, D_RNN), lambda i: (i, 0)),
        ],
        out_shape=[
            jax.ShapeDtypeStruct((n, D_ATTN), _BF16),
            cache_shape, cache_shape, lat_kv_shape, lat_kv_shape,
            jax.ShapeDtypeStruct((n, D_GATES), _BF16),
            jax.ShapeDtypeStruct((n, D_RNN), _BF16),
        ],
        scratch_shapes=[
            pltpu.VMEM((D_MODEL, D_IN), _BF16),
            pltpu.VMEM((2, D_MODEL, PROJ_TN), _F32),
            pltpu.SemaphoreType.DMA((2,)),
        ],
        input_output_aliases=aliases,
        compiler_params=pltpu.CompilerParams(
            dimension_semantics=("arbitrary",), vmem_limit_bytes=PROJ_VMEM_LIMIT),
        name="proj",
    )(*args)
    return q, (k_cache, v_cache), (k_lat, v_lat), gates, xr


def _softmax_pv_t(score_parts, sink_row, value_parts):
    m = sink_row
    for s in score_parts:
        m = jnp.maximum(m, jnp.max(s, axis=0, keepdims=True))
    acc = None
    for s, v_aug in zip(score_parts, value_parts):
        part = lax.dot_general(v_aug, jnp.exp2(s - m).astype(_BF16), (((0,), (0,)), ((), ())),
                               preferred_element_type=_F32)
        acc = part if acc is None else acc + part
    denom = acc[HEAD_DIM:HEAD_DIM + 1, :] + jnp.exp2(sink_row - m)
    return acc[:HEAD_DIM, :] / denom


def _softmax_pv(score_parts, sink_col, value_parts):
    m = sink_col
    for s in score_parts:
        m = jnp.maximum(m, jnp.max(s, axis=-1, keepdims=True))
    acc = None
    for s, v_aug in zip(score_parts, value_parts):
        part = jnp.dot(jnp.exp2(s - m).astype(_BF16), v_aug, preferred_element_type=_F32)
        acc = part if acc is None else acc + part
    denom = acc[:, HEAD_DIM:HEAD_DIM + 1] + jnp.exp2(sink_col - m)
    return acc[:, :HEAD_DIM] / denom


def _sink_column(sink_ref, kvh, rows):
    head = lax.broadcasted_iota(jnp.int32, (Q_GROUP * rows, 1), 0) // rows
    col = jnp.full((Q_GROUP * rows, 1), sink_ref[kvh * Q_GROUP], _F32)
    for g in range(1, Q_GROUP):
        col = jnp.where(head == g, sink_ref[kvh * Q_GROUP + g], col)
    return col * LOG2_E


def _with_ones(v_bf):
    return jnp.concatenate([v_bf, jnp.ones_like(v_bf)], axis=1)


def _sink_row(sink_ref, kvh, n_per_head):
    head = lax.broadcasted_iota(jnp.int32, (1, Q_GROUP * n_per_head), 1) // n_per_head
    row = jnp.full((1, Q_GROUP * n_per_head), sink_ref[kvh * Q_GROUP], _F32)
    for g in range(1, Q_GROUP):
        row = jnp.where(head == g, sink_ref[kvh * Q_GROUP + g], row)
    return row * LOG2_E


def _stack_heads(q_rows):
    return jnp.concatenate(
        [q_rows[:, g * HEAD_DIM:(g + 1) * HEAD_DIM] for g in range(Q_GROUP)], axis=0)


def _qk(q, k_bf):
    return lax.dot_general(q, k_bf, (((1,), (1,)), ((), ())), preferred_element_type=_F32)


def _ctx_attn_kernel(sink_ref, q_ref, k_ref, v_ref, o_ref, *, s_len):
    kvh = pl.program_id(1)
    sink_row = _sink_row(sink_ref, kvh, s_len)
    for bi in range(q_ref.shape[0] // s_len):
        rows = slice(bi * s_len, (bi + 1) * s_len)
        q = _stack_heads(q_ref[rows, :])
        k = k_ref[bi, _head_rows(kvh, s_len), :].astype(_BF16)
        v_aug = _with_ones(v_ref[bi, _head_rows(kvh, s_len), :].astype(_BF16))
        o_t = _softmax_pv_t([_qk(k, q)], sink_row, [v_aug])
        for g in range(Q_GROUP):
            o_ref[rows, g * HEAD_DIM:(g + 1) * HEAD_DIM] = (
                o_t[:, g * s_len:(g + 1) * s_len].T.astype(o_ref.dtype))


def _context_attention(q, q_row0, k_cache, v_cache, layer, sink, batch, s_len):
    n = batch * s_len
    tm = CTX_ATTN_SEQS * s_len
    gw = Q_GROUP * HEAD_DIM
    q_blk0 = q_row0 // tm
    kv_spec = pl.BlockSpec((CTX_ATTN_SEQS, None, N_KV_HEADS * s_len, HEAD_DIM), lambda b, h: (b, layer, 0, 0))
    return pl.pallas_call(
        functools.partial(_ctx_attn_kernel, s_len=s_len),
        grid=(n // tm, N_KV_HEADS),
        in_specs=[
            pl.BlockSpec(memory_space=pltpu.SMEM),
            pl.BlockSpec((tm, gw), lambda b, h: (q_blk0 + b, h)),
            kv_spec,
            kv_spec,
        ],
        out_specs=pl.BlockSpec((tm, gw), lambda b, h: (b, h)),
        out_shape=jax.ShapeDtypeStruct((n, D_ATTN), _BF16),
        compiler_params=pltpu.CompilerParams(dimension_semantics=("parallel", "parallel")),
        name="ctx_attention",
    )(sink, q, k_cache, v_cache)


def _lat_attn_kernel(sink_ref, q_ref, k_ref, v_ref, ck_ref, cv_ref, o_ref):
    kvh = pl.program_id(1)
    t_len = q_ref.shape[0]
    nb = t_len // Q_BLOCK
    past = ck_ref.shape[0] // N_KV_HEADS
    ck = ck_ref[_head_rows(kvh, past), :].astype(_BF16)
    cv_aug = _with_ones(cv_ref[_head_rows(kvh, past), :].astype(_BF16))
    v_aug = _with_ones(v_ref[...])
    sink_col = _sink_column(sink_ref, kvh, Q_BLOCK)
    rows = Q_GROUP * Q_BLOCK
    r = lax.broadcasted_iota(jnp.int32, (rows, Q_BLOCK), 0) % Q_BLOCK
    c = lax.broadcasted_iota(jnp.int32, (rows, Q_BLOCK), 1)
    for jb in range(nb):
        r0 = jb * Q_BLOCK
        q = _stack_heads(q_ref[r0:r0 + Q_BLOCK, :])
        lo = max(r0 - WINDOW, 0)
        hi = min(r0 + Q_BLOCK + WINDOW, t_len)
        s = _qk(q, k_ref[lo:hi, :])
        parts = []
        off = 0
        if jb > 0:
            parts.append(jnp.where(c >= r, s[:, :Q_BLOCK], NEG_INF))
            off = Q_BLOCK
        parts.append(s[:, off:off + Q_BLOCK])
        off += Q_BLOCK
        if jb < nb - 1:
            parts.append(jnp.where(c <= r, s[:, off:], NEG_INF))
        s_local = jnp.concatenate(parts, axis=1)
        o = _softmax_pv([s_local, _qk(q, ck)], sink_col, [v_aug[lo:hi], cv_aug])
        for g in range(Q_GROUP):
            o_ref[r0:r0 + Q_BLOCK, g * HEAD_DIM:(g + 1) * HEAD_DIM] = (
                o[g * Q_BLOCK:(g + 1) * Q_BLOCK].astype(o_ref.dtype))


def _latent_attention(q, q_row0, k, v, cache_k4, cache_v4, layer, sink, batch, t_len):
    n = batch * t_len
    gw = Q_GROUP * HEAD_DIM
    q_blk0 = q_row0 // t_len
    cache_spec = pl.BlockSpec((None, None, cache_k4.shape[2], HEAD_DIM), lambda b, h: (b, layer, 0, 0))
    return pl.pallas_call(
        _lat_attn_kernel,
        grid=(batch, N_KV_HEADS),
        in_specs=[
            pl.BlockSpec(memory_space=pltpu.SMEM),
            pl.BlockSpec((t_len, gw), lambda b, h: (q_blk0 + b, h)),
            pl.BlockSpec((t_len, HEAD_DIM), lambda b, h: (b, h)),
            pl.BlockSpec((t_len, HEAD_DIM), lambda b, h: (b, h)),
            cache_spec,
            cache_spec,
        ],
        out_specs=pl.BlockSpec((t_len, gw), lambda b, h: (b, h)),
        out_shape=jax.ShapeDtypeStruct((n, D_ATTN), _BF16),
        compiler_params=pltpu.CompilerParams(
            dimension_semantics=("parallel", "parallel"), vmem_limit_bytes=VMEM_LIMIT),
        name="latent_attention",
    )(sink, q, k, v, cache_k4, cache_v4)


def _shift_rows(x, d, t_len, offset=0):
    n = x.shape[0]
    row = (lax.broadcasted_iota(jnp.int32, x.shape, 0) + (n - offset)) % t_len
    rolled = pltpu.roll(x, d % n, axis=0)
    valid = (row >= d) if d > 0 else (row < t_len + d)
    return jnp.where(valid, rolled, 0.0)


def _sqrt_nonneg(z):
    return z * lax.rsqrt(jnp.maximum(z, F32_TINY))


def _rnn_kernel(x_ref, cw_ref, cb_ref, wg_ref, bg_ref, lam_ref, h0_ref, y_ref, hl_ref,
                af_scr, bf_scr, ab_scr, bb_scr, hf_scr, hb_scr, *, nb, nc):
    t_len = x_ref.shape[1]
    pitch = t_len + SLOT_PAD
    scr = ((af_scr, bf_scr), (ab_scr, bb_scr))
    per_group = SEQ_SLOTS // 2
    kb = min(per_group, max(1, PASS1_ROWS // t_len))

    def seq_of(slot):
        g, u = slot % 2, slot // 2
        return (g * per_group + u, 0) if nc == 1 else (u, g)

    for g in range(2):
        b_first, cl = seq_of(g)
        offset = SLOT_PAD * g
        lanes = slice(cl * LANES, (cl + 1) * LANES)
        cw = cw_ref[:, lanes]
        cb = cb_ref[:, lanes]
        wg = wg_ref[cl]
        bg = bg_ref[cl]
        lam = lam_ref[:, lanes]
        c_softplus = LRU_C * (jnp.maximum(-lam, 0.0) + jnp.log1p(jnp.exp(-jnp.abs(lam))))

        def per_chunk(it, carry, lanes=lanes, cw=cw, cb=cb, wg=wg, bg=bg,
                      c_softplus=c_softplus, g=g, b_first=b_first, offset=offset):
            n_rows = kb * t_len
            x = x_ref[pl.ds(b_first + it * kb, kb), :, lanes].astype(_F32).reshape(n_rows, LANES)
            if offset:
                x = pltpu.roll(x, offset, axis=0)
            xc = None
            for j in range(CONV_W):
                d = CONV_LEFT - j
                tap = cw[j:j + 1] * (_shift_rows(x, d, t_len, offset) if d else x)
                xc = tap if xc is None else xc + tap
            xc = xc + cb
            half_gates = jnp.dot(xc.astype(_BF16), wg, preferred_element_type=_F32) + bg
            half_xc = 0.5 * xc
            for d in range(2):
                t_r = jnp.tanh(half_gates[:, (2 * d) * LANES:(2 * d + 1) * LANES])
                t_i = jnp.tanh(half_gates[:, (2 * d + 1) * LANES:(2 * d + 2) * LANES])
                k = 0.5 * c_softplus[d:d + 1]
                neg_log_a = t_r * k + k
                a = jnp.exp2(neg_log_a * -LOG2_E)
                bterm = _sqrt_nonneg(jnp.tanh(neg_log_a) * (a * a + 1.0)) * ((t_i + 1.0) * half_xc)
                for jb in range(kb):
                    slot = 2 * (it * kb + jb) + g
                    row0 = pl.multiple_of(slot * pitch - offset, SUBLANES)
                    for dst, val in ((scr[d][0], a), (scr[d][1], bterm)):
                        dst[pl.ds(row0, t_len), :] = val[jb * t_len:(jb + 1) * t_len]
                        if offset:
                            w0 = ((jb + 1) * t_len) % n_rows
                            dst[pl.ds(row0 + t_len, SUBLANES), :] = val[w0:w0 + SUBLANES]
            return carry

        if per_group == kb:
            per_chunk(0, 0)
        else:
            lax.fori_loop(0, per_group // kb, per_chunk, 0)

    def slot_rows(ref3, d):
        rows = []
        for slot in range(SEQ_SLOTS):
            b, cl = seq_of(slot)
            rows.append(ref3[b, d:d + 1, cl * LANES:(cl + 1) * LANES])
        return jnp.concatenate(rows, axis=0)

    for slot in range(SEQ_SLOTS):
        tile = (slot * pitch + t_len) // SUBLANES * SUBLANES
        for h_scr in (hf_scr, hb_scr):
            h_scr[pl.ds(tile, SUBLANES), :] = jnp.zeros((SUBLANES, LANES), _F32)

    def advance2(a_scr, b_scr, h_scr, t0, t1, h):
        r0 = pl.ds(t0, SEQ_SLOTS, stride=pitch)
        r1 = pl.ds(t1, SEQ_SLOTS, stride=pitch)
        a0, b0, a1, b1 = a_scr[r0, :], b_scr[r0, :], a_scr[r1, :], b_scr[r1, :]
        h_scr[r0, :] = a0 * h + b0
        h = (a1 * a0) * h + (a1 * b0 + b1)
        h_scr[r1, :] = h
        return h

    hf, hb = slot_rows(h0_ref, 0), slot_rows(h0_ref, 1)
    for t in range(0, t_len, 2):
        hf = advance2(af_scr, bf_scr, hf_scr, t, t + 1, hf)
        hb = advance2(ab_scr, bb_scr, hb_scr, t_len - 1 - t, t_len - 2 - t, hb)

    for slot in range(SEQ_SLOTS):
        b, cl = seq_of(slot)
        lanes = slice(cl * LANES, (cl + 1) * LANES)
        hl_ref[b, 0:1, lanes] = hf[slot:slot + 1, :]
        hl_ref[b, 1:2, lanes] = hb[slot:slot + 1, :]
        if slot % 2 == 0:
            rows = pl.ds(slot * pitch, t_len)
            y = hf_scr[rows, :] + hb_scr[rows, :]
        else:
            rows = pl.ds(slot * pitch - SLOT_PAD, t_len + SUBLANES)
            y = pltpu.roll(hf_scr[rows, :] + hb_scr[rows, :], t_len + SUBLANES - SLOT_PAD, axis=0)[:t_len]
        y_ref[b, :, lanes] = y.astype(y_ref.dtype)


def _rnn(xr, row0, h0, conv_w, conv_b, wg, bg, lam, batch, t_len):
    nb = min(batch, SEQ_SLOTS)
    blk0 = row0 // (nb * t_len)
    nc = SEQ_SLOTS // nb
    assert nc in (1, 2) and batch % nb == 0, "a grid step holds 8 batches x 1 or 4 batches x 2 channel blocks"
    cw = nc * LANES
    scratch = pltpu.VMEM((SEQ_SLOTS * (t_len + SLOT_PAD), LANES), _F32)
    y, h_last = pl.pallas_call(
        functools.partial(_rnn_kernel, nb=nb, nc=nc),
        grid=(batch // nb, D_RNN // cw),
        in_specs=[
            pl.BlockSpec((nb, t_len, cw), lambda g, c: (blk0 + g, 0, c)),
            pl.BlockSpec((CONV_W, cw), lambda g, c: (0, c)),
            pl.BlockSpec((1, cw), lambda g, c: (0, c)),
            pl.BlockSpec((nc, RNN_BLOCK, 4 * LANES), lambda g, c: (c, 0, 0)),
            pl.BlockSpec((nc, 1, 4 * LANES), lambda g, c: (c, 0, 0)),
            pl.BlockSpec((2, cw), lambda g, c: (0, c)),
            pl.BlockSpec((nb, 2, cw), lambda g, c: (g, 0, c)),
        ],
        out_specs=[
            pl.BlockSpec((nb, t_len, cw), lambda g, c: (g, 0, c)),
            pl.BlockSpec((nb, 2, cw), lambda g, c: (g, 0, c)),
        ],
        out_shape=[
            jax.ShapeDtypeStruct((batch, t_len, D_RNN), _BF16),
            jax.ShapeDtypeStruct((batch, 2, D_RNN), _F32),
        ],
        scratch_shapes=[scratch] * 6,
        compiler_params=pltpu.CompilerParams(
            dimension_semantics=("parallel", "parallel"), vmem_limit_bytes=RNN_VMEM_LIMIT),
        name="rnn",
    )(xr.reshape(xr.shape[0] // t_len, t_len, D_RNN), conv_w, conv_b.reshape(1, D_RNN), wg, bg, lam, h0)
    return y.reshape(batch * t_len, D_RNN), h_last


def _out_kernel(x_ref, gate_ref, oa_ref, or_ref, gates_ref, wa_ref, wr_ref, wo_ref, g_ref, b_ref, o_ref):
    ga_ref, gr_ref, mga_ref, mgr_ref = (
        gates_ref.at[:, p * D_MODEL:(p + 1) * D_MODEL] for p in range(4))
    def silu_bf(x):
        h = 0.5 * x
        return h * jnp.tanh(h) + h

    def logistic_f32(x):
        return (0.5 * jnp.tanh(0.5 * x) + 0.5).astype(_F32)

    ya = jnp.dot(oa_ref[...] * silu_bf(ga_ref[...]), wa_ref[...], preferred_element_type=_F32)
    yr = jnp.dot(or_ref[...] * silu_bf(gr_ref[...]), wr_ref[...], preferred_element_type=_F32)
    merged = logistic_f32(mga_ref[...]) * ya + logistic_f32(mgr_ref[...]) * yr
    out = jnp.dot(merged.astype(_BF16), wo_ref[...], preferred_element_type=_F32)
    z = DEEPNORM_ALPHA * x_ref[...] + gate_ref[...] * out
    o_ref[...] = _layer_norm_rows(z) * g_ref[...] + b_ref[...]


def _merge_residual(x2, mod3, o_attn, o_rnn, gates, gates_row0, wa_bf, wr_bf, wo_bf, layer, ln_g, ln_b,
                    cond_row_of_block, tm):
    n = x2.shape[0]
    row_blk = pl.BlockSpec((tm, D_MODEL), lambda i: (i, 0))
    gates_blk = pl.BlockSpec((tm, D_GATES), lambda i: (gates_row0 // tm + i, 0))
    whole = pl.BlockSpec((None, D_MODEL, D_MODEL), lambda i: (layer, 0, 0))
    vec = pl.BlockSpec((1, D_MODEL), lambda i: (0, 0))
    return pl.pallas_call(
        _out_kernel,
        grid=(n // tm,),
        in_specs=[
            row_blk,
            pl.BlockSpec((None, 1, D_MODEL), lambda i: (cond_row_of_block(i), 0, 2)),
            row_blk, row_blk, gates_blk,
            whole, whole, whole, vec, vec,
        ],
        out_specs=row_blk,
        out_shape=jax.ShapeDtypeStruct((n, D_MODEL), _F32),
        compiler_params=pltpu.CompilerParams(
            dimension_semantics=("parallel",), vmem_limit_bytes=VMEM_LIMIT),
        name="merge_residual",
    )(x2, mod3, o_attn, o_rnn, gates, wa_bf, wr_bf, wo_bf,
      ln_g.reshape(1, D_MODEL), ln_b.reshape(1, D_MODEL))


def _rope_tables(t_len):
    quarter = HEAD_DIM // 4
    inv_freq = ROPE_BASE ** (-np.arange(quarter, dtype=np.float64) / quarter)
    pos = np.arange(t_len)
    ang_r = (pos // GRID_W).astype(np.float64)[:, None] * inv_freq[None, :]
    ang_c = (pos % GRID_W).astype(np.float64)[:, None] * inv_freq[None, :]
    cos = np.concatenate([np.cos(ang_r)] * 2 + [np.cos(ang_c)] * 2, axis=1)
    sin = np.concatenate([-np.sin(ang_r), np.sin(ang_r), -np.sin(ang_c), np.sin(ang_c)], axis=1)
    return cos.astype(np.float32), sin.astype(np.float32)


def _gate_weights(rg_wa, rg_ba, rg_wx, rg_bx):
    wg = (0.5 * jnp.concatenate([rg_wa[0], rg_wx[0], rg_wa[1], rg_wx[1]], axis=-1)).astype(_BF16)
    blk = lambda v: v.reshape(N_RNN_BLOCKS, 1, RNN_BLOCK)
    bg = 0.5 * jnp.concatenate([blk(rg_ba[0]), blk(rg_bx[0]), blk(rg_ba[1]), blk(rg_bx[1])], axis=-1)
    return wg, bg


def kernel(x_prompt, x_sample, cache_k, cache_v, state_h, c, c_ctx, w_mod, b_mod, w_in,
           attn_sink, conv_w, conv_b, rg_wa, rg_ba, rg_wx, rg_bx, rg_lambda,
           w_br_attn, w_br_rnn, w_out, ln_g, ln_b):
    bp, s_len, _ = x_prompt.shape
    bd, t_len, _ = x_sample.shape
    past = cache_k.shape[2]

    cond8 = jnp.zeros((COND_ROWS, D_MODEL), _F32).at[0].set(c_ctx).at[1:1 + bd].set(c)
    mod = _modulation(cond8, w_mod, b_mod)

    wa_bf = w_br_attn.astype(_BF16)
    wr_bf = w_br_rnn.astype(_BF16)
    wo_bf = w_out.astype(_BF16)
    rope_tabs = _rope_tables(t_len)
    cache_k4 = cache_k.reshape(bd, DEPTH, past * N_KV_HEADS, HEAD_DIM)
    cache_v4 = cache_v.reshape(bd, DEPTH, past * N_KV_HEADS, HEAD_DIM)

    tm_proj, tm_out = 512, 512
    ctx_row = lambda i: 0
    lat_row_out = lambda i: 1 + (i * tm_out) // t_len
    n_lat = bd * t_len

    xp = x_prompt.reshape(bp * s_len, D_MODEL)
    xs = x_sample.reshape(n_lat, D_MODEL)
    h0_zero = jnp.zeros((bp, 2, D_RNN), _F32)
    hs = []
    kv_caches = None
    for l in range(DEPTH):
        mod3 = mod[l].reshape(COND_ROWS, 1, 3 * D_MODEL)
        wg, bg = _gate_weights(rg_wa[l], rg_ba[l], rg_wx[l], rg_bx[l])
        q, kv_caches, (k_lat, v_lat), gates, xr = _project(
            xs, xp, mod3, w_in, l, tm_proj, rope_tabs, (bd, t_len), (bp, s_len), kv_caches)

        o_attn = _context_attention(q, n_lat, kv_caches[0], kv_caches[1], l, attn_sink[l], bp, s_len)
        o_rnn, h_fin = _rnn(xr, n_lat, h0_zero, conv_w[l], conv_b[l], wg, bg, rg_lambda[l], bp, s_len)
        xp = _merge_residual(xp, mod3, o_attn, o_rnn, gates, n_lat, wa_bf, wr_bf, wo_bf, l,
                             ln_g[l], ln_b[l], ctx_row, tm_out)
        hs.append(h_fin)

        o_attn = _latent_attention(q, 0, k_lat, v_lat, cache_k4, cache_v4, l, attn_sink[l], bd, t_len)
        o_rnn, _ = _rnn(xr, 0, state_h[:, l], conv_w[l], conv_b[l], wg, bg, rg_lambda[l], bd, t_len)
        xs = _merge_residual(xs, mod3, o_attn, o_rnn, gates, 0, wa_bf, wr_bf, wo_bf, l,
                             ln_g[l], ln_b[l], lat_row_out, tm_out)

    y_prompt = xp.reshape(bp, s_len, D_MODEL)
    y_sample = xs.reshape(bd, t_len, D_MODEL)
    cache_shape = (bp, DEPTH, s_len, N_KV_HEADS, HEAD_DIM)
    new_k, new_v = (cache.reshape(cache_shape) for cache in kv_caches)
    return (y_prompt, y_sample, new_k, new_v, jnp.stack(hs, axis=1))
```

```python
import functools

import jax
import jax.numpy as jnp
import numpy as np
from jax import lax
from jax.experimental import pallas as pl
from jax.experimental.pallas import tpu as pltpu

D_MODEL = 1024
DEPTH = 2
GRID_W = 64
N_HEADS = 8
N_KV_HEADS = 2
HEAD_DIM = 128
Q_GROUP = N_HEADS // N_KV_HEADS
D_ATTN = N_HEADS * HEAD_DIM
D_KV = N_KV_HEADS * HEAD_DIM
WINDOW = 128
Q_BLOCK = 128
D_RNN = D_MODEL
N_RNN_BLOCKS = 8
RNN_BLOCK = D_RNN // N_RNN_BLOCKS
CONV_W = 4
CONV_LEFT = 2
LRU_C = 8.0
ROPE_BASE = 10000.0
D_IN = 2 * D_ATTN + 2 * D_KV + 2 * D_RNN + 2 * D_MODEL
DEEPNORM_ALPHA = (2 * DEPTH) ** 0.25
LN_EPS = 1e-6
NEG_INF = -1e30
ATTN_SCALE = HEAD_DIM ** -0.5

LANES = 128
SUBLANES = 8
SEQ_SLOTS = SUBLANES
SLOT_PAD = 4
CTX_ATTN_SEQS = 8
PASS1_ROWS = 1024
COND_ROWS = 8
PROJ_TN = 512
N_PROJ_BLOCKS = D_IN // PROJ_TN
D_REST = D_IN - D_ATTN - 2 * D_KV
_REST_COL = (0, 4, 1, 2, 3)
D_GATES = 4 * D_MODEL
XR_COL0 = 4 * D_MODEL
VMEM_LIMIT = 48 * 1024 * 1024
PROJ_VMEM_LIMIT = 56 * 1024 * 1024
RNN_VMEM_LIMIT = 56 * 1024 * 1024
F32_TINY = 1.1754943508222875e-38
LOG2_E = 1.4426950408889634
Q_PRESCALE = ATTN_SCALE * LOG2_E

_BF16 = jnp.bfloat16
_F32 = jnp.float32


def _silu(x):
    return x * jax.nn.sigmoid(x)


def _layer_norm_rows(x):
    mu = jnp.mean(x, axis=-1, keepdims=True)
    xc = x - mu
    var = jnp.mean(xc * xc, axis=-1, keepdims=True)
    return xc * lax.rsqrt(var + LN_EPS)


def _head_rows(head, n_rows):
    return pl.ds(head, n_rows, stride=N_KV_HEADS)


def _mod_kernel(cond_ref, w_ref, b_ref, o_ref):
    a = _silu(cond_ref[...]).astype(_BF16)
    o_ref[...] = jnp.dot(a, w_ref[...].astype(_BF16), preferred_element_type=_F32) + b_ref[...]


def _modulation(cond8, w_mod, b_mod):
    tn = D_MODEL
    return pl.pallas_call(
        _mod_kernel,
        grid=(DEPTH, 3 * D_MODEL // tn),
        in_specs=[
            pl.BlockSpec((COND_ROWS, D_MODEL), lambda l, j: (0, 0)),
            pl.BlockSpec((None, D_MODEL, tn), lambda l, j: (l, 0, j)),
            pl.BlockSpec((None, 1, tn), lambda l, j: (l, 0, j)),
        ],
        out_specs=pl.BlockSpec((None, COND_ROWS, tn), lambda l, j: (l, 0, j)),
        out_shape=jax.ShapeDtypeStruct((DEPTH, COND_ROWS, 3 * D_MODEL), _F32),
        compiler_params=pltpu.CompilerParams(dimension_semantics=("parallel", "parallel")),
        name="modulation",
    )(cond8, w_mod, b_mod.reshape(DEPTH, 1, 3 * D_MODEL))


def _rope(x, cos, sin_signed):
    quarter = HEAD_DIM // 4
    lane = lax.broadcasted_iota(jnp.int32, x.shape, 1)
    partner = jnp.where((lane // quarter) % 2 == 0,
                        pltpu.roll(x, HEAD_DIM - quarter, axis=1),
                        pltpu.roll(x, quarter, axis=1))
    return x * cos + partner * sin_signed


def _proj_kernel(xl_ref, xc_ref, shift_ref, scale_ref, w_hbm, cos_ref, sin_ref, *rest,
                 layer, n_lat_blocks):
    q_ref, kc_ref, vc_ref, kl_ref, vl_ref, r_ref, w_scr, stage_scr, sem = rest[-9:]
    i = pl.program_id(0)

    def chunk_copy(j):
        return pltpu.make_async_copy(
            w_hbm.at[layer, :, pl.ds(j * PROJ_TN, PROJ_TN)], stage_scr.at[j % 2], sem.at[j % 2])

    def fetch_chunk(j):
        if j == 0:
            chunk_copy(0).start()
        if j + 1 < N_PROJ_BLOCKS:
            chunk_copy(j + 1).start()
        chunk_copy(j).wait()
        w_scr[:, j * PROJ_TN:(j + 1) * PROJ_TN] = stage_scr[j % 2].astype(_BF16)

    x = jnp.where(i < n_lat_blocks, xl_ref[...], xc_ref[...])
    y = _layer_norm_rows(x)
    h = (y * (1.0 + scale_ref[...]) + shift_ref[...]).astype(_BF16)

    def rope(blk):
        return _rope(blk, cos_ref[...], sin_ref[...])

    def put_cache(ref, head, val):
        s_len = ref.shape[-2] // N_KV_HEADS
        for bi in range(ref.shape[0]):
            rows = val[bi * s_len:(bi + 1) * s_len].astype(ref.dtype)
            if len(ref.shape) == 3:
                ref[bi, _head_rows(head, s_len), :] = rows
            else:
                for l in range(ref.shape[1]):
                    ref[bi, l, _head_rows(head, s_len), :] = rows if l == layer else jnp.zeros_like(rows)

    def project_block(stream_weights):
        for j in range(N_PROJ_BLOCKS):
            c0 = j * PROJ_TN
            if stream_weights:
                fetch_chunk(j)
            acc = jnp.dot(h, w_scr[:, c0:c0 + PROJ_TN], preferred_element_type=_F32)
            if c0 < D_ATTN:
                for hh in range(PROJ_TN // HEAD_DIM):
                    sl = slice(hh * HEAD_DIM, (hh + 1) * HEAD_DIM)
                    q_ref[:, c0 + hh * HEAD_DIM:c0 + (hh + 1) * HEAD_DIM] = (
                        (rope(acc[:, sl]) * Q_PRESCALE).astype(q_ref.dtype))
            elif c0 == D_ATTN:
                for hh in range(N_KV_HEADS):
                    sl = slice(hh * HEAD_DIM, (hh + 1) * HEAD_DIM)
                    k_rot = rope(acc[:, sl])
                    put_cache(kc_ref, hh, k_rot)
                    put_cache(vc_ref, hh, acc[:, D_KV + hh * HEAD_DIM:D_KV + (hh + 1) * HEAD_DIM])
                    kl_ref[:, sl] = k_rot.astype(kl_ref.dtype)
                vl_ref[...] = acc[:, D_KV:].astype(vl_ref.dtype)
            else:
                off = c0 - D_ATTN - 2 * D_KV
                r0 = _REST_COL[off // D_MODEL] * D_MODEL + off % D_MODEL
                r_ref[:, r0:r0 + PROJ_TN] = acc.astype(r_ref.dtype)

    pl.when(i == 0)(functools.partial(project_block, True))
    pl.when(i != 0)(functools.partial(project_block, False))


def _project(x_lat, x_ctx, mod3, w_in, layer, tm, rope_tabs, lat_dims, ctx_dims, prev_caches):
    (bd, t_len), (bp, s_len) = lat_dims, ctx_dims
    n_lat, n_ctx = bd * t_len, bp * s_len
    n = n_lat + n_ctx
    nl_blocks = n_lat // tm
    is_lat = lambda i: i < nl_blocks
    ctx_blk = lambda i: jnp.maximum(i - nl_blocks, 0)
    cond_row = lambda i: jnp.where(is_lat(i), 1 + (i * tm) // t_len, 0)
    cos, sin = rope_tabs
    blocks_per_seq = t_len // tm
    cos = jnp.asarray(np.concatenate([np.ones((tm, HEAD_DIM), np.float32), cos], axis=0))
    sin = jnp.asarray(np.concatenate([np.zeros((tm, HEAD_DIM), np.float32), sin], axis=0))
    tab_blk = lambda i: jnp.where(is_lat(i), 1 + i % blocks_per_seq, 0)

    in_specs = [
        pl.BlockSpec((tm, D_MODEL), lambda i: (jnp.minimum(i, nl_blocks - 1), 0)),
        pl.BlockSpec((tm, D_MODEL), lambda i: (ctx_blk(i), 0)),
        pl.BlockSpec((None, 1, D_MODEL), lambda i: (cond_row(i), 0, 0)),
        pl.BlockSpec((None, 1, D_MODEL), lambda i: (cond_row(i), 0, 1)),
        pl.BlockSpec(memory_space=pl.ANY),
        pl.BlockSpec((tm, HEAD_DIM), lambda i: (tab_blk(i), 0)),
        pl.BlockSpec((tm, HEAD_DIM), lambda i: (tab_blk(i), 0)),
    ]
    args = [x_lat, x_ctx, mod3, mod3, w_in, cos, sin]
    aliases = {}
    seqs = tm // s_len
    cache_rows = N_KV_HEADS * s_len
    if prev_caches is None:
        cache_spec = pl.BlockSpec((seqs, DEPTH, cache_rows, HEAD_DIM), lambda i: (ctx_blk(i), 0, 0, 0))
    else:
        cache_spec = pl.BlockSpec((seqs, None, cache_rows, HEAD_DIM), lambda i: (ctx_blk(i), layer, 0, 0))
        aliases = {len(args): 1, len(args) + 1: 2}
        in_specs += [pl.BlockSpec(memory_space=pl.ANY)] * 2
        args += list(prev_caches)
    cache_shape = jax.ShapeDtypeStruct((bp, DEPTH, cache_rows, HEAD_DIM), _F32)
    lat_kv_spec = pl.BlockSpec((tm, D_KV), lambda i: (jnp.minimum(i, nl_blocks), 0))
    lat_kv_shape = jax.ShapeDtypeStruct((n_lat + tm, D_KV), _BF16)
    q, k_cache, v_cache, k_lat, v_lat, rest = pl.pallas_call(
        functools.partial(_proj_kernel, layer=layer, n_lat_blocks=nl_blocks),
        grid=(n // tm,),
        in_specs=in_specs,
        out_specs=[
            pl.BlockSpec((tm, D_ATTN), lambda i: (i, 0)),
            cache_spec, cache_spec, lat_kv_spec, lat_kv_spec,
            pl.BlockSpec((tm, D_REST), lambda i: (i, 0)),
        ],
        out_shape=[
            jax.ShapeDtypeStruct((n, D_ATTN), _BF16),
            cache_shape, cache_shape, lat_kv_shape, lat_kv_shape,
            jax.ShapeDtypeStruct((n, D_REST), _BF16),
        ],
        scratch_shapes=[
            pltpu.VMEM((D_MODEL, D_IN), _BF16),
            pltpu.VMEM((2, D_MODEL, PROJ_TN), _F32),
            pltpu.SemaphoreType.DMA((2,)),
        ],
        input_output_aliases=aliases,
        compiler_params=pltpu.CompilerParams(
            dimension_semantics=("arbitrary",), vmem_limit_bytes=PROJ_VMEM_LIMIT),
        name="proj",
    )(*args)
    return q, (k_cache, v_cache), (k_lat, v_lat), rest


def _softmax_pv_t(score_parts, sink_row, value_parts):
    m = sink_row
    for s in score_parts:
        m = jnp.maximum(m, jnp.max(s, axis=0, keepdims=True))
    acc = None
    for s, v_aug in zip(score_parts, value_parts):
        part = lax.dot_general(v_aug, jnp.exp2(s - m).astype(_BF16), (((0,), (0,)), ((), ())),
                               preferred_element_type=_F32)
        acc = part if acc is None else acc + part
    denom = acc[HEAD_DIM:HEAD_DIM + 1, :] + jnp.exp2(sink_row - m)
    return acc[:HEAD_DIM, :] / denom


def _softmax_pv(score_parts, sink_col, value_parts):
    m = sink_col
    for s in score_parts:
        m = jnp.maximum(m, jnp.max(s, axis=-1, keepdims=True))
    acc = None
    for s, v_aug in zip(score_parts, value_parts):
        part = jnp.dot(jnp.exp2(s - m).astype(_BF16), v_aug, preferred_element_type=_F32)
        acc = part if acc is None else acc + part
    denom = acc[:, HEAD_DIM:HEAD_DIM + 1] + jnp.exp2(sink_col - m)
    return acc[:, :HEAD_DIM] / denom


def _sink_column(sink_ref, kvh, rows):
    head = lax.broadcasted_iota(jnp.int32, (Q_GROUP * rows, 1), 0) // rows
    col = jnp.full((Q_GROUP * rows, 1), sink_ref[kvh * Q_GROUP], _F32)
    for g in range(1, Q_GROUP):
        col = jnp.where(head == g, sink_ref[kvh * Q_GROUP + g], col)
    return col * LOG2_E


def _with_ones(v_bf):
    return jnp.concatenate([v_bf, jnp.ones_like(v_bf)], axis=1)


def _sink_row(sink_ref, kvh, n_per_head):
    head = lax.broadcasted_iota(jnp.int32, (1, Q_GROUP * n_per_head), 1) // n_per_head
    row = jnp.full((1, Q_GROUP * n_per_head), sink_ref[kvh * Q_GROUP], _F32)
    for g in range(1, Q_GROUP):
        row = jnp.where(head == g, sink_ref[kvh * Q_GROUP + g], row)
    return row * LOG2_E


def _stack_heads(q_rows):
    return jnp.concatenate(
        [q_rows[:, g * HEAD_DIM:(g + 1) * HEAD_DIM] for g in range(Q_GROUP)], axis=0)


def _qk(q, k_bf):
    return lax.dot_general(q, k_bf, (((1,), (1,)), ((), ())), preferred_element_type=_F32)


def _ctx_attn_kernel(sink_ref, q_ref, k_ref, v_ref, o_ref, *, s_len):
    kvh = pl.program_id(1)
    sink_row = _sink_row(sink_ref, kvh, s_len)
    for bi in range(q_ref.shape[0] // s_len):
        rows = slice(bi * s_len, (bi + 1) * s_len)
        q = _stack_heads(q_ref[rows, :])
        k = k_ref[bi, _head_rows(kvh, s_len), :].astype(_BF16)
        v_aug = _with_ones(v_ref[bi, _head_rows(kvh, s_len), :].astype(_BF16))
        o_t = _softmax_pv_t([_qk(k, q)], sink_row, [v_aug])
        for g in range(Q_GROUP):
            o_ref[rows, g * HEAD_DIM:(g + 1) * HEAD_DIM] = (
                o_t[:, g * s_len:(g + 1) * s_len].T.astype(o_ref.dtype))


def _context_attention(q, q_row0, k_cache, v_cache, layer, sink, batch, s_len):
    n = batch * s_len
    tm = CTX_ATTN_SEQS * s_len
    gw = Q_GROUP * HEAD_DIM
    q_blk0 = q_row0 // tm
    kv_spec = pl.BlockSpec((CTX_ATTN_SEQS, None, N_KV_HEADS * s_len, HEAD_DIM), lambda b, h: (b, layer, 0, 0))
    return pl.pallas_call(
        functools.partial(_ctx_attn_kernel, s_len=s_len),
        grid=(n // tm, N_KV_HEADS),
        in_specs=[
            pl.BlockSpec(memory_space=pltpu.SMEM),
            pl.BlockSpec((tm, gw), lambda b, h: (q_blk0 + b, h)),
            kv_spec,
            kv_spec,
        ],
        out_specs=pl.BlockSpec((tm, gw), lambda b, h: (b, h)),
        out_shape=jax.ShapeDtypeStruct((n, D_ATTN), _BF16),
        compiler_params=pltpu.CompilerParams(dimension_semantics=("parallel", "parallel")),
        name="ctx_attention",
    )(sink, q, k_cache, v_cache)


def _lat_attn_kernel(sink_ref, q_ref, k_ref, v_ref, ck_ref, cv_ref, o_ref):
    kvh = pl.program_id(1)
    t_len = q_ref.shape[0]
    nb = t_len // Q_BLOCK
    past = ck_ref.shape[0] // N_KV_HEADS
    ck = ck_ref[_head_rows(kvh, past), :].astype(_BF16)
    cv_aug = _with_ones(cv_ref[_head_rows(kvh, past), :].astype(_BF16))
    v_aug = _with_ones(v_ref[...])
    sink_col = _sink_column(sink_ref, kvh, Q_BLOCK)
    rows = Q_GROUP * Q_BLOCK
    r = lax.broadcasted_iota(jnp.int32, (rows, Q_BLOCK), 0) % Q_BLOCK
    c = lax.broadcasted_iota(jnp.int32, (rows, Q_BLOCK), 1)
    for jb in range(nb):
        r0 = jb * Q_BLOCK
        q = _stack_heads(q_ref[r0:r0 + Q_BLOCK, :])
        lo = max(r0 - WINDOW, 0)
        hi = min(r0 + Q_BLOCK + WINDOW, t_len)
        s = _qk(q, k_ref[lo:hi, :])
        parts = []
        off = 0
        if jb > 0:
            parts.append(jnp.where(c >= r, s[:, :Q_BLOCK], NEG_INF))
            off = Q_BLOCK
        parts.append(s[:, off:off + Q_BLOCK])
        off += Q_BLOCK
        if jb < nb - 1:
            parts.append(jnp.where(c <= r, s[:, off:], NEG_INF))
        s_local = jnp.concatenate(parts, axis=1)
        o = _softmax_pv([s_local, _qk(q, ck)], sink_col, [v_aug[lo:hi], cv_aug])
        for g in range(Q_GROUP):
            o_ref[r0:r0 + Q_BLOCK, g * HEAD_DIM:(g + 1) * HEAD_DIM] = (
                o[g * Q_BLOCK:(g + 1) * Q_BLOCK].astype(o_ref.dtype))


def _latent_attention(q, q_row0, k, v, cache_k4, cache_v4, layer, sink, batch, t_len):
    n = batch * t_len
    gw = Q_GROUP * HEAD_DIM
    q_blk0 = q_row0 // t_len
    cache_spec = pl.BlockSpec((None, None, cache_k4.shape[2], HEAD_DIM), lambda b, h: (b, layer, 0, 0))
    return pl.pallas_call(
        _lat_attn_kernel,
        grid=(batch, N_KV_HEADS),
        in_specs=[
            pl.BlockSpec(memory_space=pltpu.SMEM),
            pl.BlockSpec((t_len, gw), lambda b, h: (q_blk0 + b, h)),
            pl.BlockSpec((t_len, HEAD_DIM), lambda b, h: (b, h)),
            pl.BlockSpec((t_len, HEAD_DIM), lambda b, h: (b, h)),
            cache_spec,
            cache_spec,
        ],
        out_specs=pl.BlockSpec((t_len, gw), lambda b, h: (b, h)),
        out_shape=jax.ShapeDtypeStruct((n, D_ATTN), _BF16),
        compiler_params=pltpu.CompilerParams(
            dimension_semantics=("parallel", "parallel"), vmem_limit_bytes=VMEM_LIMIT),
        name="latent_attention",
    )(sink, q, k, v, cache_k4, cache_v4)


def _shift_rows(x, d, t_len, offset=0):
    n = x.shape[0]
    row = (lax.broadcasted_iota(jnp.int32, x.shape, 0) + (n - offset)) % t_len
    rolled = pltpu.roll(x, d % n, axis=0)
    valid = (row >= d) if d > 0 else (row < t_len + d)
    return jnp.where(valid, rolled, 0.0)


def _sqrt_nonneg(z):
    return z * lax.rsqrt(jnp.maximum(z, F32_TINY))


def _rnn_kernel(x_ref, cw_ref, cb_ref, wg_ref, lam_ref, h0_ref, y_ref, hl_ref,
                af_scr, bf_scr, ab_scr, bb_scr, hf_scr, hb_scr, *, nb, nc):
    t_len = x_ref.shape[1]
    pitch = t_len + SLOT_PAD
    scr = ((af_scr, bf_scr), (ab_scr, bb_scr))
    per_group = SEQ_SLOTS // 2
    kb = min(per_group, max(1, PASS1_ROWS // t_len))

    def seq_of(slot):
        g, u = slot % 2, slot // 2
        return (g * per_group + u, 0) if nc == 1 else (u, g)

    for g in range(2):
        b_first, cl = seq_of(g)
        offset = SLOT_PAD * g
        lanes = slice(cl * LANES, (cl + 1) * LANES)
        cw = cw_ref[:, lanes]
        cb = cb_ref[:, lanes]
        wg = wg_ref[cl]
        lam = lam_ref[:, lanes]
        c_softplus = LRU_C * (jnp.maximum(-lam, 0.0) + jnp.log1p(jnp.exp(-jnp.abs(lam))))

        def per_chunk(it, carry, lanes=lanes, cw=cw, cb=cb, wg=wg,
                      c_softplus=c_softplus, g=g, b_first=b_first, offset=offset):
            n_rows = kb * t_len
            x = x_ref[pl.ds(b_first + it * kb, kb), :, lanes].astype(_F32).reshape(n_rows, LANES)
            if offset:
                x = pltpu.roll(x, offset, axis=0)
            xc = None
            for j in range(CONV_W):
                d = CONV_LEFT - j
                tap = cw[j:j + 1] * (_shift_rows(x, d, t_len, offset) if d else x)
                xc = tap if xc is None else xc + tap
            xc = xc + cb
            lane = lax.broadcasted_iota(jnp.int32, xc.shape, 1)
            bias_taps = jnp.where(lane < 2, 1.0, 0.0).astype(_BF16)
            lhs = jnp.concatenate([xc.astype(_BF16), bias_taps], axis=1)
            half_gates = jnp.dot(lhs, wg, preferred_element_type=_F32)
            half_xc = 0.5 * xc
            for d in range(2):
                t_r = jnp.tanh(half_gates[:, (2 * d) * LANES:(2 * d + 1) * LANES])
                t_i = jnp.tanh(half_gates[:, (2 * d + 1) * LANES:(2 * d + 2) * LANES])
                k = 0.5 * c_softplus[d:d + 1]
                neg_log_a = t_r * k + k
                a = jnp.exp2(neg_log_a * -LOG2_E)
                bterm = _sqrt_nonneg(jnp.tanh(neg_log_a) * (a * a + 1.0)) * ((t_i + 1.0) * half_xc)
                for jb in range(kb):
                    slot = 2 * (it * kb + jb) + g
                    row0 = pl.multiple_of(slot * pitch - offset, SUBLANES)
                    for dst, val in ((scr[d][0], a), (scr[d][1], bterm)):
                        dst[pl.ds(row0, t_len), :] = val[jb * t_len:(jb + 1) * t_len]
                        if offset:
                            w0 = ((jb + 1) * t_len) % n_rows
                            dst[pl.ds(row0 + t_len, SUBLANES), :] = val[w0:w0 + SUBLANES]
            return carry

        if per_group == kb:
            per_chunk(0, 0)
        else:
            lax.fori_loop(0, per_group // kb, per_chunk, 0)

    def slot_rows(ref3, d):
        rows = []
        for slot in range(SEQ_SLOTS):
            b, cl = seq_of(slot)
            rows.append(ref3[b, d:d + 1, cl * LANES:(cl + 1) * LANES])
        return jnp.concatenate(rows, axis=0)

    for slot in range(SEQ_SLOTS):
        tile = (slot * pitch + t_len) // SUBLANES * SUBLANES
        for h_scr in (hf_scr, hb_scr):
            h_scr[pl.ds(tile, SUBLANES), :] = jnp.zeros((SUBLANES, LANES), _F32)

    def advance2(a_scr, b_scr, h_scr, t0, t1, h):
        r0 = pl.ds(t0, SEQ_SLOTS, stride=pitch)
        r1 = pl.ds(t1, SEQ_SLOTS, stride=pitch)
        a0, b0, a1, b1 = a_scr[r0, :], b_scr[r0, :], a_scr[r1, :], b_scr[r1, :]
        h_scr[r0, :] = a0 * h + b0
        h = (a1 * a0) * h + (a1 * b0 + b1)
        h_scr[r1, :] = h
        return h

    hf, hb = slot_rows(h0_ref, 0), slot_rows(h0_ref, 1)
    for t in range(0, t_len, 2):
        hf = advance2(af_scr, bf_scr, hf_scr, t, t + 1, hf)
        hb = advance2(ab_scr, bb_scr, hb_scr, t_len - 1 - t, t_len - 2 - t, hb)

    for slot in range(SEQ_SLOTS):
        b, cl = seq_of(slot)
        lanes = slice(cl * LANES, (cl + 1) * LANES)
        hl_ref[b, 0:1, lanes] = hf[slot:slot + 1, :]
        hl_ref[b, 1:2, lanes] = hb[slot:slot + 1, :]
        if slot % 2 == 0:
            rows = pl.ds(slot * pitch, t_len)
            y = hf_scr[rows, :] + hb_scr[rows, :]
        else:
            rows = pl.ds(slot * pitch - SLOT_PAD, t_len + SUBLANES)
            y = pltpu.roll(hf_scr[rows, :] + hb_scr[rows, :], t_len + SUBLANES - SLOT_PAD, axis=0)[:t_len]
        y_ref[b, :, lanes] = y.astype(y_ref.dtype)


def _rnn(rest, row0, h0, conv_w, conv_b, wg, lam, batch, t_len):
    nb = min(batch, SEQ_SLOTS)
    blk0 = row0 // (nb * t_len)
    nc = SEQ_SLOTS // nb
    assert nc in (1, 2) and batch % nb == 0, "a grid step holds 8 batches x 1 or 4 batches x 2 channel blocks"
    cw = nc * LANES
    xr_block0 = XR_COL0 // cw
    scratch = pltpu.VMEM((SEQ_SLOTS * (t_len + SLOT_PAD), LANES), _F32)
    y, h_last = pl.pallas_call(
        functools.partial(_rnn_kernel, nb=nb, nc=nc),
        grid=(batch // nb, D_RNN // cw),
        in_specs=[
            pl.BlockSpec((nb, t_len, cw), lambda g, c: (blk0 + g, 0, xr_block0 + c)),
            pl.BlockSpec((CONV_W, cw), lambda g, c: (0, c)),
            pl.BlockSpec((1, cw), lambda g, c: (0, c)),
            pl.BlockSpec((nc, 2 * RNN_BLOCK, 4 * LANES), lambda g, c: (c, 0, 0)),
            pl.BlockSpec((2, cw), lambda g, c: (0, c)),
            pl.BlockSpec((nb, 2, cw), lambda g, c: (g, 0, c)),
        ],
        out_specs=[
            pl.BlockSpec((nb, t_len, cw), lambda g, c: (g, 0, c)),
            pl.BlockSpec((nb, 2, cw), lambda g, c: (g, 0, c)),
        ],
        out_shape=[
            jax.ShapeDtypeStruct((batch, t_len, D_RNN), _BF16),
            jax.ShapeDtypeStruct((batch, 2, D_RNN), _F32),
        ],
        scratch_shapes=[scratch] * 6,
        compiler_params=pltpu.CompilerParams(
            dimension_semantics=("parallel", "parallel"), vmem_limit_bytes=RNN_VMEM_LIMIT),
        name="rnn",
    )(rest.reshape(rest.shape[0] // t_len, t_len, D_REST), conv_w, conv_b.reshape(1, D_RNN), wg, lam, h0)
    return y.reshape(batch * t_len, D_RNN), h_last


def _out_kernel(x_ref, gate_ref, oa_ref, or_ref, gates_ref, wa_ref, wr_ref, wo_ref, g_ref, b_ref, o_ref):
    ga_ref, gr_ref, mga_ref, mgr_ref = (
        gates_ref.at[:, p * D_MODEL:(p + 1) * D_MODEL] for p in range(4))
    def silu_bf(x):
        h = 0.5 * x
        return h * jnp.tanh(h) + h

    def logistic_f32(x):
        return (0.5 * jnp.tanh(0.5 * x) + 0.5).astype(_F32)

    ya = jnp.dot(oa_ref[...] * silu_bf(ga_ref[...]), wa_ref[...], preferred_element_type=_F32)
    yr = jnp.dot(or_ref[...] * silu_bf(gr_ref[...]), wr_ref[...], preferred_element_type=_F32)
    merged = logistic_f32(mga_ref[...]) * ya + logistic_f32(mgr_ref[...]) * yr
    out = jnp.dot(merged.astype(_BF16), wo_ref[...], preferred_element_type=_F32)
    z = DEEPNORM_ALPHA * x_ref[...] + gate_ref[...] * out
    o_ref[...] = _layer_norm_rows(z) * g_ref[...] + b_ref[...]


def _merge_residual(x2, mod3, o_attn, o_rnn, rest, rest_row0, wa_bf, wr_bf, wo_bf, layer, ln_g, ln_b,
                    cond_row_of_block, tm):
    n = x2.shape[0]
    row_blk = pl.BlockSpec((tm, D_MODEL), lambda i: (i, 0))
    gates_blk = pl.BlockSpec((tm, D_GATES), lambda i: (rest_row0 // tm + i, 0))
    whole = pl.BlockSpec((None, D_MODEL, D_MODEL), lambda i: (layer, 0, 0))
    vec = pl.BlockSpec((1, D_MODEL), lambda i: (0, 0))
    return pl.pallas_call(
        _out_kernel,
        grid=(n // tm,),
        in_specs=[
            row_blk,
            pl.BlockSpec((None, 1, D_MODEL), lambda i: (cond_row_of_block(i), 0, 2)),
            row_blk, row_blk, gates_blk,
            whole, whole, whole, vec, vec,
        ],
        out_specs=row_blk,
        out_shape=jax.ShapeDtypeStruct((n, D_MODEL), _F32),
        compiler_params=pltpu.CompilerParams(
            dimension_semantics=("parallel",), vmem_limit_bytes=VMEM_LIMIT),
        name="merge_residual",
    )(x2, mod3, o_attn, o_rnn, rest, wa_bf, wr_bf, wo_bf,
      ln_g.reshape(1, D_MODEL), ln_b.reshape(1, D_MODEL))


def _rope_tables(t_len):
    quarter = HEAD_DIM // 4
    inv_freq = ROPE_BASE ** (-np.arange(quarter, dtype=np.float64) / quarter)
    pos = np.arange(t_len)
    ang_r = (pos // GRID_W).astype(np.float64)[:, None] * inv_freq[None, :]
    ang_c = (pos % GRID_W).astype(np.float64)[:, None] * inv_freq[None, :]
    cos = np.concatenate([np.cos(ang_r)] * 2 + [np.cos(ang_c)] * 2, axis=1)
    sin = np.concatenate([-np.sin(ang_r), np.sin(ang_r), -np.sin(ang_c), np.sin(ang_c)], axis=1)
    return cos.astype(np.float32), sin.astype(np.float32)


def _gate_weights(rg_wa, rg_ba, rg_wx, rg_bx):
    w = 0.5 * jnp.concatenate([rg_wa[0], rg_wx[0], rg_wa[1], rg_wx[1]], axis=-1)
    blk = lambda v: v.reshape(N_RNN_BLOCKS, 1, RNN_BLOCK)
    b = 0.5 * jnp.concatenate([blk(rg_ba[0]), blk(rg_bx[0]), blk(rg_ba[1]), blk(rg_bx[1])], axis=-1)
    b_hi = b.astype(_BF16)
    b_lo = (b - b_hi.astype(_F32)).astype(_BF16)
    pad = jnp.zeros((N_RNN_BLOCKS, RNN_BLOCK - 2, 4 * RNN_BLOCK), _BF16)
    return jnp.concatenate([w.astype(_BF16), b_hi, b_lo, pad], axis=1)


def kernel(x_prompt, x_sample, cache_k, cache_v, state_h, c, c_ctx, w_mod, b_mod, w_in,
           attn_sink, conv_w, conv_b, rg_wa, rg_ba, rg_wx, rg_bx, rg_lambda,
           w_br_attn, w_br_rnn, w_out, ln_g, ln_b):
    bp, s_len, _ = x_prompt.shape
    bd, t_len, _ = x_sample.shape
    past = cache_k.shape[2]

    cond8 = jnp.zeros((COND_ROWS, D_MODEL), _F32).at[0].set(c_ctx).at[1:1 + bd].set(c)
    mod = _modulation(cond8, w_mod, b_mod)

    wa_bf = w_br_attn.astype(_BF16)
    wr_bf = w_br_rnn.astype(_BF16)
    wo_bf = w_out.astype(_BF16)
    rope_tabs = _rope_tables(t_len)
    cache_k4 = cache_k.reshape(bd, DEPTH, past * N_KV_HEADS, HEAD_DIM)
    cache_v4 = cache_v.reshape(bd, DEPTH, past * N_KV_HEADS, HEAD_DIM)

    tm_proj, tm_out = 512, 512
    ctx_row = lambda i: 0
    lat_row_out = lambda i: 1 + (i * tm_out) // t_len
    n_lat = bd * t_len

    xp = x_prompt.reshape(bp * s_len, D_MODEL)
    xs = x_sample.reshape(n_lat, D_MODEL)
    h0_zero = jnp.zeros((bp, 2, D_RNN), _F32)
    hs = []
    kv_caches = None
    for l in range(DEPTH):
        mod3 = mod[l].reshape(COND_ROWS, 1, 3 * D_MODEL)
        wg = _gate_weights(rg_wa[l], rg_ba[l], rg_wx[l], rg_bx[l])
        q, kv_caches, (k_lat, v_lat), rest = _project(
            xs, xp, mod3, w_in, l, tm_proj, rope_tabs, (bd, t_len), (bp, s_len), kv_caches)

        o_attn = _context_attention(q, n_lat, kv_caches[0], kv_caches[1], l, attn_sink[l], bp, s_len)
        o_rnn, h_fin = _rnn(rest, n_lat, h0_zero, conv_w[l], conv_b[l], wg, rg_lambda[l], bp, s_len)
        xp = _merge_residual(xp, mod3, o_attn, o_rnn, rest, n_lat, wa_bf, wr_bf, wo_bf, l,
                             ln_g[l], ln_b[l], ctx_row, tm_out)
        hs.append(h_fin)

        o_attn = _latent_attention(q, 0, k_lat, v_lat, cache_k4, cache_v4, l, attn_sink[l], bd, t_len)
        o_rnn, _ = _rnn(rest, 0, state_h[:, l], conv_w[l], conv_b[l], wg, rg_lambda[l], bd, t_len)
        xs = _merge_residual(xs, mod3, o_attn, o_rnn, rest, 0, wa_bf, wr_bf, wo_bf, l,
                             ln_g[l], ln_b[l], lat_row_out, tm_out)

    y_prompt = xp.reshape(bp, s_len, D_MODEL)
    y_sample = xs.reshape(bd, t_len, D_MODEL)
    cache_shape = (bp, DEPTH, s_len, N_KV_HEADS, HEAD_DIM)
    new_k, new_v = (cache.reshape(cache_shape) for cache in kv_caches)
    return (y_prompt, y_sample, new_k, new_v, jnp.stack(hs, axis=1))
```

```python
import functools

import jax
import jax.numpy as jnp
import numpy as np
from jax import lax
from jax.experimental import pallas as pl
from jax.experimental.pallas import tpu as pltpu

D_MODEL = 1024
DEPTH = 2
GRID_W = 64
N_HEADS = 8
N_KV_HEADS = 2
HEAD_DIM = 128
Q_GROUP = N_HEADS // N_KV_HEADS
D_ATTN = N_HEADS * HEAD_DIM
D_KV = N_KV_HEADS * HEAD_DIM
WINDOW = 128
Q_BLOCK = 128
D_RNN = D_MODEL
N_RNN_BLOCKS = 8
RNN_BLOCK = D_RNN // N_RNN_BLOCKS
CONV_W = 4
CONV_LEFT = 2
LRU_C = 8.0
ROPE_BASE = 10000.0
D_IN = 2 * D_ATTN + 2 * D_KV + 2 * D_RNN + 2 * D_MODEL
DEEPNORM_ALPHA = (2 * DEPTH) ** 0.25
LN_EPS = 1e-6
NEG_INF = -1e30
ATTN_SCALE = HEAD_DIM ** -0.5

LANES = 128
SUBLANES = 8
SEQ_SLOTS = SUBLANES
SLOT_PAD = 4
CTX_ATTN_SEQS = 8
PASS1_ROWS = 1024
COND_ROWS = 8
PROJ_TN = 512
N_PROJ_BLOCKS = D_IN // PROJ_TN
D_REST = D_IN - D_ATTN - 2 * D_KV
_REST_COL = (0, 4, 1, 2, 3)
D_GATES = 4 * D_MODEL
XR_COL0 = 4 * D_MODEL
VMEM_LIMIT = 48 * 1024 * 1024
PROJ_VMEM_LIMIT = 56 * 1024 * 1024
RNN_VMEM_LIMIT = 56 * 1024 * 1024
F32_TINY = 1.1754943508222875e-38
LOG2_E = 1.4426950408889634
Q_PRESCALE = ATTN_SCALE * LOG2_E

_BF16 = jnp.bfloat16
_F32 = jnp.float32


def _silu(x):
    return x * jax.nn.sigmoid(x)


def _layer_norm_rows(x):
    mu = jnp.mean(x, axis=-1, keepdims=True)
    xc = x - mu
    var = jnp.mean(xc * xc, axis=-1, keepdims=True)
    return xc * lax.rsqrt(var + LN_EPS)


def _head_rows(head, n_rows):
    return pl.ds(head, n_rows, stride=N_KV_HEADS)


def _mod_kernel(cond_ref, w_ref, b_ref, o_ref):
    a = _silu(cond_ref[...]).astype(_BF16)
    o_ref[...] = jnp.dot(a, w_ref[...].astype(_BF16), preferred_element_type=_F32) + b_ref[...]


def _modulation(cond8, w_mod, b_mod):
    tn = D_MODEL
    return pl.pallas_call(
        _mod_kernel,
        grid=(DEPTH, 3 * D_MODEL // tn),
        in_specs=[
            pl.BlockSpec((COND_ROWS, D_MODEL), lambda l, j: (0, 0)),
            pl.BlockSpec((None, D_MODEL, tn), lambda l, j: (l, 0, j)),
            pl.BlockSpec((None, 1, tn), lambda l, j: (l, 0, j)),
        ],
        out_specs=pl.BlockSpec((None, COND_ROWS, tn), lambda l, j: (l, 0, j)),
        out_shape=jax.ShapeDtypeStruct((DEPTH, COND_ROWS, 3 * D_MODEL), _F32),
        compiler_params=pltpu.CompilerParams(dimension_semantics=("parallel", "parallel")),
        name="modulation",
    )(cond8, w_mod, b_mod.reshape(DEPTH, 1, 3 * D_MODEL))


def _rope(x, cos, sin_signed):
    quarter = HEAD_DIM // 4
    lane = lax.broadcasted_iota(jnp.int32, x.shape, 1)
    partner = jnp.where((lane // quarter) % 2 == 0,
                        pltpu.roll(x, HEAD_DIM - quarter, axis=1),
                        pltpu.roll(x, quarter, axis=1))
    return x * cos + partner * sin_signed


def _proj_kernel(xl_ref, xc_ref, shift_ref, scale_ref, w_hbm, cos_ref, sin_ref, *rest,
                 layer, n_lat_blocks):
    q_ref, kc_ref, vc_ref, kl_ref, vl_ref, r_ref, w_scr, stage_scr, sem = rest[-9:]
    i = pl.program_id(0)

    def chunk_copy(j):
        return pltpu.make_async_copy(
            w_hbm.at[layer, :, pl.ds(j * PROJ_TN, PROJ_TN)], stage_scr.at[j % 2], sem.at[j % 2])

    def fetch_chunk(j):
        if j == 0:
            chunk_copy(0).start()
        if j + 1 < N_PROJ_BLOCKS:
            chunk_copy(j + 1).start()
        chunk_copy(j).wait()
        w_scr[:, j * PROJ_TN:(j + 1) * PROJ_TN] = stage_scr[j % 2].astype(_BF16)

    x = jnp.where(i < n_lat_blocks, xl_ref[...], xc_ref[...])
    y = _layer_norm_rows(x)
    h = (y * (1.0 + scale_ref[...]) + shift_ref[...]).astype(_BF16)

    def rope(blk):
        return _rope(blk, cos_ref[...], sin_ref[...])

    def put_cache(ref, head, val):
        s_len = ref.shape[-2] // N_KV_HEADS
        for bi in range(ref.shape[0]):
            rows = val[bi * s_len:(bi + 1) * s_len].astype(ref.dtype)
            if len(ref.shape) == 3:
                ref[bi, _head_rows(head, s_len), :] = rows
            else:
                for l in range(ref.shape[1]):
                    ref[bi, l, _head_rows(head, s_len), :] = rows if l == layer else jnp.zeros_like(rows)

    def project_block(stream_weights):
        for j in range(N_PROJ_BLOCKS):
            c0 = j * PROJ_TN
            if stream_weights:
                fetch_chunk(j)
            acc = jnp.dot(h, w_scr[:, c0:c0 + PROJ_TN], preferred_element_type=_F32)
            if c0 < D_ATTN:
                for hh in range(PROJ_TN // HEAD_DIM):
                    sl = slice(hh * HEAD_DIM, (hh + 1) * HEAD_DIM)
                    q_ref[:, c0 + hh * HEAD_DIM:c0 + (hh + 1) * HEAD_DIM] = (
                        (rope(acc[:, sl]) * Q_PRESCALE).astype(q_ref.dtype))
            elif c0 == D_ATTN:
                for hh in range(N_KV_HEADS):
                    sl = slice(hh * HEAD_DIM, (hh + 1) * HEAD_DIM)
                    k_rot = rope(acc[:, sl])
                    put_cache(kc_ref, hh, k_rot)
                    put_cache(vc_ref, hh, acc[:, D_KV + hh * HEAD_DIM:D_KV + (hh + 1) * HEAD_DIM])
                    kl_ref[:, sl] = k_rot.astype(kl_ref.dtype)
                vl_ref[...] = acc[:, D_KV:].astype(vl_ref.dtype)
            else:
                off = c0 - D_ATTN - 2 * D_KV
                r0 = _REST_COL[off // D_MODEL] * D_MODEL + off % D_MODEL
                r_ref[:, r0:r0 + PROJ_TN] = acc.astype(r_ref.dtype)

    pl.when(i == 0)(functools.partial(project_block, True))
    pl.when(i != 0)(functools.partial(project_block, False))


def _project(x_lat, x_ctx, mod3, w_in, layer, tm, rope_tabs, lat_dims, ctx_dims, prev_caches):
    (bd, t_len), (bp, s_len) = lat_dims, ctx_dims
    n_lat, n_ctx = bd * t_len, bp * s_len
    n = n_lat + n_ctx
    nl_blocks = n_lat // tm
    is_lat = lambda i: i < nl_blocks
    ctx_blk = lambda i: jnp.maximum(i - nl_blocks, 0)
    cond_row = lambda i: jnp.where(is_lat(i), 1 + (i * tm) // t_len, 0)
    cos, sin = rope_tabs
    blocks_per_seq = t_len // tm
    cos = jnp.asarray(np.concatenate([np.ones((tm, HEAD_DIM), np.float32), cos], axis=0))
    sin = jnp.asarray(np.concatenate([np.zeros((tm, HEAD_DIM), np.float32), sin], axis=0))
    tab_blk = lambda i: jnp.where(is_lat(i), 1 + i % blocks_per_seq, 0)

    in_specs = [
        pl.BlockSpec((tm, D_MODEL), lambda i: (jnp.minimum(i, nl_blocks - 1), 0)),
        pl.BlockSpec((tm, D_MODEL), lambda i: (ctx_blk(i), 0)),
        pl.BlockSpec((None, 1, D_MODEL), lambda i: (cond_row(i), 0, 0)),
        pl.BlockSpec((None, 1, D_MODEL), lambda i: (cond_row(i), 0, 1)),
        pl.BlockSpec(memory_space=pl.ANY),
        pl.BlockSpec((tm, HEAD_DIM), lambda i: (tab_blk(i), 0)),
        pl.BlockSpec((tm, HEAD_DIM), lambda i: (tab_blk(i), 0)),
    ]
    args = [x_lat, x_ctx, mod3, mod3, w_in, cos, sin]
    aliases = {}
    seqs = tm // s_len
    cache_rows = N_KV_HEADS * s_len
    if prev_caches is None:
        cache_spec = pl.BlockSpec((seqs, DEPTH, cache_rows, HEAD_DIM), lambda i: (ctx_blk(i), 0, 0, 0))
    else:
        cache_spec = pl.BlockSpec((seqs, None, cache_rows, HEAD_DIM), lambda i: (ctx_blk(i), layer, 0, 0))
        aliases = {len(args): 1, len(args) + 1: 2}
        in_specs += [pl.BlockSpec(memory_space=pl.ANY)] * 2
        args += list(prev_caches)
    cache_shape = jax.ShapeDtypeStruct((bp, DEPTH, cache_rows, HEAD_DIM), _F32)
    lat_kv_spec = pl.BlockSpec((tm, D_KV), lambda i: (jnp.minimum(i, nl_blocks), 0))
    lat_kv_shape = jax.ShapeDtypeStruct((n_lat + tm, D_KV), _BF16)
    q, k_cache, v_cache, k_lat, v_lat, rest = pl.pallas_call(
        functools.partial(_proj_kernel, layer=layer, n_lat_blocks=nl_blocks),
        grid=(n // tm,),
        in_specs=in_specs,
        out_specs=[
            pl.BlockSpec((tm, D_ATTN), lambda i: (i, 0)),
            cache_spec, cache_spec, lat_kv_spec, lat_kv_spec,
            pl.BlockSpec((tm, D_REST), lambda i: (i, 0)),
        ],
        out_shape=[
            jax.ShapeDtypeStruct((n, D_ATTN), _BF16),
            cache_shape, cache_shape, lat_kv_shape, lat_kv_shape,
            jax.ShapeDtypeStruct((n, D_REST), _BF16),
        ],
        scratch_shapes=[
            pltpu.VMEM((D_MODEL, D_IN), _BF16),
            pltpu.VMEM((2, D_MODEL, PROJ_TN), _F32),
            pltpu.SemaphoreType.DMA((2,)),
        ],
        input_output_aliases=aliases,
        compiler_params=pltpu.CompilerParams(
            dimension_semantics=("arbitrary",), vmem_limit_bytes=PROJ_VMEM_LIMIT),
        name="proj",
    )(*args)
    return q, (k_cache, v_cache), (k_lat, v_lat), rest


def _softmax_pv_t(score_parts, sink_row, value_parts):
    m = sink_row
    for s in score_parts:
        m = jnp.maximum(m, jnp.max(s, axis=0, keepdims=True))
    acc = None
    for s, v_aug in zip(score_parts, value_parts):
        part = lax.dot_general(v_aug, jnp.exp2(s - m).astype(_BF16), (((0,), (0,)), ((), ())),
                               preferred_element_type=_F32)
        acc = part if acc is None else acc + part
    denom = acc[HEAD_DIM:HEAD_DIM + 1, :] + jnp.exp2(sink_row - m)
    return acc[:HEAD_DIM, :] / denom


def _softmax_pv(score_parts, sink_col, value_parts):
    m = sink_col
    for s in score_parts:
        m = jnp.maximum(m, jnp.max(s, axis=-1, keepdims=True))
    acc = None
    for s, v_aug in zip(score_parts, value_parts):
        part = jnp.dot(jnp.exp2(s - m).astype(_BF16), v_aug, preferred_element_type=_F32)
        acc = part if acc is None else acc + part
    denom = acc[:, HEAD_DIM:HEAD_DIM + 1] + jnp.exp2(sink_col - m)
    return acc[:, :HEAD_DIM] / denom


def _sink_column(sink_ref, kvh, rows):
    head = lax.broadcasted_iota(jnp.int32, (Q_GROUP * rows, 1), 0) // rows
    col = jnp.full((Q_GROUP * rows, 1), sink_ref[kvh * Q_GROUP], _F32)
    for g in range(1, Q_GROUP):
        col = jnp.where(head == g, sink_ref[kvh * Q_GROUP + g], col)
    return col * LOG2_E


def _with_ones(v_bf):
    return jnp.concatenate([v_bf, jnp.ones_like(v_bf)], axis=1)


def _sink_row(sink_ref, kvh, n_per_head):
    head = lax.broadcasted_iota(jnp.int32, (1, Q_GROUP * n_per_head), 1) // n_per_head
    row = jnp.full((1, Q_GROUP * n_per_head), sink_ref[kvh * Q_GROUP], _F32)
    for g in range(1, Q_GROUP):
        row = jnp.where(head == g, sink_ref[kvh * Q_GROUP + g], row)
    return row * LOG2_E


def _stack_heads(q_rows):
    return jnp.concatenate(
        [q_rows[:, g * HEAD_DIM:(g + 1) * HEAD_DIM] for g in range(Q_GROUP)], axis=0)


def _qk(q, k_bf):
    return lax.dot_general(q, k_bf, (((1,), (1,)), ((), ())), preferred_element_type=_F32)


def _ctx_attn_kernel(sink_ref, q_ref, k_ref, v_ref, o_ref, *, s_len):
    kvh = pl.program_id(1)
    sink_row = _sink_row(sink_ref, kvh, s_len)
    for bi in range(q_ref.shape[0] // s_len):
        rows = slice(bi * s_len, (bi + 1) * s_len)
        q = _stack_heads(q_ref[rows, :])
        k = k_ref[bi, _head_rows(kvh, s_len), :].astype(_BF16)
        v_aug = _with_ones(v_ref[bi, _head_rows(kvh, s_len), :].astype(_BF16))
        o_t = _softmax_pv_t([_qk(k, q)], sink_row, [v_aug])
        for g in range(Q_GROUP):
            o_ref[rows, g * HEAD_DIM:(g + 1) * HEAD_DIM] = (
                o_t[:, g * s_len:(g + 1) * s_len].T.astype(o_ref.dtype))


def _context_attention(q, q_row0, k_cache, v_cache, layer, sink, batch, s_len):
    n = batch * s_len
    tm = CTX_ATTN_SEQS * s_len
    gw = Q_GROUP * HEAD_DIM
    q_blk0 = q_row0 // tm
    kv_spec = pl.BlockSpec((CTX_ATTN_SEQS, None, N_KV_HEADS * s_len, HEAD_DIM), lambda b, h: (b, layer, 0, 0))
    return pl.pallas_call(
        functools.partial(_ctx_attn_kernel, s_len=s_len),
        grid=(n // tm, N_KV_HEADS),
        in_specs=[
            pl.BlockSpec(memory_space=pltpu.SMEM),
            pl.BlockSpec((tm, gw), lambda b, h: (q_blk0 + b, h)),
            kv_spec,
            kv_spec,
        ],
        out_specs=pl.BlockSpec((tm, gw), lambda b, h: (b, h)),
        out_shape=jax.ShapeDtypeStruct((n, D_ATTN), _BF16),
        compiler_params=pltpu.CompilerParams(dimension_semantics=("parallel", "parallel")),
        name="ctx_attention",
    )(sink, q, k_cache, v_cache)


def _lat_attn_kernel(sink_ref, q_ref, k_ref, v_ref, ck_ref, cv_ref, o_ref):
    kvh = pl.program_id(1)
    t_len = q_ref.shape[0]
    nb = t_len // Q_BLOCK
    past = ck_ref.shape[0] // N_KV_HEADS
    ck = ck_ref[_head_rows(kvh, past), :].astype(_BF16)
    cv_aug = _with_ones(cv_ref[_head_rows(kvh, past), :].astype(_BF16))
    v_aug = _with_ones(v_ref[...])
    sink_col = _sink_column(sink_ref, kvh, Q_BLOCK)
    rows = Q_GROUP * Q_BLOCK
    r = lax.broadcasted_iota(jnp.int32, (rows, Q_BLOCK), 0) % Q_BLOCK
    c = lax.broadcasted_iota(jnp.int32, (rows, Q_BLOCK), 1)
    for jb in range(nb):
        r0 = jb * Q_BLOCK
        q = _stack_heads(q_ref[r0:r0 + Q_BLOCK, :])
        lo = max(r0 - WINDOW, 0)
        hi = min(r0 + Q_BLOCK + WINDOW, t_len)
        s = _qk(q, k_ref[lo:hi, :])
        parts = []
        off = 0
        if jb > 0:
            parts.append(jnp.where(c >= r, s[:, :Q_BLOCK], NEG_INF))
            off = Q_BLOCK
        parts.append(s[:, off:off + Q_BLOCK])
        off += Q_BLOCK
        if jb < nb - 1:
            parts.append(jnp.where(c <= r, s[:, off:], NEG_INF))
        s_local = jnp.concatenate(parts, axis=1)
        o = _softmax_pv([s_local, _qk(q, ck)], sink_col, [v_aug[lo:hi], cv_aug])
        for g in range(Q_GROUP):
            o_ref[r0:r0 + Q_BLOCK, g * HEAD_DIM:(g + 1) * HEAD_DIM] = (
                o[g * Q_BLOCK:(g + 1) * Q_BLOCK].astype(o_ref.dtype))


def _latent_attention(q, q_row0, k, v, cache_k4, cache_v4, layer, sink, batch, t_len):
    n = batch * t_len
    gw = Q_GROUP * HEAD_DIM
    q_blk0 = q_row0 // t_len
    cache_spec = pl.BlockSpec((None, None, cache_k4.shape[2], HEAD_DIM), lambda b, h: (b, layer, 0, 0))
    return pl.pallas_call(
        _lat_attn_kernel,
        grid=(batch, N_KV_HEADS),
        in_specs=[
            pl.BlockSpec(memory_space=pltpu.SMEM),
            pl.BlockSpec((t_len, gw), lambda b, h: (q_blk0 + b, h)),
            pl.BlockSpec((t_len, HEAD_DIM), lambda b, h: (b, h)),
            pl.BlockSpec((t_len, HEAD_DIM), lambda b, h: (b, h)),
            cache_spec,
            cache_spec,
        ],
        out_specs=pl.BlockSpec((t_len, gw), lambda b, h: (b, h)),
        out_shape=jax.ShapeDtypeStruct((n, D_ATTN), _BF16),
        compiler_params=pltpu.CompilerParams(
            dimension_semantics=("parallel", "parallel"), vmem_limit_bytes=VMEM_LIMIT),
        name="latent_attention",
    )(sink, q, k, v, cache_k4, cache_v4)


def _shift_rows(x, d, t_len, offset=0):
    n = x.shape[0]
    row = (lax.broadcasted_iota(jnp.int32, x.shape, 0) + (n - offset)) % t_len
    rolled = pltpu.roll(x, d % n, axis=0)
    valid = (row >= d) if d > 0 else (row < t_len + d)
    return jnp.where(valid, rolled, 0.0)


def _sqrt_nonneg(z):
    return z * lax.rsqrt(jnp.maximum(z, F32_TINY))


def _rnn_kernel(x_ref, cw_ref, cb_ref, wg_ref, lam_ref, h0_ref, y_ref, hl_ref,
                af_scr, bf_scr, ab_scr, bb_scr, hf_scr, hb_scr, *, nb, nc):
    t_len = x_ref.shape[1]
    pitch = t_len + SLOT_PAD
    scr = ((af_scr, bf_scr), (ab_scr, bb_scr))
    per_group = SEQ_SLOTS // 2
    kb = min(per_group, max(1, PASS1_ROWS // t_len))

    def seq_of(slot):
        g, u = slot % 2, slot // 2
        return (g * per_group + u, 0) if nc == 1 else (u, g)

    for g in range(2):
        b_first, cl = seq_of(g)
        offset = SLOT_PAD * g
        lanes = slice(cl * LANES, (cl + 1) * LANES)
        cw = 0.5 * cw_ref[:, lanes]
        cb = 0.5 * cb_ref[:, lanes]
        wg = wg_ref[cl]
        lam = lam_ref[:, lanes]
        c_softplus = LRU_C * (jnp.maximum(-lam, 0.0) + jnp.log1p(jnp.exp(-jnp.abs(lam))))

        def per_chunk(it, carry, lanes=lanes, cw=cw, cb=cb, wg=wg,
                      c_softplus=c_softplus, g=g, b_first=b_first, offset=offset):
            n_rows = kb * t_len
            x = x_ref[pl.ds(b_first + it * kb, kb), :, lanes].astype(_F32).reshape(n_rows, LANES)
            if offset:
                x = pltpu.roll(x, offset, axis=0)
            half_xc = None
            for j in range(CONV_W):
                d = CONV_LEFT - j
                tap = cw[j:j + 1] * (_shift_rows(x, d, t_len, offset) if d else x)
                half_xc = tap if half_xc is None else half_xc + tap
            half_xc = half_xc + cb
            lane = lax.broadcasted_iota(jnp.int32, half_xc.shape, 1)
            bias_taps = jnp.where(lane < 2, 1.0, 0.0).astype(_BF16)
            lhs = jnp.concatenate([half_xc.astype(_BF16), bias_taps], axis=1)
            half_gates = jnp.dot(lhs, wg, preferred_element_type=_F32)
            for d in range(2):
                t_r = jnp.tanh(half_gates[:, (2 * d) * LANES:(2 * d + 1) * LANES])
                t_i = jnp.tanh(half_gates[:, (2 * d + 1) * LANES:(2 * d + 2) * LANES])
                k = 0.5 * c_softplus[d:d + 1]
                neg_log_a = t_r * k + k
                a = jnp.exp2(neg_log_a * -LOG2_E)
                bterm = _sqrt_nonneg(jnp.tanh(neg_log_a) * (a * a + 1.0)) * ((t_i + 1.0) * half_xc)
                for jb in range(kb):
                    slot = 2 * (it * kb + jb) + g
                    row0 = pl.multiple_of(slot * pitch - offset, SUBLANES)
                    for dst, val in ((scr[d][0], a), (scr[d][1], bterm)):
                        dst[pl.ds(row0, t_len), :] = val[jb * t_len:(jb + 1) * t_len]
                        if offset:
                            w0 = ((jb + 1) * t_len) % n_rows
                            dst[pl.ds(row0 + t_len, SUBLANES), :] = val[w0:w0 + SUBLANES]
            return carry

        if per_group == kb:
            per_chunk(0, 0)
        else:
            lax.fori_loop(0, per_group // kb, per_chunk, 0)

    def slot_rows(ref3, d):
        rows = []
        for slot in range(SEQ_SLOTS):
            b, cl = seq_of(slot)
            rows.append(ref3[b, d:d + 1, cl * LANES:(cl + 1) * LANES])
        return jnp.concatenate(rows, axis=0)

    for slot in range(SEQ_SLOTS):
        tile = (slot * pitch + t_len) // SUBLANES * SUBLANES
        for h_scr in (hf_scr, hb_scr):
            h_scr[pl.ds(tile, SUBLANES), :] = jnp.zeros((SUBLANES, LANES), _F32)

    def advance2(a_scr, b_scr, h_scr, t0, t1, h):
        r0 = pl.ds(t0, SEQ_SLOTS, stride=pitch)
        r1 = pl.ds(t1, SEQ_SLOTS, stride=pitch)
        a0, b0, a1, b1 = a_scr[r0, :], b_scr[r0, :], a_scr[r1, :], b_scr[r1, :]
        h_scr[r0, :] = a0 * h + b0
        h = (a1 * a0) * h + (a1 * b0 + b1)
        h_scr[r1, :] = h
        return h

    hf, hb = slot_rows(h0_ref, 0), slot_rows(h0_ref, 1)
    for t in range(0, t_len, 2):
        hf = advance2(af_scr, bf_scr, hf_scr, t, t + 1, hf)
        hb = advance2(ab_scr, bb_scr, hb_scr, t_len - 1 - t, t_len - 2 - t, hb)

    for slot in range(SEQ_SLOTS):
        b, cl = seq_of(slot)
        lanes = slice(cl * LANES, (cl + 1) * LANES)
        hl_ref[b, 0:1, lanes] = hf[slot:slot + 1, :]
        hl_ref[b, 1:2, lanes] = hb[slot:slot + 1, :]
        if slot % 2 == 0:
            rows = pl.ds(slot * pitch, t_len)
            y = hf_scr[rows, :] + hb_scr[rows, :]
        else:
            rows = pl.ds(slot * pitch - SLOT_PAD, t_len + SUBLANES)
            y = pltpu.roll(hf_scr[rows, :] + hb_scr[rows, :], t_len + SUBLANES - SLOT_PAD, axis=0)[:t_len]
        y_ref[b, :, lanes] = y.astype(y_ref.dtype)


def _rnn(rest, row0, h0, conv_w, conv_b, wg, lam, batch, t_len):
    nb = min(batch, SEQ_SLOTS)
    blk0 = row0 // (nb * t_len)
    nc = SEQ_SLOTS // nb
    assert nc in (1, 2) and batch % nb == 0, "a grid step holds 8 batches x 1 or 4 batches x 2 channel blocks"
    cw = nc * LANES
    xr_block0 = XR_COL0 // cw
    scratch = pltpu.VMEM((SEQ_SLOTS * (t_len + SLOT_PAD), LANES), _F32)
    y, h_last = pl.pallas_call(
        functools.partial(_rnn_kernel, nb=nb, nc=nc),
        grid=(batch // nb, D_RNN // cw),
        in_specs=[
            pl.BlockSpec((nb, t_len, cw), lambda g, c: (blk0 + g, 0, xr_block0 + c)),
            pl.BlockSpec((CONV_W, cw), lambda g, c: (0, c)),
            pl.BlockSpec((1, cw), lambda g, c: (0, c)),
            pl.BlockSpec((nc, 2 * RNN_BLOCK, 4 * LANES), lambda g, c: (c, 0, 0)),
            pl.BlockSpec((2, cw), lambda g, c: (0, c)),
            pl.BlockSpec((nb, 2, cw), lambda g, c: (g, 0, c)),
        ],
        out_specs=[
            pl.BlockSpec((nb, t_len, cw), lambda g, c: (g, 0, c)),
            pl.BlockSpec((nb, 2, cw), lambda g, c: (g, 0, c)),
        ],
        out_shape=[
            jax.ShapeDtypeStruct((batch, t_len, D_RNN), _BF16),
            jax.ShapeDtypeStruct((batch, 2, D_RNN), _F32),
        ],
        scratch_shapes=[scratch] * 6,
        compiler_params=pltpu.CompilerParams(
            dimension_semantics=("parallel", "parallel"), vmem_limit_bytes=RNN_VMEM_LIMIT),
        name="rnn",
    )(rest.reshape(rest.shape[0] // t_len, t_len, D_REST), conv_w, conv_b.reshape(1, D_RNN), wg, lam, h0)
    return y.reshape(batch * t_len, D_RNN), h_last


def _out_kernel(x_ref, gate_ref, oa_ref, or_ref, gates_ref, wa_ref, wr_ref, wo_ref, g_ref, b_ref, o_ref):
    ga_ref, gr_ref, mga_ref, mgr_ref = (
        gates_ref.at[:, p * D_MODEL:(p + 1) * D_MODEL] for p in range(4))
    def silu_bf(x):
        h = 0.5 * x
        return h * jnp.tanh(h) + h

    def logistic_f32(x):
        return (0.5 * jnp.tanh(0.5 * x) + 0.5).astype(_F32)

    ya = jnp.dot(oa_ref[...] * silu_bf(ga_ref[...]), wa_ref[...], preferred_element_type=_F32)
    yr = jnp.dot(or_ref[...] * silu_bf(gr_ref[...]), wr_ref[...], preferred_element_type=_F32)
    merged = logistic_f32(mga_ref[...]) * ya + logistic_f32(mgr_ref[...]) * yr
    out = jnp.dot(merged.astype(_BF16), wo_ref[...], preferred_element_type=_F32)
    z = DEEPNORM_ALPHA * x_ref[...] + gate_ref[...] * out
    o_ref[...] = _layer_norm_rows(z) * g_ref[...] + b_ref[...]


def _merge_residual(x2, mod3, o_attn, o_rnn, rest, rest_row0, wa_bf, wr_bf, wo_bf, layer, ln_g, ln_b,
                    cond_row_of_block, tm):
    n = x2.shape[0]
    row_blk = pl.BlockSpec((tm, D_MODEL), lambda i: (i, 0))
    gates_blk = pl.BlockSpec((tm, D_GATES), lambda i: (rest_row0 // tm + i, 0))
    whole = pl.BlockSpec((None, D_MODEL, D_MODEL), lambda i: (layer, 0, 0))
    vec = pl.BlockSpec((1, D_MODEL), lambda i: (0, 0))
    return pl.pallas_call(
        _out_kernel,
        grid=(n // tm,),
        in_specs=[
            row_blk,
            pl.BlockSpec((None, 1, D_MODEL), lambda i: (cond_row_of_block(i), 0, 2)),
            row_blk, row_blk, gates_blk,
            whole, whole, whole, vec, vec,
        ],
        out_specs=row_blk,
        out_shape=jax.ShapeDtypeStruct((n, D_MODEL), _F32),
        compiler_params=pltpu.CompilerParams(
            dimension_semantics=("parallel",), vmem_limit_bytes=VMEM_LIMIT),
        name="merge_residual",
    )(x2, mod3, o_attn, o_rnn, rest, wa_bf, wr_bf, wo_bf,
      ln_g.reshape(1, D_MODEL), ln_b.reshape(1, D_MODEL))


def _rope_tables(t_len):
    quarter = HEAD_DIM // 4
    inv_freq = ROPE_BASE ** (-np.arange(quarter, dtype=np.float64) / quarter)
    pos = np.arange(t_len)
    ang_r = (pos // GRID_W).astype(np.float64)[:, None] * inv_freq[None, :]
    ang_c = (pos % GRID_W).astype(np.float64)[:, None] * inv_freq[None, :]
    cos = np.concatenate([np.cos(ang_r)] * 2 + [np.cos(ang_c)] * 2, axis=1)
    sin = np.concatenate([-np.sin(ang_r), np.sin(ang_r), -np.sin(ang_c), np.sin(ang_c)], axis=1)
    return cos.astype(np.float32), sin.astype(np.float32)


def _gate_weights(rg_wa, rg_ba, rg_wx, rg_bx):
    w = jnp.concatenate([rg_wa[0], rg_wx[0], rg_wa[1], rg_wx[1]], axis=-1)
    blk = lambda v: v.reshape(N_RNN_BLOCKS, 1, RNN_BLOCK)
    b = 0.5 * jnp.concatenate([blk(rg_ba[0]), blk(rg_bx[0]), blk(rg_ba[1]), blk(rg_bx[1])], axis=-1)
    b_hi = b.astype(_BF16)
    b_lo = (b - b_hi.astype(_F32)).astype(_BF16)
    pad = jnp.zeros((N_RNN_BLOCKS, RNN_BLOCK - 2, 4 * RNN_BLOCK), _BF16)
    return jnp.concatenate([w.astype(_BF16), b_hi, b_lo, pad], axis=1)


def kernel(x_prompt, x_sample, cache_k, cache_v, state_h, c, c_ctx, w_mod, b_mod, w_in,
           attn_sink, conv_w, conv_b, rg_wa, rg_ba, rg_wx, rg_bx, rg_lambda,
           w_br_attn, w_br_rnn, w_out, ln_g, ln_b):
    bp, s_len, _ = x_prompt.shape
    bd, t_len, _ = x_sample.shape
    past = cache_k.shape[2]

    cond8 = jnp.zeros((COND_ROWS, D_MODEL), _F32).at[0].set(c_ctx).at[1:1 + bd].set(c)
    mod = _modulation(cond8, w_mod, b_mod)

    wa_bf = w_br_attn.astype(_BF16)
    wr_bf = w_br_rnn.astype(_BF16)
    wo_bf = w_out.astype(_BF16)
    rope_tabs = _rope_tables(t_len)
    cache_k4 = cache_k.reshape(bd, DEPTH, past * N_KV_HEADS, HEAD_DIM)
    cache_v4 = cache_v.reshape(bd, DEPTH, past * N_KV_HEADS, HEAD_DIM)

    tm_proj, tm_out = 512, 512
    ctx_row = lambda i: 0
    lat_row_out = lambda i: 1 + (i * tm_out) // t_len
    n_lat = bd * t_len

    xp = x_prompt.reshape(bp * s_len, D_MODEL)
    xs = x_sample.reshape(n_lat, D_MODEL)
    h0_zero = jnp.zeros((bp, 2, D_RNN), _F32)
    hs = []
    kv_caches = None
    for l in range(DEPTH):
        mod3 = mod[l].reshape(COND_ROWS, 1, 3 * D_MODEL)
        wg = _gate_weights(rg_wa[l], rg_ba[l], rg_wx[l], rg_bx[l])
        q, kv_caches, (k_lat, v_lat), rest = _project(
            xs, xp, mod3, w_in, l, tm_proj, rope_tabs, (bd, t_len), (bp, s_len), kv_caches)

        o_attn = _context_attention(q, n_lat, kv_caches[0], kv_caches[1], l, attn_sink[l], bp, s_len)
        o_rnn, h_fin = _rnn(rest, n_lat, h0_zero, conv_w[l], conv_b[l], wg, rg_lambda[l], bp, s_len)
        xp = _merge_residual(xp, mod3, o_attn, o_rnn, rest, n_lat, wa_bf, wr_bf, wo_bf, l,
                             ln_g[l], ln_b[l], ctx_row, tm_out)
        hs.append(h_fin)

        o_attn = _latent_attention(q, 0, k_lat, v_lat, cache_k4, cache_v4, l, attn_sink[l], bd, t_len)
        o_rnn, _ = _rnn(rest, 0, state_h[:, l], conv_w[l], conv_b[l], wg, rg_lambda[l], bd, t_len)
        xs = _merge_residual(xs, mod3, o_attn, o_rnn, rest, 0, wa_bf, wr_bf, wo_bf, l,
                             ln_g[l], ln_b[l], lat_row_out, tm_out)

    y_prompt = xp.reshape(bp, s_len, D_MODEL)
    y_sample = xs.reshape(bd, t_len, D_MODEL)
    cache_shape = (bp, DEPTH, s_len, N_KV_HEADS, HEAD_DIM)
    new_k, new_v = (cache.reshape(cache_shape) for cache in kv_caches)
    return (y_prompt, y_sample, new_k, new_v, jnp.stack(hs, axis=1))
```

```python
import functools

import jax
import jax.numpy as jnp
import numpy as np
from jax import lax
from jax.experimental import pallas as pl
from jax.experimental.pallas import tpu as pltpu

D_MODEL = 1024
DEPTH = 2
GRID_W = 64
N_HEADS = 8
N_KV_HEADS = 2
HEAD_DIM = 128
Q_GROUP = N_HEADS // N_KV_HEADS
D_ATTN = N_HEADS * HEAD_DIM
D_KV = N_KV_HEADS * HEAD_DIM
WINDOW = 128
Q_BLOCK = 128
D_RNN = D_MODEL
N_RNN_BLOCKS = 8
RNN_BLOCK = D_RNN // N_RNN_BLOCKS
CONV_W = 4
CONV_LEFT = 2
LRU_C = 8.0
ROPE_BASE = 10000.0
D_IN = 2 * D_ATTN + 2 * D_KV + 2 * D_RNN + 2 * D_MODEL
DEEPNORM_ALPHA = (2 * DEPTH) ** 0.25
LN_EPS = 1e-6
NEG_INF = -1e30
ATTN_SCALE = HEAD_DIM ** -0.5

LANES = 128
SUBLANES = 8
SEQ_SLOTS = SUBLANES
SLOT_PAD = 4
CTX_ATTN_SEQS = 8
PASS1_ROWS = 1024
COND_ROWS = 8
PROJ_TN = 512
N_PROJ_BLOCKS = D_IN // PROJ_TN
D_REST = D_IN - D_ATTN - 2 * D_KV
_REST_COL = (0, 4, 1, 2, 3)
D_GATES = 4 * D_MODEL
XR_COL0 = 4 * D_MODEL
VMEM_LIMIT = 48 * 1024 * 1024
PROJ_VMEM_LIMIT = 56 * 1024 * 1024
RNN_VMEM_LIMIT = 56 * 1024 * 1024
F32_TINY = 1.1754943508222875e-38
LOG2_E = 1.4426950408889634
Q_PRESCALE = ATTN_SCALE * LOG2_E

_BF16 = jnp.bfloat16
_F32 = jnp.float32


def _silu(x):
    return x * jax.nn.sigmoid(x)


def _layer_norm_rows(x):
    mu = jnp.mean(x, axis=-1, keepdims=True)
    xc = x - mu
    var = jnp.mean(xc * xc, axis=-1, keepdims=True)
    return xc * lax.rsqrt(var + LN_EPS)


def _head_rows(head, n_rows):
    return pl.ds(head, n_rows, stride=N_KV_HEADS)


def _mod_kernel(cond_ref, w_ref, b_ref, o_ref):
    a = _silu(cond_ref[...]).astype(_BF16)
    o_ref[...] = jnp.dot(a, w_ref[...].astype(_BF16), preferred_element_type=_F32) + b_ref[...]


def _modulation(cond8, w_mod, b_mod):
    tn = D_MODEL
    return pl.pallas_call(
        _mod_kernel,
        grid=(DEPTH, 3 * D_MODEL // tn),
        in_specs=[
            pl.BlockSpec((COND_ROWS, D_MODEL), lambda l, j: (0, 0)),
            pl.BlockSpec((None, D_MODEL, tn), lambda l, j: (l, 0, j)),
            pl.BlockSpec((None, 1, tn), lambda l, j: (l, 0, j)),
        ],
        out_specs=pl.BlockSpec((None, COND_ROWS, tn), lambda l, j: (l, 0, j)),
        out_shape=jax.ShapeDtypeStruct((DEPTH, COND_ROWS, 3 * D_MODEL), _F32),
        compiler_params=pltpu.CompilerParams(dimension_semantics=("parallel", "parallel")),
        name="modulation",
    )(cond8, w_mod, b_mod.reshape(DEPTH, 1, 3 * D_MODEL))


def _rope(x, cos, sin_signed):
    quarter = HEAD_DIM // 4
    lane = lax.broadcasted_iota(jnp.int32, x.shape, 1)
    partner = jnp.where((lane // quarter) % 2 == 0,
                        pltpu.roll(x, HEAD_DIM - quarter, axis=1),
                        pltpu.roll(x, quarter, axis=1))
    return x * cos + partner * sin_signed


def _proj_kernel(xl_ref, xc_ref, shift_ref, scale_ref, w_hbm, cos_ref, sin_ref, *rest,
                 layer, n_lat_blocks):
    q_ref, kc_ref, vc_ref, kl_ref, vl_ref, r_ref, w_scr, stage_scr, sem = rest[-9:]
    i = pl.program_id(0)

    def chunk_copy(j):
        return pltpu.make_async_copy(
            w_hbm.at[layer, :, pl.ds(j * PROJ_TN, PROJ_TN)], stage_scr.at[j % 2], sem.at[j % 2])

    def fetch_chunk(j):
        if j == 0:
            chunk_copy(0).start()
        if j + 1 < N_PROJ_BLOCKS:
            chunk_copy(j + 1).start()
        chunk_copy(j).wait()
        w_scr[:, j * PROJ_TN:(j + 1) * PROJ_TN] = stage_scr[j % 2].astype(_BF16)

    x = jnp.where(i < n_lat_blocks, xl_ref[...], xc_ref[...])
    y = _layer_norm_rows(x)
    h = (y * (1.0 + scale_ref[...]) + shift_ref[...]).astype(_BF16)

    def rope(blk):
        return _rope(blk, cos_ref[...], sin_ref[...])

    def put_cache(ref, head, val):
        s_len = ref.shape[-2] // N_KV_HEADS
        for bi in range(ref.shape[0]):
            rows = val[bi * s_len:(bi + 1) * s_len].astype(ref.dtype)
            if len(ref.shape) == 3:
                ref[bi, _head_rows(head, s_len), :] = rows
            else:
                for l in range(ref.shape[1]):
                    ref[bi, l, _head_rows(head, s_len), :] = rows if l == layer else jnp.zeros_like(rows)

    def project_block(stream_weights):
        for j in range(N_PROJ_BLOCKS):
            c0 = j * PROJ_TN
            if stream_weights:
                fetch_chunk(j)
            acc = jnp.dot(h, w_scr[:, c0:c0 + PROJ_TN], preferred_element_type=_F32)
            if c0 < D_ATTN:
                for hh in range(PROJ_TN // HEAD_DIM):
                    sl = slice(hh * HEAD_DIM, (hh + 1) * HEAD_DIM)
                    q_ref[:, c0 + hh * HEAD_DIM:c0 + (hh + 1) * HEAD_DIM] = (
                        (rope(acc[:, sl]) * Q_PRESCALE).astype(q_ref.dtype))
            elif c0 == D_ATTN:
                for hh in range(N_KV_HEADS):
                    sl = slice(hh * HEAD_DIM, (hh + 1) * HEAD_DIM)
                    k_rot = rope(acc[:, sl])
                    put_cache(kc_ref, hh, k_rot)
                    put_cache(vc_ref, hh, acc[:, D_KV + hh * HEAD_DIM:D_KV + (hh + 1) * HEAD_DIM])
                    kl_ref[:, sl] = k_rot.astype(kl_ref.dtype)
                vl_ref[...] = acc[:, D_KV:].astype(vl_ref.dtype)
            else:
                off = c0 - D_ATTN - 2 * D_KV
                r0 = _REST_COL[off // D_MODEL] * D_MODEL + off % D_MODEL
                r_ref[:, r0:r0 + PROJ_TN] = acc.astype(r_ref.dtype)

    pl.when(i == 0)(functools.partial(project_block, True))
    pl.when(i != 0)(functools.partial(project_block, False))


def _project(x_lat, x_ctx, mod3, w_in, layer, tm, rope_tabs, lat_dims, ctx_dims, prev_caches):
    (bd, t_len), (bp, s_len) = lat_dims, ctx_dims
    n_lat, n_ctx = bd * t_len, bp * s_len
    n = n_lat + n_ctx
    nl_blocks = n_lat // tm
    is_lat = lambda i: i < nl_blocks
    ctx_blk = lambda i: jnp.maximum(i - nl_blocks, 0)
    cond_row = lambda i: jnp.where(is_lat(i), 1 + (i * tm) // t_len, 0)
    cos, sin = rope_tabs
    blocks_per_seq = t_len // tm
    cos = jnp.asarray(np.concatenate([np.ones((tm, HEAD_DIM), np.float32), cos], axis=0))
    sin = jnp.asarray(np.concatenate([np.zeros((tm, HEAD_DIM), np.float32), sin], axis=0))
    tab_blk = lambda i: jnp.where(is_lat(i), 1 + i % blocks_per_seq, 0)

    in_specs = [
        pl.BlockSpec((tm, D_MODEL), lambda i: (jnp.minimum(i, nl_blocks - 1), 0)),
        pl.BlockSpec((tm, D_MODEL), lambda i: (ctx_blk(i), 0)),
        pl.BlockSpec((None, 1, D_MODEL), lambda i: (cond_row(i), 0, 0)),
        pl.BlockSpec((None, 1, D_MODEL), lambda i: (cond_row(i), 0, 1)),
        pl.BlockSpec(memory_space=pl.ANY),
        pl.BlockSpec((tm, HEAD_DIM), lambda i: (tab_blk(i), 0)),
        pl.BlockSpec((tm, HEAD_DIM), lambda i: (tab_blk(i), 0)),
    ]
    args = [x_lat, x_ctx, mod3, mod3, w_in, cos, sin]
    aliases = {}
    seqs = tm // s_len
    cache_rows = N_KV_HEADS * s_len
    if prev_caches is None:
        cache_spec = pl.BlockSpec((seqs, DEPTH, cache_rows, HEAD_DIM), lambda i: (ctx_blk(i), 0, 0, 0))
    else:
        cache_spec = pl.BlockSpec((seqs, None, cache_rows, HEAD_DIM), lambda i: (ctx_blk(i), layer, 0, 0))
        aliases = {len(args): 1, len(args) + 1: 2}
        in_specs += [pl.BlockSpec(memory_space=pl.ANY)] * 2
        args += list(prev_caches)
    cache_shape = jax.ShapeDtypeStruct((bp, DEPTH, cache_rows, HEAD_DIM), _F32)
    lat_kv_spec = pl.BlockSpec((tm, D_KV), lambda i: (jnp.minimum(i, nl_blocks), 0))
    lat_kv_shape = jax.ShapeDtypeStruct((n_lat + tm, D_KV), _BF16)
    q, k_cache, v_cache, k_lat, v_lat, rest = pl.pallas_call(
        functools.partial(_proj_kernel, layer=layer, n_lat_blocks=nl_blocks),
        grid=(n // tm,),
        in_specs=in_specs,
        out_specs=[
            pl.BlockSpec((tm, D_ATTN), lambda i: (i, 0)),
            cache_spec, cache_spec, lat_kv_spec, lat_kv_spec,
            pl.BlockSpec((tm, D_REST), lambda i: (i, 0)),
        ],
        out_shape=[
            jax.ShapeDtypeStruct((n, D_ATTN), _BF16),
            cache_shape, cache_shape, lat_kv_shape, lat_kv_shape,
            jax.ShapeDtypeStruct((n, D_REST), _BF16),
        ],
        scratch_shapes=[
            pltpu.VMEM((D_MODEL, D_IN), _BF16),
            pltpu.VMEM((2, D_MODEL, PROJ_TN), _F32),
            pltpu.SemaphoreType.DMA((2,)),
        ],
        input_output_aliases=aliases,
        compiler_params=pltpu.CompilerParams(
            dimension_semantics=("arbitrary",), vmem_limit_bytes=PROJ_VMEM_LIMIT),
        name="proj",
    )(*args)
    return q, (k_cache, v_cache), (k_lat, v_lat), rest


def _softmax_pv_t(score_parts, sink_row, value_parts):
    m = sink_row
    for s in score_parts:
        m = jnp.maximum(m, jnp.max(s, axis=0, keepdims=True))
    acc = None
    for s, v_aug in zip(score_parts, value_parts):
        part = lax.dot_general(v_aug, jnp.exp2(s - m).astype(_BF16), (((0,), (0,)), ((), ())),
                               preferred_element_type=_F32)
        acc = part if acc is None else acc + part
    denom = acc[HEAD_DIM:HEAD_DIM + 1, :] + jnp.exp2(sink_row - m)
    return acc[:HEAD_DIM, :] / denom


def _softmax_pv(score_parts, sink_col, value_parts):
    m = sink_col
    for s in score_parts:
        m = jnp.maximum(m, jnp.max(s, axis=-1, keepdims=True))
    acc = None
    for s, v_aug in zip(score_parts, value_parts):
        part = jnp.dot(jnp.exp2(s - m).astype(_BF16), v_aug, preferred_element_type=_F32)
        acc = part if acc is None else acc + part
    denom = acc[:, HEAD_DIM:HEAD_DIM + 1] + jnp.exp2(sink_col - m)
    return acc[:, :HEAD_DIM] / denom


def _sink_column(sink_ref, kvh, rows):
    head = lax.broadcasted_iota(jnp.int32, (Q_GROUP * rows, 1), 0) // rows
    col = jnp.full((Q_GROUP * rows, 1), sink_ref[kvh * Q_GROUP], _F32)
    for g in range(1, Q_GROUP):
        col = jnp.where(head == g, sink_ref[kvh * Q_GROUP + g], col)
    return col * LOG2_E


def _with_ones(v_bf):
    return jnp.concatenate([v_bf, jnp.ones_like(v_bf)], axis=1)


def _sink_row(sink_ref, kvh, n_per_head):
    head = lax.broadcasted_iota(jnp.int32, (1, Q_GROUP * n_per_head), 1) // n_per_head
    row = jnp.full((1, Q_GROUP * n_per_head), sink_ref[kvh * Q_GROUP], _F32)
    for g in range(1, Q_GROUP):
        row = jnp.where(head == g, sink_ref[kvh * Q_GROUP + g], row)
    return row * LOG2_E


def _stack_heads(q_rows):
    return jnp.concatenate(
        [q_rows[:, g * HEAD_DIM:(g + 1) * HEAD_DIM] for g in range(Q_GROUP)], axis=0)


def _qk(q, k_bf):
    return lax.dot_general(q, k_bf, (((1,), (1,)), ((), ())), preferred_element_type=_F32)


def _ctx_attn_kernel(sink_ref, q_ref, k_ref, v_ref, o_ref, *, s_len):
    kvh = pl.program_id(1)
    sink_row = _sink_row(sink_ref, kvh, s_len)
    for bi in range(q_ref.shape[0] // s_len):
        rows = slice(bi * s_len, (bi + 1) * s_len)
        q = _stack_heads(q_ref[rows, :])
        k = k_ref[bi, _head_rows(kvh, s_len), :].astype(_BF16)
        v_aug = _with_ones(v_ref[bi, _head_rows(kvh, s_len), :].astype(_BF16))
        o_t = _softmax_pv_t([_qk(k, q)], sink_row, [v_aug])
        for g in range(Q_GROUP):
            o_ref[rows, g * HEAD_DIM:(g + 1) * HEAD_DIM] = (
                o_t[:, g * s_len:(g + 1) * s_len].T.astype(o_ref.dtype))


def _context_attention(q, q_row0, k_cache, v_cache, layer, sink, batch, s_len):
    n = batch * s_len
    tm = CTX_ATTN_SEQS * s_len
    gw = Q_GROUP * HEAD_DIM
    q_blk0 = q_row0 // tm
    kv_spec = pl.BlockSpec((CTX_ATTN_SEQS, None, N_KV_HEADS * s_len, HEAD_DIM), lambda b, h: (b, layer, 0, 0))
    return pl.pallas_call(
        functools.partial(_ctx_attn_kernel, s_len=s_len),
        grid=(n // tm, N_KV_HEADS),
        in_specs=[
            pl.BlockSpec(memory_space=pltpu.SMEM),
            pl.BlockSpec((tm, gw), lambda b, h: (q_blk0 + b, h)),
            kv_spec,
            kv_spec,
        ],
        out_specs=pl.BlockSpec((tm, gw), lambda b, h: (b, h)),
        out_shape=jax.ShapeDtypeStruct((n, D_ATTN), _BF16),
        compiler_params=pltpu.CompilerParams(dimension_semantics=("parallel", "parallel")),
        name="ctx_attention",
    )(sink, q, k_cache, v_cache)


def _lat_attn_kernel(sink_ref, q_ref, k_ref, v_ref, ck_ref, cv_ref, o_ref):
    kvh = pl.program_id(1)
    t_len = q_ref.shape[0]
    nb = t_len // Q_BLOCK
    past = ck_ref.shape[0] // N_KV_HEADS
    ck = ck_ref[_head_rows(kvh, past), :].astype(_BF16)
    cv_aug = _with_ones(cv_ref[_head_rows(kvh, past), :].astype(_BF16))
    v_aug = _with_ones(v_ref[...])
    sink_col = _sink_column(sink_ref, kvh, Q_BLOCK)
    rows = Q_GROUP * Q_BLOCK
    r = lax.broadcasted_iota(jnp.int32, (rows, Q_BLOCK), 0) % Q_BLOCK
    c = lax.broadcasted_iota(jnp.int32, (rows, Q_BLOCK), 1)
    for jb in range(nb):
        r0 = jb * Q_BLOCK
        q = _stack_heads(q_ref[r0:r0 + Q_BLOCK, :])
        lo = max(r0 - WINDOW, 0)
        hi = min(r0 + Q_BLOCK + WINDOW, t_len)
        s = _qk(q, k_ref[lo:hi, :])
        parts = []
        off = 0
        if jb > 0:
            parts.append(jnp.where(c >= r, s[:, :Q_BLOCK], NEG_INF))
            off = Q_BLOCK
        parts.append(s[:, off:off + Q_BLOCK])
        off += Q_BLOCK
        if jb < nb - 1:
            parts.append(jnp.where(c <= r, s[:, off:], NEG_INF))
        s_local = jnp.concatenate(parts, axis=1)
        o = _softmax_pv([s_local, _qk(q, ck)], sink_col, [v_aug[lo:hi], cv_aug])
        for g in range(Q_GROUP):
            o_ref[r0:r0 + Q_BLOCK, g * HEAD_DIM:(g + 1) * HEAD_DIM] = (
                o[g * Q_BLOCK:(g + 1) * Q_BLOCK].astype(o_ref.dtype))


def _latent_attention(q, q_row0, k, v, cache_k4, cache_v4, layer, sink, batch, t_len):
    n = batch * t_len
    gw = Q_GROUP * HEAD_DIM
    q_blk0 = q_row0 // t_len
    cache_spec = pl.BlockSpec((None, None, cache_k4.shape[2], HEAD_DIM), lambda b, h: (b, layer, 0, 0))
    return pl.pallas_call(
        _lat_attn_kernel,
        grid=(batch, N_KV_HEADS),
        in_specs=[
            pl.BlockSpec(memory_space=pltpu.SMEM),
            pl.BlockSpec((t_len, gw), lambda b, h: (q_blk0 + b, h)),
            pl.BlockSpec((t_len, HEAD_DIM), lambda b, h: (b, h)),
            pl.BlockSpec((t_len, HEAD_DIM), lambda b, h: (b, h)),
            cache_spec,
            cache_spec,
        ],
        out_specs=pl.BlockSpec((t_len, gw), lambda b, h: (b, h)),
        out_shape=jax.ShapeDtypeStruct((n, D_ATTN), _BF16),
        compiler_params=pltpu.CompilerParams(
            dimension_semantics=("parallel", "parallel"), vmem_limit_bytes=VMEM_LIMIT),
        name="latent_attention",
    )(sink, q, k, v, cache_k4, cache_v4)


def _shift_rows(x, d, t_len, offset=0):
    n = x.shape[0]
    row = (lax.broadcasted_iota(jnp.int32, x.shape, 0) + (n - offset)) % t_len
    rolled = pltpu.roll(x, d % n, axis=0)
    valid = (row >= d) if d > 0 else (row < t_len + d)
    return jnp.where(valid, rolled, 0.0)


def _sqrt_nonneg(z):
    return z * lax.rsqrt(jnp.maximum(z, F32_TINY))


def _rnn_kernel(x_ref, cw_ref, cb_ref, wg_ref, lam_ref, h0_ref, y_ref, hl_ref,
                af_scr, bf_scr, ab_scr, bb_scr, hf_scr, hb_scr, *, nb, nc):
    t_len = x_ref.shape[1]
    pitch = t_len + SLOT_PAD
    scr = ((af_scr, bf_scr), (ab_scr, bb_scr))
    per_group = SEQ_SLOTS // 2
    kb = min(per_group, max(1, PASS1_ROWS // t_len))

    def seq_of(slot):
        g, u = slot % 2, slot // 2
        return (g * per_group + u, 0) if nc == 1 else (u, g)

    for g in range(2):
        b_first, cl = seq_of(g)
        offset = SLOT_PAD * g
        lanes = slice(cl * LANES, (cl + 1) * LANES)
        cw = 0.5 * cw_ref[:, lanes]
        cb = 0.5 * cb_ref[:, lanes]
        wg = wg_ref[cl]
        lam = lam_ref[:, lanes]
        c_softplus = LRU_C * (jnp.maximum(-lam, 0.0) + jnp.log1p(jnp.exp(-jnp.abs(lam))))

        def per_chunk(it, carry, lanes=lanes, cw=cw, cb=cb, wg=wg,
                      c_softplus=c_softplus, g=g, b_first=b_first, offset=offset):
            n_rows = kb * t_len
            x = x_ref[pl.ds(b_first + it * kb, kb), :, lanes].astype(_F32).reshape(n_rows, LANES)
            if offset:
                x = pltpu.roll(x, offset, axis=0)
            half_xc = None
            for j in range(CONV_W):
                d = CONV_LEFT - j
                tap = cw[j:j + 1] * (_shift_rows(x, d, t_len, offset) if d else x)
                half_xc = tap if half_xc is None else half_xc + tap
            half_xc = half_xc + cb
            lane = lax.broadcasted_iota(jnp.int32, half_xc.shape, 1)
            bias_taps = jnp.where(lane < 2, 1.0, 0.0).astype(_BF16)
            lhs = jnp.concatenate([half_xc.astype(_BF16), bias_taps], axis=1)
            half_gates = jnp.dot(lhs, wg, preferred_element_type=_F32)
            for d in range(2):
                t_r = jnp.tanh(half_gates[:, (2 * d) * LANES:(2 * d + 1) * LANES])
                t_i = jnp.tanh(half_gates[:, (2 * d + 1) * LANES:(2 * d + 2) * LANES])
                k = 0.5 * c_softplus[d:d + 1]
                neg_log_a = t_r * k + k
                a = jnp.exp2(neg_log_a * -LOG2_E)
                bterm = _sqrt_nonneg(jnp.tanh(neg_log_a) * (a * a + 1.0)) * ((t_i + 1.0) * half_xc)
                for jb in range(kb):
                    slot = 2 * (it * kb + jb) + g
                    row0 = pl.multiple_of(slot * pitch - offset, SUBLANES)
                    for dst, val in ((scr[d][0], a), (scr[d][1], bterm)):
                        dst[pl.ds(row0, t_len), :] = val[jb * t_len:(jb + 1) * t_len]
                        if offset:
                            w0 = ((jb + 1) * t_len) % n_rows
                            dst[pl.ds(row0 + t_len, SUBLANES), :] = val[w0:w0 + SUBLANES]
            return carry

        if per_group == kb:
            per_chunk(0, 0)
        else:
            lax.fori_loop(0, per_group // kb, per_chunk, 0)

    def slot_rows(ref3, d):
        rows = []
        for slot in range(SEQ_SLOTS):
            b, cl = seq_of(slot)
            rows.append(ref3[b, d:d + 1, cl * LANES:(cl + 1) * LANES])
        return jnp.concatenate(rows, axis=0)

    for slot in range(SEQ_SLOTS):
        tile = (slot * pitch + t_len) // SUBLANES * SUBLANES
        for h_scr in (hf_scr, hb_scr):
            h_scr[pl.ds(tile, SUBLANES), :] = jnp.zeros((SUBLANES, LANES), _F32)

    def advance2(a_scr, b_scr, h_scr, t0, t1, h):
        r0 = pl.ds(t0, SEQ_SLOTS, stride=pitch)
        r1 = pl.ds(t1, SEQ_SLOTS, stride=pitch)
        a0, b0, a1, b1 = a_scr[r0, :], b_scr[r0, :], a_scr[r1, :], b_scr[r1, :]
        h_scr[r0, :] = a0 * h + b0
        h = (a1 * a0) * h + (a1 * b0 + b1)
        h_scr[r1, :] = h
        return h

    hf, hb = slot_rows(h0_ref, 0), slot_rows(h0_ref, 1)
    for t in range(0, t_len, 2):
        hf = advance2(af_scr, bf_scr, hf_scr, t, t + 1, hf)
        hb = advance2(ab_scr, bb_scr, hb_scr, t_len - 1 - t, t_len - 2 - t, hb)

    for slot in range(SEQ_SLOTS):
        b, cl = seq_of(slot)
        lanes = slice(cl * LANES, (cl + 1) * LANES)
        hl_ref[b, 0:1, lanes] = hf[slot:slot + 1, :]
        hl_ref[b, 1:2, lanes] = hb[slot:slot + 1, :]
        if slot % 2 == 0:
            rows = pl.ds(slot * pitch, t_len)
            y = hf_scr[rows, :] + hb_scr[rows, :]
        else:
            rows = pl.ds(slot * pitch - SLOT_PAD, t_len + SUBLANES)
            y = pltpu.roll(hf_scr[rows, :] + hb_scr[rows, :], t_len + SUBLANES - SLOT_PAD, axis=0)[:t_len]
        y_ref[b, :, lanes] = y.astype(y_ref.dtype)


def _rnn(rest, row0, h0, conv_w, conv_b, wg, lam, batch, t_len):
    nb = min(batch, SEQ_SLOTS)
    blk0 = row0 // (nb * t_len)
    nc = SEQ_SLOTS // nb
    assert nc in (1, 2) and batch % nb == 0, "a grid step holds 8 batches x 1 or 4 batches x 2 channel blocks"
    cw = nc * LANES
    xr_block0 = XR_COL0 // cw
    scratch = pltpu.VMEM((SEQ_SLOTS * (t_len + SLOT_PAD), LANES), _F32)
    y, h_last = pl.pallas_call(
        functools.partial(_rnn_kernel, nb=nb, nc=nc),
        grid=(batch // nb, D_RNN // cw),
        in_specs=[
            pl.BlockSpec((nb, t_len, cw), lambda g, c: (blk0 + g, 0, xr_block0 + c)),
            pl.BlockSpec((CONV_W, cw), lambda g, c: (0, c)),
            pl.BlockSpec((1, cw), lambda g, c: (0, c)),
            pl.BlockSpec((nc, 2 * RNN_BLOCK, 4 * LANES), lambda g, c: (c, 0, 0)),
            pl.BlockSpec((2, cw), lambda g, c: (0, c)),
            pl.BlockSpec((nb, 2, cw), lambda g, c: (g, 0, c)),
        ],
        out_specs=[
            pl.BlockSpec((nb, t_len, cw), lambda g, c: (g, 0, c)),
            pl.BlockSpec((nb, 2, cw), lambda g, c: (g, 0, c)),
        ],
        out_shape=[
            jax.ShapeDtypeStruct((batch, t_len, D_RNN), _BF16),
            jax.ShapeDtypeStruct((batch, 2, D_RNN), _F32),
        ],
        scratch_shapes=[scratch] * 6,
        compiler_params=pltpu.CompilerParams(
            dimension_semantics=("parallel", "parallel"), vmem_limit_bytes=RNN_VMEM_LIMIT),
        name="rnn",
    )(rest.reshape(rest.shape[0] // t_len, t_len, D_REST), conv_w, conv_b.reshape(1, D_RNN), wg, lam, h0)
    return y.reshape(batch * t_len, D_RNN), h_last


def _out_kernel(xl_ref, xc_ref, gate_ref, oal_ref, oac_ref, orl_ref, orc_ref, gates_ref,
                wa_ref, wr_ref, wo_ref, g_ref, b_ref, ol_ref, oc_ref, *, n_lat_blocks):
    i = pl.program_id(0)

    @pl.when(i < n_lat_blocks)
    def _():
        _out_block(xl_ref, gate_ref, oal_ref, orl_ref, gates_ref, wa_ref, wr_ref, wo_ref, g_ref, b_ref, ol_ref)

    @pl.when(i >= n_lat_blocks)
    def _():
        _out_block(xc_ref, gate_ref, oac_ref, orc_ref, gates_ref, wa_ref, wr_ref, wo_ref, g_ref, b_ref, oc_ref)


def _out_block(x_ref, gate_ref, oa_ref, or_ref, gates_ref, wa_ref, wr_ref, wo_ref, g_ref, b_ref, o_ref):
    ga_ref, gr_ref, mga_ref, mgr_ref = (
        gates_ref.at[:, p * D_MODEL:(p + 1) * D_MODEL] for p in range(4))
    def silu_bf(x):
        h = 0.5 * x
        return h * jnp.tanh(h) + h

    def logistic_f32(x):
        return (0.5 * jnp.tanh(0.5 * x) + 0.5).astype(_F32)

    ya = jnp.dot(oa_ref[...] * silu_bf(ga_ref[...]), wa_ref[...], preferred_element_type=_F32)
    yr = jnp.dot(or_ref[...] * silu_bf(gr_ref[...]), wr_ref[...], preferred_element_type=_F32)
    merged = logistic_f32(mga_ref[...]) * ya + logistic_f32(mgr_ref[...]) * yr
    out = jnp.dot(merged.astype(_BF16), wo_ref[...], preferred_element_type=_F32)
    z = DEEPNORM_ALPHA * x_ref[...] + gate_ref[...] * out
    o_ref[...] = _layer_norm_rows(z) * g_ref[...] + b_ref[...]


def _merge_residual(x_lat, x_ctx, mod3, oa_lat, oa_ctx, or_lat, or_ctx, rest, wa_bf, wr_bf, wo_bf,
                    layer, ln_g, ln_b, t_len, tm):
    n_lat, n_ctx = x_lat.shape[0], x_ctx.shape[0]
    nl, nc = n_lat // tm, n_ctx // tm
    lat_blk = pl.BlockSpec((tm, D_MODEL), lambda i: (jnp.minimum(i, nl - 1), 0))
    ctx_blk = pl.BlockSpec((tm, D_MODEL), lambda i: (jnp.maximum(i - nl, 0), 0))
    gates_blk = pl.BlockSpec((tm, D_GATES), lambda i: (i, 0))
    whole = pl.BlockSpec((None, D_MODEL, D_MODEL), lambda i: (layer, 0, 0), pipeline_mode=pl.Buffered(1))
    vec = pl.BlockSpec((1, D_MODEL), lambda i: (0, 0))
    cond_row = lambda i: jnp.where(i < nl, 1 + (i * tm) // t_len, 0)
    return pl.pallas_call(
        functools.partial(_out_kernel, n_lat_blocks=nl),
        grid=(nl + nc,),
        in_specs=[
            lat_blk, ctx_blk,
            pl.BlockSpec((None, 1, D_MODEL), lambda i: (cond_row(i), 0, 2)),
            lat_blk, ctx_blk, lat_blk, ctx_blk, gates_blk,
            whole, whole, whole, vec, vec,
        ],
        out_specs=[lat_blk, ctx_blk],
        out_shape=[jax.ShapeDtypeStruct((n_lat, D_MODEL), _F32),
                   jax.ShapeDtypeStruct((n_ctx, D_MODEL), _F32)],
        compiler_params=pltpu.CompilerParams(
            dimension_semantics=("arbitrary",), vmem_limit_bytes=VMEM_LIMIT),
        name="merge_residual",
    )(x_lat, x_ctx, mod3, oa_lat, oa_ctx, or_lat, or_ctx, rest, wa_bf, wr_bf, wo_bf,
      ln_g.reshape(1, D_MODEL), ln_b.reshape(1, D_MODEL))


def _rope_tables(t_len):
    quarter = HEAD_DIM // 4
    inv_freq = ROPE_BASE ** (-np.arange(quarter, dtype=np.float64) / quarter)
    pos = np.arange(t_len)
    ang_r = (pos // GRID_W).astype(np.float64)[:, None] * inv_freq[None, :]
    ang_c = (pos % GRID_W).astype(np.float64)[:, None] * inv_freq[None, :]
    cos = np.concatenate([np.cos(ang_r)] * 2 + [np.cos(ang_c)] * 2, axis=1)
    sin = np.concatenate([-np.sin(ang_r), np.sin(ang_r), -np.sin(ang_c), np.sin(ang_c)], axis=1)
    return cos.astype(np.float32), sin.astype(np.float32)


def _gate_weights(rg_wa, rg_ba, rg_wx, rg_bx):
    w = jnp.concatenate([rg_wa[0], rg_wx[0], rg_wa[1], rg_wx[1]], axis=-1)
    blk = lambda v: v.reshape(N_RNN_BLOCKS, 1, RNN_BLOCK)
    b = 0.5 * jnp.concatenate([blk(rg_ba[0]), blk(rg_bx[0]), blk(rg_ba[1]), blk(rg_bx[1])], axis=-1)
    b_hi = b.astype(_BF16)
    b_lo = (b - b_hi.astype(_F32)).astype(_BF16)
    pad = jnp.zeros((N_RNN_BLOCKS, RNN_BLOCK - 2, 4 * RNN_BLOCK), _BF16)
    return jnp.concatenate([w.astype(_BF16), b_hi, b_lo, pad], axis=1)


def kernel(x_prompt, x_sample, cache_k, cache_v, state_h, c, c_ctx, w_mod, b_mod, w_in,
           attn_sink, conv_w, conv_b, rg_wa, rg_ba, rg_wx, rg_bx, rg_lambda,
           w_br_attn, w_br_rnn, w_out, ln_g, ln_b):
    bp, s_len, _ = x_prompt.shape
    bd, t_len, _ = x_sample.shape
    past = cache_k.shape[2]

    cond8 = jnp.zeros((COND_ROWS, D_MODEL), _F32).at[0].set(c_ctx).at[1:1 + bd].set(c)
    mod = _modulation(cond8, w_mod, b_mod)

    wa_bf = w_br_attn.astype(_BF16)
    wr_bf = w_br_rnn.astype(_BF16)
    wo_bf = w_out.astype(_BF16)
    rope_tabs = _rope_tables(t_len)
    cache_k4 = cache_k.reshape(bd, DEPTH, past * N_KV_HEADS, HEAD_DIM)
    cache_v4 = cache_v.reshape(bd, DEPTH, past * N_KV_HEADS, HEAD_DIM)

    tm_proj, tm_out = 512, 512
    n_lat = bd * t_len

    xp = x_prompt.reshape(bp * s_len, D_MODEL)
    xs = x_sample.reshape(n_lat, D_MODEL)
    h0_zero = jnp.zeros((bp, 2, D_RNN), _F32)
    hs = []
    kv_caches = None
    for l in range(DEPTH):
        mod3 = mod[l].reshape(COND_ROWS, 1, 3 * D_MODEL)
        wg = _gate_weights(rg_wa[l], rg_ba[l], rg_wx[l], rg_bx[l])
        q, kv_caches, (k_lat, v_lat), rest = _project(
            xs, xp, mod3, w_in, l, tm_proj, rope_tabs, (bd, t_len), (bp, s_len), kv_caches)

        oa_ctx = _context_attention(q, n_lat, kv_caches[0], kv_caches[1], l, attn_sink[l], bp, s_len)
        or_ctx, h_fin = _rnn(rest, n_lat, h0_zero, conv_w[l], conv_b[l], wg, rg_lambda[l], bp, s_len)
        hs.append(h_fin)

        oa_lat = _latent_attention(q, 0, k_lat, v_lat, cache_k4, cache_v4, l, attn_sink[l], bd, t_len)
        or_lat, _ = _rnn(rest, 0, state_h[:, l], conv_w[l], conv_b[l], wg, rg_lambda[l], bd, t_len)

        xs, xp = _merge_residual(xs, xp, mod3, oa_lat, oa_ctx, or_lat, or_ctx, rest, wa_bf, wr_bf, wo_bf, l,
                                 ln_g[l], ln_b[l], t_len, tm_out)

    y_prompt = xp.reshape(bp, s_len, D_MODEL)
    y_sample = xs.reshape(bd, t_len, D_MODEL)
    cache_shape = (bp, DEPTH, s_len, N_KV_HEADS, HEAD_DIM)
    new_k, new_v = (cache.reshape(cache_shape) for cache in kv_caches)
    return (y_prompt, y_sample, new_k, new_v, jnp.stack(hs, axis=1))
```

```python
import functools

import jax
import jax.numpy as jnp
import numpy as np
from jax import lax
from jax.experimental import pallas as pl
from jax.experimental.pallas import tpu as pltpu

D_MODEL = 1024
DEPTH = 2
GRID_W = 64
N_HEADS = 8
N_KV_HEADS = 2
HEAD_DIM = 128
Q_GROUP = N_HEADS // N_KV_HEADS
D_ATTN = N_HEADS * HEAD_DIM
D_KV = N_KV_HEADS * HEAD_DIM
WINDOW = 128
Q_BLOCK = 128
D_RNN = D_MODEL
N_RNN_BLOCKS = 8
RNN_BLOCK = D_RNN // N_RNN_BLOCKS
CONV_W = 4
CONV_LEFT = 2
LRU_C = 8.0
ROPE_BASE = 10000.0
D_IN = 2 * D_ATTN + 2 * D_KV + 2 * D_RNN + 2 * D_MODEL
DEEPNORM_ALPHA = (2 * DEPTH) ** 0.25
LN_EPS = 1e-6
NEG_INF = -1e30
ATTN_SCALE = HEAD_DIM ** -0.5

LANES = 128
SUBLANES = 8
SEQ_SLOTS = SUBLANES
SLOT_PAD = 4
CTX_ATTN_SEQS = 8
PASS1_ROWS = 1024
COND_ROWS = 8
PROJ_TN = 512
N_PROJ_BLOCKS = D_IN // PROJ_TN
D_REST = D_IN - D_ATTN - 2 * D_KV
_REST_COL = (0, 4, 1, 2, 3)
D_GATES = 4 * D_MODEL
XR_COL0 = 4 * D_MODEL
VMEM_LIMIT = 48 * 1024 * 1024
PROJ_VMEM_LIMIT = 56 * 1024 * 1024
RNN_VMEM_LIMIT = 56 * 1024 * 1024
F32_TINY = 1.1754943508222875e-38
LOG2_E = 1.4426950408889634
Q_PRESCALE = ATTN_SCALE * LOG2_E

_BF16 = jnp.bfloat16
_F32 = jnp.float32


def _silu(x):
    return x * jax.nn.sigmoid(x)


def _layer_norm_rows(x):
    mu = jnp.mean(x, axis=-1, keepdims=True)
    xc = x - mu
    var = jnp.mean(xc * xc, axis=-1, keepdims=True)
    return xc * lax.rsqrt(var + LN_EPS)


def _head_rows(head, n_rows):
    return pl.ds(head, n_rows, stride=N_KV_HEADS)


def _mod_kernel(cond_ref, w_ref, b_ref, o_ref):
    a = _silu(cond_ref[...]).astype(_BF16)
    o_ref[...] = jnp.dot(a, w_ref[...].astype(_BF16), preferred_element_type=_F32) + b_ref[...]


def _modulation(cond8, w_mod, b_mod):
    tn = D_MODEL
    return pl.pallas_call(
        _mod_kernel,
        grid=(DEPTH, 3 * D_MODEL // tn),
        in_specs=[
            pl.BlockSpec((COND_ROWS, D_MODEL), lambda l, j: (0, 0)),
            pl.BlockSpec((None, D_MODEL, tn), lambda l, j: (l, 0, j)),
            pl.BlockSpec((None, 1, tn), lambda l, j: (l, 0, j)),
        ],
        out_specs=pl.BlockSpec((None, COND_ROWS, tn), lambda l, j: (l, 0, j)),
        out_shape=jax.ShapeDtypeStruct((DEPTH, COND_ROWS, 3 * D_MODEL), _F32),
        compiler_params=pltpu.CompilerParams(dimension_semantics=("parallel", "parallel")),
        name="modulation",
    )(cond8, w_mod, b_mod.reshape(DEPTH, 1, 3 * D_MODEL))


def _rope(x, cos, sin_signed):
    quarter = HEAD_DIM // 4
    lane = lax.broadcasted_iota(jnp.int32, x.shape, 1)
    partner = jnp.where((lane // quarter) % 2 == 0,
                        pltpu.roll(x, HEAD_DIM - quarter, axis=1),
                        pltpu.roll(x, quarter, axis=1))
    return x * cos + partner * sin_signed


def _proj_kernel(xl_ref, xc_ref, shift_ref, scale_ref, w_hbm, cos_ref, sin_ref, *rest,
                 layer, n_lat_blocks):
    q_ref, kc_ref, vc_ref, kl_ref, vl_ref, r_ref, w_scr, stage_scr, sem = rest[-9:]
    i = pl.program_id(0)

    def chunk_copy(j):
        return pltpu.make_async_copy(
            w_hbm.at[layer, :, pl.ds(j * PROJ_TN, PROJ_TN)], stage_scr.at[j % 2], sem.at[j % 2])

    def fetch_chunk(j):
        if j == 0:
            chunk_copy(0).start()
        if j + 1 < N_PROJ_BLOCKS:
            chunk_copy(j + 1).start()
        chunk_copy(j).wait()
        w_scr[:, j * PROJ_TN:(j + 1) * PROJ_TN] = stage_scr[j % 2].astype(_BF16)

    x = jnp.where(i < n_lat_blocks, xl_ref[...], xc_ref[...])
    y = _layer_norm_rows(x)
    h = (y * (1.0 + scale_ref[...]) + shift_ref[...]).astype(_BF16)

    def rope(blk):
        return _rope(blk, cos_ref[...], sin_ref[...])

    def put_cache(ref, head, val):
        s_len = ref.shape[-2] // N_KV_HEADS
        for bi in range(ref.shape[0]):
            rows = val[bi * s_len:(bi + 1) * s_len].astype(ref.dtype)
            if len(ref.shape) == 3:
                ref[bi, _head_rows(head, s_len), :] = rows
            else:
                for l in range(ref.shape[1]):
                    ref[bi, l, _head_rows(head, s_len), :] = rows if l == layer else jnp.zeros_like(rows)

    def project_block(stream_weights):
        for j in range(N_PROJ_BLOCKS):
            c0 = j * PROJ_TN
            if stream_weights:
                fetch_chunk(j)
            acc = jnp.dot(h, w_scr[:, c0:c0 + PROJ_TN], preferred_element_type=_F32)
            if c0 < D_ATTN:
                for hh in range(PROJ_TN // HEAD_DIM):
                    sl = slice(hh * HEAD_DIM, (hh + 1) * HEAD_DIM)
                    q_ref[:, c0 + hh * HEAD_DIM:c0 + (hh + 1) * HEAD_DIM] = (
                        (rope(acc[:, sl]) * Q_PRESCALE).astype(q_ref.dtype))
            elif c0 == D_ATTN:
                for hh in range(N_KV_HEADS):
                    sl = slice(hh * HEAD_DIM, (hh + 1) * HEAD_DIM)
                    k_rot = rope(acc[:, sl])
                    put_cache(kc_ref, hh, k_rot)
                    put_cache(vc_ref, hh, acc[:, D_KV + hh * HEAD_DIM:D_KV + (hh + 1) * HEAD_DIM])
                    kl_ref[:, sl] = k_rot.astype(kl_ref.dtype)
                vl_ref[...] = acc[:, D_KV:].astype(vl_ref.dtype)
            else:
                off = c0 - D_ATTN - 2 * D_KV
                r0 = _REST_COL[off // D_MODEL] * D_MODEL + off % D_MODEL
                r_ref[:, r0:r0 + PROJ_TN] = acc.astype(r_ref.dtype)

    pl.when(i == 0)(functools.partial(project_block, True))
    pl.when(i != 0)(functools.partial(project_block, False))


def _project(x_lat, x_ctx, mod3, w_in, layer, tm, rope_tabs, lat_dims, ctx_dims, prev_caches):
    (bd, t_len), (bp, s_len) = lat_dims, ctx_dims
    n_lat, n_ctx = bd * t_len, bp * s_len
    n = n_lat + n_ctx
    nl_blocks = n_lat // tm
    is_lat = lambda i: i < nl_blocks
    ctx_blk = lambda i: jnp.maximum(i - nl_blocks, 0)
    cond_row = lambda i: jnp.where(is_lat(i), 1 + (i * tm) // t_len, 0)
    cos, sin = rope_tabs
    blocks_per_seq = t_len // tm
    cos = jnp.asarray(np.concatenate([np.ones((tm, HEAD_DIM), np.float32), cos], axis=0))
    sin = jnp.asarray(np.concatenate([np.zeros((tm, HEAD_DIM), np.float32), sin], axis=0))
    tab_blk = lambda i: jnp.where(is_lat(i), 1 + i % blocks_per_seq, 0)

    in_specs = [
        pl.BlockSpec((tm, D_MODEL), lambda i: (jnp.minimum(i, nl_blocks - 1), 0)),
        pl.BlockSpec((tm, D_MODEL), lambda i: (ctx_blk(i), 0)),
        pl.BlockSpec((None, 1, D_MODEL), lambda i: (cond_row(i), 0, 0)),
        pl.BlockSpec((None, 1, D_MODEL), lambda i: (cond_row(i), 0, 1)),
        pl.BlockSpec(memory_space=pl.ANY),
        pl.BlockSpec((tm, HEAD_DIM), lambda i: (tab_blk(i), 0)),
        pl.BlockSpec((tm, HEAD_DIM), lambda i: (tab_blk(i), 0)),
    ]
    args = [x_lat, x_ctx, mod3, mod3, w_in, cos, sin]
    aliases = {}
    seqs = tm // s_len
    cache_rows = N_KV_HEADS * s_len
    if prev_caches is None:
        cache_spec = pl.BlockSpec((seqs, DEPTH, cache_rows, HEAD_DIM), lambda i: (ctx_blk(i), 0, 0, 0))
    else:
        cache_spec = pl.BlockSpec((seqs, None, cache_rows, HEAD_DIM), lambda i: (ctx_blk(i), layer, 0, 0))
        aliases = {len(args): 1, len(args) + 1: 2}
        in_specs += [pl.BlockSpec(memory_space=pl.ANY)] * 2
        args += list(prev_caches)
    cache_shape = jax.ShapeDtypeStruct((bp, DEPTH, cache_rows, HEAD_DIM), _F32)
    lat_kv_spec = pl.BlockSpec((tm, D_KV), lambda i: (jnp.minimum(i, nl_blocks), 0))
    lat_kv_shape = jax.ShapeDtypeStruct((n_lat + tm, D_KV), _BF16)
    q, k_cache, v_cache, k_lat, v_lat, rest = pl.pallas_call(
        functools.partial(_proj_kernel, layer=layer, n_lat_blocks=nl_blocks),
        grid=(n // tm,),
        in_specs=in_specs,
        out_specs=[
            pl.BlockSpec((tm, D_ATTN), lambda i: (i, 0)),
            cache_spec, cache_spec, lat_kv_spec, lat_kv_spec,
            pl.BlockSpec((tm, D_REST), lambda i: (i, 0)),
        ],
        out_shape=[
            jax.ShapeDtypeStruct((n, D_ATTN), _BF16),
            cache_shape, cache_shape, lat_kv_shape, lat_kv_shape,
            jax.ShapeDtypeStruct((n, D_REST), _BF16),
        ],
        scratch_shapes=[
            pltpu.VMEM((D_MODEL, D_IN), _BF16),
            pltpu.VMEM((2, D_MODEL, PROJ_TN), _F32),
            pltpu.SemaphoreType.DMA((2,)),
        ],
        input_output_aliases=aliases,
        compiler_params=pltpu.CompilerParams(
            dimension_semantics=("arbitrary",), vmem_limit_bytes=PROJ_VMEM_LIMIT),
        name="proj",
    )(*args)
    return q, (k_cache, v_cache), (k_lat, v_lat), rest


def _softmax_pv_t(score_parts, sink_row, value_parts):
    m = sink_row
    for s in score_parts:
        m = jnp.maximum(m, jnp.max(s, axis=0, keepdims=True))
    acc = None
    for s, v_aug in zip(score_parts, value_parts):
        part = lax.dot_general(v_aug, jnp.exp2(s - m).astype(_BF16), (((0,), (0,)), ((), ())),
                               preferred_element_type=_F32)
        acc = part if acc is None else acc + part
    denom = acc[HEAD_DIM:HEAD_DIM + 1, :] + jnp.exp2(sink_row - m)
    return acc[:HEAD_DIM, :] / denom


def _softmax_pv(score_parts, sink_col, value_parts):
    m = sink_col
    for s in score_parts:
        m = jnp.maximum(m, jnp.max(s, axis=-1, keepdims=True))
    acc = None
    for s, v_aug in zip(score_parts, value_parts):
        part = jnp.dot(jnp.exp2(s - m).astype(_BF16), v_aug, preferred_element_type=_F32)
        acc = part if acc is None else acc + part
    denom = acc[:, HEAD_DIM:HEAD_DIM + 1] + jnp.exp2(sink_col - m)
    return acc[:, :HEAD_DIM] / denom


def _sink_column(sink_ref, kvh, rows):
    head = lax.broadcasted_iota(jnp.int32, (Q_GROUP * rows, 1), 0) // rows
    col = jnp.full((Q_GROUP * rows, 1), sink_ref[kvh * Q_GROUP], _F32)
    for g in range(1, Q_GROUP):
        col = jnp.where(head == g, sink_ref[kvh * Q_GROUP + g], col)
    return col * LOG2_E


def _with_ones(v_bf):
    return jnp.concatenate([v_bf, jnp.ones_like(v_bf)], axis=1)


def _sink_row(sink_ref, kvh, n_per_head):
    head = lax.broadcasted_iota(jnp.int32, (1, Q_GROUP * n_per_head), 1) // n_per_head
    row = jnp.full((1, Q_GROUP * n_per_head), sink_ref[kvh * Q_GROUP], _F32)
    for g in range(1, Q_GROUP):
        row = jnp.where(head == g, sink_ref[kvh * Q_GROUP + g], row)
    return row * LOG2_E


def _stack_heads(q_rows):
    return jnp.concatenate(
        [q_rows[:, g * HEAD_DIM:(g + 1) * HEAD_DIM] for g in range(Q_GROUP)], axis=0)


def _qk(q, k_bf):
    return lax.dot_general(q, k_bf, (((1,), (1,)), ((), ())), preferred_element_type=_F32)


def _ctx_attn_kernel(sink_ref, q_ref, k_ref, v_ref, o_ref, *, s_len):
    kvh = pl.program_id(1)
    sink_row = _sink_row(sink_ref, kvh, s_len)
    for bi in range(q_ref.shape[0] // s_len):
        rows = slice(bi * s_len, (bi + 1) * s_len)
        q = _stack_heads(q_ref[rows, :])
        k = k_ref[bi, _head_rows(kvh, s_len), :].astype(_BF16)
        v_aug = _with_ones(v_ref[bi, _head_rows(kvh, s_len), :].astype(_BF16))
        o_t = _softmax_pv_t([_qk(k, q)], sink_row, [v_aug])
        for g in range(Q_GROUP):
            o_ref[rows, g * HEAD_DIM:(g + 1) * HEAD_DIM] = (
                o_t[:, g * s_len:(g + 1) * s_len].T.astype(o_ref.dtype))


def _context_attention(q, q_row0, k_cache, v_cache, layer, sink, batch, s_len):
    n = batch * s_len
    tm = CTX_ATTN_SEQS * s_len
    gw = Q_GROUP * HEAD_DIM
    q_blk0 = q_row0 // tm
    kv_spec = pl.BlockSpec((CTX_ATTN_SEQS, None, N_KV_HEADS * s_len, HEAD_DIM), lambda b, h: (b, layer, 0, 0))
    return pl.pallas_call(
        functools.partial(_ctx_attn_kernel, s_len=s_len),
        grid=(n // tm, N_KV_HEADS),
        in_specs=[
            pl.BlockSpec(memory_space=pltpu.SMEM),
            pl.BlockSpec((tm, gw), lambda b, h: (q_blk0 + b, h)),
            kv_spec,
            kv_spec,
        ],
        out_specs=pl.BlockSpec((tm, gw), lambda b, h: (b, h)),
        out_shape=jax.ShapeDtypeStruct((n, D_ATTN), _BF16),
        compiler_params=pltpu.CompilerParams(dimension_semantics=("parallel", "parallel")),
        name="ctx_attention",
    )(sink, q, k_cache, v_cache)


def _lat_attn_kernel(sink_ref, q_ref, k_ref, v_ref, ck_ref, cv_ref, o_ref):
    kvh = pl.program_id(1)
    t_len = q_ref.shape[0]
    nb = t_len // Q_BLOCK
    past = ck_ref.shape[0] // N_KV_HEADS
    ck = ck_ref[_head_rows(kvh, past), :].astype(_BF16)
    cv_aug = _with_ones(cv_ref[_head_rows(kvh, past), :].astype(_BF16))
    v_aug = _with_ones(v_ref[...])
    sink_col = _sink_column(sink_ref, kvh, Q_BLOCK)
    rows = Q_GROUP * Q_BLOCK
    r = lax.broadcasted_iota(jnp.int32, (rows, Q_BLOCK), 0) % Q_BLOCK
    c = lax.broadcasted_iota(jnp.int32, (rows, Q_BLOCK), 1)
    for jb in range(nb):
        r0 = jb * Q_BLOCK
        q = _stack_heads(q_ref[r0:r0 + Q_BLOCK, :])
        lo = max(r0 - WINDOW, 0)
        hi = min(r0 + Q_BLOCK + WINDOW, t_len)
        s = _qk(q, k_ref[lo:hi, :])
        parts = []
        off = 0
        if jb > 0:
            parts.append(jnp.where(c >= r, s[:, :Q_BLOCK], NEG_INF))
            off = Q_BLOCK
        parts.append(s[:, off:off + Q_BLOCK])
        off += Q_BLOCK
        if jb < nb - 1:
            parts.append(jnp.where(c <= r, s[:, off:], NEG_INF))
        s_local = jnp.concatenate(parts, axis=1)
        o = _softmax_pv([s_local, _qk(q, ck)], sink_col, [v_aug[lo:hi], cv_aug])
        for g in range(Q_GROUP):
            o_ref[r0:r0 + Q_BLOCK, g * HEAD_DIM:(g + 1) * HEAD_DIM] = (
                o[g * Q_BLOCK:(g + 1) * Q_BLOCK].astype(o_ref.dtype))


def _latent_attention(q, q_row0, k, v, cache_k4, cache_v4, layer, sink, batch, t_len):
    n = batch * t_len
    gw = Q_GROUP * HEAD_DIM
    q_blk0 = q_row0 // t_len
    cache_spec = pl.BlockSpec((None, None, cache_k4.shape[2], HEAD_DIM), lambda b, h: (b, layer, 0, 0))
    return pl.pallas_call(
        _lat_attn_kernel,
        grid=(batch, N_KV_HEADS),
        in_specs=[
            pl.BlockSpec(memory_space=pltpu.SMEM),
            pl.BlockSpec((t_len, gw), lambda b, h: (q_blk0 + b, h)),
            pl.BlockSpec((t_len, HEAD_DIM), lambda b, h: (b, h)),
            pl.BlockSpec((t_len, HEAD_DIM), lambda b, h: (b, h)),
            cache_spec,
            cache_spec,
        ],
        out_specs=pl.BlockSpec((t_len, gw), lambda b, h: (b, h)),
        out_shape=jax.ShapeDtypeStruct((n, D_ATTN), _BF16),
        compiler_params=pltpu.CompilerParams(
            dimension_semantics=("parallel", "parallel"), vmem_limit_bytes=VMEM_LIMIT),
        name="latent_attention",
    )(sink, q, k, v, cache_k4, cache_v4)


def _shift_rows(x, d, t_len, offset=0):
    n = x.shape[0]
    row = (lax.broadcasted_iota(jnp.int32, x.shape, 0) + (n - offset)) % t_len
    rolled = pltpu.roll(x, d % n, axis=0)
    valid = (row >= d) if d > 0 else (row < t_len + d)
    return jnp.where(valid, rolled, 0.0)


def _sqrt_nonneg(z):
    return z * lax.rsqrt(jnp.maximum(z, F32_TINY))


def _rnn_kernel(x_ref, cw_ref, cb_ref, wg_ref, lam_ref, h0_ref, y_ref, hl_ref,
                af_scr, bf_scr, ab_scr, bb_scr, hf_scr, hb_scr, *, nb, nc):
    t_len = x_ref.shape[1]
    pitch = t_len + SLOT_PAD
    scr = ((af_scr, bf_scr), (ab_scr, bb_scr))
    per_group = SEQ_SLOTS // 2
    kb = min(per_group, max(1, PASS1_ROWS // t_len))

    def seq_of(slot):
        g, u = slot % 2, slot // 2
        return (g * per_group + u, 0) if nc == 1 else (u, g)

    for g in range(2):
        b_first, cl = seq_of(g)
        offset = SLOT_PAD * g
        lanes = slice(cl * LANES, (cl + 1) * LANES)
        cw = 0.5 * cw_ref[:, lanes]
        cb = 0.5 * cb_ref[:, lanes]
        wg = wg_ref[cl]
        lam = lam_ref[:, lanes]
        c_softplus = LRU_C * (jnp.maximum(-lam, 0.0) + jnp.log1p(jnp.exp(-jnp.abs(lam))))

        def per_chunk(it, carry, lanes=lanes, cw=cw, cb=cb, wg=wg,
                      c_softplus=c_softplus, g=g, b_first=b_first, offset=offset):
            n_rows = kb * t_len
            x = x_ref[pl.ds(b_first + it * kb, kb), :, lanes].astype(_F32).reshape(n_rows, LANES)
            if offset:
                x = pltpu.roll(x, offset, axis=0)
            half_xc = None
            for j in range(CONV_W):
                d = CONV_LEFT - j
                tap = cw[j:j + 1] * (_shift_rows(x, d, t_len, offset) if d else x)
                half_xc = tap if half_xc is None else half_xc + tap
            half_xc = half_xc + cb
            lane = lax.broadcasted_iota(jnp.int32, half_xc.shape, 1)
            bias_taps = jnp.where(lane < 2, 1.0, 0.0).astype(_BF16)
            lhs = jnp.concatenate([half_xc.astype(_BF16), bias_taps], axis=1)
            half_gates = jnp.dot(lhs, wg, preferred_element_type=_F32)
            for d in range(2):
                t_r = jnp.tanh(half_gates[:, (2 * d) * LANES:(2 * d + 1) * LANES])
                t_i = jnp.tanh(half_gates[:, (2 * d + 1) * LANES:(2 * d + 2) * LANES])
                k = 0.5 * c_softplus[d:d + 1]
                neg_log_a = t_r * k + k
                a = jnp.exp2(neg_log_a * -LOG2_E)
                bterm = _sqrt_nonneg(jnp.tanh(neg_log_a) * (a * a + 1.0)) * ((t_i + 1.0) * half_xc)
                for jb in range(kb):
                    slot = 2 * (it * kb + jb) + g
                    row0 = pl.multiple_of(slot * pitch - offset, SUBLANES)
                    for dst, val in ((scr[d][0], a), (scr[d][1], bterm)):
                        dst[pl.ds(row0, t_len), :] = val[jb * t_len:(jb + 1) * t_len]
                        if offset:
                            w0 = ((jb + 1) * t_len) % n_rows
                            dst[pl.ds(row0 + t_len, SUBLANES), :] = val[w0:w0 + SUBLANES]
            return carry

        if per_group == kb:
            per_chunk(0, 0)
        else:
            lax.fori_loop(0, per_group // kb, per_chunk, 0)

    def slot_rows(ref3, d):
        rows = []
        for slot in range(SEQ_SLOTS):
            b, cl = seq_of(slot)
            rows.append(ref3[b, d:d + 1, cl * LANES:(cl + 1) * LANES])
        return jnp.concatenate(rows, axis=0)

    for slot in range(SEQ_SLOTS):
        tile = (slot * pitch + t_len) // SUBLANES * SUBLANES
        for h_scr in (hf_scr, hb_scr):
            h_scr[pl.ds(tile, SUBLANES), :] = jnp.zeros((SUBLANES, LANES), _F32)

    def advance2(a_scr, b_scr, h_scr, t0, t1, h):
        r0 = pl.ds(t0, SEQ_SLOTS, stride=pitch)
        r1 = pl.ds(t1, SEQ_SLOTS, stride=pitch)
        a0, b0, a1, b1 = a_scr[r0, :], b_scr[r0, :], a_scr[r1, :], b_scr[r1, :]
        h_scr[r0, :] = a0 * h + b0
        h = (a1 * a0) * h + (a1 * b0 + b1)
        h_scr[r1, :] = h
        return h

    hf, hb = slot_rows(h0_ref, 0), slot_rows(h0_ref, 1)
    for t in range(0, t_len, 2):
        hf = advance2(af_scr, bf_scr, hf_scr, t, t + 1, hf)
        hb = advance2(ab_scr, bb_scr, hb_scr, t_len - 1 - t, t_len - 2 - t, hb)

    for slot in range(SEQ_SLOTS):
        b, cl = seq_of(slot)
        lanes = slice(cl * LANES, (cl + 1) * LANES)
        hl_ref[b, 0:1, lanes] = hf[slot:slot + 1, :]
        hl_ref[b, 1:2, lanes] = hb[slot:slot + 1, :]
        if slot % 2 == 0:
            rows = pl.ds(slot * pitch, t_len)
            y = hf_scr[rows, :] + hb_scr[rows, :]
        else:
            rows = pl.ds(slot * pitch - SLOT_PAD, t_len + SUBLANES)
            y = pltpu.roll(hf_scr[rows, :] + hb_scr[rows, :], t_len + SUBLANES - SLOT_PAD, axis=0)[:t_len]
        y_ref[b, :, lanes] = y.astype(y_ref.dtype)


def _rnn(rest, row0, h0, h0_layer, conv_w, conv_b, wg, lam, batch, t_len):
    nb = min(batch, SEQ_SLOTS)
    blk0 = row0 // (nb * t_len)
    nc = SEQ_SLOTS // nb
    assert nc in (1, 2) and batch % nb == 0, "a grid step holds 8 batches x 1 or 4 batches x 2 channel blocks"
    cw = nc * LANES
    xr_block0 = XR_COL0 // cw
    scratch = pltpu.VMEM((SEQ_SLOTS * (t_len + SLOT_PAD), LANES), _F32)
    y, h_last = pl.pallas_call(
        functools.partial(_rnn_kernel, nb=nb, nc=nc),
        grid=(batch // nb, D_RNN // cw),
        in_specs=[
            pl.BlockSpec((nb, t_len, cw), lambda g, c: (blk0 + g, 0, xr_block0 + c)),
            pl.BlockSpec((CONV_W, cw), lambda g, c: (0, c)),
            pl.BlockSpec((1, cw), lambda g, c: (0, c)),
            pl.BlockSpec((nc, 2 * RNN_BLOCK, 4 * LANES), lambda g, c: (c, 0, 0)),
            pl.BlockSpec((2, cw), lambda g, c: (0, c)),
            pl.BlockSpec((nb, None, 2, cw), lambda g, c: (g, h0_layer, 0, c)),
        ],
        out_specs=[
            pl.BlockSpec((nb, t_len, cw), lambda g, c: (g, 0, c)),
            pl.BlockSpec((nb, 2, cw), lambda g, c: (g, 0, c)),
        ],
        out_shape=[
            jax.ShapeDtypeStruct((batch, t_len, D_RNN), _BF16),
            jax.ShapeDtypeStruct((batch, 2, D_RNN), _F32),
        ],
        scratch_shapes=[scratch] * 6,
        compiler_params=pltpu.CompilerParams(
            dimension_semantics=("parallel", "parallel"), vmem_limit_bytes=RNN_VMEM_LIMIT),
        name="rnn",
    )(rest.reshape(rest.shape[0] // t_len, t_len, D_REST), conv_w, conv_b.reshape(1, D_RNN), wg, lam, h0)
    return y.reshape(batch * t_len, D_RNN), h_last


def _out_kernel(xl_ref, xc_ref, gate_ref, oal_ref, oac_ref, orl_ref, orc_ref, gates_ref,
                wa_ref, wr_ref, wo_ref, g_ref, b_ref, ol_ref, oc_ref, *, n_lat_blocks):
    i = pl.program_id(0)

    @pl.when(i < n_lat_blocks)
    def _():
        _out_block(xl_ref, gate_ref, oal_ref, orl_ref, gates_ref, wa_ref, wr_ref, wo_ref, g_ref, b_ref, ol_ref)

    @pl.when(i >= n_lat_blocks)
    def _():
        _out_block(xc_ref, gate_ref, oac_ref, orc_ref, gates_ref, wa_ref, wr_ref, wo_ref, g_ref, b_ref, oc_ref)


def _out_block(x_ref, gate_ref, oa_ref, or_ref, gates_ref, wa_ref, wr_ref, wo_ref, g_ref, b_ref, o_ref):
    ga_ref, gr_ref, mga_ref, mgr_ref = (
        gates_ref.at[:, p * D_MODEL:(p + 1) * D_MODEL] for p in range(4))
    def silu_bf(x):
        h = 0.5 * x
        return h * jnp.tanh(h) + h

    def logistic_f32(x):
        return (0.5 * jnp.tanh(0.5 * x) + 0.5).astype(_F32)

    ya = jnp.dot(oa_ref[...] * silu_bf(ga_ref[...]), wa_ref[...], preferred_element_type=_F32)
    yr = jnp.dot(or_ref[...] * silu_bf(gr_ref[...]), wr_ref[...], preferred_element_type=_F32)
    merged = logistic_f32(mga_ref[...]) * ya + logistic_f32(mgr_ref[...]) * yr
    out = jnp.dot(merged.astype(_BF16), wo_ref[...], preferred_element_type=_F32)
    z = DEEPNORM_ALPHA * x_ref[...] + gate_ref[...] * out
    o_ref[...] = _layer_norm_rows(z) * g_ref[...] + b_ref[...]


def _merge_residual(x_lat, x_ctx, mod3, oa_lat, oa_ctx, or_lat, or_ctx, rest, wa_bf, wr_bf, wo_bf,
                    layer, ln_g, ln_b, t_len, tm):
    n_lat, n_ctx = x_lat.shape[0], x_ctx.shape[0]
    nl, nc = n_lat // tm, n_ctx // tm
    lat_blk = pl.BlockSpec((tm, D_MODEL), lambda i: (jnp.minimum(i, nl - 1), 0))
    ctx_blk = pl.BlockSpec((tm, D_MODEL), lambda i: (jnp.maximum(i - nl, 0), 0))
    gates_blk = pl.BlockSpec((tm, D_GATES), lambda i: (i, 0))
    whole = pl.BlockSpec((None, D_MODEL, D_MODEL), lambda i: (layer, 0, 0), pipeline_mode=pl.Buffered(1))
    vec = pl.BlockSpec((1, D_MODEL), lambda i: (0, 0))
    cond_row = lambda i: jnp.where(i < nl, 1 + (i * tm) // t_len, 0)
    return pl.pallas_call(
        functools.partial(_out_kernel, n_lat_blocks=nl),
        grid=(nl + nc,),
        in_specs=[
            lat_blk, ctx_blk,
            pl.BlockSpec((None, 1, D_MODEL), lambda i: (cond_row(i), 0, 2)),
            lat_blk, ctx_blk, lat_blk, ctx_blk, gates_blk,
            whole, whole, whole, vec, vec,
        ],
        out_specs=[lat_blk, ctx_blk],
        out_shape=[jax.ShapeDtypeStruct((n_lat, D_MODEL), _F32),
                   jax.ShapeDtypeStruct((n_ctx, D_MODEL), _F32)],
        compiler_params=pltpu.CompilerParams(
            dimension_semantics=("arbitrary",), vmem_limit_bytes=VMEM_LIMIT),
        name="merge_residual",
    )(x_lat, x_ctx, mod3, oa_lat, oa_ctx, or_lat, or_ctx, rest, wa_bf, wr_bf, wo_bf,
      ln_g.reshape(1, D_MODEL), ln_b.reshape(1, D_MODEL))


def _rope_tables(t_len):
    quarter = HEAD_DIM // 4
    inv_freq = ROPE_BASE ** (-np.arange(quarter, dtype=np.float64) / quarter)
    pos = np.arange(t_len)
    ang_r = (pos // GRID_W).astype(np.float64)[:, None] * inv_freq[None, :]
    ang_c = (pos % GRID_W).astype(np.float64)[:, None] * inv_freq[None, :]
    cos = np.concatenate([np.cos(ang_r)] * 2 + [np.cos(ang_c)] * 2, axis=1)
    sin = np.concatenate([-np.sin(ang_r), np.sin(ang_r), -np.sin(ang_c), np.sin(ang_c)], axis=1)
    return cos.astype(np.float32), sin.astype(np.float32)


def _gate_weights(rg_wa, rg_ba, rg_wx, rg_bx):
    w = jnp.concatenate([rg_wa[0], rg_wx[0], rg_wa[1], rg_wx[1]], axis=-1)
    blk = lambda v: v.reshape(N_RNN_BLOCKS, 1, RNN_BLOCK)
    b = 0.5 * jnp.concatenate([blk(rg_ba[0]), blk(rg_bx[0]), blk(rg_ba[1]), blk(rg_bx[1])], axis=-1)
    b_hi = b.astype(_BF16)
    b_lo = (b - b_hi.astype(_F32)).astype(_BF16)
    pad = jnp.zeros((N_RNN_BLOCKS, RNN_BLOCK - 2, 4 * RNN_BLOCK), _BF16)
    return jnp.concatenate([w.astype(_BF16), b_hi, b_lo, pad], axis=1)


def kernel(x_prompt, x_sample, cache_k, cache_v, state_h, c, c_ctx, w_mod, b_mod, w_in,
           attn_sink, conv_w, conv_b, rg_wa, rg_ba, rg_wx, rg_bx, rg_lambda,
           w_br_attn, w_br_rnn, w_out, ln_g, ln_b):
    bp, s_len, _ = x_prompt.shape
    bd, t_len, _ = x_sample.shape
    past = cache_k.shape[2]

    cond8 = jnp.zeros((COND_ROWS, D_MODEL), _F32).at[0].set(c_ctx).at[1:1 + bd].set(c)
    mod = _modulation(cond8, w_mod, b_mod)

    wa_bf = w_br_attn.astype(_BF16)
    wr_bf = w_br_rnn.astype(_BF16)
    wo_bf = w_out.astype(_BF16)
    rope_tabs = _rope_tables(t_len)
    cache_k4 = cache_k.reshape(bd, DEPTH, past * N_KV_HEADS, HEAD_DIM)
    cache_v4 = cache_v.reshape(bd, DEPTH, past * N_KV_HEADS, HEAD_DIM)

    tm_proj, tm_out = 512, 512
    n_lat = bd * t_len

    xp = x_prompt.reshape(bp * s_len, D_MODEL)
    xs = x_sample.reshape(n_lat, D_MODEL)
    h0_zero = jnp.zeros((bp, 1, 2, D_RNN), _F32)
    hs = []
    kv_caches = None
    for l in range(DEPTH):
        mod3 = mod[l].reshape(COND_ROWS, 1, 3 * D_MODEL)
        wg = _gate_weights(rg_wa[l], rg_ba[l], rg_wx[l], rg_bx[l])
        q, kv_caches, (k_lat, v_lat), rest = _project(
            xs, xp, mod3, w_in, l, tm_proj, rope_tabs, (bd, t_len), (bp, s_len), kv_caches)

        oa_ctx = _context_attention(q, n_lat, kv_caches[0], kv_caches[1], l, attn_sink[l], bp, s_len)
        or_ctx, h_fin = _rnn(rest, n_lat, h0_zero, 0, conv_w[l], conv_b[l], wg, rg_lambda[l], bp, s_len)
        hs.append(h_fin)

        oa_lat = _latent_attention(q, 0, k_lat, v_lat, cache_k4, cache_v4, l, attn_sink[l], bd, t_len)
        or_lat, _ = _rnn(rest, 0, state_h, l, conv_w[l], conv_b[l], wg, rg_lambda[l], bd, t_len)

        xs, xp = _merge_residual(xs, xp, mod3, oa_lat, oa_ctx, or_lat, or_ctx, rest, wa_bf, wr_bf, wo_bf, l,
                                 ln_g[l], ln_b[l], t_len, tm_out)

    y_prompt = xp.reshape(bp, s_len, D_MODEL)
    y_sample = xs.reshape(bd, t_len, D_MODEL)
    cache_shape = (bp, DEPTH, s_len, N_KV_HEADS, HEAD_DIM)
    new_k, new_v = (cache.reshape(cache_shape) for cache in kv_caches)
    return (y_prompt, y_sample, new_k, new_v, jnp.stack(hs, axis=1))
```
